```python
import math
import jax
import jax.numpy as jnp
from jax import lax
import numpy as np

D_MODEL = 1024
BATCH = 2
SEQ = 8192
DEPTH = 1

HEAD_DIM = 64
A_HEADS = 8
B_HEADS = 8
N_SELF_HEADS = A_HEADS + B_HEADS
W_A = A_HEADS * HEAD_DIM
W_B = B_HEADS * HEAD_DIM
MOBA_BLOCK = 256
MOBA_TOPK = 3
MOBA_Q_CHUNK = 64
Q_BLOCK = 128
DSA_KV_RANK = 256
IDX_HEADS = 8
IDX_DIM = 32
DSA_TOPK = 256
REL_BUCKETS = 32
REL_MAX_DIST = 128
X_HEADS = 4
W_X = X_HEADS * HEAD_DIM
MEM_LEN = 256
N_GROUPS = 4
EXPERTS_PER_GROUP = 8
N_EXPERTS = N_GROUPS * EXPERTS_PER_GROUP
EXPERT_TOPK = 2
D_EXPERT = 512
MOE_BLOCK = 256
EPS = 1e-6
NEG = -1e30

OFF_AQ = 0
OFF_AK = OFF_AQ + W_A
OFF_AV = OFF_AK + W_A
OFF_BQ = OFF_AV + W_A
OFF_CKV = OFF_BQ + W_B
OFF_IQ = OFF_CKV + DSA_KV_RANK
OFF_IK = OFF_IQ + IDX_HEADS * IDX_DIM
OFF_IW = OFF_IK + IDX_DIM
OFF_GA = OFF_IW + IDX_HEADS
OFF_GB = OFF_GA + D_MODEL
IN_COLS = OFF_GB + D_MODEL

kernel_name = 'hybrid_moba_dsa_hmoe_block'


def rms_norm(x, g):
    xf = x.astype(jnp.float32)
    y = xf * lax.rsqrt(jnp.mean(xf * xf, axis=-1, keepdims=True) + EPS)
    return (y * g.astype(jnp.float32)).astype(x.dtype)


def t5_bucket(dist):
    n = jnp.maximum(dist, 0)
    max_exact = REL_BUCKETS // 2
    nf = jnp.maximum(n, 1).astype(jnp.float32)
    large = max_exact + (jnp.log(nf / max_exact) / math.log(REL_MAX_DIST / max_exact)
                         * (REL_BUCKETS - max_exact)).astype(jnp.int32)
    large = jnp.minimum(large, REL_BUCKETS - 1)
    return jnp.where(n < max_exact, n, large)


def moba_attention(q, k, v, bias_tab):
    Bn, T, H, Dh = q.shape
    nkb = -(-T // MOBA_BLOCK)
    tp = nkb * MOBA_BLOCK
    pad = ((0, 0), (0, tp - T), (0, 0), (0, 0))
    kb = jnp.pad(k, pad).reshape(Bn, nkb, MOBA_BLOCK, H, Dh).transpose(0, 3, 1, 2, 4)
    vb = jnp.pad(v, pad).reshape(Bn, nkb, MOBA_BLOCK, H, Dh).transpose(0, 3, 1, 2, 4)
    kmean = jnp.mean(kb, axis=3)
    tk = min(MOBA_TOPK, nkb)
    scale = Dh ** -0.5
    bidx = jnp.arange(Bn)[:, None, None, None]
    hidx = jnp.arange(H)[None, None, :, None]
    hidx5 = jnp.arange(H)[None, None, :, None, None]
    inblk = jnp.arange(MOBA_BLOCK)
    blk_ids = jnp.arange(nkb)

    def chunk(i):
        q0 = i * MOBA_Q_CHUNK
        qb = lax.dynamic_slice_in_dim(q, q0, MOBA_Q_CHUNK, axis=1)
        tq = q0 + jnp.arange(MOBA_Q_CHUNK)
        own = q0 // MOBA_BLOCK
        gs = jnp.einsum('bchd,bhnd->bchn', qb, kmean).astype(jnp.float32)
        gs = jnp.where(blk_ids < own, gs, -jnp.inf)
        _, idx = lax.top_k(gs, tk)
        valid = idx < own
        k_sel = kb[bidx, hidx, idx]
        v_sel = vb[bidx, hidx, idx]
        pos = idx[..., None] * MOBA_BLOCK + inblk
        b_sel = bias_tab[hidx5, t5_bucket(tq[None, :, None, None, None] - pos)]
        l_sel = jnp.einsum('bchd,bchnkd->bchnk', qb, k_sel).astype(jnp.float32) * scale + b_sel
        l_sel = jnp.where(valid[..., None], l_sel, NEG).reshape(Bn, MOBA_Q_CHUNK, H, tk * MOBA_BLOCK)
        k_own = lax.dynamic_slice_in_dim(kb, own, 1, axis=2)[:, :, 0]
        v_own = lax.dynamic_slice_in_dim(vb, own, 1, axis=2)[:, :, 0]
        dist_own = tq[:, None] - (own * MOBA_BLOCK + inblk)[None, :]
        b_own = bias_tab[:, t5_bucket(dist_own)].transpose(1, 0, 2)[None]
        l_own = jnp.einsum('bchd,bhkd->bchk', qb, k_own).astype(jnp.float32) * scale + b_own
        l_own = jnp.where((dist_own >= 0)[None, :, None, :], l_own, NEG)
        p = jax.nn.softmax(jnp.concatenate([l_sel, l_own], axis=-1), axis=-1)
        n_sel = tk * MOBA_BLOCK
        p_sel = p[..., :n_sel].reshape(Bn, MOBA_Q_CHUNK, H, tk, MOBA_BLOCK).astype(v.dtype)
        p_own = p[..., n_sel:].astype(v.dtype)
        return (jnp.einsum('bchnk,bchnkd->bchd', p_sel, v_sel)
                + jnp.einsum('bchk,bhkd->bchd', p_own, v_own))

    out = lax.map(chunk, jnp.arange(T // MOBA_Q_CHUNK))
    return out.transpose(1, 0, 2, 3, 4).reshape(Bn, T, H, Dh)


def dsa_attention(q, k, v, q_idx, k_idx, w_idx, bias_tab):
    Bn, T, H, Dh = q.shape
    ksel = min(DSA_TOPK, T // 4)
    scale = Dh ** -0.5
    bidx = jnp.arange(Bn)[:, None, None]
    spos = jnp.arange(T)

    def chunk(i):
        q0 = i * Q_BLOCK
        qb = lax.dynamic_slice_in_dim(q, q0, Q_BLOCK, axis=1)
        qi = lax.dynamic_slice_in_dim(q_idx, q0, Q_BLOCK, axis=1)
        wi = lax.dynamic_slice_in_dim(w_idx, q0, Q_BLOCK, axis=1)
        tq = q0 + jnp.arange(Q_BLOCK)
        s = jax.nn.relu(jnp.einsum('bchd,bsd->bchs', qi, k_idx).astype(jnp.float32))
        s = jnp.einsum('bchs,bch->bcs', s, wi.astype(jnp.float32))
        s = jnp.where((spos[None, :] <= tq[:, None])[None], s, -jnp.inf)
        _, idx = lax.top_k(s, ksel)
        dist = tq[None, :, None] - idx
        k_g = k[bidx, idx]
        v_g = v[bidx, idx]
        bias = jnp.moveaxis(bias_tab[:, t5_bucket(dist)], 0, 2)
        l = jnp.einsum('bchd,bckhd->bchk', qb, k_g).astype(jnp.float32) * scale + bias
        l = jnp.where((dist >= 0)[:, :, None, :], l, NEG)
        p = jax.nn.softmax(l, axis=-1).astype(v.dtype)
        return jnp.einsum('bchk,bckhd->bchd', p, v_g)

    out = lax.map(chunk, jnp.arange(T // Q_BLOCK))
    return out.transpose(1, 0, 2, 3, 4).reshape(Bn, T, H, Dh)


def memory_cross_attention(h, m, wq, wk, wv, wo):
    Bn, T, _ = h.shape
    q = (h @ wq).reshape(Bn, T, X_HEADS, HEAD_DIM)
    km = (m @ wk).reshape(Bn, -1, X_HEADS, HEAD_DIM)
    vm = (m @ wv).reshape(Bn, -1, X_HEADS, HEAD_DIM)
    l = jnp.einsum('bthd,bmhd->bhtm', q, km).astype(jnp.float32) * HEAD_DIM ** -0.5
    p = jax.nn.softmax(l, axis=-1).astype(vm.dtype)
    o = jnp.einsum('bhtm,bmhd->bthd', p, vm).reshape(Bn, T, W_X)
    return o @ wo


def hier_moe(h, w_group, b_group, w_router, b_router, w1, w3, w2):
    Bn, T, D = h.shape
    N = Bn * T
    hf = h.reshape(N, D)
    g_logits = (hf @ w_group + b_group).astype(jnp.float32)
    g_prob = jax.nn.softmax(g_logits, axis=-1)
    g_sel = jnp.argmax(g_logits, axis=-1)
    g_w = jnp.take_along_axis(g_prob, g_sel[:, None], axis=1)
    e_logits = (hf @ w_router + b_router).astype(jnp.float32).reshape(N, N_GROUPS, EXPERTS_PER_GROUP)
    within = jnp.take_along_axis(e_logits, g_sel[:, None, None], axis=1)[:, 0]
    top_v, top_i = lax.top_k(within, EXPERT_TOPK)
    top_p = jax.nn.softmax(top_v, axis=-1)
    weight = g_w * top_p
    eid = g_sel[:, None] * EXPERTS_PER_GROUP + top_i
    A = N * EXPERT_TOPK
    e_flat = eid.reshape(A)
    tok = jnp.repeat(jnp.arange(N, dtype=jnp.int32), EXPERT_TOPK)
    w_flat = weight.reshape(A)
    counts = jax.ops.segment_sum(jnp.ones((A,), jnp.int32), e_flat, num_segments=N_EXPERTS)
    padded = ((counts + MOE_BLOCK - 1) // MOE_BLOCK) * MOE_BLOCK
    seg_end = jnp.cumsum(padded)
    seg_start = seg_end - padded
    first = jnp.cumsum(counts) - counts
    order = jnp.argsort(e_flat, stable=True)
    e_s = e_flat[order]
    dest = seg_start[e_s] + (jnp.arange(A) - first[e_s])
    R = A + N_EXPERTS * MOE_BLOCK
    NB = R // MOE_BLOCK
    row_tok = jnp.full((R,), N, jnp.int32).at[dest].set(tok[order])
    row_w = jnp.zeros((R,), jnp.float32).at[dest].set(w_flat[order])
    hpad = jnp.concatenate([hf, jnp.zeros((1, D), hf.dtype)], axis=0)
    xr = hpad[row_tok].reshape(NB, MOE_BLOCK, D)
    blk_e = jnp.minimum(jnp.searchsorted(seg_end, jnp.arange(NB) * MOE_BLOCK, side='right'),
                        N_EXPERTS - 1)

    def expert_block(args):
        xb, e = args
        return (jax.nn.silu(xb @ w1[e]) * (xb @ w3[e])) @ w2[e]

    yr = lax.map(expert_block, (xr, blk_e)).reshape(R, D)
    out = jnp.zeros((N + 1, D), jnp.float32).at[row_tok].add(yr.astype(jnp.float32) * row_w[:, None])
    return out[:N].astype(h.dtype).reshape(Bn, T, D)


def setup_inputs(seed: int = 0) -> dict:
    key = jax.random.key(seed)
    ks = jax.random.split(key, 26)

    def nrm(k, shape, scale):
        return jax.random.normal(k, shape, jnp.float32) * scale

    def gain(k, shape):
        return 1.0 + 0.02 * jax.random.normal(k, shape, jnp.float32)

    L = DEPTH
    return {
        'x': nrm(ks[0], (BATCH, SEQ, D_MODEL), 1.0),
        'mem': nrm(ks[1], (BATCH, MEM_LEN, D_MODEL), 1.0),
        'rel_bias': nrm(ks[2], (REL_BUCKETS, N_SELF_HEADS), 0.5),
        'final_norm': gain(ks[3], (D_MODEL,)),
        'norm_mix': gain(ks[4], (L, D_MODEL)),
        'w_in': nrm(ks[5], (L, D_MODEL, IN_COLS), D_MODEL ** -0.5),
        'ckv_norm': gain(ks[6], (L, DSA_KV_RANK)),
        'w_uk': nrm(ks[7], (L, DSA_KV_RANK, W_B), DSA_KV_RANK ** -0.5),
        'w_uv': nrm(ks[8], (L, DSA_KV_RANK, W_B), DSA_KV_RANK ** -0.5),
        'w_oa': nrm(ks[9], (L, W_A, D_MODEL), W_A ** -0.5),
        'w_ob': nrm(ks[10], (L, W_B, D_MODEL), W_B ** -0.5),
        'w_out': nrm(ks[11], (L, D_MODEL, D_MODEL), D_MODEL ** -0.5),
        'norm_x': gain(ks[12], (L, D_MODEL)),
        'mem_norm': gain(ks[13], (L, D_MODEL)),
        'wq_x': nrm(ks[14], (L, D_MODEL, W_X), D_MODEL ** -0.5),
        'wk_x': nrm(ks[15], (L, D_MODEL, W_X), D_MODEL ** -0.5),
        'wv_x': nrm(ks[16], (L, D_MODEL, W_X), D_MODEL ** -0.5),
        'wo_x': nrm(ks[17], (L, W_X, D_MODEL), W_X ** -0.5),
        'norm_moe': gain(ks[18], (L, D_MODEL)),
        'w_group': nrm(ks[19], (L, D_MODEL, N_GROUPS), D_MODEL ** -0.5),
        'b_group': nrm(ks[20], (L, N_GROUPS), 0.01),
        'w_router': nrm(ks[21], (L, D_MODEL, N_EXPERTS), D_MODEL ** -0.5),
        'b_router': nrm(ks[22], (L, N_EXPERTS), 0.01),
        'w1': nrm(ks[23], (L, N_EXPERTS, D_MODEL, D_EXPERT), D_MODEL ** -0.5),
        'w3': nrm(ks[24], (L, N_EXPERTS, D_MODEL, D_EXPERT), D_MODEL ** -0.5),
        'w2': nrm(ks[25], (L, N_EXPERTS, D_EXPERT, D_MODEL), D_EXPERT ** -0.5),
    }


def reference(x, mem, rel_bias, final_norm, norm_mix, w_in, ckv_norm, w_uk, w_uv, w_oa, w_ob,
              w_out, norm_x, mem_norm, wq_x, wk_x, wv_x, wo_x, norm_moe, w_group, b_group,
              w_router, b_router, w1, w3, w2):
    Bn, T, _ = x.shape
    bias_a = rel_bias[:, :A_HEADS].T
    bias_b = rel_bias[:, A_HEADS:].T
    idx_scale = (IDX_HEADS ** -0.5) * (IDX_DIM ** -0.5)
    for l in range(DEPTH):
        h = rms_norm(x, norm_mix[l])
        z = h @ w_in[l]
        qa = z[..., OFF_AQ:OFF_AK].reshape(Bn, T, A_HEADS, HEAD_DIM)
        ka = z[..., OFF_AK:OFF_AV].reshape(Bn, T, A_HEADS, HEAD_DIM)
        va = z[..., OFF_AV:OFF_BQ].reshape(Bn, T, A_HEADS, HEAD_DIM)
        oa = moba_attention(qa, ka, va, bias_a).reshape(Bn, T, W_A) @ w_oa[l]
        qb = z[..., OFF_BQ:OFF_CKV].reshape(Bn, T, B_HEADS, HEAD_DIM)
        ckv = rms_norm(z[..., OFF_CKV:OFF_IQ], ckv_norm[l])
        kb = (ckv @ w_uk[l]).reshape(Bn, T, B_HEADS, HEAD_DIM)
        vb = (ckv @ w_uv[l]).reshape(Bn, T, B_HEADS, HEAD_DIM)
        q_idx = z[..., OFF_IQ:OFF_IK].reshape(Bn, T, IDX_HEADS, IDX_DIM)
        k_idx = z[..., OFF_IK:OFF_IW]
        w_idx = z[..., OFF_IW:OFF_GA] * idx_scale
        ob = dsa_attention(qb, kb, vb, q_idx, k_idx, w_idx, bias_b).reshape(Bn, T, W_B) @ w_ob[l]
        ga = jax.nn.sigmoid(z[..., OFF_GA:OFF_GB])
        gb = jax.nn.sigmoid(z[..., OFF_GB:IN_COLS])
        x = x + (ga * oa + gb * ob) @ w_out[l]
        x = x + memory_cross_attention(rms_norm(x, norm_x[l]), rms_norm(mem, mem_norm[l]),
                                       wq_x[l], wk_x[l], wv_x[l], wo_x[l])
        x = x + hier_moe(rms_norm(x, norm_moe[l]), w_group[l], b_group[l], w_router[l],
                         b_router[l], w1[l], w3[l], w2[l])
    return rms_norm(x, final_norm)
```

```python
import functools
import math

import jax
import jax.numpy as jnp
from jax import lax
from jax.experimental import pallas as pl
from jax.experimental.pallas import tpu as pltpu

D_MODEL = 1024
HEAD_DIM = 64
A_HEADS = 8
B_HEADS = 8
W_A = A_HEADS * HEAD_DIM
W_B = B_HEADS * HEAD_DIM
MOBA_BLOCK = 256
MOBA_TOPK = 3
DSA_KV_RANK = 256
IDX_HEADS = 8
IDX_DIM = 32
DSA_TOPK = 256
REL_BUCKETS = 32
REL_MAX_DIST = 128
X_HEADS = 4
W_X = X_HEADS * HEAD_DIM
N_GROUPS = 4
EXPERTS_PER_GROUP = 8
N_EXPERTS = N_GROUPS * EXPERTS_PER_GROUP
D_EXPERT = 512
MOE_BLOCK = 256
EPS = 1e-6
NEG = -1e30

OFF_AQ = 0
OFF_AK = OFF_AQ + W_A
OFF_AV = OFF_AK + W_A
OFF_BQ = OFF_AV + W_A
OFF_CKV = OFF_BQ + W_B
OFF_IQ = OFF_CKV + DSA_KV_RANK
OFF_IK = OFF_IQ + IDX_HEADS * IDX_DIM
OFF_IW = OFF_IK + IDX_DIM
OFF_GA = OFF_IW + IDX_HEADS
OFF_GB = OFF_GA + D_MODEL
IN_COLS = OFF_GB + D_MODEL

LANES = 128
TILE = 256
VMEM_LIMIT = 56 * 1024 * 1024
BF16 = jnp.bfloat16
F32 = jnp.float32
I32 = jnp.int32

P_QA, P_KA, P_VA, P_QB = 0, 512, 1024, 1536
P_CKV, P_IQ, P_IK4, P_IW = 2048, 2304, 2560, 2688
P_COLS = 2816

R_GRP = 0
R_EXP = N_GROUPS


def _nt(a, b):
    return lax.dot_general(a, b, (((1,), (1,)), ((), ())), preferred_element_type=F32)


def _mm(a, b):
    return jnp.dot(a, b, preferred_element_type=F32)


def _rms(xf, g):
    return xf * lax.rsqrt(jnp.mean(xf * xf, axis=-1, keepdims=True) + EPS) * g


def _params(sem, vmem=VMEM_LIMIT):
    return pltpu.CompilerParams(dimension_semantics=sem, vmem_limit_bytes=vmem)


def _mem_kv_kernel(mem_ref, g_ref, w_ref, o_ref):
    m = _rms(mem_ref[0], g_ref[...]).astype(BF16)
    o_ref[0] = _mm(m, w_ref[...]).astype(BF16)


def _mem_kv(mem, g, wkv):
    bn, ml, d = mem.shape
    return pl.pallas_call(
        _mem_kv_kernel,
        grid=(bn,),
        in_specs=[pl.BlockSpec((1, ml, d), lambda b: (b, 0, 0)),
                  pl.BlockSpec((1, d), lambda b: (0, 0)),
                  pl.BlockSpec((d, 2 * W_X), lambda b: (0, 0))],
        out_specs=pl.BlockSpec((1, ml, 2 * W_X), lambda b: (b, 0, 0)),
        out_shape=jax.ShapeDtypeStruct((bn, ml, 2 * W_X), BF16),
        compiler_params=_params(("arbitrary",)),
        name="mem_kv",
    )(mem, g, wkv)


def _proj_in_kernel(x_ref, g_ref, w_ref, cg_ref, wkv_ref,
                    qa_ref, ka_ref, va_ref, qb_ref, kb_ref, vb_ref, qi_ref, ki_ref, wi_ref, km_ref):
    h = _rms(x_ref[...], g_ref[...]).astype(BF16)
    z = _mm(h, w_ref[...])
    scale = HEAD_DIM ** -0.5
    qa_ref[...] = (z[:, P_QA:P_KA] * scale).astype(BF16)
    zk = z[:, P_KA:P_VA]
    ka_ref[...] = zk.astype(BF16)
    km_ref[0] = jnp.mean(zk, axis=0, keepdims=True)
    va_ref[...] = z[:, P_VA:P_QB].astype(BF16)
    qb_ref[...] = (z[:, P_QB:P_CKV] * scale).astype(BF16)
    ckv = _rms(z[:, P_CKV:P_IQ], cg_ref[...]).astype(BF16)
    kv = _mm(ckv, wkv_ref[...])
    kb_ref[...] = kv[:, :W_B].astype(BF16)
    vb_ref[...] = kv[:, W_B:].astype(BF16)
    qi_ref[...] = z[:, P_IQ:P_IK4].astype(BF16)
    ki_ref[...] = z[:, P_IK4:P_IW].astype(BF16)
    wi_ref[...] = z[:, P_IW:P_COLS] * ((IDX_HEADS ** -0.5) * (IDX_DIM ** -0.5))


def _proj_in(xf, g, wp, cg, wkv):
    n, d = xf.shape
    nt = n // TILE
    row = lambda w: pl.BlockSpec((TILE, w), lambda i: (i, 0))
    full = lambda a: pl.BlockSpec(a.shape, lambda i: (0,) * a.ndim)
    outs = [(W_A, BF16), (W_A, BF16), (W_A, BF16), (W_B, BF16), (W_B, BF16), (W_B, BF16),
            (IDX_HEADS * IDX_DIM, BF16), (LANES, BF16), (LANES, F32)]
    return pl.pallas_call(
        _proj_in_kernel,
        grid=(nt,),
        in_specs=[row(d), full(g), full(wp), full(cg), full(wkv)],
        out_specs=[row(w) for w, _ in outs] + [pl.BlockSpec((1, 1, W_A), lambda i: (i, 0, 0))],
        out_shape=[jax.ShapeDtypeStruct((n, w), t) for w, t in outs]
        + [jax.ShapeDtypeStruct((nt, 1, W_A), F32)],
        compiler_params=_params(("arbitrary",)),
        name="proj_in",
    )(xf, g, wp, cg, wkv)


def _flash_init(m_ref, l_ref, acc_ref):
    m_ref[...] = jnp.full(m_ref.shape, NEG, F32)
    l_ref[...] = jnp.zeros(l_ref.shape, F32)
    acc_ref[...] = jnp.zeros(acc_ref.shape, F32)


def _flash_step(hd, s, v, m_ref, l_ref, acc_ref):
    m_old = m_ref[hd]
    m_new = jnp.maximum(m_old, jnp.max(s, axis=1, keepdims=True))
    alpha = jnp.exp(m_old - m_new)
    p = jnp.exp(s - jnp.concatenate([m_new, m_new], axis=1))
    l_ref[hd] = alpha * l_ref[hd] + jnp.sum(p, axis=1, keepdims=True)
    acc_ref[hd] = alpha * acc_ref[hd] + _mm(p.astype(BF16), v)
    m_ref[hd] = m_new


def _split_pair(qp):
    lane = lax.broadcasted_iota(I32, qp.shape, 1)
    zero = jnp.zeros_like(qp)
    return jnp.where(lane < HEAD_DIM, qp, zero), jnp.where(lane >= HEAD_DIM, qp, zero)


def _merge_pair(acc_ref, l_ref):
    lane = lax.broadcasted_iota(I32, (TILE, LANES), 1)
    return jnp.where(lane < HEAD_DIM, acc_ref[0] / l_ref[0], acc_ref[1] / l_ref[1])


def _moba_kernel(far_ref, q_ref, k_ref, v_ref, km_ref, bias_ref, o_ref,
                 sel_ref, m_ref, l_ref, acc_ref):
    p_idx = pl.program_id(1)
    i = pl.program_id(2)
    nblk = km_ref.shape[1]
    qs = _split_pair(q_ref[0])
    km = km_ref[0].astype(BF16)
    blk = lax.broadcasted_iota(I32, (TILE, nblk), 1)
    valid = blk < i

    for hd in range(2):
        g = jnp.where(valid, _nt(qs[hd], km), -jnp.inf)
        sel = jnp.zeros((TILE, nblk), jnp.bool_)
        for _ in range(MOBA_TOPK):
            top = jnp.max(g, axis=1, keepdims=True)
            first = jnp.min(jnp.where(g == top, blk, nblk), axis=1, keepdims=True)
            pick = blk == first
            sel = jnp.logical_or(sel, jnp.logical_and(pick, valid))
            g = jnp.where(pick, -jnp.inf, g)
        addm = jnp.where(sel, 0.0, NEG).astype(F32)
        for j in range(nblk):
            sel_ref[hd, j] = jnp.broadcast_to(addm[:, j:j + 1], (TILE, LANES))

    _flash_init(m_ref, l_ref, acc_ref)

    def kv(j):
        off = pl.multiple_of(j * TILE, TILE)
        return k_ref[0, pl.ds(off, TILE), :], v_ref[0, pl.ds(off, TILE), :]

    k_own, v_own = kv(i)
    for hd in range(2):
        _flash_step(hd, _nt(qs[hd], k_own) + bias_ref[hd, 0], v_own, m_ref, l_ref, acc_ref)

    @pl.when(i >= 1)
    def _():
        k_p, v_p = kv(i - 1)
        for hd in range(2):
            msk = sel_ref[hd, i - 1]
            s = _nt(qs[hd], k_p) + bias_ref[hd, 1] + jnp.concatenate([msk, msk], axis=1)
            _flash_step(hd, s, v_p, m_ref, l_ref, acc_ref)

    def far(j, carry):
        k_j, v_j = kv(j)
        for hd in range(2):
            msk = sel_ref[hd, j] + far_ref[2 * p_idx + hd]
            s = _nt(qs[hd], k_j) + jnp.concatenate([msk, msk], axis=1)
            _flash_step(hd, s, v_j, m_ref, l_ref, acc_ref)
        return carry

    lax.fori_loop(0, jnp.maximum(i - 1, 0), far, 0)
    o_ref[0] = _merge_pair(acc_ref, l_ref).astype(BF16)


def _moba(qa, ka, va, kmean, bias_near, bias_far):
    bn, t, _ = qa.shape
    nq = t // TILE
    npair = A_HEADS // 2
    return pl.pallas_call(
        _moba_kernel,
        grid=(bn, npair, nq),
        in_specs=[pl.BlockSpec(memory_space=pltpu.SMEM),
                  pl.BlockSpec((1, TILE, LANES), lambda b, p, i: (b, i, p)),
                  pl.BlockSpec((1, t, LANES), lambda b, p, i: (b, 0, p)),
                  pl.BlockSpec((1, t, LANES), lambda b, p, i: (b, 0, p)),
                  pl.BlockSpec((1, nq, LANES), lambda b, p, i: (b, 0, p)),
                  pl.BlockSpec((2, 2, TILE, TILE), lambda b, p, i: (p, 0, 0, 0))],
        out_specs=pl.BlockSpec((1, TILE, LANES), lambda b, p, i: (b, i, p)),
        out_shape=jax.ShapeDtypeStruct((bn, t, W_A), BF16),
        scratch_shapes=[pltpu.VMEM((2, nq, TILE, LANES), F32),
                        pltpu.VMEM((2, TILE, LANES), F32),
                        pltpu.VMEM((2, TILE, LANES), F32),
                        pltpu.VMEM((2, TILE, LANES), F32)],
        compiler_params=_params(("arbitrary", "arbitrary", "arbitrary")),
        name="moba",
    )(bias_far, qa, ka, va, kmean, bias_near)


ROWS = 128


def _key_to_f32(key):
    bits = key ^ ((key >> 31) & 0x7FFFFFFF)
    val = lax.bitcast_convert_type(bits, F32)
    ninf_key = jnp.int32(-(2 ** 31) + 0x7FFFFF)
    return jnp.where(key <= ninf_key, -jnp.inf, val)


def _dsa_kernel(far_ref, qi_ref, ki_ref, wi_ref, q_ref, k_ref, v_ref, bias_ref, o_ref,
                s_ref, wb_ref, m_ref, l_ref, acc_ref):
    i = pl.program_id(1)
    nch = i + 1
    row = lax.broadcasted_iota(I32, (TILE, TILE), 0)
    col = lax.broadcasted_iota(I32, (TILE, TILE), 1)
    lane = lax.broadcasted_iota(I32, (TILE, LANES), 1)

    wi = wi_ref[0]
    for h in range(IDX_HEADS):
        wb_ref[h] = jnp.broadcast_to(wi[:, h:h + 1], (TILE, LANES))
    qi = qi_ref[0]
    per_group = LANES // IDX_DIM
    qh = []
    for h in range(IDX_HEADS):
        g, r = divmod(h, per_group)
        qg = qi[:, g * LANES:(g + 1) * LANES]
        keep = jnp.logical_and(lane >= r * IDX_DIM, lane < (r + 1) * IDX_DIM)
        qh.append(jnp.where(keep, qg, jnp.zeros_like(qg)))

    def score(c, carry):
        off = pl.multiple_of(c * TILE, TILE)
        kc = ki_ref[0, pl.ds(off, TILE), :]
        acc = jnp.zeros((TILE, TILE), F32)
        for h in range(IDX_HEADS):
            w = wb_ref[h]
            acc = acc + jnp.maximum(_nt(qh[h], kc), 0.0) * jnp.concatenate([w, w], axis=1)
        s_ref[c] = jnp.where(c * TILE + col <= i * TILE + row, acc, -jnp.inf)
        return carry

    lax.fori_loop(0, nch, score, 0)

    def count(r0, pred):
        def body(c, acc):
            blk = s_ref[c, pl.ds(r0, ROWS), :]
            hit = pred(blk, c)
            return acc + jnp.where(hit[:, :LANES], 1.0, 0.0) + jnp.where(hit[:, LANES:], 1.0, 0.0)
        acc = lax.fori_loop(0, nch, body, jnp.zeros((ROWS, LANES), F32))
        return jnp.sum(acc, axis=1, keepdims=True)

    def wide(x):
        return jnp.broadcast_to(x, (ROWS, TILE))

    int_min = jnp.int32(-(2 ** 31))
    ccol = lax.broadcasted_iota(I32, (ROWS, TILE), 1)
    for rg in range(TILE // ROWS):
        r0 = rg * ROWS

        def bit_step(b, ans):
            cand = ans | (jnp.int32(1) << (31 - b))
            cf = wide(_key_to_f32(cand ^ int_min))
            n = count(r0, lambda blk, c: blk >= cf)
            return jnp.where(n >= DSA_TOPK, cand, ans)

        ans = lax.fori_loop(0, 32, bit_step, jnp.zeros((ROWS, 1), I32))
        thr = wide(_key_to_f32(ans ^ int_min))
        n_gt = count(r0, lambda blk, c: blk > thr)
        n_ge = count(r0, lambda blk, c: blk >= thr)
        need = DSA_TOPK - n_gt

        def cut_step(b, cut):
            cand = cut | (jnp.int32(1) << (13 - b))
            cw = wide(cand)
            n = count(r0, lambda blk, c: jnp.logical_and(blk == thr, c * TILE + ccol < cw))
            return jnp.where(n < need, cand, cut)

        cut = lax.cond(jnp.max(n_ge) > DSA_TOPK,
                       lambda: lax.fori_loop(0, 14, cut_step, jnp.zeros((ROWS, 1), I32)),
                       lambda: jnp.full((ROWS, 1), 2 ** 30, I32))
        cut = wide(jnp.where(n_ge > DSA_TOPK, cut, 2 ** 30))
        qpos = i * TILE + r0 + lax.broadcasted_iota(I32, (ROWS, TILE), 0)

        def to_mask(c, carry):
            blk = s_ref[c, pl.ds(r0, ROWS), :]
            kpos = c * TILE + ccol
            keep = jnp.logical_or(blk > thr, jnp.logical_and(blk == thr, kpos <= cut))
            keep = jnp.logical_and(keep, kpos <= qpos)
            s_ref[c, pl.ds(r0, ROWS), :] = jnp.where(keep, 0.0, NEG).astype(F32)
            return carry

        lax.fori_loop(0, nch, to_mask, 0)

    for p_idx in range(B_HEADS // 2):
        lo = p_idx * LANES
        qs = _split_pair(q_ref[0, :, lo:lo + LANES])
        _flash_init(m_ref, l_ref, acc_ref)

        def kv(j, lo=lo):
            off = pl.multiple_of(j * TILE, TILE)
            return k_ref[0, pl.ds(off, TILE), lo:lo + LANES], v_ref[0, pl.ds(off, TILE), lo:lo + LANES]

        def far(j, carry, p_idx=p_idx, qs=qs, kv=kv):
            k_j, v_j = kv(j)
            msk = s_ref[j]
            for hd in range(2):
                s = _nt(qs[hd], k_j) + (msk + far_ref[2 * p_idx + hd])
                _flash_step(hd, s, v_j, m_ref, l_ref, acc_ref)
            return carry

        lax.fori_loop(0, jnp.maximum(i - 1, 0), far, 0)

        @pl.when(i >= 1)
        def _(p_idx=p_idx, qs=qs, kv=kv):
            k_p, v_p = kv(i - 1)
            msk = s_ref[i - 1]
            for hd in range(2):
                s = _nt(qs[hd], k_p) + bias_ref[2 * p_idx + hd, 1] + msk
                _flash_step(hd, s, v_p, m_ref, l_ref, acc_ref)

        k_own, v_own = kv(i)
        msk = s_ref[i]
        for hd in range(2):
            s = _nt(qs[hd], k_own) + bias_ref[2 * p_idx + hd, 0] + msk
            _flash_step(hd, s, v_own, m_ref, l_ref, acc_ref)
        o_ref[0, :, lo:lo + LANES] = _merge_pair(acc_ref, l_ref).astype(BF16)


def _dsa(qi, ki4, wi, qb, kb, vb, bias_near, bias_far):
    bn, t, _ = qb.shape
    nq = t // TILE
    res = lambda w: pl.BlockSpec((1, t, w), lambda b, i: (b, 0, 0), pipeline_mode=pl.Buffered(1))
    qrow = lambda w: pl.BlockSpec((1, TILE, w), lambda b, i: (b, i, 0))
    return pl.pallas_call(
        _dsa_kernel,
        grid=(bn, nq),
        in_specs=[pl.BlockSpec(memory_space=pltpu.SMEM),
                  qrow(IDX_HEADS * IDX_DIM), res(LANES), qrow(LANES), qrow(W_B), res(W_B), res(W_B),
                  pl.BlockSpec(bias_near.shape, lambda b, i: (0, 0, 0, 0), pipeline_mode=pl.Buffered(1))],
        out_specs=qrow(W_B),
        out_shape=jax.ShapeDtypeStruct((bn, t, W_B), BF16),
        scratch_shapes=[pltpu.VMEM((nq, TILE, TILE), F32),
                        pltpu.VMEM((IDX_HEADS, TILE, LANES), F32),
                        pltpu.VMEM((2, TILE, LANES), F32),
                        pltpu.VMEM((2, TILE, LANES), F32),
                        pltpu.VMEM((2, TILE, LANES), F32)],
        compiler_params=_params(("arbitrary", "arbitrary")),
        name="dsa",
    )(bias_far, qi, ki4, wi, qb, kb, vb, bias_near)


def _post_kernel(x_ref, aa_ref, ab_ref, gm_ref, wg_ref, woa_ref, wob_ref, wout_ref,
                 gx_ref, wq_ref, kvm_ref, wox_ref, gmoe_ref, wrt_ref, brt_ref,
                 x2_ref, hm_ref, rw_ref, ids_ref, cnt_ref, carry_ref):
    step = pl.program_id(0)
    xf = x_ref[...]
    h = _rms(xf, gm_ref[...]).astype(BF16)
    gates = jax.nn.sigmoid(_mm(h, wg_ref[...]))
    oa = _mm(aa_ref[...], woa_ref[...])
    ob = _mm(ab_ref[...], wob_ref[...])
    mrg = gates[:, :D_MODEL] * oa + gates[:, D_MODEL:] * ob
    x1 = xf + _mm(mrg.astype(BF16), wout_ref[...])

    hx = _rms(x1, gx_ref[...]).astype(BF16)
    q = (_mm(hx, wq_ref[...]) * (HEAD_DIM ** -0.5)).astype(BF16)
    kvm = kvm_ref[0]
    lane = lax.broadcasted_iota(I32, (TILE, LANES), 1)
    outs = []
    for p_idx in range(X_HEADS // 2):
        lo = p_idx * LANES
        qs = _split_pair(q[:, lo:lo + LANES])
        km = kvm[:, lo:lo + LANES]
        vm = kvm[:, W_X + lo:W_X + lo + LANES]
        o = []
        for hd in range(2):
            s = _nt(qs[hd], km)
            p = jnp.exp(s - jnp.max(s, axis=1, keepdims=True))
            o.append(_mm(p.astype(BF16), vm) / jnp.sum(p, axis=1, keepdims=True))
        outs.append(jnp.where(lane < HEAD_DIM, o[0], o[1]))
    xo = jnp.concatenate(outs, axis=1).astype(BF16)
    x2 = x1 + _mm(xo, wox_ref[...])
    x2_ref[...] = x2

    hm = _rms(x2, gmoe_ref[...])
    hm_ref[...] = hm
    logits = jnp.dot(hm, wrt_ref[...], preferred_element_type=F32,
                     precision=lax.Precision.HIGHEST) + brt_ref[...]
    big = LANES

    def argmax(v):
        top = jnp.max(v, axis=1, keepdims=True)
        return top, jnp.min(jnp.where(v == top, lane, big), axis=1, keepdims=True)

    is_grp = lane < R_EXP
    gtop, gsel = argmax(jnp.where(is_grp, logits, -jnp.inf))
    gw = 1.0 / jnp.sum(jnp.where(is_grp, jnp.exp(logits - gtop), 0.0), axis=1, keepdims=True)
    first = R_EXP + gsel * EXPERTS_PER_GROUP
    inside = jnp.logical_and(lane >= first, lane < first + EXPERTS_PER_GROUP)
    within = jnp.where(inside, logits, -jnp.inf)
    v0, i0 = argmax(within)
    v1, i1 = argmax(jnp.where(lane == i0, -jnp.inf, within))
    e1 = jnp.exp(v1 - v0)
    w0 = gw * (1.0 / (1.0 + e1))
    w1 = gw * (e1 / (1.0 + e1))
    rw_ref[...] = jnp.where(lane == 0, w0, jnp.where(lane == 1, w1, 0.0))

    @pl.when(step == 0)
    def _():
        carry_ref[...] = jnp.zeros(carry_ref.shape, F32)

    hit0 = lane == i0
    hit1 = lane == i1
    onehot = jnp.where(jnp.logical_or(hit0, hit1), 1.0, 0.0)
    tri = (lax.broadcasted_iota(I32, (TILE, TILE), 1) < lax.broadcasted_iota(I32, (TILE, TILE), 0))
    base = carry_ref[...] + _mm(tri.astype(BF16), onehot.astype(BF16))
    r0 = jnp.sum(jnp.where(hit0, base, 0.0), axis=1, keepdims=True)
    r1 = jnp.sum(jnp.where(hit1, base, 0.0), axis=1, keepdims=True)
    total = carry_ref[...] + jnp.sum(onehot, axis=0, keepdims=True)
    carry_ref[...] = total
    cnt_ref[...] = total.astype(I32)
    slab = jnp.where(lane == 0, (i0 - R_EXP).astype(F32),
                     jnp.where(lane == 1, (i1 - R_EXP).astype(F32),
                               jnp.where(lane == 2, r0, jnp.where(lane == 3, r1, 0.0))))
    ids_ref[0] = slab.T[:8].astype(I32)


def _post(xf, aa, ab, kvm, tiles_per_batch, weights):
    n, d = xf.shape
    nt = n // TILE
    row = lambda w: pl.BlockSpec((TILE, w), lambda i: (i, 0))
    full = lambda a: pl.BlockSpec(a.shape, lambda i: (0,) * a.ndim)
    gm, wg, woa, wob, wout, gx, wq, wox, gmoe, wrt, brt = weights
    return pl.pallas_call(
        _post_kernel,
        grid=(nt,),
        in_specs=[row(d), row(W_A), row(W_B), full(gm), full(wg), full(woa), full(wob), full(wout),
                  full(gx), full(wq),
                  pl.BlockSpec((1,) + kvm.shape[1:], lambda i: (i // tiles_per_batch, 0, 0)),
                  full(wox), full(gmoe), full(wrt), full(brt)],
        out_specs=[row(d), row(d), row(LANES),
                   pl.BlockSpec((1, 8, TILE), lambda i: (i, 0, 0)),
                   pl.BlockSpec((1, LANES), lambda i: (0, 0))],
        out_shape=[jax.ShapeDtypeStruct((n, d), F32), jax.ShapeDtypeStruct((n, d), F32),
                   jax.ShapeDtypeStruct((n, LANES), F32),
                   jax.ShapeDtypeStruct((nt, 8, TILE), I32),
                   jax.ShapeDtypeStruct((1, LANES), I32)],
        scratch_shapes=[pltpu.VMEM((1, LANES), F32)],
        compiler_params=_params(("arbitrary",)),
        name="post",
    )(xf, aa, ab, gm, wg, woa, wob, wout, gx, wq, kvm, wox, gmoe, wrt, brt)


def _segment_starts(cnt_ref, seg_ref):
    def body(e, acc):
        seg_ref[e] = acc
        c = cnt_ref[0, R_EXP + e]
        return acc + ((c + (MOE_BLOCK - 1)) >> 8 << 8)
    return lax.fori_loop(0, N_EXPERTS, body, jnp.int32(0))


def _row_copies(ids_ref, seg_ref, k, t, rows_ref, tile_ref, sem, gather):
    dest = seg_ref[ids_ref[0, k, t]] + ids_ref[0, 2 + k, t]
    if gather:
        return pltpu.make_async_copy(rows_ref.at[pl.ds(dest, 1)], tile_ref.at[k, pl.ds(t, 1)], sem)
    return pltpu.make_async_copy(tile_ref.at[pl.ds(t, 1)], rows_ref.at[pl.ds(dest, 1)], sem)


def _dispatch_kernel(ids_ref, cnt_ref, hm_ref, xr_in_ref, xr_ref, blk_ref, seg_ref, sem):
    del xr_in_ref
    step = pl.program_id(0)
    nb = blk_ref.shape[1] - 1

    @pl.when(step == 0)
    def _():
        used = _segment_starts(cnt_ref, seg_ref)

        def per_expert(e, last):
            c = cnt_ref[0, R_EXP + e]
            b0 = seg_ref[e] >> 8
            n = (c + (MOE_BLOCK - 1)) >> 8

            def fill(kk, carry):
                blk_ref[0, b0 + kk] = e
                return carry
            lax.fori_loop(0, n, fill, 0)
            return jnp.where(n > 0, e, last)
        last = lax.fori_loop(0, N_EXPERTS, per_expert, jnp.int32(0))

        def tail(b, carry):
            blk_ref[0, b] = last
            return carry
        lax.fori_loop(used >> 8, nb, tail, 0)
        blk_ref[0, nb] = used >> 8

    def issue(t, carry):
        for k in range(2):
            _row_copies(ids_ref, seg_ref, k, t, xr_ref, hm_ref, sem, gather=False).start()
        return carry
    lax.fori_loop(0, TILE, issue, 0)

    def drain(t, carry):
        for k in range(2):
            _row_copies(ids_ref, seg_ref, k, t, xr_ref, hm_ref, sem, gather=False).wait()
        return carry
    lax.fori_loop(0, TILE, drain, 0)


def _dispatch(ids, cnt, hm, n_rows):
    n, d = hm.shape
    nt = n // TILE
    nb = n_rows // MOE_BLOCK
    zeros = jnp.zeros((n_rows, d), F32)
    return pl.pallas_call(
        _dispatch_kernel,
        grid=(nt,),
        in_specs=[pl.BlockSpec((1, 8, TILE), lambda i: (i, 0, 0), memory_space=pltpu.SMEM),
                  pl.BlockSpec(memory_space=pltpu.SMEM),
                  pl.BlockSpec((TILE, d), lambda i: (i, 0)),
                  pl.BlockSpec(memory_space=pl.ANY)],
        out_specs=[pl.BlockSpec(memory_space=pl.ANY),
                   pl.BlockSpec(memory_space=pltpu.SMEM)],
        out_shape=[jax.ShapeDtypeStruct((n_rows, d), F32),
                   jax.ShapeDtypeStruct((1, nb + 1), I32)],
        scratch_shapes=[pltpu.SMEM((N_EXPERTS,), I32), pltpu.SemaphoreType.DMA(())],
        input_output_aliases={3: 0},
        compiler_params=_params(("arbitrary",)),
        name="dispatch",
    )(ids, cnt, hm, zeros)


def _experts_kernel(blk_ref, x_ref, w1_ref, w3_ref, w2_ref, y_ref, w1b, w3b, w2b):
    b = pl.program_id(0)
    nb = pl.num_programs(0)
    e = blk_ref[0, b]
    prev = blk_ref[0, jnp.maximum(b - 1, 0)]

    @pl.when(jnp.logical_or(b == 0, e != prev))
    def _():
        w1b[...] = w1_ref[0].astype(BF16)
        w3b[...] = w3_ref[0].astype(BF16)
        w2b[...] = w2_ref[0].astype(BF16)

    used = blk_ref[0, nb]

    @pl.when(b < used)
    def _():
        xb = x_ref[...].astype(BF16)
        a = _mm(xb, w1b[...])
        g = _mm(xb, w3b[...])
        y_ref[...] = _mm((a * jax.nn.sigmoid(a) * g).astype(BF16), w2b[...])

    @pl.when(b >= used)
    def _():
        y_ref[...] = jnp.zeros(y_ref.shape, F32)


def _experts(blk, xr, w1, w3, w2):
    n_rows, d = xr.shape
    nb = n_rows // MOE_BLOCK
    wspec = lambda s: pl.BlockSpec((1,) + s, lambda b, blk: (blk[0, b], 0, 0))
    return pl.pallas_call(
        _experts_kernel,
        grid_spec=pltpu.PrefetchScalarGridSpec(
            num_scalar_prefetch=1,
            grid=(nb,),
            in_specs=[pl.BlockSpec((MOE_BLOCK, d), lambda b, blk: (b, 0)),
                      wspec((d, D_EXPERT)), wspec((d, D_EXPERT)), wspec((D_EXPERT, d))],
            out_specs=pl.BlockSpec((MOE_BLOCK, d), lambda b, blk: (b, 0)),
            scratch_shapes=[pltpu.VMEM((d, D_EXPERT), BF16), pltpu.VMEM((d, D_EXPERT), BF16),
                            pltpu.VMEM((D_EXPERT, d), BF16)]),
        out_shape=jax.ShapeDtypeStruct((n_rows, d), F32),
        compiler_params=_params(("arbitrary",)),
        name="experts",
    )(blk, xr, w1, w3, w2)


def _combine_kernel(ids_ref, cnt_ref, x2_ref, rw_ref, g_ref, yr_ref, o_ref, y_ref, seg_ref, sem):
    step = pl.program_id(0)

    @pl.when(step == 0)
    def _():
        _segment_starts(cnt_ref, seg_ref)

    def issue(t, carry):
        for k in range(2):
            _row_copies(ids_ref, seg_ref, k, t, yr_ref, y_ref, sem, gather=True).start()
        return carry
    lax.fori_loop(0, TILE, issue, 0)

    def drain(t, carry):
        for k in range(2):
            _row_copies(ids_ref, seg_ref, k, t, yr_ref, y_ref, sem, gather=True).wait()
        return carry
    lax.fori_loop(0, TILE, drain, 0)

    rw = rw_ref[...]
    x3 = x2_ref[...] + (y_ref[0] * rw[:, 0:1] + y_ref[1] * rw[:, 1:2])
    o_ref[...] = _rms(x3, g_ref[...])


def _combine(ids, cnt, x2, rw, g, yr):
    n, d = x2.shape
    nt = n // TILE
    row = lambda w: pl.BlockSpec((TILE, w), lambda i: (i, 0))
    return pl.pallas_call(
        _combine_kernel,
        grid=(nt,),
        in_specs=[pl.BlockSpec((1, 8, TILE), lambda i: (i, 0, 0), memory_space=pltpu.SMEM),
                  pl.BlockSpec(memory_space=pltpu.SMEM),
                  row(d), row(LANES), pl.BlockSpec((1, d), lambda i: (0, 0)),
                  pl.BlockSpec(memory_space=pl.ANY)],
        out_specs=row(d),
        out_shape=jax.ShapeDtypeStruct((n, d), F32),
        scratch_shapes=[pltpu.VMEM((2, TILE, d), F32), pltpu.SMEM((N_EXPERTS,), I32),
                        pltpu.SemaphoreType.DMA(())],
        compiler_params=_params(("arbitrary",)),
        name="combine",
    )(ids, cnt, x2, rw, g, yr)


def _t5_bucket(dist):
    n = jnp.maximum(dist, 0)
    max_exact = REL_BUCKETS // 2
    nf = jnp.maximum(n, 1).astype(F32)
    large = max_exact + (jnp.log(nf / max_exact) / math.log(REL_MAX_DIST / max_exact)
                         * (REL_BUCKETS - max_exact)).astype(I32)
    large = jnp.minimum(large, REL_BUCKETS - 1)
    return jnp.where(n < max_exact, n, large)


def _bias_tables(tab):
    r = jnp.arange(TILE)[:, None]
    c = jnp.arange(TILE)[None, :]
    own = jnp.where(r >= c, tab[:, _t5_bucket(r - c)], NEG)
    prev = tab[:, _t5_bucket(r - c + TILE)]
    return jnp.stack([own, prev], axis=1).astype(F32), tab[:, REL_BUCKETS - 1].astype(F32)


def _pad_cols(w, width):
    return jnp.pad(w, ((0, 0), (0, width - w.shape[1])))


def kernel(x, mem, rel_bias, final_norm, norm_mix, w_in, ckv_norm, w_uk, w_uv, w_oa, w_ob,
           w_out, norm_x, mem_norm, wq_x, wk_x, wv_x, wo_x, norm_moe, w_group, b_group,
           w_router, b_router, w1, w3, w2):
    bn, t, d = x.shape
    n = bn * t
    assert t % TILE == 0 and norm_mix.shape[0] == 1
    near_a, far_a = _bias_tables(rel_bias[:, :A_HEADS].T)
    near_b, far_b = _bias_tables(rel_bias[:, A_HEADS:].T)
    row = lambda v: v.reshape(1, -1).astype(F32)

    wi = w_in[0]
    wp = jnp.concatenate(
        [wi[:, OFF_AQ:OFF_IK], jnp.tile(wi[:, OFF_IK:OFF_IW], (1, LANES // IDX_DIM)),
         _pad_cols(wi[:, OFF_IW:OFF_GA], LANES)], axis=1).astype(BF16)
    wkv = jnp.concatenate([w_uk[0], w_uv[0]], axis=1).astype(BF16)
    xf = x.reshape(n, d)
    qa, ka, va, qb, kb, vb, qi, ki4, widx, kmean = _proj_in(xf, row(norm_mix[0]), wp, row(ckv_norm[0]), wkv)

    b3 = lambda a: a.reshape(bn, t, a.shape[-1])
    attn_a = _moba(b3(qa), b3(ka), b3(va), kmean.reshape(bn, t // TILE, W_A), near_a, far_a)
    attn_b = _dsa(b3(qi), b3(ki4), b3(widx), b3(qb), b3(kb), b3(vb), near_b, far_b)

    kvm = _mem_kv(mem, row(mem_norm[0]), jnp.concatenate([wk_x[0], wv_x[0]], axis=1).astype(BF16))
    wrt = _pad_cols(jnp.concatenate([w_group[0], w_router[0]], axis=1), LANES).astype(F32)
    brt = _pad_cols(jnp.concatenate([b_group[0], b_router[0]]).reshape(1, -1), LANES).astype(F32)
    weights = (row(norm_mix[0]), wi[:, OFF_GA:IN_COLS].astype(BF16), w_oa[0].astype(BF16),
               w_ob[0].astype(BF16), w_out[0].astype(BF16), row(norm_x[0]), wq_x[0].astype(BF16),
               wo_x[0].astype(BF16), row(norm_moe[0]), wrt, brt)
    x2, hm, rw, ids, cnt = _post(xf, attn_a.reshape(n, W_A), attn_b.reshape(n, W_B), kvm, t // TILE, weights)

    n_rows = 2 * n + N_EXPERTS * MOE_BLOCK
    xr, blk = _dispatch(ids, cnt, hm, n_rows)
    yr = _experts(blk, xr, w1[0], w3[0], w2[0])
    out = _combine(ids, cnt, x2, rw, row(final_norm), yr)
    return out.reshape(bn, t, d)
```

```python
import functools
import math

import jax
import jax.numpy as jnp
from jax import lax
from jax.experimental import pallas as pl
from jax.experimental.pallas import tpu as pltpu

D_MODEL = 1024
HEAD_DIM = 64
A_HEADS = 8
B_HEADS = 8
W_A = A_HEADS * HEAD_DIM
W_B = B_HEADS * HEAD_DIM
MOBA_BLOCK = 256
MOBA_TOPK = 3
DSA_KV_RANK = 256
IDX_HEADS = 8
IDX_DIM = 32
DSA_TOPK = 256
REL_BUCKETS = 32
REL_MAX_DIST = 128
X_HEADS = 4
W_X = X_HEADS * HEAD_DIM
N_GROUPS = 4
EXPERTS_PER_GROUP = 8
N_EXPERTS = N_GROUPS * EXPERTS_PER_GROUP
D_EXPERT = 512
MOE_BLOCK = 256
EPS = 1e-6
NEG = -1e30

OFF_AQ = 0
OFF_AK = OFF_AQ + W_A
OFF_AV = OFF_AK + W_A
OFF_BQ = OFF_AV + W_A
OFF_CKV = OFF_BQ + W_B
OFF_IQ = OFF_CKV + DSA_KV_RANK
OFF_IK = OFF_IQ + IDX_HEADS * IDX_DIM
OFF_IW = OFF_IK + IDX_DIM
OFF_GA = OFF_IW + IDX_HEADS
OFF_GB = OFF_GA + D_MODEL
IN_COLS = OFF_GB + D_MODEL

LANES = 128
TILE = 256
VMEM_LIMIT = 56 * 1024 * 1024
BF16 = jnp.bfloat16
F32 = jnp.float32
I32 = jnp.int32

P_QA, P_KA, P_VA, P_QB = 0, 512, 1024, 1536
P_CKV, P_IQ, P_IK4, P_IW = 2048, 2304, 2560, 2688
P_COLS = 2816

R_GRP = 0
R_EXP = N_GROUPS


def _nt(a, b):
    return lax.dot_general(a, b, (((1,), (1,)), ((), ())), preferred_element_type=F32)


def _mm(a, b):
    return jnp.dot(a, b, preferred_element_type=F32)


def _rms(xf, g):
    return xf * lax.rsqrt(jnp.mean(xf * xf, axis=-1, keepdims=True) + EPS) * g


def _params(sem, vmem=VMEM_LIMIT):
    return pltpu.CompilerParams(dimension_semantics=sem, vmem_limit_bytes=vmem)


def _mem_kv_kernel(mem_ref, g_ref, w_ref, o_ref):
    m = _rms(mem_ref[0], g_ref[...]).astype(BF16)
    o_ref[0] = _mm(m, w_ref[...]).astype(BF16)


def _mem_kv(mem, g, wkv):
    bn, ml, d = mem.shape
    return pl.pallas_call(
        _mem_kv_kernel,
        grid=(bn,),
        in_specs=[pl.BlockSpec((1, ml, d), lambda b: (b, 0, 0)),
                  pl.BlockSpec((1, d), lambda b: (0, 0)),
                  pl.BlockSpec((d, 2 * W_X), lambda b: (0, 0))],
        out_specs=pl.BlockSpec((1, ml, 2 * W_X), lambda b: (b, 0, 0)),
        out_shape=jax.ShapeDtypeStruct((bn, ml, 2 * W_X), BF16),
        compiler_params=_params(("arbitrary",)),
        name="mem_kv",
    )(mem, g, wkv)


def _head_slabs(zp, extra):
    lane = lax.broadcasted_iota(I32, zp.shape, 1)
    low = lane < HEAD_DIM
    return jnp.where(low, zp, extra), jnp.where(low, pltpu.roll(zp, HEAD_DIM, 1), extra)


def _proj_in_kernel(blocks_per_seq, x_ref, g_ref, w_ref, cg_ref, wkv_ref,
                    qa_ref, ka_ref, va_ref, qb_ref, kb_ref, vb_ref, qi_ref, ki_ref, wi_ref, km_ref):
    h = _rms(x_ref[...], g_ref[...]).astype(BF16)
    z = _mm(h, w_ref[...])
    scale = HEAD_DIM ** -0.5
    zk = z[:, P_KA:P_VA]
    km_ref[0] = jnp.mean(zk, axis=0, keepdims=True)
    lane = lax.broadcasted_iota(I32, (TILE, LANES), 1)
    blk = pl.program_id(0) % blocks_per_seq
    onehot = jnp.where(lane == HEAD_DIM + blk, 1.0, 0.0)
    zero = jnp.zeros((TILE, LANES), F32)
    for p_idx in range(A_HEADS // 2):
        lo = p_idx * LANES
        qe, qo = _head_slabs(z[:, P_QA + lo:P_QA + lo + LANES] * scale, zero)
        ke, ko = _head_slabs(zk[:, lo:lo + LANES], onehot)
        qa_ref[:, 2 * lo:2 * lo + 2 * LANES] = jnp.concatenate([qe, qo], axis=1).astype(BF16)
        ka_ref[:, 2 * lo:2 * lo + 2 * LANES] = jnp.concatenate([ke, ko], axis=1).astype(BF16)
    va_ref[...] = z[:, P_VA:P_QB].astype(BF16)
    qb_ref[...] = (z[:, P_QB:P_CKV] * scale).astype(BF16)
    ckv = _rms(z[:, P_CKV:P_IQ], cg_ref[...]).astype(BF16)
    kv = _mm(ckv, wkv_ref[...])
    kb_ref[...] = kv[:, :W_B].astype(BF16)
    vb_ref[...] = kv[:, W_B:].astype(BF16)
    qi_ref[...] = z[:, P_IQ:P_IK4].astype(BF16)
    ki_ref[...] = z[:, P_IK4:P_IW].astype(BF16)
    wi_ref[...] = z[:, P_IW:P_COLS] * ((IDX_HEADS ** -0.5) * (IDX_DIM ** -0.5))


def _proj_in(xf, g, wp, cg, wkv, blocks_per_seq):
    n, d = xf.shape
    nt = n // TILE
    row = lambda w: pl.BlockSpec((TILE, w), lambda i: (i, 0))
    full = lambda a: pl.BlockSpec(a.shape, lambda i: (0,) * a.ndim)
    outs = [(2 * W_A, BF16), (2 * W_A, BF16), (W_A, BF16), (W_B, BF16), (W_B, BF16), (W_B, BF16),
            (IDX_HEADS * IDX_DIM, BF16), (LANES, BF16), (LANES, F32)]
    return pl.pallas_call(
        functools.partial(_proj_in_kernel, blocks_per_seq),
        grid=(nt,),
        in_specs=[row(d), full(g), full(wp), full(cg), full(wkv)],
        out_specs=[row(w) for w, _ in outs] + [pl.BlockSpec((1, 1, W_A), lambda i: (i, 0, 0))],
        out_shape=[jax.ShapeDtypeStruct((n, w), t) for w, t in outs]
        + [jax.ShapeDtypeStruct((nt, 1, W_A), F32)],
        compiler_params=_params(("arbitrary",)),
        name="proj_in",
    )(xf, g, wp, cg, wkv)


def _sm_init(mx_ref, ls_ref, acc_ref):
    mx_ref[...] = jnp.full(mx_ref.shape, NEG, F32)
    ls_ref[...] = jnp.zeros(ls_ref.shape, F32)
    acc_ref[...] = jnp.zeros(acc_ref.shape, F32)


def _sm_logits(hd, slot, s, sc_ref, mx_ref):
    sc_ref[hd, slot] = s
    mx_ref[hd] = jnp.maximum(mx_ref[hd], jnp.maximum(s[:, :LANES], s[:, LANES:]))


def _sm_rowmax(mx_ref):
    for hd in range(2):
        mx_ref[hd] = jnp.broadcast_to(jnp.max(mx_ref[hd], axis=1, keepdims=True), (TILE, LANES))


def _sm_values(hd, slots, v, sc_ref, mx_ref, ls_ref, acc_ref):
    m = mx_ref[hd]
    m2 = jnp.concatenate([m, m], axis=1)
    ps = [jnp.exp(sc_ref[hd, sl] - m2) for sl in slots]
    tot = ps[0][:, :LANES] + ps[0][:, LANES:]
    for p in ps[1:]:
        tot = tot + p[:, :LANES] + p[:, LANES:]
    ls_ref[hd] = ls_ref[hd] + tot
    pb = ps[0] if len(ps) == 1 else jnp.concatenate(ps, axis=1)
    acc_ref[hd] = acc_ref[hd] + _mm(pb.astype(BF16), v)


def _sm_output(ls_ref, acc_ref):
    lane = lax.broadcasted_iota(I32, (TILE, LANES), 1)
    out = [acc_ref[hd] / jnp.sum(ls_ref[hd], axis=1, keepdims=True) for hd in range(2)]
    return jnp.where(lane < HEAD_DIM, out[0], out[1])


def _split_pair(qp):
    lane = lax.broadcasted_iota(I32, qp.shape, 1)
    zero = jnp.zeros_like(qp)
    return jnp.where(lane < HEAD_DIM, qp, zero), jnp.where(lane >= HEAD_DIM, qp, zero)


def _moba_kernel(q_ref, k_ref, v_ref, km_ref, bias_ref, o_ref, sc_ref, mx_ref, ls_ref, acc_ref):
    i = pl.program_id(2)
    nq = pl.num_programs(2)
    lane = lax.broadcasted_iota(I32, (TILE, LANES), 1)
    blk = lane - HEAD_DIM
    in_blk = jnp.logical_and(blk >= 0, blk < nq)
    past = jnp.logical_and(in_blk, blk < i)
    slot_prev, slot_own = nq, nq + 1

    qs, q_far, q_prev = [], [], []
    for hd in range(2):
        q = q_ref[0, :, hd * LANES:(hd + 1) * LANES]
        g = jnp.where(past, _nt(q, km_ref[0, hd]), -jnp.inf)
        sel = jnp.zeros((TILE, LANES), jnp.bool_)
        for _ in range(MOBA_TOPK):
            top = jnp.max(g, axis=1, keepdims=True)
            hit = jnp.logical_and(g == top, in_blk)
            first = jnp.min(jnp.where(hit, blk, nq), axis=1, keepdims=True)
            pick = blk == first
            sel = jnp.logical_or(sel, jnp.logical_and(pick, past))
            g = jnp.where(pick, -jnp.inf, g)
        addm = jnp.where(sel, 0.0, NEG)
        far_m = jnp.where(jnp.logical_and(in_blk, blk < i - 1), addm, jnp.where(in_blk, NEG, 0.0))
        prev_m = jnp.where(jnp.logical_and(in_blk, blk == i - 1), addm, 0.0)
        qs.append(q)
        q_far.append(q + far_m.astype(BF16))
        q_prev.append(q + prev_m.astype(BF16))

    def keys(j, hd):
        off = pl.multiple_of(j * TILE, TILE)
        return k_ref[0, pl.ds(off, TILE), hd * LANES:(hd + 1) * LANES]

    n_grp = jnp.maximum(i, 1) // 2
    _sm_init(mx_ref, ls_ref, acc_ref)

    def far_logits(g, carry):
        for u in range(2):
            j = 2 * g + u
            for hd in range(2):
                _sm_logits(hd, j, _nt(q_far[hd], keys(j, hd)), sc_ref, mx_ref)
        return carry

    lax.fori_loop(0, n_grp, far_logits, 0)

    @pl.when(i >= 1)
    def _():
        for hd in range(2):
            _sm_logits(hd, slot_prev, _nt(q_prev[hd], keys(i - 1, hd)) + bias_ref[hd, 1], sc_ref, mx_ref)

    for hd in range(2):
        _sm_logits(hd, slot_own, _nt(qs[hd], keys(i, hd)) + bias_ref[hd, 0], sc_ref, mx_ref)
    _sm_rowmax(mx_ref)

    def vals(j, n):
        off = pl.multiple_of(j * TILE, TILE)
        return v_ref[0, pl.ds(off, n * TILE), :]

    def far_values(g, carry):
        v2 = vals(2 * g, 2)
        for hd in range(2):
            _sm_values(hd, [2 * g, 2 * g + 1], v2, sc_ref, mx_ref, ls_ref, acc_ref)
        return carry

    lax.fori_loop(0, n_grp, far_values, 0)

    @pl.when(i >= 1)
    def _():
        for hd in range(2):
            _sm_values(hd, [slot_prev], vals(i - 1, 1), sc_ref, mx_ref, ls_ref, acc_ref)

    for hd in range(2):
        _sm_values(hd, [slot_own], vals(i, 1), sc_ref, mx_ref, ls_ref, acc_ref)
    o_ref[0] = _sm_output(ls_ref, acc_ref).astype(BF16)


def _moba(qa, ka, va, kmp, bias_near):
    bn, t, _ = va.shape
    nq = t // TILE
    npair = A_HEADS // 2
    return pl.pallas_call(
        _moba_kernel,
        grid=(bn, npair, nq),
        in_specs=[pl.BlockSpec((1, TILE, 2 * LANES), lambda b, p, i: (b, i, p)),
                  pl.BlockSpec((1, t, 2 * LANES), lambda b, p, i: (b, 0, p)),
                  pl.BlockSpec((1, t, LANES), lambda b, p, i: (b, 0, p)),
                  pl.BlockSpec((1, 2, LANES, LANES), lambda b, p, i: (b, p, 0, 0)),
                  pl.BlockSpec((2, 2, TILE, TILE), lambda b, p, i: (p, 0, 0, 0))],
        out_specs=pl.BlockSpec((1, TILE, LANES), lambda b, p, i: (b, i, p)),
        out_shape=jax.ShapeDtypeStruct((bn, t, W_A), BF16),
        scratch_shapes=[pltpu.VMEM((2, nq + 2, TILE, TILE), F32),
                        pltpu.VMEM((2, TILE, LANES), F32),
                        pltpu.VMEM((2, TILE, LANES), F32),
                        pltpu.VMEM((2, TILE, LANES), F32)],
        compiler_params=_params(("arbitrary", "arbitrary", "arbitrary")),
        name="moba",
    )(qa, ka, va, kmp, bias_near)


ROWS = 128


def _dsa_select(i, qi_ref, ki_ref, wi_ref, s_ref, wb_ref):
    nch = i + 1
    row = lax.broadcasted_iota(I32, (TILE, TILE), 0)
    col = lax.broadcasted_iota(I32, (TILE, TILE), 1)
    lane = lax.broadcasted_iota(I32, (TILE, LANES), 1)

    wi = wi_ref[0]
    for h in range(IDX_HEADS):
        wb_ref[h] = jnp.broadcast_to(wi[:, h:h + 1], (TILE, LANES))
    qi = qi_ref[0]
    per_group = LANES // IDX_DIM
    qh = []
    for h in range(IDX_HEADS):
        g, r = divmod(h, per_group)
        qg = qi[:, g * LANES:(g + 1) * LANES]
        keep = jnp.logical_and(lane >= r * IDX_DIM, lane < (r + 1) * IDX_DIM)
        qh.append(jnp.where(keep, qg, jnp.zeros_like(qg)))

    def score(c, carry):
        off = pl.multiple_of(c * TILE, TILE)
        kc = ki_ref[0, pl.ds(off, TILE), :]
        acc = jnp.zeros((TILE, TILE), F32)
        for h in range(IDX_HEADS):
            w = wb_ref[h]
            acc = acc + jnp.maximum(_nt(qh[h], kc), 0.0) * jnp.concatenate([w, w], axis=1)
        s_ref[c] = jnp.where(c * TILE + col <= i * TILE + row, acc, -jnp.inf)
        return carry

    lax.fori_loop(0, nch, score, 0)

    def fold(r0, init, step):
        return lax.fori_loop(0, nch, lambda c, acc: step(acc, s_ref[c, pl.ds(r0, ROWS), :], c), init)

    def count(r0, pred):
        def step(acc, blk, c):
            hit = pred(blk, c)
            return acc + jnp.where(hit[:, :LANES], 1.0, 0.0) + jnp.where(hit[:, LANES:], 1.0, 0.0)
        return jnp.sum(fold(r0, jnp.zeros((ROWS, LANES), F32), step), axis=1, keepdims=True)

    def wide(x):
        return jnp.broadcast_to(x, (ROWS, TILE))

    ccol = lax.broadcasted_iota(I32, (ROWS, TILE), 1)
    cut_bits = max(1, (s_ref.shape[0] * TILE - 1).bit_length())
    for rg in range(TILE // ROWS):
        r0 = rg * ROWS
        hi = jnp.max(fold(r0, jnp.full((ROWS, LANES), -jnp.inf, F32),
                          lambda acc, blk, c: jnp.maximum(acc, jnp.maximum(blk[:, :LANES], blk[:, LANES:]))),
                     axis=1, keepdims=True)

        def low_step(acc, blk, c):
            blk = jnp.where(blk == -jnp.inf, jnp.inf, blk)
            return jnp.minimum(acc, jnp.minimum(blk[:, :LANES], blk[:, LANES:]))
        lo = jnp.min(fold(r0, jnp.full((ROWS, LANES), jnp.inf, F32), low_step), axis=1, keepdims=True)

        top_ties = count(r0, lambda blk, c: blk >= wide(hi)) >= DSA_TOPK

        def search_more(state):
            return jnp.logical_and(jnp.min(state[3]) == 0, state[4] < 400)

        def search(state):
            lo, hi, thr, done, it = state
            mid = 0.5 * lo + 0.5 * hi
            closed = jnp.logical_or(mid <= lo, mid >= hi)
            n = count(r0, lambda blk, c: blk >= wide(mid))
            found = jnp.where(closed, lo, mid)
            stop = jnp.logical_or(closed, n == DSA_TOPK)
            live = done == 0
            thr = jnp.where(jnp.logical_and(live, stop), found, thr)
            go = jnp.logical_and(live, jnp.logical_not(stop))
            lo = jnp.where(jnp.logical_and(go, n > DSA_TOPK), mid, lo)
            hi = jnp.where(jnp.logical_and(go, n < DSA_TOPK), mid, hi)
            return lo, hi, thr, jnp.where(stop, 1, done), it + 1

        state = (lo, hi, hi, top_ties.astype(I32), jnp.int32(0))
        thr = wide(lax.while_loop(search_more, search, state)[2])
        n_gt = count(r0, lambda blk, c: blk > thr)
        n_ge = count(r0, lambda blk, c: blk >= thr)
        need = DSA_TOPK - n_gt

        def cut_step(b, cut):
            cand = cut | (jnp.int32(1) << (cut_bits - 1 - b))
            cw = wide(cand)
            n = count(r0, lambda blk, c: jnp.logical_and(blk == thr, c * TILE + ccol < cw))
            return jnp.where(n < need, cand, cut)

        cut = lax.cond(jnp.max(n_ge) > DSA_TOPK,
                       lambda: lax.fori_loop(0, cut_bits, cut_step, jnp.zeros((ROWS, 1), I32)),
                       lambda: jnp.full((ROWS, 1), 2 ** 30, I32))
        cut = wide(jnp.where(n_ge > DSA_TOPK, cut, 2 ** 30))
        qpos = i * TILE + r0 + lax.broadcasted_iota(I32, (ROWS, TILE), 0)

        def to_mask(c, carry):
            blk = s_ref[c, pl.ds(r0, ROWS), :]
            kpos = c * TILE + ccol
            keep = jnp.logical_or(blk > thr, jnp.logical_and(blk == thr, kpos <= cut))
            keep = jnp.logical_and(keep, kpos <= qpos)
            s_ref[c, pl.ds(r0, ROWS), :] = jnp.where(keep, 0.0, NEG).astype(F32)
            return carry

        lax.fori_loop(0, nch, to_mask, 0)

def _dsa_kernel(qi_ref, ki_ref, wi_ref, q_ref, k_ref, v_ref, bias_ref, o_ref,
                s_ref, wb_ref, sc_ref, mx_ref, ls_ref, acc_ref):
    i = pl.program_id(1)
    pair = pl.program_id(2)

    @pl.when(jnp.logical_and(pair == 0, i == 0))
    def _():
        row = lax.broadcasted_iota(I32, (TILE, TILE), 0)
        col = lax.broadcasted_iota(I32, (TILE, TILE), 1)
        s_ref[0] = jnp.where(col <= row, 0.0, NEG)

    @pl.when(jnp.logical_and(pair == 0, i >= 1))
    def _():
        _dsa_select(i, qi_ref, ki_ref, wi_ref, s_ref, wb_ref)

    qs = _split_pair(q_ref[0])
    n_far = jnp.maximum(i - 1, 0)
    n_grp = n_far // 2
    odd = n_far % 2 == 1
    _sm_init(mx_ref, ls_ref, acc_ref)

    def rows(ref, j, n):
        off = pl.multiple_of(j * TILE, TILE)
        return ref[0, pl.ds(off, n * TILE), :]

    def logits(j, near=None):
        k_j = rows(k_ref, j, 1)
        msk = s_ref[j]
        for hd in range(2):
            s = _nt(qs[hd], k_j) + msk
            if near is not None:
                s = s + bias_ref[hd, near]
            _sm_logits(hd, j, s, sc_ref, mx_ref)

    def far_logits(g, carry):
        logits(2 * g)
        logits(2 * g + 1)
        return carry

    lax.fori_loop(0, n_grp, far_logits, 0)
    pl.when(odd)(lambda: logits(n_far - 1))
    pl.when(i >= 1)(lambda: logits(i - 1, near=1))
    logits(i, near=0)
    _sm_rowmax(mx_ref)

    def values(j, n):
        v = rows(v_ref, j, n)
        for hd in range(2):
            _sm_values(hd, [j + u for u in range(n)], v, sc_ref, mx_ref, ls_ref, acc_ref)

    def far_values(g, carry):
        values(2 * g, 2)
        return carry

    lax.fori_loop(0, n_grp, far_values, 0)
    pl.when(odd)(lambda: values(n_far - 1, 1))
    pl.when(i >= 1)(lambda: values(i - 1, 1))
    values(i, 1)
    o_ref[0] = _sm_output(ls_ref, acc_ref).astype(BF16)


def _dsa(qi, ki4, wi, qb, kb, vb, bias_near):
    bn, t, _ = qb.shape
    nq = t // TILE
    npair = B_HEADS // 2
    qrow = lambda w: pl.BlockSpec((1, TILE, w), lambda b, i, p: (b, i, 0))
    return pl.pallas_call(
        _dsa_kernel,
        grid=(bn, nq, npair),
        in_specs=[qrow(IDX_HEADS * IDX_DIM),
                  pl.BlockSpec((1, t, LANES), lambda b, i, p: (b, 0, 0), pipeline_mode=pl.Buffered(1)),
                  qrow(LANES),
                  pl.BlockSpec((1, TILE, LANES), lambda b, i, p: (b, i, p)),
                  pl.BlockSpec((1, t, LANES), lambda b, i, p: (b, 0, p)),
                  pl.BlockSpec((1, t, LANES), lambda b, i, p: (b, 0, p)),
                  pl.BlockSpec((2, 2, TILE, TILE), lambda b, i, p: (p, 0, 0, 0))],
        out_specs=pl.BlockSpec((1, TILE, LANES), lambda b, i, p: (b, i, p)),
        out_shape=jax.ShapeDtypeStruct((bn, t, W_B), BF16),
        scratch_shapes=[pltpu.VMEM((nq, TILE, TILE), F32),
                        pltpu.VMEM((IDX_HEADS, TILE, LANES), F32),
                        pltpu.VMEM((2, nq, TILE, TILE), F32),
                        pltpu.VMEM((2, TILE, LANES), F32),
                        pltpu.VMEM((2, TILE, LANES), F32),
                        pltpu.VMEM((2, TILE, LANES), F32)],
        compiler_params=_params(("arbitrary", "arbitrary", "arbitrary")),
        name="dsa",
    )(qi, ki4, wi, qb, kb, vb, bias_near)


def _post_kernel(x_ref, aa_ref, ab_ref, gm_ref, wg_ref, woa_ref, wob_ref, wout_ref,
                 gx_ref, wq_ref, kvm_ref, wox_ref, gmoe_ref, wrt_ref, brt_ref,
                 x2_ref, hm_ref, rw_ref, ids_ref, cnt_ref, carry_ref):
    step = pl.program_id(0)
    xf = x_ref[...]
    h = _rms(xf, gm_ref[...]).astype(BF16)
    gates = jax.nn.sigmoid(_mm(h, wg_ref[...]))
    oa = _mm(aa_ref[...], woa_ref[...])
    ob = _mm(ab_ref[...], wob_ref[...])
    mrg = gates[:, :D_MODEL] * oa + gates[:, D_MODEL:] * ob
    x1 = xf + _mm(mrg.astype(BF16), wout_ref[...])

    hx = _rms(x1, gx_ref[...]).astype(BF16)
    q = (_mm(hx, wq_ref[...]) * (HEAD_DIM ** -0.5)).astype(BF16)
    kvm = kvm_ref[0]
    lane = lax.broadcasted_iota(I32, (TILE, LANES), 1)
    outs = []
    for p_idx in range(X_HEADS // 2):
        lo = p_idx * LANES
        qs = _split_pair(q[:, lo:lo + LANES])
        km = kvm[:, lo:lo + LANES]
        vm = kvm[:, W_X + lo:W_X + lo + LANES]
        o = []
        for hd in range(2):
            s = _nt(qs[hd], km)
            p = jnp.exp(s - jnp.max(s, axis=1, keepdims=True))
            o.append(_mm(p.astype(BF16), vm) / jnp.sum(p, axis=1, keepdims=True))
        outs.append(jnp.where(lane < HEAD_DIM, o[0], o[1]))
    xo = jnp.concatenate(outs, axis=1).astype(BF16)
    x2 = x1 + _mm(xo, wox_ref[...])
    x2_ref[...] = x2

    hm = _rms(x2, gmoe_ref[...])
    hm_ref[...] = hm
    logits = jnp.dot(hm, wrt_ref[...], preferred_element_type=F32,
                     precision=lax.Precision.HIGHEST) + brt_ref[...]
    big = LANES

    def argmax(v):
        top = jnp.max(v, axis=1, keepdims=True)
        return top, jnp.min(jnp.where(v == top, lane, big), axis=1, keepdims=True)

    is_grp = lane < R_EXP
    gtop, gsel = argmax(jnp.where(is_grp, logits, -jnp.inf))
    gw = 1.0 / jnp.sum(jnp.where(is_grp, jnp.exp(logits - gtop), 0.0), axis=1, keepdims=True)
    first = R_EXP + gsel * EXPERTS_PER_GROUP
    inside = jnp.logical_and(lane >= first, lane < first + EXPERTS_PER_GROUP)
    within = jnp.where(inside, logits, -jnp.inf)
    v0, i0 = argmax(within)
    v1, i1 = argmax(jnp.where(lane == i0, -jnp.inf, within))
    e1 = jnp.exp(v1 - v0)
    w0 = gw * (1.0 / (1.0 + e1))
    w1 = gw * (e1 / (1.0 + e1))
    rw_ref[...] = jnp.where(lane == 0, w0, jnp.where(lane == 1, w1, 0.0))

    @pl.when(step == 0)
    def _():
        carry_ref[...] = jnp.zeros(carry_ref.shape, F32)

    hit0 = lane == i0
    hit1 = lane == i1
    onehot = jnp.where(jnp.logical_or(hit0, hit1), 1.0, 0.0)
    tri = (lax.broadcasted_iota(I32, (TILE, TILE), 1) < lax.broadcasted_iota(I32, (TILE, TILE), 0))
    base = carry_ref[...] + _mm(tri.astype(BF16), onehot.astype(BF16))
    r0 = jnp.sum(jnp.where(hit0, base, 0.0), axis=1, keepdims=True)
    r1 = jnp.sum(jnp.where(hit1, base, 0.0), axis=1, keepdims=True)
    total = carry_ref[...] + jnp.sum(onehot, axis=0, keepdims=True)
    carry_ref[...] = total
    cnt_ref[...] = total.astype(I32)
    slab = jnp.where(lane == 0, (i0 - R_EXP).astype(F32),
                     jnp.where(lane == 1, (i1 - R_EXP).astype(F32),
                               jnp.where(lane == 2, r0, jnp.where(lane == 3, r1, 0.0))))
    ids_ref[0] = slab.T[:8].astype(I32)


def _post(xf, aa, ab, kvm, tiles_per_batch, weights):
    n, d = xf.shape
    nt = n // TILE
    row = lambda w: pl.BlockSpec((TILE, w), lambda i: (i, 0))
    full = lambda a: pl.BlockSpec(a.shape, lambda i: (0,) * a.ndim)
    gm, wg, woa, wob, wout, gx, wq, wox, gmoe, wrt, brt = weights
    return pl.pallas_call(
        _post_kernel,
        grid=(nt,),
        in_specs=[row(d), row(W_A), row(W_B), full(gm), full(wg), full(woa), full(wob), full(wout),
                  full(gx), full(wq),
                  pl.BlockSpec((1,) + kvm.shape[1:], lambda i: (i // tiles_per_batch, 0, 0)),
                  full(wox), full(gmoe), full(wrt), full(brt)],
        out_specs=[row(d), row(d), row(LANES),
                   pl.BlockSpec((1, 8, TILE), lambda i: (i, 0, 0)),
                   pl.BlockSpec((1, LANES), lambda i: (0, 0))],
        out_shape=[jax.ShapeDtypeStruct((n, d), F32), jax.ShapeDtypeStruct((n, d), F32),
                   jax.ShapeDtypeStruct((n, LANES), F32),
                   jax.ShapeDtypeStruct((nt, 8, TILE), I32),
                   jax.ShapeDtypeStruct((1, LANES), I32)],
        scratch_shapes=[pltpu.VMEM((1, LANES), F32)],
        compiler_params=_params(("arbitrary",)),
        name="post",
    )(xf, aa, ab, gm, wg, woa, wob, wout, gx, wq, kvm, wox, gmoe, wrt, brt)


def _segment_starts(cnt_ref, seg_ref):
    def body(e, acc):
        seg_ref[e] = acc
        c = cnt_ref[0, R_EXP + e]
        return acc + ((c + (MOE_BLOCK - 1)) >> 8 << 8)
    return lax.fori_loop(0, N_EXPERTS, body, jnp.int32(0))


def _row_copies(ids_ref, seg_ref, k, t, rows_ref, tile_ref, sem, gather):
    dest = seg_ref[ids_ref[0, k, t]] + ids_ref[0, 2 + k, t]
    if gather:
        return pltpu.make_async_copy(rows_ref.at[pl.ds(dest, 1)], tile_ref.at[k, pl.ds(t, 1)], sem)
    return pltpu.make_async_copy(tile_ref.at[pl.ds(t, 1)], rows_ref.at[pl.ds(dest, 1)], sem)


def _dispatch_kernel(ids_ref, cnt_ref, hm_ref, xr_in_ref, xr_ref, blk_ref, seg_ref, sem):
    del xr_in_ref
    step = pl.program_id(0)
    nb = blk_ref.shape[1] - 1

    @pl.when(step == 0)
    def _():
        used = _segment_starts(cnt_ref, seg_ref)

        def per_expert(e, last):
            c = cnt_ref[0, R_EXP + e]
            b0 = seg_ref[e] >> 8
            n = (c + (MOE_BLOCK - 1)) >> 8

            def fill(kk, carry):
                blk_ref[0, b0 + kk] = e
                return carry
            lax.fori_loop(0, n, fill, 0)
            return jnp.where(n > 0, e, last)
        last = lax.fori_loop(0, N_EXPERTS, per_expert, jnp.int32(0))

        def tail(b, carry):
            blk_ref[0, b] = last
            return carry
        lax.fori_loop(used >> 8, nb, tail, 0)
        blk_ref[0, nb] = used >> 8

    def issue(t, carry):
        for k in range(2):
            _row_copies(ids_ref, seg_ref, k, t, xr_ref, hm_ref, sem, gather=False).start()
        return carry
    lax.fori_loop(0, TILE, issue, 0)

    def drain(t, carry):
        for k in range(2):
            _row_copies(ids_ref, seg_ref, k, t, xr_ref, hm_ref, sem, gather=False).wait()
        return carry
    lax.fori_loop(0, TILE, drain, 0)


def _dispatch(ids, cnt, hm, n_rows):
    n, d = hm.shape
    nt = n // TILE
    nb = n_rows // MOE_BLOCK
    zeros = jnp.zeros((n_rows, d), F32)
    return pl.pallas_call(
        _dispatch_kernel,
        grid=(nt,),
        in_specs=[pl.BlockSpec((1, 8, TILE), lambda i: (i, 0, 0), memory_space=pltpu.SMEM),
                  pl.BlockSpec(memory_space=pltpu.SMEM),
                  pl.BlockSpec((TILE, d), lambda i: (i, 0)),
                  pl.BlockSpec(memory_space=pl.ANY)],
        out_specs=[pl.BlockSpec(memory_space=pl.ANY),
                   pl.BlockSpec(memory_space=pltpu.SMEM)],
        out_shape=[jax.ShapeDtypeStruct((n_rows, d), F32),
                   jax.ShapeDtypeStruct((1, nb + 1), I32)],
        scratch_shapes=[pltpu.SMEM((N_EXPERTS,), I32), pltpu.SemaphoreType.DMA(())],
        input_output_aliases={3: 0},
        compiler_params=_params(("arbitrary",)),
        name="dispatch",
    )(ids, cnt, hm, zeros)


def _experts_kernel(blk_ref, x_ref, w1_ref, w3_ref, w2_ref, y_ref, w1b, w3b, w2b):
    b = pl.program_id(0)
    nb = pl.num_programs(0)
    e = blk_ref[0, b]
    prev = blk_ref[0, jnp.maximum(b - 1, 0)]

    @pl.when(jnp.logical_or(b == 0, e != prev))
    def _():
        w1b[...] = w1_ref[0].astype(BF16)
        w3b[...] = w3_ref[0].astype(BF16)
        w2b[...] = w2_ref[0].astype(BF16)

    used = blk_ref[0, nb]

    @pl.when(b < used)
    def _():
        xb = x_ref[...].astype(BF16)
        a = _mm(xb, w1b[...])
        g = _mm(xb, w3b[...])
        y_ref[...] = _mm((a * jax.nn.sigmoid(a) * g).astype(BF16), w2b[...])

    @pl.when(b >= used)
    def _():
        y_ref[...] = jnp.zeros(y_ref.shape, F32)


def _experts(blk, xr, w1, w3, w2):
    n_rows, d = xr.shape
    nb = n_rows // MOE_BLOCK
    wspec = lambda s: pl.BlockSpec((1,) + s, lambda b, blk: (blk[0, b], 0, 0))
    return pl.pallas_call(
        _experts_kernel,
        grid_spec=pltpu.PrefetchScalarGridSpec(
            num_scalar_prefetch=1,
            grid=(nb,),
            in_specs=[pl.BlockSpec((MOE_BLOCK, d), lambda b, blk: (b, 0)),
                      wspec((d, D_EXPERT)), wspec((d, D_EXPERT)), wspec((D_EXPERT, d))],
            out_specs=pl.BlockSpec((MOE_BLOCK, d), lambda b, blk: (b, 0)),
            scratch_shapes=[pltpu.VMEM((d, D_EXPERT), BF16), pltpu.VMEM((d, D_EXPERT), BF16),
                            pltpu.VMEM((D_EXPERT, d), BF16)]),
        out_shape=jax.ShapeDtypeStruct((n_rows, d), F32),
        compiler_params=_params(("arbitrary",)),
        name="experts",
    )(blk, xr, w1, w3, w2)


def _combine_kernel(ids_ref, cnt_ref, x2_ref, rw_ref, g_ref, yr_ref, o_ref, y_ref, seg_ref, sem):
    step = pl.program_id(0)

    @pl.when(step == 0)
    def _():
        _segment_starts(cnt_ref, seg_ref)

    def issue(t, carry):
        for k in range(2):
            _row_copies(ids_ref, seg_ref, k, t, yr_ref, y_ref, sem, gather=True).start()
        return carry
    lax.fori_loop(0, TILE, issue, 0)

    def drain(t, carry):
        for k in range(2):
            _row_copies(ids_ref, seg_ref, k, t, yr_ref, y_ref, sem, gather=True).wait()
        return carry
    lax.fori_loop(0, TILE, drain, 0)

    rw = rw_ref[...]
    x3 = x2_ref[...] + (y_ref[0] * rw[:, 0:1] + y_ref[1] * rw[:, 1:2])
    o_ref[...] = _rms(x3, g_ref[...])


def _combine(ids, cnt, x2, rw, g, yr):
    n, d = x2.shape
    nt = n // TILE
    row = lambda w: pl.BlockSpec((TILE, w), lambda i: (i, 0))
    return pl.pallas_call(
        _combine_kernel,
        grid=(nt,),
        in_specs=[pl.BlockSpec((1, 8, TILE), lambda i: (i, 0, 0), memory_space=pltpu.SMEM),
                  pl.BlockSpec(memory_space=pltpu.SMEM),
                  row(d), row(LANES), pl.BlockSpec((1, d), lambda i: (0, 0)),
                  pl.BlockSpec(memory_space=pl.ANY)],
        out_specs=row(d),
        out_shape=jax.ShapeDtypeStruct((n, d), F32),
        scratch_shapes=[pltpu.VMEM((2, TILE, d), F32), pltpu.SMEM((N_EXPERTS,), I32),
                        pltpu.SemaphoreType.DMA(())],
        compiler_params=_params(("arbitrary",)),
        name="combine",
    )(ids, cnt, x2, rw, g, yr)


def _t5_bucket(dist):
    n = jnp.maximum(dist, 0)
    max_exact = REL_BUCKETS // 2
    nf = jnp.maximum(n, 1).astype(F32)
    large = max_exact + (jnp.log(nf / max_exact) / math.log(REL_MAX_DIST / max_exact)
                         * (REL_BUCKETS - max_exact)).astype(I32)
    large = jnp.minimum(large, REL_BUCKETS - 1)
    return jnp.where(n < max_exact, n, large)


def _bias_tables(tab):
    r = jnp.arange(TILE)[:, None]
    c = jnp.arange(TILE)[None, :]
    rel = tab - tab[:, REL_BUCKETS - 1:]

    def tile(dist):
        onehot = (_t5_bucket(dist)[..., None] == jnp.arange(REL_BUCKETS)).astype(F32)
        return jnp.einsum('rcb,hb->hrc', onehot, rel, precision=lax.Precision.HIGHEST)

    own = jnp.where(r >= c, tile(r - c), NEG)
    return jnp.stack([own, tile(r - c + TILE)], axis=1).astype(F32)


def _block_mean_slabs(kmean, bn, nq):
    km = kmean.reshape(bn, nq, A_HEADS, HEAD_DIM).transpose(0, 2, 1, 3)
    pad = ((0, 0), (0, 0), (HEAD_DIM, LANES - HEAD_DIM - nq), (0, LANES - HEAD_DIM))
    return jnp.pad(km, pad).astype(BF16)


def _pad_cols(w, width):
    return jnp.pad(w, ((0, 0), (0, width - w.shape[1])))


def kernel(x, mem, rel_bias, final_norm, norm_mix, w_in, ckv_norm, w_uk, w_uv, w_oa, w_ob,
           w_out, norm_x, mem_norm, wq_x, wk_x, wv_x, wo_x, norm_moe, w_group, b_group,
           w_router, b_router, w1, w3, w2):
    bn, t, d = x.shape
    n = bn * t
    nq = t // TILE
    assert t % TILE == 0 and nq <= LANES - HEAD_DIM and norm_mix.shape[0] == 1
    near_a = _bias_tables(rel_bias[:, :A_HEADS].T)
    near_b = _bias_tables(rel_bias[:, A_HEADS:].T)
    row = lambda v: v.reshape(1, -1).astype(F32)

    wi = w_in[0]
    wp = jnp.concatenate(
        [wi[:, OFF_AQ:OFF_IK], jnp.tile(wi[:, OFF_IK:OFF_IW], (1, LANES // IDX_DIM)),
         _pad_cols(wi[:, OFF_IW:OFF_GA], LANES)], axis=1).astype(BF16)
    wkv = jnp.concatenate([w_uk[0], w_uv[0]], axis=1).astype(BF16)
    xf = x.reshape(n, d)
    qa, ka, va, qb, kb, vb, qi, ki4, widx, kmean = _proj_in(
        xf, row(norm_mix[0]), wp, row(ckv_norm[0]), wkv, nq)

    b3 = lambda a: a.reshape(bn, t, a.shape[-1])
    attn_a = _moba(b3(qa), b3(ka), b3(va), _block_mean_slabs(kmean, bn, nq), near_a)
    attn_b = _dsa(b3(qi), b3(ki4), b3(widx), b3(qb), b3(kb), b3(vb), near_b)

    kvm = _mem_kv(mem, row(mem_norm[0]), jnp.concatenate([wk_x[0], wv_x[0]], axis=1).astype(BF16))
    wrt = _pad_cols(jnp.concatenate([w_group[0], w_router[0]], axis=1), LANES).astype(F32)
    brt = _pad_cols(jnp.concatenate([b_group[0], b_router[0]]).reshape(1, -1), LANES).astype(F32)
    weights = (row(norm_mix[0]), wi[:, OFF_GA:IN_COLS].astype(BF16), w_oa[0].astype(BF16),
               w_ob[0].astype(BF16), w_out[0].astype(BF16), row(norm_x[0]), wq_x[0].astype(BF16),
               wo_x[0].astype(BF16), row(norm_moe[0]), wrt, brt)
    x2, hm, rw, ids, cnt = _post(xf, attn_a.reshape(n, W_A), attn_b.reshape(n, W_B), kvm, t // TILE, weights)

    n_rows = 2 * n + N_EXPERTS * MOE_BLOCK
    xr, blk = _dispatch(ids, cnt, hm, n_rows)
    yr = _experts(blk, xr, w1[0], w3[0], w2[0])
    out = _combine(ids, cnt, x2, rw, row(final_norm), yr)
    return out.reshape(bn, t, d)
```

```python
import functools
import math

import jax
import jax.numpy as jnp
from jax import lax
from jax.experimental import pallas as pl
from jax.experimental.pallas import tpu as pltpu

D_MODEL = 1024
HEAD_DIM = 64
A_HEADS = 8
B_HEADS = 8
W_A = A_HEADS * HEAD_DIM
W_B = B_HEADS * HEAD_DIM
MOBA_BLOCK = 256
MOBA_TOPK = 3
DSA_KV_RANK = 256
IDX_HEADS = 8
IDX_DIM = 32
DSA_TOPK = 256
REL_BUCKETS = 32
REL_MAX_DIST = 128
X_HEADS = 4
W_X = X_HEADS * HEAD_DIM
N_GROUPS = 4
EXPERTS_PER_GROUP = 8
N_EXPERTS = N_GROUPS * EXPERTS_PER_GROUP
D_EXPERT = 512
MOE_BLOCK = 256
EPS = 1e-6
NEG = -1e30

OFF_AQ = 0
OFF_AK = OFF_AQ + W_A
OFF_AV = OFF_AK + W_A
OFF_BQ = OFF_AV + W_A
OFF_CKV = OFF_BQ + W_B
OFF_IQ = OFF_CKV + DSA_KV_RANK
OFF_IK = OFF_IQ + IDX_HEADS * IDX_DIM
OFF_IW = OFF_IK + IDX_DIM
OFF_GA = OFF_IW + IDX_HEADS
OFF_GB = OFF_GA + D_MODEL
IN_COLS = OFF_GB + D_MODEL

LANES = 128
TILE = 256
VMEM_LIMIT = 56 * 1024 * 1024
BF16 = jnp.bfloat16
F32 = jnp.float32
I32 = jnp.int32

P_QA, P_KA, P_VA, P_QB = 0, 512, 1024, 1536
P_CKV, P_IQ, P_IK4, P_IW = 2048, 2304, 2560, 2688
P_COLS = 2816

R_GRP = 0
R_EXP = N_GROUPS


def _nt(a, b):
    return lax.dot_general(a, b, (((1,), (1,)), ((), ())), preferred_element_type=F32)


def _mm(a, b):
    return jnp.dot(a, b, preferred_element_type=F32)


def _rms(xf, g):
    return xf * lax.rsqrt(jnp.mean(xf * xf, axis=-1, keepdims=True) + EPS) * g


def _params(sem, vmem=VMEM_LIMIT):
    return pltpu.CompilerParams(dimension_semantics=sem, vmem_limit_bytes=vmem)


def _mem_kv_kernel(mem_ref, g_ref, w_ref, o_ref):
    m = _rms(mem_ref[0], g_ref[...]).astype(BF16)
    o_ref[0] = _mm(m, w_ref[...]).astype(BF16)


def _mem_kv(mem, g, wkv):
    bn, ml, d = mem.shape
    return pl.pallas_call(
        _mem_kv_kernel,
        grid=(bn,),
        in_specs=[pl.BlockSpec((1, ml, d), lambda b: (b, 0, 0)),
                  pl.BlockSpec((1, d), lambda b: (0, 0)),
                  pl.BlockSpec((d, 2 * W_X), lambda b: (0, 0))],
        out_specs=pl.BlockSpec((1, ml, 2 * W_X), lambda b: (b, 0, 0)),
        out_shape=jax.ShapeDtypeStruct((bn, ml, 2 * W_X), BF16),
        compiler_params=_params(("arbitrary",)),
        name="mem_kv",
    )(mem, g, wkv)


def _head_slabs(zp, extra):
    lane = lax.broadcasted_iota(I32, zp.shape, 1)
    low = lane < HEAD_DIM
    return jnp.where(low, zp, extra), jnp.where(low, pltpu.roll(zp, HEAD_DIM, 1), extra)


def _proj_in_kernel(blocks_per_seq, x_ref, g_ref, w_ref, cg_ref, wkv_ref,
                    qa_ref, ka_ref, va_ref, qb_ref, kb_ref, vb_ref, qi_ref, ki_ref, wi_ref, km_ref):
    h = _rms(x_ref[...], g_ref[...]).astype(BF16)
    z = _mm(h, w_ref[...])
    scale = HEAD_DIM ** -0.5
    zk = z[:, P_KA:P_VA]
    km_ref[0] = jnp.mean(zk, axis=0, keepdims=True)
    lane = lax.broadcasted_iota(I32, (TILE, LANES), 1)
    blk = pl.program_id(0) % blocks_per_seq
    onehot = jnp.where(lane == HEAD_DIM + blk, 1.0, 0.0)
    zero = jnp.zeros((TILE, LANES), F32)
    for p_idx in range(A_HEADS // 2):
        lo = p_idx * LANES
        qe, qo = _head_slabs(z[:, P_QA + lo:P_QA + lo + LANES] * scale, zero)
        ke, ko = _head_slabs(zk[:, lo:lo + LANES], onehot)
        qa_ref[:, 2 * lo:2 * lo + 2 * LANES] = jnp.concatenate([qe, qo], axis=1).astype(BF16)
        ka_ref[:, 2 * lo:2 * lo + 2 * LANES] = jnp.concatenate([ke, ko], axis=1).astype(BF16)
    va_ref[...] = z[:, P_VA:P_QB].astype(BF16)
    qb_ref[...] = (z[:, P_QB:P_CKV] * scale).astype(BF16)
    ckv = _rms(z[:, P_CKV:P_IQ], cg_ref[...]).astype(BF16)
    kv = _mm(ckv, wkv_ref[...])
    kb_ref[...] = kv[:, :W_B].astype(BF16)
    vb_ref[...] = kv[:, W_B:].astype(BF16)
    qi_ref[...] = z[:, P_IQ:P_IK4].astype(BF16)
    ki_ref[...] = z[:, P_IK4:P_IW].astype(BF16)
    wi_ref[...] = z[:, P_IW:P_COLS] * ((IDX_HEADS ** -0.5) * (IDX_DIM ** -0.5))


def _proj_in(xf, g, wp, cg, wkv, blocks_per_seq):
    n, d = xf.shape
    nt = n // TILE
    row = lambda w: pl.BlockSpec((TILE, w), lambda i: (i, 0))
    full = lambda a: pl.BlockSpec(a.shape, lambda i: (0,) * a.ndim)
    outs = [(2 * W_A, BF16), (2 * W_A, BF16), (W_A, BF16), (W_B, BF16), (W_B, BF16), (W_B, BF16),
            (IDX_HEADS * IDX_DIM, BF16), (LANES, BF16), (LANES, F32)]
    return pl.pallas_call(
        functools.partial(_proj_in_kernel, blocks_per_seq),
        grid=(nt,),
        in_specs=[row(d), full(g), full(wp), full(cg), full(wkv)],
        out_specs=[row(w) for w, _ in outs] + [pl.BlockSpec((1, 1, W_A), lambda i: (i, 0, 0))],
        out_shape=[jax.ShapeDtypeStruct((n, w), t) for w, t in outs]
        + [jax.ShapeDtypeStruct((nt, 1, W_A), F32)],
        compiler_params=_params(("arbitrary",)),
        name="proj_in",
    )(xf, g, wp, cg, wkv)


def _sm_init(mx_ref, ls_ref, acc_ref):
    mx_ref[...] = jnp.full(mx_ref.shape, NEG, F32)
    ls_ref[...] = jnp.zeros(ls_ref.shape, F32)
    acc_ref[...] = jnp.zeros(acc_ref.shape, F32)


def _sm_logits(hd, slot, s, sc_ref, mx_ref):
    sc_ref[hd, slot] = s
    mx_ref[hd] = jnp.maximum(mx_ref[hd], jnp.maximum(s[:, :LANES], s[:, LANES:]))


def _sm_rowmax(mx_ref):
    for hd in range(2):
        mx_ref[hd] = jnp.broadcast_to(jnp.max(mx_ref[hd], axis=1, keepdims=True), (TILE, LANES))


def _sm_values(hd, slots, v, sc_ref, mx_ref, ls_ref, acc_ref):
    m = mx_ref[hd]
    m2 = jnp.concatenate([m, m], axis=1)
    ps = [jnp.exp(sc_ref[hd, sl] - m2) for sl in slots]
    tot = ps[0][:, :LANES] + ps[0][:, LANES:]
    for p in ps[1:]:
        tot = tot + p[:, :LANES] + p[:, LANES:]
    ls_ref[hd] = ls_ref[hd] + tot
    pb = ps[0] if len(ps) == 1 else jnp.concatenate(ps, axis=1)
    acc_ref[hd] = acc_ref[hd] + _mm(pb.astype(BF16), v)


def _sm_output(ls_ref, acc_ref):
    lane = lax.broadcasted_iota(I32, (TILE, LANES), 1)
    out = [acc_ref[hd] / jnp.sum(ls_ref[hd], axis=1, keepdims=True) for hd in range(2)]
    return jnp.where(lane < HEAD_DIM, out[0], out[1])


def _split_pair(qp):
    lane = lax.broadcasted_iota(I32, qp.shape, 1)
    zero = jnp.zeros_like(qp)
    return jnp.where(lane < HEAD_DIM, qp, zero), jnp.where(lane >= HEAD_DIM, qp, zero)


def _moba_kernel(q_ref, k_ref, v_ref, km_ref, bias_ref, o_ref, sc_ref, mx_ref, ls_ref, acc_ref):
    i = pl.program_id(2)
    nq = pl.num_programs(2)
    lane = lax.broadcasted_iota(I32, (TILE, LANES), 1)
    blk = lane - HEAD_DIM
    in_blk = jnp.logical_and(blk >= 0, blk < nq)
    past = jnp.logical_and(in_blk, blk < i)
    slot_prev, slot_own = nq, nq + 1

    qs, q_far, q_prev = [], [], []
    for hd in range(2):
        q = q_ref[0, :, hd * LANES:(hd + 1) * LANES]
        g = jnp.where(past, _nt(q, km_ref[0, hd]), -jnp.inf)
        sel = jnp.zeros((TILE, LANES), jnp.bool_)
        for _ in range(MOBA_TOPK):
            top = jnp.max(g, axis=1, keepdims=True)
            hit = jnp.logical_and(g == top, in_blk)
            first = jnp.min(jnp.where(hit, blk, nq), axis=1, keepdims=True)
            pick = blk == first
            sel = jnp.logical_or(sel, jnp.logical_and(pick, past))
            g = jnp.where(pick, -jnp.inf, g)
        addm = jnp.where(sel, 0.0, NEG)
        far_m = jnp.where(jnp.logical_and(in_blk, blk < i - 1), addm, jnp.where(in_blk, NEG, 0.0))
        prev_m = jnp.where(jnp.logical_and(in_blk, blk == i - 1), addm, 0.0)
        qs.append(q)
        q_far.append(q + far_m.astype(BF16))
        q_prev.append(q + prev_m.astype(BF16))

    def keys(j, hd):
        off = pl.multiple_of(j * TILE, TILE)
        return k_ref[0, pl.ds(off, TILE), hd * LANES:(hd + 1) * LANES]

    n_grp = jnp.maximum(i, 1) // 2
    _sm_init(mx_ref, ls_ref, acc_ref)

    def far_logits(g, carry):
        for u in range(2):
            j = 2 * g + u
            for hd in range(2):
                _sm_logits(hd, j, _nt(q_far[hd], keys(j, hd)), sc_ref, mx_ref)
        return carry

    lax.fori_loop(0, n_grp, far_logits, 0)

    @pl.when(i >= 1)
    def _():
        for hd in range(2):
            _sm_logits(hd, slot_prev, _nt(q_prev[hd], keys(i - 1, hd)) + bias_ref[hd, 1], sc_ref, mx_ref)

    for hd in range(2):
        _sm_logits(hd, slot_own, _nt(qs[hd], keys(i, hd)) + bias_ref[hd, 0], sc_ref, mx_ref)
    _sm_rowmax(mx_ref)

    def vals(j, n):
        off = pl.multiple_of(j * TILE, TILE)
        return v_ref[0, pl.ds(off, n * TILE), :]

    def far_values(g, carry):
        v2 = vals(2 * g, 2)
        for hd in range(2):
            _sm_values(hd, [2 * g, 2 * g + 1], v2, sc_ref, mx_ref, ls_ref, acc_ref)
        return carry

    lax.fori_loop(0, n_grp, far_values, 0)

    @pl.when(i >= 1)
    def _():
        for hd in range(2):
            _sm_values(hd, [slot_prev], vals(i - 1, 1), sc_ref, mx_ref, ls_ref, acc_ref)

    for hd in range(2):
        _sm_values(hd, [slot_own], vals(i, 1), sc_ref, mx_ref, ls_ref, acc_ref)
    o_ref[0] = _sm_output(ls_ref, acc_ref).astype(BF16)


def _moba(qa, ka, va, kmp, bias_near):
    bn, t, _ = va.shape
    nq = t // TILE
    npair = A_HEADS // 2
    return pl.pallas_call(
        _moba_kernel,
        grid=(bn, npair, nq),
        in_specs=[pl.BlockSpec((1, TILE, 2 * LANES), lambda b, p, i: (b, i, p)),
                  pl.BlockSpec((1, t, 2 * LANES), lambda b, p, i: (b, 0, p)),
                  pl.BlockSpec((1, t, LANES), lambda b, p, i: (b, 0, p)),
                  pl.BlockSpec((1, 2, LANES, LANES), lambda b, p, i: (b, p, 0, 0)),
                  pl.BlockSpec((2, 2, TILE, TILE), lambda b, p, i: (p, 0, 0, 0))],
        out_specs=pl.BlockSpec((1, TILE, LANES), lambda b, p, i: (b, i, p)),
        out_shape=jax.ShapeDtypeStruct((bn, t, W_A), BF16),
        scratch_shapes=[pltpu.VMEM((2, nq + 2, TILE, TILE), F32),
                        pltpu.VMEM((2, TILE, LANES), F32),
                        pltpu.VMEM((2, TILE, LANES), F32),
                        pltpu.VMEM((2, TILE, LANES), F32)],
        compiler_params=_params(("arbitrary", "arbitrary", "arbitrary")),
        name="moba",
    )(qa, ka, va, kmp, bias_near)


def _dsa_select(i, qi_ref, ki_ref, wi_ref, s_ref, wb_ref):
    nch = i + 1
    sub = TILE // 8
    lane = lax.broadcasted_iota(I32, (TILE, LANES), 1)
    key_in = (lax.broadcasted_iota(I32, (sub, 8, TILE), 0) * 8 + lax.broadcasted_iota(I32, (sub, 8, TILE), 1))
    qpos = i * TILE + lax.broadcasted_iota(I32, (sub, 8, TILE), 2)

    def rows8(x):
        return jnp.broadcast_to(x, (8, TILE))[None]

    w_t = wi_ref[0].T
    qi = qi_ref[0]
    per_group = LANES // IDX_DIM
    qh = []
    for h in range(IDX_HEADS):
        g, r = divmod(h, per_group)
        qg = qi[:, g * LANES:(g + 1) * LANES]
        keep = jnp.logical_and(lane >= r * IDX_DIM, lane < (r + 1) * IDX_DIM)
        qh.append(jnp.where(keep, qg, jnp.zeros_like(qg)))
        wb_ref[h] = jnp.broadcast_to(w_t[h:h + 1], (8, TILE))

    def score(c, carry):
        off = pl.multiple_of(c * TILE, TILE)
        kc = ki_ref[0, pl.ds(off, TILE), :]
        acc = jnp.zeros((sub, 8, TILE), F32)
        for h in range(IDX_HEADS):
            acc = acc + jnp.maximum(_nt(kc, qh[h]), 0.0).reshape(sub, 8, TILE) * wb_ref[h][None]
        s_ref[c] = jnp.where(c * TILE + key_in <= qpos, acc, -jnp.inf).reshape(TILE, TILE)
        return carry

    lax.fori_loop(0, nch, score, 0)

    def fold(init, step):
        return lax.fori_loop(0, nch, lambda c, acc: step(acc, s_ref[c].reshape(sub, 8, TILE), c), init)

    def count(pred):
        def step(acc, blk, c):
            return acc + jnp.sum(jnp.where(pred(blk, c * TILE + key_in), 1.0, 0.0), axis=0)
        return jnp.sum(fold(jnp.zeros((8, TILE), F32), step), axis=0, keepdims=True)

    wide = rows8

    def extreme(op, fill, reduce):
        def step(acc, blk, c):
            return op(acc, reduce(jnp.where(blk == -jnp.inf, fill, blk), axis=0))
        return reduce(fold(jnp.full((8, TILE), fill, F32), step), axis=0, keepdims=True)

    hi = extreme(jnp.maximum, -jnp.inf, jnp.max)
    lo = extreme(jnp.minimum, jnp.inf, jnp.min)

    n_pos = count(lambda half, kpos: half > 0.0)
    n_nonneg = count(lambda half, kpos: half >= 0.0)
    top = hi
    top_ties = count(lambda half, kpos: half >= wide(top)) >= DSA_TOPK
    at_zero = jnp.logical_and(n_pos < DSA_TOPK, n_nonneg >= DSA_TOPK)
    lo = jnp.where(n_pos >= DSA_TOPK, 0.0, lo)
    hi = jnp.where(n_nonneg < DSA_TOPK, 0.0, hi)

    def search_more(state):
        return jnp.logical_and(jnp.min(state[3]) == 0, state[4] < 400)

    def search(state):
        lo, hi, thr, done, it = state
        mid = 0.5 * lo + 0.5 * hi
        closed = jnp.logical_or(mid <= lo, mid >= hi)
        n = count(lambda half, kpos: half >= wide(mid))
        found = jnp.where(closed, lo, mid)
        stop = jnp.logical_or(closed, n == DSA_TOPK)
        live = done == 0
        thr = jnp.where(jnp.logical_and(live, stop), found, thr)
        go = jnp.logical_and(live, jnp.logical_not(stop))
        lo = jnp.where(jnp.logical_and(go, n > DSA_TOPK), mid, lo)
        hi = jnp.where(jnp.logical_and(go, n < DSA_TOPK), mid, hi)
        return lo, hi, thr, jnp.where(stop, 1, done), it + 1

    thr0 = jnp.where(top_ties, top, 0.0)
    done0 = jnp.logical_or(top_ties, at_zero).astype(I32)
    thr = wide(lax.while_loop(search_more, search, (lo, hi, thr0, done0, jnp.int32(0)))[2])
    n_gt = count(lambda half, kpos: half > thr)
    n_ge = count(lambda half, kpos: half >= thr)
    need = DSA_TOPK - n_gt

    cut_bits = max(1, (s_ref.shape[0] * TILE - 1).bit_length())

    def cut_step(b, cut):
        cand = cut | (jnp.int32(1) << (cut_bits - 1 - b))
        cw = wide(cand)
        n = count(lambda half, kpos: jnp.logical_and(half == thr, kpos < cw))
        return jnp.where(n < need, cand, cut)

    cut = lax.cond(jnp.max(n_ge) > DSA_TOPK,
                   lambda: lax.fori_loop(0, cut_bits, cut_step, jnp.zeros((1, TILE), I32)),
                   lambda: jnp.full((1, TILE), 2 ** 30, I32))
    cut = wide(jnp.where(n_ge > DSA_TOPK, cut, 2 ** 30))

    def to_mask(c, carry):
        blk = s_ref[c].reshape(sub, 8, TILE)
        kpos = c * TILE + key_in
        keep = jnp.logical_or(blk > thr, jnp.logical_and(blk == thr, kpos <= cut))
        mask_t = jnp.where(jnp.logical_and(keep, kpos <= qpos), 0.0, NEG).reshape(TILE, TILE)
        s_ref[c] = mask_t.T
        return carry

    lax.fori_loop(0, nch, to_mask, 0)


def _dsa_kernel(qi_ref, ki_ref, wi_ref, q_ref, k_ref, v_ref, bias_ref, o_ref,
                s_ref, wb_ref, sc_ref, mx_ref, ls_ref, acc_ref):
    i = pl.program_id(1)
    pair = pl.program_id(2)

    @pl.when(jnp.logical_and(pair == 0, i == 0))
    def _():
        row = lax.broadcasted_iota(I32, (TILE, TILE), 0)
        col = lax.broadcasted_iota(I32, (TILE, TILE), 1)
        s_ref[0] = jnp.where(col <= row, 0.0, NEG)

    @pl.when(jnp.logical_and(pair == 0, i >= 1))
    def _():
        _dsa_select(i, qi_ref, ki_ref, wi_ref, s_ref, wb_ref)

    qs = _split_pair(q_ref[0])
    n_far = jnp.maximum(i - 1, 0)
    n_grp = n_far // 2
    odd = n_far % 2 == 1
    _sm_init(mx_ref, ls_ref, acc_ref)

    def rows(ref, j, n):
        off = pl.multiple_of(j * TILE, TILE)
        return ref[0, pl.ds(off, n * TILE), :]

    def logits(j, near=None):
        k_j = rows(k_ref, j, 1)
        msk = s_ref[j]
        for hd in range(2):
            s = _nt(qs[hd], k_j) + msk
            if near is not None:
                s = s + bias_ref[hd, near]
            _sm_logits(hd, j, s, sc_ref, mx_ref)

    def far_logits(g, carry):
        logits(2 * g)
        logits(2 * g + 1)
        return carry

    lax.fori_loop(0, n_grp, far_logits, 0)
    pl.when(odd)(lambda: logits(n_far - 1))
    pl.when(i >= 1)(lambda: logits(i - 1, near=1))
    logits(i, near=0)
    _sm_rowmax(mx_ref)

    def values(j, n):
        v = rows(v_ref, j, n)
        for hd in range(2):
            _sm_values(hd, [j + u for u in range(n)], v, sc_ref, mx_ref, ls_ref, acc_ref)

    def far_values(g, carry):
        values(2 * g, 2)
        return carry

    lax.fori_loop(0, n_grp, far_values, 0)
    pl.when(odd)(lambda: values(n_far - 1, 1))
    pl.when(i >= 1)(lambda: values(i - 1, 1))
    values(i, 1)
    o_ref[0] = _sm_output(ls_ref, acc_ref).astype(BF16)


def _dsa(qi, ki4, wi, qb, kb, vb, bias_near):
    bn, t, _ = qb.shape
    nq = t // TILE
    npair = B_HEADS // 2
    qrow = lambda w: pl.BlockSpec((1, TILE, w), lambda b, i, p: (b, i, 0))
    return pl.pallas_call(
        _dsa_kernel,
        grid=(bn, nq, npair),
        in_specs=[qrow(IDX_HEADS * IDX_DIM),
                  pl.BlockSpec((1, t, LANES), lambda b, i, p: (b, 0, 0), pipeline_mode=pl.Buffered(1)),
                  qrow(LANES),
                  pl.BlockSpec((1, TILE, LANES), lambda b, i, p: (b, i, p)),
                  pl.BlockSpec((1, t, LANES), lambda b, i, p: (b, 0, p)),
                  pl.BlockSpec((1, t, LANES), lambda b, i, p: (b, 0, p)),
                  pl.BlockSpec((2, 2, TILE, TILE), lambda b, i, p: (p, 0, 0, 0))],
        out_specs=pl.BlockSpec((1, TILE, LANES), lambda b, i, p: (b, i, p)),
        out_shape=jax.ShapeDtypeStruct((bn, t, W_B), BF16),
        scratch_shapes=[pltpu.VMEM((nq, TILE, TILE), F32),
                        pltpu.VMEM((IDX_HEADS, 8, TILE), F32),
                        pltpu.VMEM((2, nq, TILE, TILE), F32),
                        pltpu.VMEM((2, TILE, LANES), F32),
                        pltpu.VMEM((2, TILE, LANES), F32),
                        pltpu.VMEM((2, TILE, LANES), F32)],
        compiler_params=_params(("arbitrary", "arbitrary", "arbitrary")),
        name="dsa",
    )(qi, ki4, wi, qb, kb, vb, bias_near)


def _post_kernel(x_ref, aa_ref, ab_ref, gm_ref, wg_ref, woa_ref, wob_ref, wout_ref,
                 gx_ref, wq_ref, kvm_ref, wox_ref, gmoe_ref, wrt_ref, brt_ref,
                 x2_ref, hm_ref, rw_ref, ids_ref, cnt_ref, carry_ref):
    step = pl.program_id(0)
    xf = x_ref[...]
    h = _rms(xf, gm_ref[...]).astype(BF16)
    gates = jax.nn.sigmoid(_mm(h, wg_ref[...]))
    oa = _mm(aa_ref[...], woa_ref[...])
    ob = _mm(ab_ref[...], wob_ref[...])
    mrg = gates[:, :D_MODEL] * oa + gates[:, D_MODEL:] * ob
    x1 = xf + _mm(mrg.astype(BF16), wout_ref[...])

    hx = _rms(x1, gx_ref[...]).astype(BF16)
    q = (_mm(hx, wq_ref[...]) * (HEAD_DIM ** -0.5)).astype(BF16)
    kvm = kvm_ref[0]
    lane = lax.broadcasted_iota(I32, (TILE, LANES), 1)
    outs = []
    for p_idx in range(X_HEADS // 2):
        lo = p_idx * LANES
        qs = _split_pair(q[:, lo:lo + LANES])
        km = kvm[:, lo:lo + LANES]
        vm = kvm[:, W_X + lo:W_X + lo + LANES]
        o = []
        for hd in range(2):
            s = _nt(qs[hd], km)
            p = jnp.exp(s - jnp.max(s, axis=1, keepdims=True))
            o.append(_mm(p.astype(BF16), vm) / jnp.sum(p, axis=1, keepdims=True))
        outs.append(jnp.where(lane < HEAD_DIM, o[0], o[1]))
    xo = jnp.concatenate(outs, axis=1).astype(BF16)
    x2 = x1 + _mm(xo, wox_ref[...])
    x2_ref[...] = x2

    hm = _rms(x2, gmoe_ref[...])
    hm_ref[...] = hm
    logits = jnp.dot(hm, wrt_ref[...], preferred_element_type=F32,
                     precision=lax.Precision.HIGHEST) + brt_ref[...]
    big = LANES

    def argmax(v):
        top = jnp.max(v, axis=1, keepdims=True)
        return top, jnp.min(jnp.where(v == top, lane, big), axis=1, keepdims=True)

    is_grp = lane < R_EXP
    gtop, gsel = argmax(jnp.where(is_grp, logits, -jnp.inf))
    gw = 1.0 / jnp.sum(jnp.where(is_grp, jnp.exp(logits - gtop), 0.0), axis=1, keepdims=True)
    first = R_EXP + gsel * EXPERTS_PER_GROUP
    inside = jnp.logical_and(lane >= first, lane < first + EXPERTS_PER_GROUP)
    within = jnp.where(inside, logits, -jnp.inf)
    v0, i0 = argmax(within)
    v1, i1 = argmax(jnp.where(lane == i0, -jnp.inf, within))
    e1 = jnp.exp(v1 - v0)
    w0 = gw * (1.0 / (1.0 + e1))
    w1 = gw * (e1 / (1.0 + e1))
    rw_ref[...] = jnp.where(lane == 0, w0, jnp.where(lane == 1, w1, 0.0))

    @pl.when(step == 0)
    def _():
        carry_ref[...] = jnp.zeros(carry_ref.shape, F32)

    hit0 = lane == i0
    hit1 = lane == i1
    onehot = jnp.where(jnp.logical_or(hit0, hit1), 1.0, 0.0)
    tri = (lax.broadcasted_iota(I32, (TILE, TILE), 1) < lax.broadcasted_iota(I32, (TILE, TILE), 0))
    base = carry_ref[...] + _mm(tri.astype(BF16), onehot.astype(BF16))
    r0 = jnp.sum(jnp.where(hit0, base, 0.0), axis=1, keepdims=True)
    r1 = jnp.sum(jnp.where(hit1, base, 0.0), axis=1, keepdims=True)
    total = carry_ref[...] + jnp.sum(onehot, axis=0, keepdims=True)
    carry_ref[...] = total
    cnt_ref[...] = total.astype(I32)
    slab = jnp.where(lane == 0, (i0 - R_EXP).astype(F32),
                     jnp.where(lane == 1, (i1 - R_EXP).astype(F32),
                               jnp.where(lane == 2, r0, jnp.where(lane == 3, r1, 0.0))))
    ids_ref[0] = slab.T[:8].astype(I32)


def _post(xf, aa, ab, kvm, tiles_per_batch, weights):
    n, d = xf.shape
    nt = n // TILE
    row = lambda w: pl.BlockSpec((TILE, w), lambda i: (i, 0))
    full = lambda a: pl.BlockSpec(a.shape, lambda i: (0,) * a.ndim)
    gm, wg, woa, wob, wout, gx, wq, wox, gmoe, wrt, brt = weights
    return pl.pallas_call(
        _post_kernel,
        grid=(nt,),
        in_specs=[row(d), row(W_A), row(W_B), full(gm), full(wg), full(woa), full(wob), full(wout),
                  full(gx), full(wq),
                  pl.BlockSpec((1,) + kvm.shape[1:], lambda i: (i // tiles_per_batch, 0, 0)),
                  full(wox), full(gmoe), full(wrt), full(brt)],
        out_specs=[row(d), row(d), row(LANES),
                   pl.BlockSpec((1, 8, TILE), lambda i: (i, 0, 0)),
                   pl.BlockSpec((1, LANES), lambda i: (0, 0))],
        out_shape=[jax.ShapeDtypeStruct((n, d), F32), jax.ShapeDtypeStruct((n, d), F32),
                   jax.ShapeDtypeStruct((n, LANES), F32),
                   jax.ShapeDtypeStruct((nt, 8, TILE), I32),
                   jax.ShapeDtypeStruct((1, LANES), I32)],
        scratch_shapes=[pltpu.VMEM((1, LANES), F32)],
        compiler_params=_params(("arbitrary",)),
        name="post",
    )(xf, aa, ab, gm, wg, woa, wob, wout, gx, wq, kvm, wox, gmoe, wrt, brt)


def _segment_starts(cnt_ref, seg_ref):
    def body(e, acc):
        seg_ref[e] = acc
        c = cnt_ref[0, R_EXP + e]
        return acc + ((c + (MOE_BLOCK - 1)) >> 8 << 8)
    return lax.fori_loop(0, N_EXPERTS, body, jnp.int32(0))


def _row_copies(ids_ref, seg_ref, k, t, rows_ref, tile_ref, sem, gather):
    dest = seg_ref[ids_ref[0, k, t]] + ids_ref[0, 2 + k, t]
    if gather:
        return pltpu.make_async_copy(rows_ref.at[pl.ds(dest, 1)], tile_ref.at[k, pl.ds(t, 1)], sem)
    return pltpu.make_async_copy(tile_ref.at[pl.ds(t, 1)], rows_ref.at[pl.ds(dest, 1)], sem)


def _dispatch_kernel(ids_ref, cnt_ref, hm_ref, xr_in_ref, xr_ref, blk_ref, seg_ref, sem):
    del xr_in_ref
    step = pl.program_id(0)
    nb = blk_ref.shape[1] - 1

    @pl.when(step == 0)
    def _():
        used = _segment_starts(cnt_ref, seg_ref)

        def per_expert(e, last):
            c = cnt_ref[0, R_EXP + e]
            b0 = seg_ref[e] >> 8
            n = (c + (MOE_BLOCK - 1)) >> 8

            def fill(kk, carry):
                blk_ref[0, b0 + kk] = e
                return carry
            lax.fori_loop(0, n, fill, 0)
            return jnp.where(n > 0, e, last)
        last = lax.fori_loop(0, N_EXPERTS, per_expert, jnp.int32(0))

        def tail(b, carry):
            blk_ref[0, b] = last
            return carry
        lax.fori_loop(used >> 8, nb, tail, 0)
        blk_ref[0, nb] = used >> 8

    def issue(t, carry):
        for k in range(2):
            _row_copies(ids_ref, seg_ref, k, t, xr_ref, hm_ref, sem, gather=False).start()
        return carry
    lax.fori_loop(0, TILE, issue, 0)

    def drain(t, carry):
        for k in range(2):
            _row_copies(ids_ref, seg_ref, k, t, xr_ref, hm_ref, sem, gather=False).wait()
        return carry
    lax.fori_loop(0, TILE, drain, 0)


def _dispatch(ids, cnt, hm, n_rows):
    n, d = hm.shape
    nt = n // TILE
    nb = n_rows // MOE_BLOCK
    zeros = jnp.zeros((n_rows, d), F32)
    return pl.pallas_call(
        _dispatch_kernel,
        grid=(nt,),
        in_specs=[pl.BlockSpec((1, 8, TILE), lambda i: (i, 0, 0), memory_space=pltpu.SMEM),
                  pl.BlockSpec(memory_space=pltpu.SMEM),
                  pl.BlockSpec((TILE, d), lambda i: (i, 0)),
                  pl.BlockSpec(memory_space=pl.ANY)],
        out_specs=[pl.BlockSpec(memory_space=pl.ANY),
                   pl.BlockSpec(memory_space=pltpu.SMEM)],
        out_shape=[jax.ShapeDtypeStruct((n_rows, d), F32),
                   jax.ShapeDtypeStruct((1, nb + 1), I32)],
        scratch_shapes=[pltpu.SMEM((N_EXPERTS,), I32), pltpu.SemaphoreType.DMA(())],
        input_output_aliases={3: 0},
        compiler_params=_params(("arbitrary",)),
        name="dispatch",
    )(ids, cnt, hm, zeros)


def _experts_kernel(blk_ref, x_ref, w1_ref, w3_ref, w2_ref, y_ref, w1b, w3b, w2b):
    b = pl.program_id(0)
    nb = pl.num_programs(0)
    e = blk_ref[0, b]
    prev = blk_ref[0, jnp.maximum(b - 1, 0)]

    @pl.when(jnp.logical_or(b == 0, e != prev))
    def _():
        w1b[...] = w1_ref[0].astype(BF16)
        w3b[...] = w3_ref[0].astype(BF16)
        w2b[...] = w2_ref[0].astype(BF16)

    used = blk_ref[0, nb]

    @pl.when(b < used)
    def _():
        xb = x_ref[...].astype(BF16)
        a = _mm(xb, w1b[...])
        g = _mm(xb, w3b[...])
        y_ref[...] = _mm((a * jax.nn.sigmoid(a) * g).astype(BF16), w2b[...])

    @pl.when(b >= used)
    def _():
        y_ref[...] = jnp.zeros(y_ref.shape, F32)


def _experts(blk, xr, w1, w3, w2):
    n_rows, d = xr.shape
    nb = n_rows // MOE_BLOCK
    wspec = lambda s: pl.BlockSpec((1,) + s, lambda b, blk: (blk[0, b], 0, 0))
    return pl.pallas_call(
        _experts_kernel,
        grid_spec=pltpu.PrefetchScalarGridSpec(
            num_scalar_prefetch=1,
            grid=(nb,),
            in_specs=[pl.BlockSpec((MOE_BLOCK, d), lambda b, blk: (b, 0)),
                      wspec((d, D_EXPERT)), wspec((d, D_EXPERT)), wspec((D_EXPERT, d))],
            out_specs=pl.BlockSpec((MOE_BLOCK, d), lambda b, blk: (b, 0)),
            scratch_shapes=[pltpu.VMEM((d, D_EXPERT), BF16), pltpu.VMEM((d, D_EXPERT), BF16),
                            pltpu.VMEM((D_EXPERT, d), BF16)]),
        out_shape=jax.ShapeDtypeStruct((n_rows, d), F32),
        compiler_params=_params(("arbitrary",)),
        name="experts",
    )(blk, xr, w1, w3, w2)


def _combine_kernel(ids_ref, cnt_ref, x2_ref, rw_ref, g_ref, yr_ref, o_ref, y_ref, seg_ref, sem):
    step = pl.program_id(0)

    @pl.when(step == 0)
    def _():
        _segment_starts(cnt_ref, seg_ref)

    def issue(t, carry):
        for k in range(2):
            _row_copies(ids_ref, seg_ref, k, t, yr_ref, y_ref, sem, gather=True).start()
        return carry
    lax.fori_loop(0, TILE, issue, 0)

    def drain(t, carry):
        for k in range(2):
            _row_copies(ids_ref, seg_ref, k, t, yr_ref, y_ref, sem, gather=True).wait()
        return carry
    lax.fori_loop(0, TILE, drain, 0)

    rw = rw_ref[...]
    x3 = x2_ref[...] + (y_ref[0] * rw[:, 0:1] + y_ref[1] * rw[:, 1:2])
    o_ref[...] = _rms(x3, g_ref[...])


def _combine(ids, cnt, x2, rw, g, yr):
    n, d = x2.shape
    nt = n // TILE
    row = lambda w: pl.BlockSpec((TILE, w), lambda i: (i, 0))
    return pl.pallas_call(
        _combine_kernel,
        grid=(nt,),
        in_specs=[pl.BlockSpec((1, 8, TILE), lambda i: (i, 0, 0), memory_space=pltpu.SMEM),
                  pl.BlockSpec(memory_space=pltpu.SMEM),
                  row(d), row(LANES), pl.BlockSpec((1, d), lambda i: (0, 0)),
                  pl.BlockSpec(memory_space=pl.ANY)],
        out_specs=row(d),
        out_shape=jax.ShapeDtypeStruct((n, d), F32),
        scratch_shapes=[pltpu.VMEM((2, TILE, d), F32), pltpu.SMEM((N_EXPERTS,), I32),
                        pltpu.SemaphoreType.DMA(())],
        compiler_params=_params(("arbitrary",)),
        name="combine",
    )(ids, cnt, x2, rw, g, yr)


def _t5_bucket(dist):
    n = jnp.maximum(dist, 0)
    max_exact = REL_BUCKETS // 2
    nf = jnp.maximum(n, 1).astype(F32)
    large = max_exact + (jnp.log(nf / max_exact) / math.log(REL_MAX_DIST / max_exact)
                         * (REL_BUCKETS - max_exact)).astype(I32)
    large = jnp.minimum(large, REL_BUCKETS - 1)
    return jnp.where(n < max_exact, n, large)


def _bias_tables(tab):
    r = jnp.arange(TILE)[:, None]
    c = jnp.arange(TILE)[None, :]
    rel = tab - tab[:, REL_BUCKETS - 1:]

    def tile(dist):
        onehot = (_t5_bucket(dist)[..., None] == jnp.arange(REL_BUCKETS)).astype(F32)
        return jnp.einsum('rcb,hb->hrc', onehot, rel, precision=lax.Precision.HIGHEST)

    own = jnp.where(r >= c, tile(r - c), NEG)
    return jnp.stack([own, tile(r - c + TILE)], axis=1).astype(F32)


def _block_mean_slabs(kmean, bn, nq):
    km = kmean.reshape(bn, nq, A_HEADS, HEAD_DIM).transpose(0, 2, 1, 3)
    pad = ((0, 0), (0, 0), (HEAD_DIM, LANES - HEAD_DIM - nq), (0, LANES - HEAD_DIM))
    return jnp.pad(km, pad).astype(BF16)


def _pad_cols(w, width):
    return jnp.pad(w, ((0, 0), (0, width - w.shape[1])))


def kernel(x, mem, rel_bias, final_norm, norm_mix, w_in, ckv_norm, w_uk, w_uv, w_oa, w_ob,
           w_out, norm_x, mem_norm, wq_x, wk_x, wv_x, wo_x, norm_moe, w_group, b_group,
           w_router, b_router, w1, w3, w2):
    bn, t, d = x.shape
    n = bn * t
    nq = t // TILE
    assert t % TILE == 0 and nq <= LANES - HEAD_DIM and norm_mix.shape[0] == 1
    near_a = _bias_tables(rel_bias[:, :A_HEADS].T)
    near_b = _bias_tables(rel_bias[:, A_HEADS:].T)
    row = lambda v: v.reshape(1, -1).astype(F32)

    wi = w_in[0]
    wp = jnp.concatenate(
        [wi[:, OFF_AQ:OFF_IK], jnp.tile(wi[:, OFF_IK:OFF_IW], (1, LANES // IDX_DIM)),
         _pad_cols(wi[:, OFF_IW:OFF_GA], LANES)], axis=1).astype(BF16)
    wkv = jnp.concatenate([w_uk[0], w_uv[0]], axis=1).astype(BF16)
    xf = x.reshape(n, d)
    qa, ka, va, qb, kb, vb, qi, ki4, widx, kmean = _proj_in(
        xf, row(norm_mix[0]), wp, row(ckv_norm[0]), wkv, nq)

    b3 = lambda a: a.reshape(bn, t, a.shape[-1])
    attn_a = _moba(b3(qa), b3(ka), b3(va), _block_mean_slabs(kmean, bn, nq), near_a)
    attn_b = _dsa(b3(qi), b3(ki4), b3(widx), b3(qb), b3(kb), b3(vb), near_b)

    kvm = _mem_kv(mem, row(mem_norm[0]), jnp.concatenate([wk_x[0], wv_x[0]], axis=1).astype(BF16))
    wrt = _pad_cols(jnp.concatenate([w_group[0], w_router[0]], axis=1), LANES).astype(F32)
    brt = _pad_cols(jnp.concatenate([b_group[0], b_router[0]]).reshape(1, -1), LANES).astype(F32)
    weights = (row(norm_mix[0]), wi[:, OFF_GA:IN_COLS].astype(BF16), w_oa[0].astype(BF16),
               w_ob[0].astype(BF16), w_out[0].astype(BF16), row(norm_x[0]), wq_x[0].astype(BF16),
               wo_x[0].astype(BF16), row(norm_moe[0]), wrt, brt)
    x2, hm, rw, ids, cnt = _post(xf, attn_a.reshape(n, W_A), attn_b.reshape(n, W_B), kvm, t // TILE, weights)

    n_rows = 2 * n + N_EXPERTS * MOE_BLOCK
    xr, blk = _dispatch(ids, cnt, hm, n_rows)
    yr = _experts(blk, xr, w1[0], w3[0], w2[0])
    out = _combine(ids, cnt, x2, rw, row(final_norm), yr)
    return out.reshape(bn, t, d)
```

```python
import functools
import math

import jax
import jax.numpy as jnp
from jax import lax
from jax.experimental import pallas as pl
from jax.experimental.pallas import tpu as pltpu

D_MODEL = 1024
HEAD_DIM = 64
A_HEADS = 8
B_HEADS = 8
W_A = A_HEADS * HEAD_DIM
W_B = B_HEADS * HEAD_DIM
MOBA_BLOCK = 256
MOBA_TOPK = 3
DSA_KV_RANK = 256
IDX_HEADS = 8
IDX_DIM = 32
DSA_TOPK = 256
REL_BUCKETS = 32
REL_MAX_DIST = 128
X_HEADS = 4
W_X = X_HEADS * HEAD_DIM
N_GROUPS = 4
EXPERTS_PER_GROUP = 8
N_EXPERTS = N_GROUPS * EXPERTS_PER_GROUP
D_EXPERT = 512
MOE_BLOCK = 256
EPS = 1e-6
NEG = -1e30

OFF_AQ = 0
OFF_AK = OFF_AQ + W_A
OFF_AV = OFF_AK + W_A
OFF_BQ = OFF_AV + W_A
OFF_CKV = OFF_BQ + W_B
OFF_IQ = OFF_CKV + DSA_KV_RANK
OFF_IK = OFF_IQ + IDX_HEADS * IDX_DIM
OFF_IW = OFF_IK + IDX_DIM
OFF_GA = OFF_IW + IDX_HEADS
OFF_GB = OFF_GA + D_MODEL
IN_COLS = OFF_GB + D_MODEL

LANES = 128
TILE = 256
VMEM_LIMIT = 56 * 1024 * 1024
BF16 = jnp.bfloat16
F32 = jnp.float32
I32 = jnp.int32

P_QA, P_KA, P_VA, P_QB = 0, 512, 1024, 1536
P_CKV, P_IQ, P_IK4, P_IW = 2048, 2304, 2560, 2688
P_COLS = 2816

R_GRP = 0
R_EXP = N_GROUPS


def _nt(a, b):
    return lax.dot_general(a, b, (((1,), (1,)), ((), ())), preferred_element_type=F32)


def _mm(a, b):
    return jnp.dot(a, b, preferred_element_type=F32)


def _rms(xf, g):
    return xf * lax.rsqrt(jnp.mean(xf * xf, axis=-1, keepdims=True) + EPS) * g


def _params(sem, vmem=VMEM_LIMIT):
    return pltpu.CompilerParams(dimension_semantics=sem, vmem_limit_bytes=vmem)


def _mem_kv_kernel(mem_ref, g_ref, w_ref, o_ref):
    m = _rms(mem_ref[0], g_ref[...]).astype(BF16)
    o_ref[0] = _mm(m, w_ref[...]).astype(BF16)


def _mem_kv(mem, g, wkv):
    bn, ml, d = mem.shape
    return pl.pallas_call(
        _mem_kv_kernel,
        grid=(bn,),
        in_specs=[pl.BlockSpec((1, ml, d), lambda b: (b, 0, 0)),
                  pl.BlockSpec((1, d), lambda b: (0, 0)),
                  pl.BlockSpec((d, 2 * W_X), lambda b: (0, 0))],
        out_specs=pl.BlockSpec((1, ml, 2 * W_X), lambda b: (b, 0, 0)),
        out_shape=jax.ShapeDtypeStruct((bn, ml, 2 * W_X), BF16),
        compiler_params=_params(("arbitrary",)),
        name="mem_kv",
    )(mem, g, wkv)


def _head_slabs(zp, extra):
    lane = lax.broadcasted_iota(I32, zp.shape, 1)
    low = lane < HEAD_DIM
    return jnp.where(low, zp, extra), jnp.where(low, pltpu.roll(zp, HEAD_DIM, 1), extra)


def _proj_in_kernel(blocks_per_seq, x_ref, g_ref, w_ref, cg_ref, wkv_ref,
                    qa_ref, ka_ref, va_ref, qb_ref, kb_ref, vb_ref, qi_ref, ki_ref, wi_ref, km_ref):
    h = _rms(x_ref[...], g_ref[...]).astype(BF16)
    z = _mm(h, w_ref[...])
    scale = HEAD_DIM ** -0.5
    zk = z[:, P_KA:P_VA]
    km_ref[0] = jnp.mean(zk, axis=0, keepdims=True)
    lane = lax.broadcasted_iota(I32, (TILE, LANES), 1)
    blk = pl.program_id(0) % blocks_per_seq
    onehot = jnp.where(lane == HEAD_DIM + blk, 1.0, 0.0)
    zero = jnp.zeros((TILE, LANES), F32)
    for p_idx in range(A_HEADS // 2):
        lo = p_idx * LANES
        qe, qo = _head_slabs(z[:, P_QA + lo:P_QA + lo + LANES] * scale, zero)
        ke, ko = _head_slabs(zk[:, lo:lo + LANES], onehot)
        qa_ref[:, 2 * lo:2 * lo + 2 * LANES] = jnp.concatenate([qe, qo], axis=1).astype(BF16)
        ka_ref[:, 2 * lo:2 * lo + 2 * LANES] = jnp.concatenate([ke, ko], axis=1).astype(BF16)
    va_ref[...] = z[:, P_VA:P_QB].astype(BF16)
    qb_ref[...] = (z[:, P_QB:P_CKV] * scale).astype(BF16)
    ckv = _rms(z[:, P_CKV:P_IQ], cg_ref[...]).astype(BF16)
    kv = _mm(ckv, wkv_ref[...])
    kb_ref[...] = kv[:, :W_B].astype(BF16)
    vb_ref[...] = kv[:, W_B:].astype(BF16)
    qi_ref[...] = z[:, P_IQ:P_IK4].astype(BF16)
    ki_ref[...] = z[:, P_IK4:P_IW].astype(BF16)
    wi_ref[...] = z[:, P_IW:P_COLS] * ((IDX_HEADS ** -0.5) * (IDX_DIM ** -0.5))


def _proj_in(xf, g, wp, cg, wkv, blocks_per_seq):
    n, d = xf.shape
    nt = n // TILE
    row = lambda w: pl.BlockSpec((TILE, w), lambda i: (i, 0))
    full = lambda a: pl.BlockSpec(a.shape, lambda i: (0,) * a.ndim)
    outs = [(2 * W_A, BF16), (2 * W_A, BF16), (W_A, BF16), (W_B, BF16), (W_B, BF16), (W_B, BF16),
            (IDX_HEADS * IDX_DIM, BF16), (LANES, BF16), (LANES, F32)]
    return pl.pallas_call(
        functools.partial(_proj_in_kernel, blocks_per_seq),
        grid=(nt,),
        in_specs=[row(d), full(g), full(wp), full(cg), full(wkv)],
        out_specs=[row(w) for w, _ in outs] + [pl.BlockSpec((1, 1, W_A), lambda i: (i, 0, 0))],
        out_shape=[jax.ShapeDtypeStruct((n, w), t) for w, t in outs]
        + [jax.ShapeDtypeStruct((nt, 1, W_A), F32)],
        compiler_params=_params(("arbitrary",)),
        name="proj_in",
    )(xf, g, wp, cg, wkv)


def _sm_init(mx_ref, acc_ref):
    mx_ref[...] = jnp.full(mx_ref.shape, NEG, F32)
    acc_ref[...] = jnp.zeros(acc_ref.shape, F32)


def _sm_logits(hd, slot, s, sc_ref, mx_ref):
    sc_ref[hd, slot] = s
    mx_ref[hd] = jnp.maximum(mx_ref[hd], jnp.maximum(s[:, :LANES], s[:, LANES:]))


def _sm_rowmax(mx_ref):
    for hd in range(2):
        mx_ref[hd] = jnp.broadcast_to(jnp.max(mx_ref[hd], axis=1, keepdims=True), (TILE, LANES))


def _sm_probs(hd, slots, sc_ref, mx_ref):
    m = mx_ref[hd]
    m2 = jnp.concatenate([m, m], axis=1)
    ps = [jnp.exp((sc_ref[hd, sl] - m2).astype(BF16)) for sl in slots]
    return ps[0] if len(ps) == 1 else jnp.concatenate(ps, axis=1)


def _sm_accumulate(hd, pb, v, acc_ref):
    lane = lax.broadcasted_iota(I32, v.shape, 1)
    own = (lane < HEAD_DIM) if hd == 0 else (lane >= HEAD_DIM)
    acc_ref[hd] = acc_ref[hd] + _mm(pb, jnp.where(own, v, jnp.ones_like(v)))


def _sm_values(hd, slots, v, sc_ref, mx_ref, acc_ref):
    _sm_accumulate(hd, _sm_probs(hd, slots, sc_ref, mx_ref), v, acc_ref)


FAR_GROUP = 4


def _sm_far_values(n_grp, vals, sc_ref, mx_ref, acc_ref, stage_ref):
    def stage(g):
        slots = [FAR_GROUP * g + u for u in range(FAR_GROUP)]
        return [_sm_probs(hd, slots, sc_ref, mx_ref) for hd in range(2)]

    def put(ps):
        for hd in range(2):
            stage_ref[hd] = ps[hd]

    pl.when(n_grp > 0)(lambda: put(stage(0)))

    def body(g, carry):
        v = vals(FAR_GROUP * g, FAR_GROUP)
        cur = [stage_ref[hd] for hd in range(2)]
        nxt = stage(jnp.minimum(g + 1, n_grp - 1))
        for hd in range(2):
            _sm_accumulate(hd, cur[hd], v, acc_ref)
        put(nxt)
        return carry

    lax.fori_loop(0, n_grp, body, 0)


def _sm_output(acc_ref):
    lane = lax.broadcasted_iota(I32, (TILE, LANES), 1)
    out = [acc_ref[hd] / pltpu.roll(acc_ref[hd], HEAD_DIM, 1) for hd in range(2)]
    return jnp.where(lane < HEAD_DIM, out[0], out[1])


def _split_pair(qp):
    lane = lax.broadcasted_iota(I32, qp.shape, 1)
    zero = jnp.zeros_like(qp)
    return jnp.where(lane < HEAD_DIM, qp, zero), jnp.where(lane >= HEAD_DIM, qp, zero)


def _moba_kernel(q_ref, k_ref, v_ref, km_ref, bias_ref, o_ref, sc_ref, mx_ref, acc_ref, stage_ref):
    i = pl.program_id(2)
    nq = pl.num_programs(2)
    lane = lax.broadcasted_iota(I32, (TILE, LANES), 1)
    blk = lane - HEAD_DIM
    in_blk = jnp.logical_and(blk >= 0, blk < nq)
    past = jnp.logical_and(in_blk, blk < i)
    slot_prev, slot_own = nq, nq + 1

    qs, q_far, q_prev = [], [], []
    for hd in range(2):
        q = q_ref[0, :, hd * LANES:(hd + 1) * LANES]
        g = jnp.where(past, _nt(q, km_ref[0, hd]), -jnp.inf)
        sel = jnp.zeros((TILE, LANES), jnp.bool_)
        for _ in range(MOBA_TOPK):
            top = jnp.max(g, axis=1, keepdims=True)
            hit = jnp.logical_and(g == top, in_blk)
            first = jnp.min(jnp.where(hit, blk, nq), axis=1, keepdims=True)
            pick = blk == first
            sel = jnp.logical_or(sel, jnp.logical_and(pick, past))
            g = jnp.where(pick, -jnp.inf, g)
        addm = jnp.where(sel, 0.0, NEG)
        far_m = jnp.where(jnp.logical_and(in_blk, blk < i - 1), addm, jnp.where(in_blk, NEG, 0.0))
        prev_m = jnp.where(jnp.logical_and(in_blk, blk == i - 1), addm, 0.0)
        qs.append(q)
        q_far.append(q + far_m.astype(BF16))
        q_prev.append(q + prev_m.astype(BF16))

    def keys(j, hd):
        off = pl.multiple_of(j * TILE, TILE)
        return k_ref[0, pl.ds(off, TILE), hd * LANES:(hd + 1) * LANES]

    n_far = jnp.maximum(i - 1, 0)
    _sm_init(mx_ref, acc_ref)

    def far_logits(g, carry):
        for u in range(FAR_GROUP):
            j = FAR_GROUP * g + u
            for hd in range(2):
                _sm_logits(hd, j, _nt(q_far[hd], keys(j, hd)), sc_ref, mx_ref)
        return carry

    n_grp = (n_far + FAR_GROUP - 1) // FAR_GROUP
    lax.fori_loop(0, n_grp, far_logits, 0)

    @pl.when(i >= 1)
    def _():
        for hd in range(2):
            _sm_logits(hd, slot_prev, _nt(q_prev[hd], keys(i - 1, hd)) + bias_ref[hd, 1], sc_ref, mx_ref)

    for hd in range(2):
        _sm_logits(hd, slot_own, _nt(qs[hd], keys(i, hd)) + bias_ref[hd, 0], sc_ref, mx_ref)
    _sm_rowmax(mx_ref)

    def vals(j, n):
        off = pl.multiple_of(j * TILE, TILE)
        return v_ref[0, pl.ds(off, n * TILE), :]

    _sm_far_values(n_grp, vals, sc_ref, mx_ref, acc_ref, stage_ref)

    @pl.when(i >= 1)
    def _():
        for hd in range(2):
            _sm_values(hd, [slot_prev], vals(i - 1, 1), sc_ref, mx_ref, acc_ref)

    for hd in range(2):
        _sm_values(hd, [slot_own], vals(i, 1), sc_ref, mx_ref, acc_ref)
    o_ref[0] = _sm_output(acc_ref).astype(BF16)


def _moba(qa, ka, va, kmp, bias_near):
    bn, t, _ = va.shape
    nq = t // TILE
    npair = A_HEADS // 2
    return pl.pallas_call(
        _moba_kernel,
        grid=(bn, npair, nq),
        in_specs=[pl.BlockSpec((1, TILE, 2 * LANES), lambda b, p, i: (b, i, p)),
                  pl.BlockSpec((1, t, 2 * LANES), lambda b, p, i: (b, 0, p)),
                  pl.BlockSpec((1, t, LANES), lambda b, p, i: (b, 0, p)),
                  pl.BlockSpec((1, 2, LANES, LANES), lambda b, p, i: (b, p, 0, 0)),
                  pl.BlockSpec((2, 2, TILE, TILE), lambda b, p, i: (p, 0, 0, 0))],
        out_specs=pl.BlockSpec((1, TILE, LANES), lambda b, p, i: (b, i, p)),
        out_shape=jax.ShapeDtypeStruct((bn, t, W_A), BF16),
        scratch_shapes=[pltpu.VMEM((2, nq + 2, TILE, TILE), F32),
                        pltpu.VMEM((2, TILE, LANES), F32),
                        pltpu.VMEM((2, TILE, LANES), F32),
                        pltpu.VMEM((2, TILE, FAR_GROUP * TILE), BF16)],
        compiler_params=_params(("arbitrary", "arbitrary", "arbitrary")),
        name="moba",
    )(qa, ka, va, kmp, bias_near)


def _dsa_select(i, qi_ref, ki_ref, wi_ref, s_ref, wb_ref):
    nch = i + 1
    sub = TILE // 8
    lane = lax.broadcasted_iota(I32, (TILE, LANES), 1)
    key_in = (lax.broadcasted_iota(I32, (sub, 8, TILE), 0) * 8 + lax.broadcasted_iota(I32, (sub, 8, TILE), 1))
    qpos = i * TILE + lax.broadcasted_iota(I32, (sub, 8, TILE), 2)

    def rows8(x):
        return jnp.broadcast_to(x, (8, TILE))[None]

    w_t = wi_ref[0].T
    qi = qi_ref[0]
    per_group = LANES // IDX_DIM
    qh = []
    for h in range(IDX_HEADS):
        g, r = divmod(h, per_group)
        qg = qi[:, g * LANES:(g + 1) * LANES]
        keep = jnp.logical_and(lane >= r * IDX_DIM, lane < (r + 1) * IDX_DIM)
        qh.append(jnp.where(keep, qg, jnp.zeros_like(qg)))
        wb_ref[h] = jnp.broadcast_to(w_t[h:h + 1], (8, TILE))

    def score(c, carry):
        off = pl.multiple_of(c * TILE, TILE)
        kc = ki_ref[0, pl.ds(off, TILE), :]
        acc = jnp.zeros((sub, 8, TILE), F32)
        for h in range(IDX_HEADS):
            acc = acc + jnp.maximum(_nt(kc, qh[h]), 0.0).reshape(sub, 8, TILE) * wb_ref[h][None]
        s_ref[c] = jnp.where(c * TILE + key_in <= qpos, acc, -jnp.inf).reshape(TILE, TILE)
        return carry

    lax.fori_loop(0, nch, score, 0)

    def fold(init, step):
        return lax.fori_loop(0, nch, lambda c, acc: step(acc, s_ref[c].reshape(sub, 8, TILE), c), init)

    def count(pred):
        def step(acc, blk, c):
            return acc + jnp.sum(jnp.where(pred(blk, c * TILE + key_in), 1.0, 0.0), axis=0)
        return jnp.sum(fold(jnp.zeros((8, TILE), F32), step), axis=0, keepdims=True)

    wide = rows8

    def extreme(op, fill, reduce):
        def step(acc, blk, c):
            return op(acc, reduce(jnp.where(blk == -jnp.inf, fill, blk), axis=0))
        return reduce(fold(jnp.full((8, TILE), fill, F32), step), axis=0, keepdims=True)

    hi = extreme(jnp.maximum, -jnp.inf, jnp.max)
    lo = extreme(jnp.minimum, jnp.inf, jnp.min)

    n_pos = count(lambda half, kpos: half > 0.0)
    n_nonneg = count(lambda half, kpos: half >= 0.0)
    top = hi
    top_ties = count(lambda half, kpos: half >= wide(top)) >= DSA_TOPK
    at_zero = jnp.logical_and(n_pos < DSA_TOPK, n_nonneg >= DSA_TOPK)
    lo = jnp.where(n_pos >= DSA_TOPK, 0.0, lo)
    hi = jnp.where(n_nonneg < DSA_TOPK, 0.0, hi)

    def search_more(state):
        return jnp.logical_and(jnp.min(state[3]) == 0, state[4] < 400)

    def search(state):
        lo, hi, thr, done, it = state
        mid = 0.5 * lo + 0.5 * hi
        closed = jnp.logical_or(mid <= lo, mid >= hi)
        n = count(lambda half, kpos: half >= wide(mid))
        found = jnp.where(closed, lo, mid)
        stop = jnp.logical_or(closed, n == DSA_TOPK)
        live = done == 0
        thr = jnp.where(jnp.logical_and(live, stop), found, thr)
        go = jnp.logical_and(live, jnp.logical_not(stop))
        lo = jnp.where(jnp.logical_and(go, n > DSA_TOPK), mid, lo)
        hi = jnp.where(jnp.logical_and(go, n < DSA_TOPK), mid, hi)
        return lo, hi, thr, jnp.where(stop, 1, done), it + 1

    thr0 = jnp.where(top_ties, top, 0.0)
    done0 = jnp.logical_or(top_ties, at_zero).astype(I32)
    thr = wide(lax.while_loop(search_more, search, (lo, hi, thr0, done0, jnp.int32(0)))[2])
    n_gt = count(lambda half, kpos: half > thr)
    n_ge = count(lambda half, kpos: half >= thr)
    need = DSA_TOPK - n_gt

    cut_bits = max(1, (s_ref.shape[0] * TILE - 1).bit_length())

    def cut_step(b, cut):
        cand = cut | (jnp.int32(1) << (cut_bits - 1 - b))
        cw = wide(cand)
        n = count(lambda half, kpos: jnp.logical_and(half == thr, kpos < cw))
        return jnp.where(n < need, cand, cut)

    cut = lax.cond(jnp.max(n_ge) > DSA_TOPK,
                   lambda: lax.fori_loop(0, cut_bits, cut_step, jnp.zeros((1, TILE), I32)),
                   lambda: jnp.full((1, TILE), 2 ** 30, I32))
    cut = wide(jnp.where(n_ge > DSA_TOPK, cut, 2 ** 30))

    def to_mask(c, carry):
        blk = s_ref[c].reshape(sub, 8, TILE)
        kpos = c * TILE + key_in
        keep = jnp.logical_or(blk > thr, jnp.logical_and(blk == thr, kpos <= cut))
        mask_t = jnp.where(jnp.logical_and(keep, kpos <= qpos), 0.0, NEG).reshape(TILE, TILE)
        s_ref[c] = mask_t.T
        return carry

    lax.fori_loop(0, nch, to_mask, 0)


def _dsa_kernel(qi_ref, ki_ref, wi_ref, q_ref, k_ref, v_ref, bias_ref, o_ref,
                s_ref, wb_ref, sc_ref, mx_ref, acc_ref, stage_ref):
    i = pl.program_id(1)
    pair = pl.program_id(2)

    @pl.when(jnp.logical_and(pair == 0, i == 0))
    def _():
        row = lax.broadcasted_iota(I32, (TILE, TILE), 0)
        col = lax.broadcasted_iota(I32, (TILE, TILE), 1)
        s_ref[0] = jnp.where(col <= row, 0.0, NEG)

    @pl.when(jnp.logical_and(pair == 0, i >= 1))
    def _():
        _dsa_select(i, qi_ref, ki_ref, wi_ref, s_ref, wb_ref)

    qs = _split_pair(q_ref[0])
    nq = pl.num_programs(1)
    n_far = jnp.maximum(i - 1, 0)
    _sm_init(mx_ref, acc_ref)

    def rows(ref, j, n):
        off = pl.multiple_of(j * TILE, TILE)
        return ref[0, pl.ds(off, n * TILE), :]

    def logits(j, slot, msk, near=None):
        k_j = rows(k_ref, j, 1)
        for hd in range(2):
            s = _nt(qs[hd], k_j) + msk
            if near is not None:
                s = s + bias_ref[hd, near]
            _sm_logits(hd, slot, s, sc_ref, mx_ref)

    def far_logits(g, carry):
        for u in range(FAR_GROUP):
            j = FAR_GROUP * g + u
            logits(j, j, jnp.where(j < i - 1, s_ref[jnp.minimum(j, i)], NEG))
        return carry

    n_grp = (n_far + FAR_GROUP - 1) // FAR_GROUP
    lax.fori_loop(0, n_grp, far_logits, 0)
    pl.when(i >= 1)(lambda: logits(i - 1, nq, s_ref[i - 1], near=1))
    logits(i, nq + 1, s_ref[i], near=0)
    _sm_rowmax(mx_ref)

    _sm_far_values(n_grp, lambda j, n: rows(v_ref, j, n), sc_ref, mx_ref, acc_ref, stage_ref)

    def near_values(j, slot):
        v = rows(v_ref, j, 1)
        for hd in range(2):
            _sm_values(hd, [slot], v, sc_ref, mx_ref, acc_ref)

    pl.when(i >= 1)(lambda: near_values(i - 1, nq))
    near_values(i, nq + 1)
    o_ref[0] = _sm_output(acc_ref).astype(BF16)


def _dsa(qi, ki4, wi, qb, kb, vb, bias_near):
    bn, t, _ = qb.shape
    nq = t // TILE
    npair = B_HEADS // 2
    qrow = lambda w: pl.BlockSpec((1, TILE, w), lambda b, i, p: (b, i, 0))
    return pl.pallas_call(
        _dsa_kernel,
        grid=(bn, nq, npair),
        in_specs=[qrow(IDX_HEADS * IDX_DIM),
                  pl.BlockSpec((1, t, LANES), lambda b, i, p: (b, 0, 0), pipeline_mode=pl.Buffered(1)),
                  qrow(LANES),
                  pl.BlockSpec((1, TILE, LANES), lambda b, i, p: (b, i, p)),
                  pl.BlockSpec((1, t, LANES), lambda b, i, p: (b, 0, p)),
                  pl.BlockSpec((1, t, LANES), lambda b, i, p: (b, 0, p)),
                  pl.BlockSpec((2, 2, TILE, TILE), lambda b, i, p: (p, 0, 0, 0))],
        out_specs=pl.BlockSpec((1, TILE, LANES), lambda b, i, p: (b, i, p)),
        out_shape=jax.ShapeDtypeStruct((bn, t, W_B), BF16),
        scratch_shapes=[pltpu.VMEM((nq, TILE, TILE), F32),
                        pltpu.VMEM((IDX_HEADS, 8, TILE), F32),
                        pltpu.VMEM((2, nq + 2, TILE, TILE), F32),
                        pltpu.VMEM((2, TILE, LANES), F32),
                        pltpu.VMEM((2, TILE, LANES), F32),
                        pltpu.VMEM((2, TILE, FAR_GROUP * TILE), BF16)],
        compiler_params=_params(("arbitrary", "arbitrary", "arbitrary")),
        name="dsa",
    )(qi, ki4, wi, qb, kb, vb, bias_near)


def _post_kernel(x_ref, aa_ref, ab_ref, gm_ref, wg_ref, woa_ref, wob_ref, wout_ref,
                 gx_ref, wq_ref, kvm_ref, wox_ref, gmoe_ref, wrt_ref, brt_ref,
                 x2_ref, hm_ref, rw_ref, ids_ref, cnt_ref, carry_ref):
    step = pl.program_id(0)
    xf = x_ref[...]
    h = _rms(xf, gm_ref[...]).astype(BF16)
    gates = jax.nn.sigmoid(_mm(h, wg_ref[...]))
    oa = _mm(aa_ref[...], woa_ref[...])
    ob = _mm(ab_ref[...], wob_ref[...])
    mrg = gates[:, :D_MODEL] * oa + gates[:, D_MODEL:] * ob
    x1 = xf + _mm(mrg.astype(BF16), wout_ref[...])

    hx = _rms(x1, gx_ref[...]).astype(BF16)
    q = (_mm(hx, wq_ref[...]) * (HEAD_DIM ** -0.5)).astype(BF16)
    kvm = kvm_ref[0]
    lane = lax.broadcasted_iota(I32, (TILE, LANES), 1)
    outs = []
    for p_idx in range(X_HEADS // 2):
        lo = p_idx * LANES
        qs = _split_pair(q[:, lo:lo + LANES])
        km = kvm[:, lo:lo + LANES]
        vm = kvm[:, W_X + lo:W_X + lo + LANES]
        o = []
        for hd in range(2):
            s = _nt(qs[hd], km)
            p = jnp.exp(s - jnp.max(s, axis=1, keepdims=True))
            o.append(_mm(p.astype(BF16), vm) / jnp.sum(p, axis=1, keepdims=True))
        outs.append(jnp.where(lane < HEAD_DIM, o[0], o[1]))
    xo = jnp.concatenate(outs, axis=1).astype(BF16)
    x2 = x1 + _mm(xo, wox_ref[...])
    x2_ref[...] = x2

    hm = _rms(x2, gmoe_ref[...])
    hm_ref[...] = hm
    logits = jnp.dot(hm, wrt_ref[...], preferred_element_type=F32,
                     precision=lax.Precision.HIGHEST) + brt_ref[...]
    big = LANES

    def argmax(v):
        top = jnp.max(v, axis=1, keepdims=True)
        return top, jnp.min(jnp.where(v == top, lane, big), axis=1, keepdims=True)

    is_grp = lane < R_EXP
    gtop, gsel = argmax(jnp.where(is_grp, logits, -jnp.inf))
    gw = 1.0 / jnp.sum(jnp.where(is_grp, jnp.exp(logits - gtop), 0.0), axis=1, keepdims=True)
    first = R_EXP + gsel * EXPERTS_PER_GROUP
    inside = jnp.logical_and(lane >= first, lane < first + EXPERTS_PER_GROUP)
    within = jnp.where(inside, logits, -jnp.inf)
    v0, i0 = argmax(within)
    v1, i1 = argmax(jnp.where(lane == i0, -jnp.inf, within))
    e1 = jnp.exp(v1 - v0)
    w0 = gw * (1.0 / (1.0 + e1))
    w1 = gw * (e1 / (1.0 + e1))
    rw_ref[...] = jnp.where(lane == 0, w0, jnp.where(lane == 1, w1, 0.0))

    @pl.when(step == 0)
    def _():
        carry_ref[...] = jnp.zeros(carry_ref.shape, F32)

    hit0 = lane == i0
    hit1 = lane == i1
    onehot = jnp.where(jnp.logical_or(hit0, hit1), 1.0, 0.0)
    tri = (lax.broadcasted_iota(I32, (TILE, TILE), 1) < lax.broadcasted_iota(I32, (TILE, TILE), 0))
    base = carry_ref[...] + _mm(tri.astype(BF16), onehot.astype(BF16))
    r0 = jnp.sum(jnp.where(hit0, base, 0.0), axis=1, keepdims=True)
    r1 = jnp.sum(jnp.where(hit1, base, 0.0), axis=1, keepdims=True)
    total = carry_ref[...] + jnp.sum(onehot, axis=0, keepdims=True)
    carry_ref[...] = total
    cnt_ref[...] = total.astype(I32)
    slab = jnp.where(lane == 0, (i0 - R_EXP).astype(F32),
                     jnp.where(lane == 1, (i1 - R_EXP).astype(F32),
                               jnp.where(lane == 2, r0, jnp.where(lane == 3, r1, 0.0))))
    ids_ref[0] = slab.T[:8].astype(I32)


def _post(xf, aa, ab, kvm, tiles_per_batch, weights):
    n, d = xf.shape
    nt = n // TILE
    row = lambda w: pl.BlockSpec((TILE, w), lambda i: (i, 0))
    full = lambda a: pl.BlockSpec(a.shape, lambda i: (0,) * a.ndim)
    gm, wg, woa, wob, wout, gx, wq, wox, gmoe, wrt, brt = weights
    return pl.pallas_call(
        _post_kernel,
        grid=(nt,),
        in_specs=[row(d), row(W_A), row(W_B), full(gm), full(wg), full(woa), full(wob), full(wout),
                  full(gx), full(wq),
                  pl.BlockSpec((1,) + kvm.shape[1:], lambda i: (i // tiles_per_batch, 0, 0)),
                  full(wox), full(gmoe), full(wrt), full(brt)],
        out_specs=[row(d), row(d), row(LANES),
                   pl.BlockSpec((1, 8, TILE), lambda i: (i, 0, 0)),
                   pl.BlockSpec((1, LANES), lambda i: (0, 0))],
        out_shape=[jax.ShapeDtypeStruct((n, d), F32), jax.ShapeDtypeStruct((n, d), F32),
                   jax.ShapeDtypeStruct((n, LANES), F32),
                   jax.ShapeDtypeStruct((nt, 8, TILE), I32),
                   jax.ShapeDtypeStruct((1, LANES), I32)],
        scratch_shapes=[pltpu.VMEM((1, LANES), F32)],
        compiler_params=_params(("arbitrary",)),
        name="post",
    )(xf, aa, ab, gm, wg, woa, wob, wout, gx, wq, kvm, wox, gmoe, wrt, brt)


def _segment_starts(cnt_ref, seg_ref):
    def body(e, acc):
        seg_ref[e] = acc
        c = cnt_ref[0, R_EXP + e]
        return acc + ((c + (MOE_BLOCK - 1)) >> 8 << 8)
    return lax.fori_loop(0, N_EXPERTS, body, jnp.int32(0))


def _row_copies(ids_ref, seg_ref, k, t, rows_ref, tile_ref, sem, gather):
    dest = seg_ref[ids_ref[0, k, t]] + ids_ref[0, 2 + k, t]
    if gather:
        return pltpu.make_async_copy(rows_ref.at[pl.ds(dest, 1)], tile_ref.at[k, pl.ds(t, 1)], sem)
    return pltpu.make_async_copy(tile_ref.at[pl.ds(t, 1)], rows_ref.at[pl.ds(dest, 1)], sem)


def _dispatch_kernel(ids_ref, cnt_ref, hm_ref, xr_in_ref, xr_ref, blk_ref, seg_ref, sem):
    del xr_in_ref
    step = pl.program_id(0)
    nb = blk_ref.shape[1] - 1

    @pl.when(step == 0)
    def _():
        used = _segment_starts(cnt_ref, seg_ref)

        def per_expert(e, last):
            c = cnt_ref[0, R_EXP + e]
            b0 = seg_ref[e] >> 8
            n = (c + (MOE_BLOCK - 1)) >> 8

            def fill(kk, carry):
                blk_ref[0, b0 + kk] = e
                return carry
            lax.fori_loop(0, n, fill, 0)
            return jnp.where(n > 0, e, last)
        last = lax.fori_loop(0, N_EXPERTS, per_expert, jnp.int32(0))

        def tail(b, carry):
            blk_ref[0, b] = last
            return carry
        lax.fori_loop(used >> 8, nb, tail, 0)
        blk_ref[0, nb] = used >> 8

    def issue(t, carry):
        for k in range(2):
            _row_copies(ids_ref, seg_ref, k, t, xr_ref, hm_ref, sem, gather=False).start()
        return carry
    lax.fori_loop(0, TILE, issue, 0)

    def drain(t, carry):
        for k in range(2):
            _row_copies(ids_ref, seg_ref, k, t, xr_ref, hm_ref, sem, gather=False).wait()
        return carry
    lax.fori_loop(0, TILE, drain, 0)


def _dispatch(ids, cnt, hm, n_rows):
    n, d = hm.shape
    nt = n // TILE
    nb = n_rows // MOE_BLOCK
    zeros = jnp.zeros((n_rows, d), F32)
    return pl.pallas_call(
        _dispatch_kernel,
        grid=(nt,),
        in_specs=[pl.BlockSpec((1, 8, TILE), lambda i: (i, 0, 0), memory_space=pltpu.SMEM),
                  pl.BlockSpec(memory_space=pltpu.SMEM),
                  pl.BlockSpec((TILE, d), lambda i: (i, 0)),
                  pl.BlockSpec(memory_space=pl.ANY)],
        out_specs=[pl.BlockSpec(memory_space=pl.ANY),
                   pl.BlockSpec(memory_space=pltpu.SMEM)],
        out_shape=[jax.ShapeDtypeStruct((n_rows, d), F32),
                   jax.ShapeDtypeStruct((1, nb + 1), I32)],
        scratch_shapes=[pltpu.SMEM((N_EXPERTS,), I32), pltpu.SemaphoreType.DMA(())],
        input_output_aliases={3: 0},
        compiler_params=_params(("arbitrary",)),
        name="dispatch",
    )(ids, cnt, hm, zeros)


def _experts_kernel(blk_ref, x_ref, w1_ref, w3_ref, w2_ref, y_ref, w1b, w3b, w2b):
    b = pl.program_id(0)
    nb = pl.num_programs(0)
    e = blk_ref[0, b]
    prev = blk_ref[0, jnp.maximum(b - 1, 0)]

    @pl.when(jnp.logical_or(b == 0, e != prev))
    def _():
        w1b[...] = w1_ref[0].astype(BF16)
        w3b[...] = w3_ref[0].astype(BF16)
        w2b[...] = w2_ref[0].astype(BF16)

    used = blk_ref[0, nb]

    @pl.when(b < used)
    def _():
        xb = x_ref[...].astype(BF16)
        a = _mm(xb, w1b[...])
        g = _mm(xb, w3b[...])
        y_ref[...] = _mm((a * jax.nn.sigmoid(a) * g).astype(BF16), w2b[...])

    @pl.when(b >= used)
    def _():
        y_ref[...] = jnp.zeros(y_ref.shape, F32)


def _experts(blk, xr, w1, w3, w2):
    n_rows, d = xr.shape
    nb = n_rows // MOE_BLOCK
    wspec = lambda s: pl.BlockSpec((1,) + s, lambda b, blk: (blk[0, b], 0, 0))
    return pl.pallas_call(
        _experts_kernel,
        grid_spec=pltpu.PrefetchScalarGridSpec(
            num_scalar_prefetch=1,
            grid=(nb,),
            in_specs=[pl.BlockSpec((MOE_BLOCK, d), lambda b, blk: (b, 0)),
                      wspec((d, D_EXPERT)), wspec((d, D_EXPERT)), wspec((D_EXPERT, d))],
            out_specs=pl.BlockSpec((MOE_BLOCK, d), lambda b, blk: (b, 0)),
            scratch_shapes=[pltpu.VMEM((d, D_EXPERT), BF16), pltpu.VMEM((d, D_EXPERT), BF16),
                            pltpu.VMEM((D_EXPERT, d), BF16)]),
        out_shape=jax.ShapeDtypeStruct((n_rows, d), F32),
        compiler_params=_params(("arbitrary",)),
        name="experts",
    )(blk, xr, w1, w3, w2)


def _combine_kernel(ids_ref, cnt_ref, x2_ref, rw_ref, g_ref, yr_ref, o_ref, y_ref, seg_ref, sem):
    step = pl.program_id(0)

    @pl.when(step == 0)
    def _():
        _segment_starts(cnt_ref, seg_ref)

    def issue(t, carry):
        for k in range(2):
            _row_copies(ids_ref, seg_ref, k, t, yr_ref, y_ref, sem, gather=True).start()
        return carry
    lax.fori_loop(0, TILE, issue, 0)

    def drain(t, carry):
        for k in range(2):
            _row_copies(ids_ref, seg_ref, k, t, yr_ref, y_ref, sem, gather=True).wait()
        return carry
    lax.fori_loop(0, TILE, drain, 0)

    rw = rw_ref[...]
    x3 = x2_ref[...] + (y_ref[0] * rw[:, 0:1] + y_ref[1] * rw[:, 1:2])
    o_ref[...] = _rms(x3, g_ref[...])


def _combine(ids, cnt, x2, rw, g, yr):
    n, d = x2.shape
    nt = n // TILE
    row = lambda w: pl.BlockSpec((TILE, w), lambda i: (i, 0))
    return pl.pallas_call(
        _combine_kernel,
        grid=(nt,),
        in_specs=[pl.BlockSpec((1, 8, TILE), lambda i: (i, 0, 0), memory_space=pltpu.SMEM),
                  pl.BlockSpec(memory_space=pltpu.SMEM),
                  row(d), row(LANES), pl.BlockSpec((1, d), lambda i: (0, 0)),
                  pl.BlockSpec(memory_space=pl.ANY)],
        out_specs=row(d),
        out_shape=jax.ShapeDtypeStruct((n, d), F32),
        scratch_shapes=[pltpu.VMEM((2, TILE, d), F32), pltpu.SMEM((N_EXPERTS,), I32),
                        pltpu.SemaphoreType.DMA(())],
        compiler_params=_params(("arbitrary",)),
        name="combine",
    )(ids, cnt, x2, rw, g, yr)


def _t5_bucket(dist):
    n = jnp.maximum(dist, 0)
    max_exact = REL_BUCKETS // 2
    nf = jnp.maximum(n, 1).astype(F32)
    large = max_exact + (jnp.log(nf / max_exact) / math.log(REL_MAX_DIST / max_exact)
                         * (REL_BUCKETS - max_exact)).astype(I32)
    large = jnp.minimum(large, REL_BUCKETS - 1)
    return jnp.where(n < max_exact, n, large)


def _bias_tables(tab):
    r = jnp.arange(TILE)[:, None]
    c = jnp.arange(TILE)[None, :]
    rel = tab - tab[:, REL_BUCKETS - 1:]

    def tile(dist):
        onehot = (_t5_bucket(dist)[..., None] == jnp.arange(REL_BUCKETS)).astype(F32)
        return jnp.einsum('rcb,hb->hrc', onehot, rel, precision=lax.Precision.HIGHEST)

    own = jnp.where(r >= c, tile(r - c), NEG)
    return jnp.stack([own, tile(r - c + TILE)], axis=1).astype(F32)


def _block_mean_slabs(kmean, bn, nq):
    km = kmean.reshape(bn, nq, A_HEADS, HEAD_DIM).transpose(0, 2, 1, 3)
    pad = ((0, 0), (0, 0), (HEAD_DIM, LANES - HEAD_DIM - nq), (0, LANES - HEAD_DIM))
    return jnp.pad(km, pad).astype(BF16)


def _pad_cols(w, width):
    return jnp.pad(w, ((0, 0), (0, width - w.shape[1])))


def kernel(x, mem, rel_bias, final_norm, norm_mix, w_in, ckv_norm, w_uk, w_uv, w_oa, w_ob,
           w_out, norm_x, mem_norm, wq_x, wk_x, wv_x, wo_x, norm_moe, w_group, b_group,
           w_router, b_router, w1, w3, w2):
    bn, t, d = x.shape
    n = bn * t
    nq = t // TILE
    assert t % TILE == 0 and nq <= LANES - HEAD_DIM and norm_mix.shape[0] == 1
    near_a = _bias_tables(rel_bias[:, :A_HEADS].T)
    near_b = _bias_tables(rel_bias[:, A_HEADS:].T)
    row = lambda v: v.reshape(1, -1).astype(F32)

    wi = w_in[0]
    wp = jnp.concatenate(
        [wi[:, OFF_AQ:OFF_IK], jnp.tile(wi[:, OFF_IK:OFF_IW], (1, LANES // IDX_DIM)),
         _pad_cols(wi[:, OFF_IW:OFF_GA], LANES)], axis=1).astype(BF16)
    wkv = jnp.concatenate([w_uk[0], w_uv[0]], axis=1).astype(BF16)
    xf = x.reshape(n, d)
    qa, ka, va, qb, kb, vb, qi, ki4, widx, kmean = _proj_in(
        xf, row(norm_mix[0]), wp, row(ckv_norm[0]), wkv, nq)

    b3 = lambda a: a.reshape(bn, t, a.shape[-1])
    attn_a = _moba(b3(qa), b3(ka), b3(va), _block_mean_slabs(kmean, bn, nq), near_a)
    attn_b = _dsa(b3(qi), b3(ki4), b3(widx), b3(qb), b3(kb), b3(vb), near_b)

    kvm = _mem_kv(mem, row(mem_norm[0]), jnp.concatenate([wk_x[0], wv_x[0]], axis=1).astype(BF16))
    wrt = _pad_cols(jnp.concatenate([w_group[0], w_router[0]], axis=1), LANES).astype(F32)
    brt = _pad_cols(jnp.concatenate([b_group[0], b_router[0]]).reshape(1, -1), LANES).astype(F32)
    weights = (row(norm_mix[0]), wi[:, OFF_GA:IN_COLS].astype(BF16), w_oa[0].astype(BF16),
               w_ob[0].astype(BF16), w_out[0].astype(BF16), row(norm_x[0]), wq_x[0].astype(BF16),
               wo_x[0].astype(BF16), row(norm_moe[0]), wrt, brt)
    x2, hm, rw, ids, cnt = _post(xf, attn_a.reshape(n, W_A), attn_b.reshape(n, W_B), kvm, t // TILE, weights)

    n_rows = 2 * n + N_EXPERTS * MOE_BLOCK
    xr, blk = _dispatch(ids, cnt, hm, n_rows)
    yr = _experts(blk, xr, w1[0], w3[0], w2[0])
    out = _combine(ids, cnt, x2, rw, row(final_norm), yr)
    return out.reshape(bn, t, d)
```

```python
import functools
import math

import jax
import jax.numpy as jnp
from jax import lax
from jax.experimental import pallas as pl
from jax.experimental.pallas import tpu as pltpu

D_MODEL = 1024
HEAD_DIM = 64
A_HEADS = 8
B_HEADS = 8
W_A = A_HEADS * HEAD_DIM
W_B = B_HEADS * HEAD_DIM
MOBA_BLOCK = 256
MOBA_TOPK = 3
DSA_KV_RANK = 256
IDX_HEADS = 8
IDX_DIM = 32
DSA_TOPK = 256
REL_BUCKETS = 32
REL_MAX_DIST = 128
X_HEADS = 4
W_X = X_HEADS * HEAD_DIM
N_GROUPS = 4
EXPERTS_PER_GROUP = 8
N_EXPERTS = N_GROUPS * EXPERTS_PER_GROUP
D_EXPERT = 512
MOE_BLOCK = 256
EPS = 1e-6
NEG = -1e30

OFF_AQ = 0
OFF_AK = OFF_AQ + W_A
OFF_AV = OFF_AK + W_A
OFF_BQ = OFF_AV + W_A
OFF_CKV = OFF_BQ + W_B
OFF_IQ = OFF_CKV + DSA_KV_RANK
OFF_IK = OFF_IQ + IDX_HEADS * IDX_DIM
OFF_IW = OFF_IK + IDX_DIM
OFF_GA = OFF_IW + IDX_HEADS
OFF_GB = OFF_GA + D_MODEL
IN_COLS = OFF_GB + D_MODEL

LANES = 128
TILE = 256
VMEM_LIMIT = 56 * 1024 * 1024
BF16 = jnp.bfloat16
F32 = jnp.float32
I32 = jnp.int32

P_QA, P_KA, P_VA, P_QB = 0, 512, 1024, 1536
P_CKV, P_IQ, P_IK4, P_IW = 2048, 2304, 2560, 2688
P_COLS = 2816

R_GRP = 0
R_EXP = N_GROUPS


def _nt(a, b):
    return lax.dot_general(a, b, (((1,), (1,)), ((), ())), preferred_element_type=F32)


def _mm(a, b):
    return jnp.dot(a, b, preferred_element_type=F32)


def _rms(xf, g):
    return xf * lax.rsqrt(jnp.mean(xf * xf, axis=-1, keepdims=True) + EPS) * g


def _params(sem, vmem=VMEM_LIMIT):
    return pltpu.CompilerParams(dimension_semantics=sem, vmem_limit_bytes=vmem)


def _mem_kv_kernel(mem_ref, g_ref, w_ref, o_ref):
    m = _rms(mem_ref[0], g_ref[...]).astype(BF16)
    o_ref[0] = _mm(m, w_ref[...]).astype(BF16)


def _mem_kv(mem, g, wkv):
    bn, ml, d = mem.shape
    return pl.pallas_call(
        _mem_kv_kernel,
        grid=(bn,),
        in_specs=[pl.BlockSpec((1, ml, d), lambda b: (b, 0, 0)),
                  pl.BlockSpec((1, d), lambda b: (0, 0)),
                  pl.BlockSpec((d, 2 * W_X), lambda b: (0, 0))],
        out_specs=pl.BlockSpec((1, ml, 2 * W_X), lambda b: (b, 0, 0)),
        out_shape=jax.ShapeDtypeStruct((bn, ml, 2 * W_X), BF16),
        compiler_params=_params(("arbitrary",)),
        name="mem_kv",
    )(mem, g, wkv)


def _head_slabs(zp, extra):
    lane = lax.broadcasted_iota(I32, zp.shape, 1)
    low = lane < HEAD_DIM
    return jnp.where(low, zp, extra), jnp.where(low, pltpu.roll(zp, HEAD_DIM, 1), extra)


def _proj_in_kernel(blocks_per_seq, x_ref, g_ref, w_ref, cg_ref, wkv_ref,
                    qa_ref, ka_ref, va_ref, qb_ref, kb_ref, vb_ref, qi_ref, ki_ref, wi_ref, km_ref):
    h = _rms(x_ref[...], g_ref[...]).astype(BF16)
    z = _mm(h, w_ref[...])
    scale = HEAD_DIM ** -0.5
    zk = z[:, P_KA:P_VA]
    km_ref[0] = jnp.mean(zk, axis=0, keepdims=True)
    lane = lax.broadcasted_iota(I32, (TILE, LANES), 1)
    blk = pl.program_id(0) % blocks_per_seq
    onehot = jnp.where(lane == HEAD_DIM + blk, 1.0, 0.0)
    zero = jnp.zeros((TILE, LANES), F32)
    for p_idx in range(A_HEADS // 2):
        lo = p_idx * LANES
        qe, qo = _head_slabs(z[:, P_QA + lo:P_QA + lo + LANES] * scale, zero)
        ke, ko = _head_slabs(zk[:, lo:lo + LANES], onehot)
        qa_ref[:, 2 * lo:2 * lo + 2 * LANES] = jnp.concatenate([qe, qo], axis=1).astype(BF16)
        ka_ref[:, 2 * lo:2 * lo + 2 * LANES] = jnp.concatenate([ke, ko], axis=1).astype(BF16)
    va_ref[...] = z[:, P_VA:P_QB].astype(BF16)
    qb_ref[...] = (z[:, P_QB:P_CKV] * scale).astype(BF16)
    ckv = _rms(z[:, P_CKV:P_IQ], cg_ref[...]).astype(BF16)
    kv = _mm(ckv, wkv_ref[...])
    kb_ref[...] = kv[:, :W_B].astype(BF16)
    vb_ref[...] = kv[:, W_B:].astype(BF16)
    qi_ref[...] = z[:, P_IQ:P_IK4].astype(BF16)
    ki_ref[...] = z[:, P_IK4:P_IW].astype(BF16)
    wi_ref[...] = z[:, P_IW:P_COLS] * ((IDX_HEADS ** -0.5) * (IDX_DIM ** -0.5))


def _proj_in(xf, g, wp, cg, wkv, blocks_per_seq):
    n, d = xf.shape
    nt = n // TILE
    row = lambda w: pl.BlockSpec((TILE, w), lambda i: (i, 0))
    full = lambda a: pl.BlockSpec(a.shape, lambda i: (0,) * a.ndim)
    outs = [(2 * W_A, BF16), (2 * W_A, BF16), (W_A, BF16), (W_B, BF16), (W_B, BF16), (W_B, BF16),
            (IDX_HEADS * IDX_DIM, BF16), (LANES, BF16), (LANES, F32)]
    return pl.pallas_call(
        functools.partial(_proj_in_kernel, blocks_per_seq),
        grid=(nt,),
        in_specs=[row(d), full(g), full(wp), full(cg), full(wkv)],
        out_specs=[row(w) for w, _ in outs] + [pl.BlockSpec((1, 1, W_A), lambda i: (i, 0, 0))],
        out_shape=[jax.ShapeDtypeStruct((n, w), t) for w, t in outs]
        + [jax.ShapeDtypeStruct((nt, 1, W_A), F32)],
        compiler_params=_params(("arbitrary",)),
        name="proj_in",
    )(xf, g, wp, cg, wkv)


def _sm_init(mx_ref, acc_ref):
    mx_ref[...] = jnp.full(mx_ref.shape, NEG, F32)
    acc_ref[...] = jnp.zeros(acc_ref.shape, F32)


def _sm_logits(hd, slot, s, sc_ref, mx_ref):
    sc_ref[hd, slot] = s
    mx_ref[hd] = jnp.maximum(mx_ref[hd], jnp.maximum(s[:, :LANES], s[:, LANES:]))


def _sm_rowmax(mx_ref):
    for hd in range(2):
        mx_ref[hd] = jnp.broadcast_to(jnp.max(mx_ref[hd], axis=1, keepdims=True), (TILE, LANES))


def _sm_probs(hd, slots, sc_ref, mx_ref):
    m = mx_ref[hd]
    m2 = jnp.concatenate([m, m], axis=1)
    ps = [jnp.exp((sc_ref[hd, sl] - m2).astype(BF16)) for sl in slots]
    return ps[0] if len(ps) == 1 else jnp.concatenate(ps, axis=1)


def _sm_accumulate(hd, pb, v, acc_ref):
    lane = lax.broadcasted_iota(I32, v.shape, 1)
    own = (lane < HEAD_DIM) if hd == 0 else (lane >= HEAD_DIM)
    acc_ref[hd] = acc_ref[hd] + _mm(pb, jnp.where(own, v, jnp.ones_like(v)))


def _sm_values(hd, slots, v, sc_ref, mx_ref, acc_ref):
    _sm_accumulate(hd, _sm_probs(hd, slots, sc_ref, mx_ref), v, acc_ref)


FAR_GROUP = 4


def _sm_far_values(n_grp, vals, sc_ref, mx_ref, acc_ref, stage_ref):
    def stage(g):
        slots = [FAR_GROUP * g + u for u in range(FAR_GROUP)]
        return [_sm_probs(hd, slots, sc_ref, mx_ref) for hd in range(2)]

    def put(ps):
        for hd in range(2):
            stage_ref[hd] = ps[hd]

    pl.when(n_grp > 0)(lambda: put(stage(0)))

    def body(g, carry):
        v = vals(FAR_GROUP * g, FAR_GROUP)
        cur = [stage_ref[hd] for hd in range(2)]
        nxt = stage(jnp.minimum(g + 1, n_grp - 1))
        for hd in range(2):
            _sm_accumulate(hd, cur[hd], v, acc_ref)
        put(nxt)
        return carry

    lax.fori_loop(0, n_grp, body, 0)


def _sm_output(acc_ref):
    lane = lax.broadcasted_iota(I32, (TILE, LANES), 1)
    out = [acc_ref[hd] / pltpu.roll(acc_ref[hd], HEAD_DIM, 1) for hd in range(2)]
    return jnp.where(lane < HEAD_DIM, out[0], out[1])


def _split_pair(qp):
    lane = lax.broadcasted_iota(I32, qp.shape, 1)
    zero = jnp.zeros_like(qp)
    return jnp.where(lane < HEAD_DIM, qp, zero), jnp.where(lane >= HEAD_DIM, qp, zero)


def _moba_kernel(q_ref, k_ref, v_ref, km_ref, bias_ref, o_ref, sc_ref, mx_ref, acc_ref, stage_ref):
    i = pl.program_id(2)
    nq = pl.num_programs(2)
    slot_prev, slot_own = nq, nq + 1
    n_slot = LANES - HEAD_DIM
    blk = lax.broadcasted_iota(I32, (n_slot, TILE), 0)
    in_blk = blk < nq
    past = blk < i
    no_mask = jnp.zeros((HEAD_DIM, TILE), F32)

    qs, q_far, q_prev = [], [], []
    for hd in range(2):
        q = q_ref[0, :, hd * LANES:(hd + 1) * LANES]
        g = jnp.where(past, _nt(km_ref[0, hd], q)[HEAD_DIM:], -jnp.inf)
        sel = jnp.zeros((n_slot, TILE), jnp.bool_)
        for _ in range(MOBA_TOPK):
            top = jnp.max(g, axis=0, keepdims=True)
            hit = jnp.logical_and(g == top, in_blk)
            first = jnp.min(jnp.where(hit, blk, n_slot), axis=0, keepdims=True)
            pick = blk == first
            sel = jnp.logical_or(sel, jnp.logical_and(pick, past))
            g = jnp.where(pick, -jnp.inf, g)
        addm = jnp.where(sel, 0.0, NEG)
        far_m = jnp.where(blk < i - 1, addm, jnp.where(in_blk, NEG, 0.0))
        prev_m = jnp.where(blk == i - 1, addm, 0.0)
        qs.append(q)
        q_far.append(q + jnp.concatenate([no_mask, far_m], axis=0).T.astype(BF16))
        q_prev.append(q + jnp.concatenate([no_mask, prev_m], axis=0).T.astype(BF16))

    def keys(j, hd):
        off = pl.multiple_of(j * TILE, TILE)
        return k_ref[0, pl.ds(off, TILE), hd * LANES:(hd + 1) * LANES]

    n_far = jnp.maximum(i - 1, 0)
    _sm_init(mx_ref, acc_ref)

    def far_logits(g, carry):
        for u in range(FAR_GROUP):
            j = FAR_GROUP * g + u
            for hd in range(2):
                _sm_logits(hd, j, _nt(q_far[hd], keys(j, hd)), sc_ref, mx_ref)
        return carry

    n_grp = (n_far + FAR_GROUP - 1) // FAR_GROUP
    lax.fori_loop(0, n_grp, far_logits, 0)

    @pl.when(i >= 1)
    def _():
        for hd in range(2):
            _sm_logits(hd, slot_prev, _nt(q_prev[hd], keys(i - 1, hd)) + bias_ref[hd, 1], sc_ref, mx_ref)

    for hd in range(2):
        _sm_logits(hd, slot_own, _nt(qs[hd], keys(i, hd)) + bias_ref[hd, 0], sc_ref, mx_ref)
    _sm_rowmax(mx_ref)

    def vals(j, n):
        off = pl.multiple_of(j * TILE, TILE)
        return v_ref[0, pl.ds(off, n * TILE), :]

    _sm_far_values(n_grp, vals, sc_ref, mx_ref, acc_ref, stage_ref)

    @pl.when(i >= 1)
    def _():
        for hd in range(2):
            _sm_values(hd, [slot_prev], vals(i - 1, 1), sc_ref, mx_ref, acc_ref)

    for hd in range(2):
        _sm_values(hd, [slot_own], vals(i, 1), sc_ref, mx_ref, acc_ref)
    o_ref[0] = _sm_output(acc_ref).astype(BF16)


def _moba(qa, ka, va, kmp, bias_near):
    bn, t, _ = va.shape
    nq = t // TILE
    npair = A_HEADS // 2
    return pl.pallas_call(
        _moba_kernel,
        grid=(bn, npair, nq),
        in_specs=[pl.BlockSpec((1, TILE, 2 * LANES), lambda b, p, i: (b, i, p)),
                  pl.BlockSpec((1, t, 2 * LANES), lambda b, p, i: (b, 0, p)),
                  pl.BlockSpec((1, t, LANES), lambda b, p, i: (b, 0, p)),
                  pl.BlockSpec((1, 2, LANES, LANES), lambda b, p, i: (b, p, 0, 0)),
                  pl.BlockSpec((2, 2, TILE, TILE), lambda b, p, i: (p, 0, 0, 0))],
        out_specs=pl.BlockSpec((1, TILE, LANES), lambda b, p, i: (b, i, p)),
        out_shape=jax.ShapeDtypeStruct((bn, t, W_A), BF16),
        scratch_shapes=[pltpu.VMEM((2, nq + 2, TILE, TILE), F32),
                        pltpu.VMEM((2, TILE, LANES), F32),
                        pltpu.VMEM((2, TILE, LANES), F32),
                        pltpu.VMEM((2, TILE, FAR_GROUP * TILE), BF16)],
        compiler_params=_params(("arbitrary", "arbitrary", "arbitrary")),
        name="moba",
    )(qa, ka, va, kmp, bias_near)


def _dsa_select(i, qi_ref, ki_ref, wi_ref, s_ref, wb_ref):
    nch = i + 1
    sub = TILE // 8
    lane = lax.broadcasted_iota(I32, (TILE, LANES), 1)
    key_in = (lax.broadcasted_iota(I32, (sub, 8, TILE), 0) * 8 + lax.broadcasted_iota(I32, (sub, 8, TILE), 1))
    qpos = i * TILE + lax.broadcasted_iota(I32, (sub, 8, TILE), 2)

    def rows8(x):
        return jnp.broadcast_to(x, (8, TILE))[None]

    w_t = wi_ref[0].T
    qi = qi_ref[0]
    per_group = LANES // IDX_DIM
    qh = []
    for h in range(IDX_HEADS):
        g, r = divmod(h, per_group)
        qg = qi[:, g * LANES:(g + 1) * LANES]
        keep = jnp.logical_and(lane >= r * IDX_DIM, lane < (r + 1) * IDX_DIM)
        qh.append(jnp.where(keep, qg, jnp.zeros_like(qg)))
        wb_ref[h] = jnp.broadcast_to(w_t[h:h + 1], (8, TILE))

    def score(c, carry):
        off = pl.multiple_of(c * TILE, TILE)
        kc = ki_ref[0, pl.ds(off, TILE), :]
        acc = jnp.zeros((sub, 8, TILE), F32)
        for h in range(IDX_HEADS):
            acc = acc + jnp.maximum(_nt(kc, qh[h]), 0.0).reshape(sub, 8, TILE) * wb_ref[h][None]
        s_ref[c] = jnp.where(c * TILE + key_in <= qpos, acc, -jnp.inf).reshape(TILE, TILE)
        return carry

    lax.fori_loop(0, nch, score, 0)

    def fold(init, step):
        return lax.fori_loop(0, nch, lambda c, acc: step(acc, s_ref[c].reshape(sub, 8, TILE), c), init)

    def count(pred):
        def step(acc, blk, c):
            return acc + jnp.sum(jnp.where(pred(blk, c * TILE + key_in), 1.0, 0.0), axis=0)
        return jnp.sum(fold(jnp.zeros((8, TILE), F32), step), axis=0, keepdims=True)

    wide = rows8

    def extreme(op, fill, reduce):
        def step(acc, blk, c):
            return op(acc, reduce(jnp.where(blk == -jnp.inf, fill, blk), axis=0))
        return reduce(fold(jnp.full((8, TILE), fill, F32), step), axis=0, keepdims=True)

    hi = extreme(jnp.maximum, -jnp.inf, jnp.max)
    lo = extreme(jnp.minimum, jnp.inf, jnp.min)

    n_pos = count(lambda half, kpos: half > 0.0)
    n_nonneg = count(lambda half, kpos: half >= 0.0)
    top = hi
    top_ties = count(lambda half, kpos: half >= wide(top)) >= DSA_TOPK
    at_zero = jnp.logical_and(n_pos < DSA_TOPK, n_nonneg >= DSA_TOPK)
    lo = jnp.where(n_pos >= DSA_TOPK, 0.0, lo)
    hi = jnp.where(n_nonneg < DSA_TOPK, 0.0, hi)

    def search_more(state):
        return jnp.logical_and(jnp.min(state[3]) == 0, state[4] < 400)

    def search(state):
        lo, hi, thr, done, it = state
        mid = 0.5 * lo + 0.5 * hi
        closed = jnp.logical_or(mid <= lo, mid >= hi)
        n = count(lambda half, kpos: half >= wide(mid))
        found = jnp.where(closed, lo, mid)
        stop = jnp.logical_or(closed, n == DSA_TOPK)
        live = done == 0
        thr = jnp.where(jnp.logical_and(live, stop), found, thr)
        go = jnp.logical_and(live, jnp.logical_not(stop))
        lo = jnp.where(jnp.logical_and(go, n > DSA_TOPK), mid, lo)
        hi = jnp.where(jnp.logical_and(go, n < DSA_TOPK), mid, hi)
        return lo, hi, thr, jnp.where(stop, 1, done), it + 1

    thr0 = jnp.where(top_ties, top, 0.0)
    done0 = jnp.logical_or(top_ties, at_zero).astype(I32)
    state = (lo, hi, thr0, done0, jnp.int32(0))
    thr = wide(lax.while_loop(search_more, lambda st: search(search(st)), state)[2])
    n_gt = count(lambda half, kpos: half > thr)
    n_ge = count(lambda half, kpos: half >= thr)
    need = DSA_TOPK - n_gt

    cut_bits = max(1, (s_ref.shape[0] * TILE - 1).bit_length())

    def cut_step(b, cut):
        cand = cut | (jnp.int32(1) << (cut_bits - 1 - b))
        cw = wide(cand)
        n = count(lambda half, kpos: jnp.logical_and(half == thr, kpos < cw))
        return jnp.where(n < need, cand, cut)

    cut = lax.cond(jnp.max(n_ge) > DSA_TOPK,
                   lambda: lax.fori_loop(0, cut_bits, cut_step, jnp.zeros((1, TILE), I32)),
                   lambda: jnp.full((1, TILE), 2 ** 30, I32))
    cut = wide(jnp.where(n_ge > DSA_TOPK, cut, 2 ** 30))

    def to_mask(c, carry):
        blk = s_ref[c].reshape(sub, 8, TILE)
        kpos = c * TILE + key_in
        keep = jnp.logical_or(blk > thr, jnp.logical_and(blk == thr, kpos <= cut))
        mask_t = jnp.where(jnp.logical_and(keep, kpos <= qpos), 0.0, NEG).reshape(TILE, TILE)
        s_ref[c] = mask_t.T
        return carry

    lax.fori_loop(0, nch, to_mask, 0)


def _dsa_kernel(qi_ref, ki_ref, wi_ref, q_ref, k_ref, v_ref, bias_ref, o_ref,
                s_ref, wb_ref, sc_ref, mx_ref, acc_ref, stage_ref):
    i = pl.program_id(1)
    pair = pl.program_id(2)

    @pl.when(jnp.logical_and(pair == 0, i == 0))
    def _():
        row = lax.broadcasted_iota(I32, (TILE, TILE), 0)
        col = lax.broadcasted_iota(I32, (TILE, TILE), 1)
        s_ref[0] = jnp.where(col <= row, 0.0, NEG)

    @pl.when(jnp.logical_and(pair == 0, i >= 1))
    def _():
        _dsa_select(i, qi_ref, ki_ref, wi_ref, s_ref, wb_ref)

    qs = _split_pair(q_ref[0])
    nq = pl.num_programs(1)
    n_far = jnp.maximum(i - 1, 0)
    _sm_init(mx_ref, acc_ref)

    def rows(ref, j, n):
        off = pl.multiple_of(j * TILE, TILE)
        return ref[0, pl.ds(off, n * TILE), :]

    def logits(j, slot, msk, near=None):
        k_j = rows(k_ref, j, 1)
        for hd in range(2):
            s = _nt(qs[hd], k_j) + msk
            if near is not None:
                s = s + bias_ref[hd, near]
            _sm_logits(hd, slot, s, sc_ref, mx_ref)

    def far_logits(g, carry):
        for u in range(FAR_GROUP):
            j = FAR_GROUP * g + u
            logits(j, j, jnp.where(j < i - 1, s_ref[jnp.minimum(j, i)], NEG))
        return carry

    n_grp = (n_far + FAR_GROUP - 1) // FAR_GROUP
    lax.fori_loop(0, n_grp, far_logits, 0)
    pl.when(i >= 1)(lambda: logits(i - 1, nq, s_ref[i - 1], near=1))
    logits(i, nq + 1, s_ref[i], near=0)
    _sm_rowmax(mx_ref)

    _sm_far_values(n_grp, lambda j, n: rows(v_ref, j, n), sc_ref, mx_ref, acc_ref, stage_ref)

    def near_values(j, slot):
        v = rows(v_ref, j, 1)
        for hd in range(2):
            _sm_values(hd, [slot], v, sc_ref, mx_ref, acc_ref)

    pl.when(i >= 1)(lambda: near_values(i - 1, nq))
    near_values(i, nq + 1)
    o_ref[0] = _sm_output(acc_ref).astype(BF16)


def _dsa(qi, ki4, wi, qb, kb, vb, bias_near):
    bn, t, _ = qb.shape
    nq = t // TILE
    npair = B_HEADS // 2
    qrow = lambda w: pl.BlockSpec((1, TILE, w), lambda b, i, p: (b, i, 0))
    return pl.pallas_call(
        _dsa_kernel,
        grid=(bn, nq, npair),
        in_specs=[qrow(IDX_HEADS * IDX_DIM),
                  pl.BlockSpec((1, t, LANES), lambda b, i, p: (b, 0, 0), pipeline_mode=pl.Buffered(1)),
                  qrow(LANES),
                  pl.BlockSpec((1, TILE, LANES), lambda b, i, p: (b, i, p)),
                  pl.BlockSpec((1, t, LANES), lambda b, i, p: (b, 0, p)),
                  pl.BlockSpec((1, t, LANES), lambda b, i, p: (b, 0, p)),
                  pl.BlockSpec((2, 2, TILE, TILE), lambda b, i, p: (p, 0, 0, 0))],
        out_specs=pl.BlockSpec((1, TILE, LANES), lambda b, i, p: (b, i, p)),
        out_shape=jax.ShapeDtypeStruct((bn, t, W_B), BF16),
        scratch_shapes=[pltpu.VMEM((nq, TILE, TILE), F32),
                        pltpu.VMEM((IDX_HEADS, 8, TILE), F32),
                        pltpu.VMEM((2, nq + 2, TILE, TILE), F32),
                        pltpu.VMEM((2, TILE, LANES), F32),
                        pltpu.VMEM((2, TILE, LANES), F32),
                        pltpu.VMEM((2, TILE, FAR_GROUP * TILE), BF16)],
        compiler_params=_params(("arbitrary", "arbitrary", "arbitrary")),
        name="dsa",
    )(qi, ki4, wi, qb, kb, vb, bias_near)


def _post_kernel(x_ref, aa_ref, ab_ref, gm_ref, wg_ref, woa_ref, wob_ref, wout_ref,
                 gx_ref, wq_ref, kvm_ref, wox_ref, gmoe_ref, wrt_ref, brt_ref,
                 x2_ref, hm_ref, rw_ref, ids_ref, cnt_ref, carry_ref):
    step = pl.program_id(0)
    xf = x_ref[...]
    h = _rms(xf, gm_ref[...]).astype(BF16)
    gates = jax.nn.sigmoid(_mm(h, wg_ref[...]))
    oa = _mm(aa_ref[...], woa_ref[...])
    ob = _mm(ab_ref[...], wob_ref[...])
    mrg = gates[:, :D_MODEL] * oa + gates[:, D_MODEL:] * ob
    x1 = xf + _mm(mrg.astype(BF16), wout_ref[...])

    hx = _rms(x1, gx_ref[...]).astype(BF16)
    q = (_mm(hx, wq_ref[...]) * (HEAD_DIM ** -0.5)).astype(BF16)
    kvm = kvm_ref[0]
    lane = lax.broadcasted_iota(I32, (TILE, LANES), 1)
    outs = []
    for p_idx in range(X_HEADS // 2):
        lo = p_idx * LANES
        qs = _split_pair(q[:, lo:lo + LANES])
        km = kvm[:, lo:lo + LANES]
        vm = kvm[:, W_X + lo:W_X + lo + LANES]
        o = []
        for hd in range(2):
            s = _nt(qs[hd], km)
            p = jnp.exp(s - jnp.max(s, axis=1, keepdims=True))
            o.append(_mm(p.astype(BF16), vm) / jnp.sum(p, axis=1, keepdims=True))
        outs.append(jnp.where(lane < HEAD_DIM, o[0], o[1]))
    xo = jnp.concatenate(outs, axis=1).astype(BF16)
    x2 = x1 + _mm(xo, wox_ref[...])
    x2_ref[...] = x2

    hm = _rms(x2, gmoe_ref[...])
    hm_ref[...] = hm
    logits = jnp.dot(hm, wrt_ref[...], preferred_element_type=F32,
                     precision=lax.Precision.HIGHEST) + brt_ref[...]
    big = LANES

    def argmax(v):
        top = jnp.max(v, axis=1, keepdims=True)
        return top, jnp.min(jnp.where(v == top, lane, big), axis=1, keepdims=True)

    is_grp = lane < R_EXP
    gtop, gsel = argmax(jnp.where(is_grp, logits, -jnp.inf))
    gw = 1.0 / jnp.sum(jnp.where(is_grp, jnp.exp(logits - gtop), 0.0), axis=1, keepdims=True)
    first = R_EXP + gsel * EXPERTS_PER_GROUP
    inside = jnp.logical_and(lane >= first, lane < first + EXPERTS_PER_GROUP)
    within = jnp.where(inside, logits, -jnp.inf)
    v0, i0 = argmax(within)
    v1, i1 = argmax(jnp.where(lane == i0, -jnp.inf, within))
    e1 = jnp.exp(v1 - v0)
    w0 = gw * (1.0 / (1.0 + e1))
    w1 = gw * (e1 / (1.0 + e1))
    rw_ref[...] = jnp.where(lane == 0, w0, jnp.where(lane == 1, w1, 0.0))

    @pl.when(step == 0)
    def _():
        carry_ref[...] = jnp.zeros(carry_ref.shape, F32)

    hit0 = lane == i0
    hit1 = lane == i1
    onehot = jnp.where(jnp.logical_or(hit0, hit1), 1.0, 0.0)
    tri = (lax.broadcasted_iota(I32, (TILE, TILE), 1) < lax.broadcasted_iota(I32, (TILE, TILE), 0))
    base = carry_ref[...] + _mm(tri.astype(BF16), onehot.astype(BF16))
    r0 = jnp.sum(jnp.where(hit0, base, 0.0), axis=1, keepdims=True)
    r1 = jnp.sum(jnp.where(hit1, base, 0.0), axis=1, keepdims=True)
    total = carry_ref[...] + jnp.sum(onehot, axis=0, keepdims=True)
    carry_ref[...] = total
    cnt_ref[...] = total.astype(I32)
    slab = jnp.where(lane == 0, (i0 - R_EXP).astype(F32),
                     jnp.where(lane == 1, (i1 - R_EXP).astype(F32),
                               jnp.where(lane == 2, r0, jnp.where(lane == 3, r1, 0.0))))
    ids_ref[0] = slab.T[:8].astype(I32)


def _post(xf, aa, ab, kvm, tiles_per_batch, weights):
    n, d = xf.shape
    nt = n // TILE
    row = lambda w: pl.BlockSpec((TILE, w), lambda i: (i, 0))
    full = lambda a: pl.BlockSpec(a.shape, lambda i: (0,) * a.ndim)
    gm, wg, woa, wob, wout, gx, wq, wox, gmoe, wrt, brt = weights
    return pl.pallas_call(
        _post_kernel,
        grid=(nt,),
        in_specs=[row(d), row(W_A), row(W_B), full(gm), full(wg), full(woa), full(wob), full(wout),
                  full(gx), full(wq),
                  pl.BlockSpec((1,) + kvm.shape[1:], lambda i: (i // tiles_per_batch, 0, 0)),
                  full(wox), full(gmoe), full(wrt), full(brt)],
        out_specs=[row(d), row(d), row(LANES),
                   pl.BlockSpec((1, 8, TILE), lambda i: (i, 0, 0)),
                   pl.BlockSpec((1, LANES), lambda i: (0, 0))],
        out_shape=[jax.ShapeDtypeStruct((n, d), F32), jax.ShapeDtypeStruct((n, d), F32),
                   jax.ShapeDtypeStruct((n, LANES), F32),
                   jax.ShapeDtypeStruct((nt, 8, TILE), I32),
                   jax.ShapeDtypeStruct((1, LANES), I32)],
        scratch_shapes=[pltpu.VMEM((1, LANES), F32)],
        compiler_params=_params(("arbitrary",)),
        name="post",
    )(xf, aa, ab, gm, wg, woa, wob, wout, gx, wq, kvm, wox, gmoe, wrt, brt)


def _segment_starts(cnt_ref, seg_ref):
    def body(e, acc):
        seg_ref[e] = acc
        c = cnt_ref[0, R_EXP + e]
        return acc + ((c + (MOE_BLOCK - 1)) >> 8 << 8)
    return lax.fori_loop(0, N_EXPERTS, body, jnp.int32(0))


def _start_row_copies(ids_ref, seg_ref, rows_ref, tile_ref, sem, gather):
    def issue(t, carry):
        for k in range(2):
            dest = seg_ref[ids_ref[0, k, t]] + ids_ref[0, 2 + k, t]
            if gather:
                cp = pltpu.make_async_copy(rows_ref.at[pl.ds(dest, 1)], tile_ref.at[k, pl.ds(t, 1)], sem)
            else:
                cp = pltpu.make_async_copy(tile_ref.at[pl.ds(t, 1)], rows_ref.at[pl.ds(dest, 1)], sem)
            cp.start(priority=k)
        return carry
    lax.fori_loop(0, TILE, issue, 0, unroll=4)


def _wait_row_copies(rows_ref, tile_ref, sem):
    pltpu.make_async_copy(rows_ref.at[pl.ds(0, TILE)], tile_ref, sem).wait()


def _dispatch_kernel(ids_ref, cnt_ref, hm_ref, xr_in_ref, xr_ref, blk_ref, seg_ref, sem):
    del xr_in_ref
    step = pl.program_id(0)
    nb = blk_ref.shape[1] - 1

    @pl.when(step == 0)
    def _():
        used = _segment_starts(cnt_ref, seg_ref)

        def per_expert(e, last):
            c = cnt_ref[0, R_EXP + e]
            b0 = seg_ref[e] >> 8
            n = (c + (MOE_BLOCK - 1)) >> 8

            def fill(kk, carry):
                blk_ref[0, b0 + kk] = e
                return carry
            lax.fori_loop(0, n, fill, 0)
            return jnp.where(n > 0, e, last)
        last = lax.fori_loop(0, N_EXPERTS, per_expert, jnp.int32(0))

        def tail(b, carry):
            blk_ref[0, b] = last
            return carry
        lax.fori_loop(used >> 8, nb, tail, 0)
        blk_ref[0, nb] = used >> 8

    _start_row_copies(ids_ref, seg_ref, xr_ref, hm_ref, sem, gather=False)
    for _ in range(2):
        _wait_row_copies(xr_ref, hm_ref, sem)


def _dispatch(ids, cnt, hm, n_rows):
    n, d = hm.shape
    nt = n // TILE
    nb = n_rows // MOE_BLOCK
    zeros = jnp.zeros((n_rows, d), F32)
    return pl.pallas_call(
        _dispatch_kernel,
        grid=(nt,),
        in_specs=[pl.BlockSpec((1, 8, TILE), lambda i: (i, 0, 0), memory_space=pltpu.SMEM),
                  pl.BlockSpec(memory_space=pltpu.SMEM),
                  pl.BlockSpec((TILE, d), lambda i: (i, 0)),
                  pl.BlockSpec(memory_space=pl.ANY)],
        out_specs=[pl.BlockSpec(memory_space=pl.ANY),
                   pl.BlockSpec(memory_space=pltpu.SMEM)],
        out_shape=[jax.ShapeDtypeStruct((n_rows, d), F32),
                   jax.ShapeDtypeStruct((1, nb + 1), I32)],
        scratch_shapes=[pltpu.SMEM((N_EXPERTS,), I32), pltpu.SemaphoreType.DMA(())],
        input_output_aliases={3: 0},
        compiler_params=_params(("arbitrary",)),
        name="dispatch",
    )(ids, cnt, hm, zeros)


def _experts_kernel(blk_ref, x_ref, w1_ref, w3_ref, w2_ref, y_ref, w1b, w3b, w2b):
    b = pl.program_id(0)
    nb = pl.num_programs(0)
    e = blk_ref[0, b]
    prev = blk_ref[0, jnp.maximum(b - 1, 0)]

    @pl.when(jnp.logical_or(b == 0, e != prev))
    def _():
        w1b[...] = w1_ref[0].astype(BF16)
        w3b[...] = w3_ref[0].astype(BF16)
        w2b[...] = w2_ref[0].astype(BF16)

    used = blk_ref[0, nb]

    @pl.when(b < used)
    def _():
        xb = x_ref[...].astype(BF16)
        a = _mm(xb, w1b[...])
        g = _mm(xb, w3b[...])
        y_ref[...] = _mm((a * jax.nn.sigmoid(a) * g).astype(BF16), w2b[...])

    @pl.when(b >= used)
    def _():
        y_ref[...] = jnp.zeros(y_ref.shape, F32)


def _experts(blk, xr, w1, w3, w2):
    n_rows, d = xr.shape
    nb = n_rows // MOE_BLOCK
    wspec = lambda s: pl.BlockSpec((1,) + s, lambda b, blk: (blk[0, b], 0, 0))
    return pl.pallas_call(
        _experts_kernel,
        grid_spec=pltpu.PrefetchScalarGridSpec(
            num_scalar_prefetch=1,
            grid=(nb,),
            in_specs=[pl.BlockSpec((MOE_BLOCK, d), lambda b, blk: (b, 0)),
                      wspec((d, D_EXPERT)), wspec((d, D_EXPERT)), wspec((D_EXPERT, d))],
            out_specs=pl.BlockSpec((MOE_BLOCK, d), lambda b, blk: (b, 0)),
            scratch_shapes=[pltpu.VMEM((d, D_EXPERT), BF16), pltpu.VMEM((d, D_EXPERT), BF16),
                            pltpu.VMEM((D_EXPERT, d), BF16)]),
        out_shape=jax.ShapeDtypeStruct((n_rows, d), F32),
        compiler_params=_params(("arbitrary",)),
        name="experts",
    )(blk, xr, w1, w3, w2)


def _combine_kernel(ids_ref, ids_next_ref, cnt_ref, x2_ref, rw_ref, g_ref, yr_ref, o_ref, y_ref, seg_ref, sem):
    step = pl.program_id(0)
    slot = step % 2

    @pl.when(step == 0)
    def _():
        _segment_starts(cnt_ref, seg_ref)
        _start_row_copies(ids_ref, seg_ref, yr_ref, y_ref.at[0], sem.at[0], gather=True)

    @pl.when(step + 1 < pl.num_programs(0))
    def _():
        _start_row_copies(ids_next_ref, seg_ref, yr_ref, y_ref.at[1 - slot], sem.at[1 - slot], gather=True)

    for k in range(2):
        _wait_row_copies(yr_ref, y_ref.at[slot, k], sem.at[slot])
    rw = rw_ref[...]
    x3 = x2_ref[...] + (y_ref[slot, 0] * rw[:, 0:1] + y_ref[slot, 1] * rw[:, 1:2])
    o_ref[...] = _rms(x3, g_ref[...])


def _combine(ids, cnt, x2, rw, g, yr):
    n, d = x2.shape
    nt = n // TILE
    row = lambda w: pl.BlockSpec((TILE, w), lambda i: (i, 0))
    return pl.pallas_call(
        _combine_kernel,
        grid=(nt,),
        in_specs=[pl.BlockSpec((1, 8, TILE), lambda i: (i, 0, 0), memory_space=pltpu.SMEM),
                  pl.BlockSpec((1, 8, TILE), lambda i: (jnp.minimum(i + 1, nt - 1), 0, 0),
                               memory_space=pltpu.SMEM),
                  pl.BlockSpec(memory_space=pltpu.SMEM),
                  row(d), row(LANES), pl.BlockSpec((1, d), lambda i: (0, 0)),
                  pl.BlockSpec(memory_space=pl.ANY)],
        out_specs=row(d),
        out_shape=jax.ShapeDtypeStruct((n, d), F32),
        scratch_shapes=[pltpu.VMEM((2, 2, TILE, d), F32), pltpu.SMEM((N_EXPERTS,), I32),
                        pltpu.SemaphoreType.DMA((2,))],
        compiler_params=_params(("arbitrary",)),
        name="combine",
    )(ids, ids, cnt, x2, rw, g, yr)


def _t5_bucket(dist):
    n = jnp.maximum(dist, 0)
    max_exact = REL_BUCKETS // 2
    nf = jnp.maximum(n, 1).astype(F32)
    large = max_exact + (jnp.log(nf / max_exact) / math.log(REL_MAX_DIST / max_exact)
                         * (REL_BUCKETS - max_exact)).astype(I32)
    large = jnp.minimum(large, REL_BUCKETS - 1)
    return jnp.where(n < max_exact, n, large)


def _bias_tables(tab):
    r = jnp.arange(TILE)[:, None]
    c = jnp.arange(TILE)[None, :]
    rel = tab - tab[:, REL_BUCKETS - 1:]

    def tile(dist):
        onehot = (_t5_bucket(dist)[..., None] == jnp.arange(REL_BUCKETS)).astype(F32)
        return jnp.einsum('rcb,hb->hrc', onehot, rel, precision=lax.Precision.HIGHEST)

    own = jnp.where(r >= c, tile(r - c), NEG)
    return jnp.stack([own, tile(r - c + TILE)], axis=1).astype(F32)


def _block_mean_slabs(kmean, bn, nq):
    km = kmean.reshape(bn, nq, A_HEADS, HEAD_DIM).transpose(0, 2, 1, 3)
    pad = ((0, 0), (0, 0), (HEAD_DIM, LANES - HEAD_DIM - nq), (0, LANES - HEAD_DIM))
    return jnp.pad(km, pad).astype(BF16)


def _pad_cols(w, width):
    return jnp.pad(w, ((0, 0), (0, width - w.shape[1])))


def kernel(x, mem, rel_bias, final_norm, norm_mix, w_in, ckv_norm, w_uk, w_uv, w_oa, w_ob,
           w_out, norm_x, mem_norm, wq_x, wk_x, wv_x, wo_x, norm_moe, w_group, b_group,
           w_router, b_router, w1, w3, w2):
    bn, t, d = x.shape
    n = bn * t
    nq = t // TILE
    assert t % TILE == 0 and nq <= LANES - HEAD_DIM and norm_mix.shape[0] == 1
    near_a = _bias_tables(rel_bias[:, :A_HEADS].T)
    near_b = _bias_tables(rel_bias[:, A_HEADS:].T)
    row = lambda v: v.reshape(1, -1).astype(F32)

    wi = w_in[0]
    wp = jnp.concatenate(
        [wi[:, OFF_AQ:OFF_IK], jnp.tile(wi[:, OFF_IK:OFF_IW], (1, LANES // IDX_DIM)),
         _pad_cols(wi[:, OFF_IW:OFF_GA], LANES)], axis=1).astype(BF16)
    wkv = jnp.concatenate([w_uk[0], w_uv[0]], axis=1).astype(BF16)
    xf = x.reshape(n, d)
    qa, ka, va, qb, kb, vb, qi, ki4, widx, kmean = _proj_in(
        xf, row(norm_mix[0]), wp, row(ckv_norm[0]), wkv, nq)

    b3 = lambda a: a.reshape(bn, t, a.shape[-1])
    attn_a = _moba(b3(qa), b3(ka), b3(va), _block_mean_slabs(kmean, bn, nq), near_a)
    attn_b = _dsa(b3(qi), b3(ki4), b3(widx), b3(qb), b3(kb), b3(vb), near_b)

    kvm = _mem_kv(mem, row(mem_norm[0]), jnp.concatenate([wk_x[0], wv_x[0]], axis=1).astype(BF16))
    wrt = _pad_cols(jnp.concatenate([w_group[0], w_router[0]], axis=1), LANES).astype(F32)
    brt = _pad_cols(jnp.concatenate([b_group[0], b_router[0]]).reshape(1, -1), LANES).astype(F32)
    weights = (row(norm_mix[0]), wi[:, OFF_GA:IN_COLS].astype(BF16), w_oa[0].astype(BF16),
               w_ob[0].astype(BF16), w_out[0].astype(BF16), row(norm_x[0]), wq_x[0].astype(BF16),
               wo_x[0].astype(BF16), row(norm_moe[0]), wrt, brt)
    x2, hm, rw, ids, cnt = _post(xf, attn_a.reshape(n, W_A), attn_b.reshape(n, W_B), kvm, t // TILE, weights)

    n_rows = 2 * n + N_EXPERTS * MOE_BLOCK
    xr, blk = _dispatch(ids, cnt, hm, n_rows)
    yr = _experts(blk, xr, w1[0], w3[0], w2[0])
    out = _combine(ids, cnt, x2, rw, row(final_norm), yr)
    return out.reshape(bn, t, d)
```

```python
import functools
import math

import jax
import jax.numpy as jnp
from jax import lax
from jax.experimental import pallas as pl
from jax.experimental.pallas import tpu as pltpu

D_MODEL = 1024
HEAD_DIM = 64
A_HEADS = 8
B_HEADS = 8
W_A = A_HEADS * HEAD_DIM
W_B = B_HEADS * HEAD_DIM
MOBA_BLOCK = 256
MOBA_TOPK = 3
DSA_KV_RANK = 256
IDX_HEADS = 8
IDX_DIM = 32
DSA_TOPK = 256
REL_BUCKETS = 32
REL_MAX_DIST = 128
X_HEADS = 4
W_X = X_HEADS * HEAD_DIM
N_GROUPS = 4
EXPERTS_PER_GROUP = 8
N_EXPERTS = N_GROUPS * EXPERTS_PER_GROUP
D_EXPERT = 512
MOE_BLOCK = 256
EPS = 1e-6
NEG = -1e30

OFF_AQ = 0
OFF_AK = OFF_AQ + W_A
OFF_AV = OFF_AK + W_A
OFF_BQ = OFF_AV + W_A
OFF_CKV = OFF_BQ + W_B
OFF_IQ = OFF_CKV + DSA_KV_RANK
OFF_IK = OFF_IQ + IDX_HEADS * IDX_DIM
OFF_IW = OFF_IK + IDX_DIM
OFF_GA = OFF_IW + IDX_HEADS
OFF_GB = OFF_GA + D_MODEL
IN_COLS = OFF_GB + D_MODEL

LANES = 128
TILE = 256
VMEM_LIMIT = 56 * 1024 * 1024
BF16 = jnp.bfloat16
F32 = jnp.float32
I32 = jnp.int32

P_QA, P_KA, P_VA, P_QB = 0, 512, 1024, 1536
P_CKV, P_IQ, P_IK4, P_IW = 2048, 2304, 2560, 2688
P_COLS = 2816

R_GRP = 0
R_EXP = N_GROUPS


def _nt(a, b):
    return lax.dot_general(a, b, (((1,), (1,)), ((), ())), preferred_element_type=F32)


def _mm(a, b):
    return jnp.dot(a, b, preferred_element_type=F32)


def _rms(xf, g):
    return xf * lax.rsqrt(jnp.mean(xf * xf, axis=-1, keepdims=True) + EPS) * g


def _params(sem, vmem=VMEM_LIMIT):
    return pltpu.CompilerParams(dimension_semantics=sem, vmem_limit_bytes=vmem)


def _mem_kv_kernel(mem_ref, g_ref, w_ref, o_ref):
    m = _rms(mem_ref[0], g_ref[...]).astype(BF16)
    o_ref[0] = _mm(m, w_ref[...]).astype(BF16)


def _mem_kv(mem, g, wkv):
    bn, ml, d = mem.shape
    return pl.pallas_call(
        _mem_kv_kernel,
        grid=(bn,),
        in_specs=[pl.BlockSpec((1, ml, d), lambda b: (b, 0, 0)),
                  pl.BlockSpec((1, d), lambda b: (0, 0)),
                  pl.BlockSpec((d, 2 * W_X), lambda b: (0, 0))],
        out_specs=pl.BlockSpec((1, ml, 2 * W_X), lambda b: (b, 0, 0)),
        out_shape=jax.ShapeDtypeStruct((bn, ml, 2 * W_X), BF16),
        compiler_params=_params(("arbitrary",)),
        name="mem_kv",
    )(mem, g, wkv)


def _head_slabs(zp, extra):
    lane = lax.broadcasted_iota(I32, zp.shape, 1)
    low = lane < HEAD_DIM
    return jnp.where(low, zp, extra), jnp.where(low, pltpu.roll(zp, HEAD_DIM, 1), extra)


def _proj_in_kernel(blocks_per_seq, x_ref, g_ref, w_ref, cg_ref, wkv_ref,
                    qa_ref, ka_ref, va_ref, qb_ref, kb_ref, vb_ref, qi_ref, ki_ref, wi_ref, km_ref):
    h = _rms(x_ref[...], g_ref[...]).astype(BF16)
    z = _mm(h, w_ref[...])
    scale = HEAD_DIM ** -0.5
    zk = z[:, P_KA:P_VA]
    km_ref[0] = jnp.mean(zk, axis=0, keepdims=True)
    lane = lax.broadcasted_iota(I32, (TILE, LANES), 1)
    blk = pl.program_id(0) % blocks_per_seq
    onehot = jnp.where(lane == HEAD_DIM + blk, 1.0, 0.0)
    zero = jnp.zeros((TILE, LANES), F32)
    for p_idx in range(A_HEADS // 2):
        lo = p_idx * LANES
        qe, qo = _head_slabs(z[:, P_QA + lo:P_QA + lo + LANES] * scale, zero)
        ke, ko = _head_slabs(zk[:, lo:lo + LANES], onehot)
        qa_ref[:, 2 * lo:2 * lo + 2 * LANES] = jnp.concatenate([qe, qo], axis=1).astype(BF16)
        ka_ref[:, 2 * lo:2 * lo + 2 * LANES] = jnp.concatenate([ke, ko], axis=1).astype(BF16)
    va_ref[...] = z[:, P_VA:P_QB].astype(BF16)
    qb_ref[...] = (z[:, P_QB:P_CKV] * scale).astype(BF16)
    ckv = _rms(z[:, P_CKV:P_IQ], cg_ref[...]).astype(BF16)
    kv = _mm(ckv, wkv_ref[...])
    kb_ref[...] = kv[:, :W_B].astype(BF16)
    vb_ref[...] = kv[:, W_B:].astype(BF16)
    qi_ref[...] = z[:, P_IQ:P_IK4].astype(BF16)
    ki_ref[...] = z[:, P_IK4:P_IW].astype(BF16)
    wi_ref[...] = z[:, P_IW:P_COLS] * ((IDX_HEADS ** -0.5) * (IDX_DIM ** -0.5))


def _proj_in(xf, g, wp, cg, wkv, blocks_per_seq):
    n, d = xf.shape
    nt = n // TILE
    row = lambda w: pl.BlockSpec((TILE, w), lambda i: (i, 0))
    full = lambda a: pl.BlockSpec(a.shape, lambda i: (0,) * a.ndim)
    outs = [(2 * W_A, BF16), (2 * W_A, BF16), (W_A, BF16), (W_B, BF16), (W_B, BF16), (W_B, BF16),
            (IDX_HEADS * IDX_DIM, BF16), (LANES, BF16), (LANES, F32)]
    return pl.pallas_call(
        functools.partial(_proj_in_kernel, blocks_per_seq),
        grid=(nt,),
        in_specs=[row(d), full(g), full(wp), full(cg), full(wkv)],
        out_specs=[row(w) for w, _ in outs] + [pl.BlockSpec((1, 1, W_A), lambda i: (i, 0, 0))],
        out_shape=[jax.ShapeDtypeStruct((n, w), t) for w, t in outs]
        + [jax.ShapeDtypeStruct((nt, 1, W_A), F32)],
        compiler_params=_params(("arbitrary",)),
        name="proj_in",
    )(xf, g, wp, cg, wkv)


def _sm_init(mx_ref, acc_ref):
    mx_ref[...] = jnp.full(mx_ref.shape, NEG, F32)
    acc_ref[...] = jnp.zeros(acc_ref.shape, F32)


LOG2E = math.log2(math.e)


def _sm_logits(hd, slot, s, sc_ref, mx_ref):
    s = s * LOG2E
    sc_ref[hd, slot] = s
    mx_ref[hd] = jnp.maximum(mx_ref[hd], jnp.maximum(s[:, :LANES], s[:, LANES:]))


def _sm_rowmax(mx_ref):
    for hd in range(2):
        mx_ref[hd] = jnp.broadcast_to(jnp.max(mx_ref[hd], axis=1, keepdims=True), (TILE, LANES))


def _sm_probs(hd, slots, sc_ref, mx_ref):
    m = mx_ref[hd]
    m2 = jnp.concatenate([m, m], axis=1)
    ps = [jnp.exp2((sc_ref[hd, sl] - m2).astype(BF16)) for sl in slots]
    return ps[0] if len(ps) == 1 else jnp.concatenate(ps, axis=1)


def _sm_accumulate(hd, pb, v, acc_ref):
    lane = lax.broadcasted_iota(I32, v.shape, 1)
    own = (lane < HEAD_DIM) if hd == 0 else (lane >= HEAD_DIM)
    acc_ref[hd] = acc_ref[hd] + _mm(pb, jnp.where(own, v, jnp.ones_like(v)))


def _sm_values(hd, slots, v, sc_ref, mx_ref, acc_ref):
    _sm_accumulate(hd, _sm_probs(hd, slots, sc_ref, mx_ref), v, acc_ref)


FAR_GROUP = 4


def _sm_far_values(n_grp, vals, sc_ref, mx_ref, acc_ref, stage_ref):
    def stage(g):
        slots = [FAR_GROUP * g + u for u in range(FAR_GROUP)]
        return [_sm_probs(hd, slots, sc_ref, mx_ref) for hd in range(2)]

    def put(ps):
        for hd in range(2):
            stage_ref[hd] = ps[hd]

    pl.when(n_grp > 0)(lambda: put(stage(0)))

    def body(g, carry):
        v = vals(FAR_GROUP * g, FAR_GROUP)
        cur = [stage_ref[hd] for hd in range(2)]
        nxt = stage(jnp.minimum(g + 1, n_grp - 1))
        for hd in range(2):
            _sm_accumulate(hd, cur[hd], v, acc_ref)
        put(nxt)
        return carry

    lax.fori_loop(0, n_grp, body, 0)


def _sm_output(acc_ref):
    lane = lax.broadcasted_iota(I32, (TILE, LANES), 1)
    out = [acc_ref[hd] / pltpu.roll(acc_ref[hd], HEAD_DIM, 1) for hd in range(2)]
    return jnp.where(lane < HEAD_DIM, out[0], out[1])


def _split_pair(qp):
    lane = lax.broadcasted_iota(I32, qp.shape, 1)
    zero = jnp.zeros_like(qp)
    return jnp.where(lane < HEAD_DIM, qp, zero), jnp.where(lane >= HEAD_DIM, qp, zero)


def _moba_kernel(q_ref, k_ref, v_ref, km_ref, bias_ref, o_ref, sc_ref, mx_ref, acc_ref, stage_ref):
    i = pl.program_id(2)
    nq = pl.num_programs(2)
    slot_prev, slot_own = nq, nq + 1
    n_slot = LANES - HEAD_DIM
    blk = lax.broadcasted_iota(I32, (n_slot, TILE), 0)
    in_blk = blk < nq
    past = blk < i
    no_mask = jnp.zeros((HEAD_DIM, TILE), F32)

    qs, q_far, q_prev = [], [], []
    for hd in range(2):
        q = q_ref[0, :, hd * LANES:(hd + 1) * LANES]
        g = jnp.where(past, _nt(km_ref[0, hd], q)[HEAD_DIM:], -jnp.inf)
        sel = jnp.zeros((n_slot, TILE), jnp.bool_)
        for _ in range(MOBA_TOPK):
            top = jnp.max(g, axis=0, keepdims=True)
            hit = jnp.logical_and(g == top, in_blk)
            first = jnp.min(jnp.where(hit, blk, n_slot), axis=0, keepdims=True)
            pick = blk == first
            sel = jnp.logical_or(sel, jnp.logical_and(pick, past))
            g = jnp.where(pick, -jnp.inf, g)
        addm = jnp.where(sel, 0.0, NEG)
        far_m = jnp.where(blk < i - 1, addm, jnp.where(in_blk, NEG, 0.0))
        prev_m = jnp.where(blk == i - 1, addm, 0.0)
        qs.append(q)
        q_far.append(q + jnp.concatenate([no_mask, far_m], axis=0).T.astype(BF16))
        q_prev.append(q + jnp.concatenate([no_mask, prev_m], axis=0).T.astype(BF16))

    def keys(j, hd):
        off = pl.multiple_of(j * TILE, TILE)
        return k_ref[0, pl.ds(off, TILE), hd * LANES:(hd + 1) * LANES]

    n_far = jnp.maximum(i - 1, 0)
    _sm_init(mx_ref, acc_ref)

    def far_logits(g, carry):
        for u in range(FAR_GROUP):
            j = FAR_GROUP * g + u
            for hd in range(2):
                _sm_logits(hd, j, _nt(q_far[hd], keys(j, hd)), sc_ref, mx_ref)
        return carry

    n_grp = (n_far + FAR_GROUP - 1) // FAR_GROUP
    lax.fori_loop(0, n_grp, far_logits, 0)

    j_prev = jnp.maximum(i - 1, 0)
    no_prev = jnp.where(i >= 1, 0.0, NEG)
    for hd in range(2):
        _sm_logits(hd, slot_prev, _nt(q_prev[hd], keys(j_prev, hd)) + (bias_ref[hd, 1] + no_prev),
                   sc_ref, mx_ref)
        _sm_logits(hd, slot_own, _nt(qs[hd], keys(i, hd)) + bias_ref[hd, 0], sc_ref, mx_ref)
    _sm_rowmax(mx_ref)

    def vals(j, n):
        off = pl.multiple_of(j * TILE, TILE)
        return v_ref[0, pl.ds(off, n * TILE), :]

    _sm_far_values(n_grp, vals, sc_ref, mx_ref, acc_ref, stage_ref)

    v_near = jnp.concatenate([vals(j_prev, 1), vals(i, 1)], axis=0)
    for hd in range(2):
        _sm_values(hd, [slot_prev, slot_own], v_near, sc_ref, mx_ref, acc_ref)
    o_ref[0] = _sm_output(acc_ref).astype(BF16)


def _moba(qa, ka, va, kmp, bias_near):
    bn, t, _ = va.shape
    nq = t // TILE
    npair = A_HEADS // 2
    return pl.pallas_call(
        _moba_kernel,
        grid=(bn, npair, nq),
        in_specs=[pl.BlockSpec((1, TILE, 2 * LANES), lambda b, p, i: (b, i, p)),
                  pl.BlockSpec((1, t, 2 * LANES), lambda b, p, i: (b, 0, p)),
                  pl.BlockSpec((1, t, LANES), lambda b, p, i: (b, 0, p)),
                  pl.BlockSpec((1, 2, LANES, LANES), lambda b, p, i: (b, p, 0, 0)),
                  pl.BlockSpec((2, 2, TILE, TILE), lambda b, p, i: (p, 0, 0, 0))],
        out_specs=pl.BlockSpec((1, TILE, LANES), lambda b, p, i: (b, i, p)),
        out_shape=jax.ShapeDtypeStruct((bn, t, W_A), BF16),
        scratch_shapes=[pltpu.VMEM((2, nq + 2, TILE, TILE), F32),
                        pltpu.VMEM((2, TILE, LANES), F32),
                        pltpu.VMEM((2, TILE, LANES), F32),
                        pltpu.VMEM((2, TILE, FAR_GROUP * TILE), BF16)],
        compiler_params=_params(("arbitrary", "arbitrary", "arbitrary")),
        name="moba",
    )(qa, ka, va, kmp, bias_near)


def _dsa_select(i, qi_ref, ki_ref, wi_ref, s_ref, wb_ref):
    nch = i + 1
    sub = TILE // 8
    lane = lax.broadcasted_iota(I32, (TILE, LANES), 1)
    key_in = (lax.broadcasted_iota(I32, (sub, 8, TILE), 0) * 8 + lax.broadcasted_iota(I32, (sub, 8, TILE), 1))
    qpos = i * TILE + lax.broadcasted_iota(I32, (sub, 8, TILE), 2)

    def rows8(x):
        return jnp.broadcast_to(x, (8, TILE))[None]

    w_t = wi_ref[0].T
    qi = qi_ref[0]
    per_group = LANES // IDX_DIM
    qh = []
    for h in range(IDX_HEADS):
        g, r = divmod(h, per_group)
        qg = qi[:, g * LANES:(g + 1) * LANES]
        keep = jnp.logical_and(lane >= r * IDX_DIM, lane < (r + 1) * IDX_DIM)
        qh.append(jnp.where(keep, qg, jnp.zeros_like(qg)))
        wb_ref[h] = jnp.broadcast_to(w_t[h:h + 1], (8, TILE))

    def score(c, carry):
        off = pl.multiple_of(c * TILE, TILE)
        kc = ki_ref[0, pl.ds(off, TILE), :]
        acc = jnp.zeros((sub, 8, TILE), F32)
        for h in range(IDX_HEADS):
            acc = acc + jnp.maximum(_nt(kc, qh[h]), 0.0).reshape(sub, 8, TILE) * wb_ref[h][None]
        s_ref[c] = jnp.where(c * TILE + key_in <= qpos, acc, -jnp.inf).reshape(TILE, TILE)
        return carry

    lax.fori_loop(0, nch, score, 0)

    def fold(init, step):
        return lax.fori_loop(0, nch, lambda c, acc: step(acc, s_ref[c].reshape(sub, 8, TILE), c), init)

    def count(pred):
        def step(acc, blk, c):
            return acc + jnp.sum(jnp.where(pred(blk, c * TILE + key_in), 1.0, 0.0), axis=0)
        return jnp.sum(fold(jnp.zeros((8, TILE), F32), step), axis=0, keepdims=True)

    wide = rows8

    def stats(acc, blk, c):
        top, low, pos, nonneg = acc
        top = jnp.maximum(top, jnp.max(blk, axis=0))
        low = jnp.minimum(low, jnp.min(jnp.where(blk == -jnp.inf, jnp.inf, blk), axis=0))
        pos = pos + jnp.sum(jnp.where(blk > 0.0, 1.0, 0.0), axis=0)
        nonneg = nonneg + jnp.sum(jnp.where(blk >= 0.0, 1.0, 0.0), axis=0)
        return top, low, pos, nonneg

    zeros8 = jnp.zeros((8, TILE), F32)
    acc = fold((zeros8 - jnp.inf, zeros8 + jnp.inf, zeros8, zeros8), stats)
    hi = jnp.max(acc[0], axis=0, keepdims=True)
    lo = jnp.min(acc[1], axis=0, keepdims=True)
    n_pos = jnp.sum(acc[2], axis=0, keepdims=True)
    n_nonneg = jnp.sum(acc[3], axis=0, keepdims=True)

    top = hi
    top_ties = count(lambda half, kpos: half >= wide(top)) >= DSA_TOPK
    at_zero = jnp.logical_and(n_pos < DSA_TOPK, n_nonneg >= DSA_TOPK)
    lo = jnp.where(n_pos >= DSA_TOPK, 0.0, lo)
    hi = jnp.where(n_nonneg < DSA_TOPK, 0.0, hi)

    def search_more(state):
        return jnp.logical_and(jnp.min(state[3]) == 0, state[4] < 400)

    def search(state):
        lo, hi, thr, done, it = state
        mid = 0.5 * lo + 0.5 * hi
        closed = jnp.logical_or(mid <= lo, mid >= hi)
        n = count(lambda half, kpos: half >= wide(mid))
        found = jnp.where(closed, lo, mid)
        stop = jnp.logical_or(closed, n == DSA_TOPK)
        live = done == 0
        thr = jnp.where(jnp.logical_and(live, stop), found, thr)
        go = jnp.logical_and(live, jnp.logical_not(stop))
        lo = jnp.where(jnp.logical_and(go, n > DSA_TOPK), mid, lo)
        hi = jnp.where(jnp.logical_and(go, n < DSA_TOPK), mid, hi)
        return lo, hi, thr, jnp.where(stop, 1, done), it + 1

    thr0 = jnp.where(top_ties, top, 0.0)
    done0 = jnp.logical_or(top_ties, at_zero).astype(I32)
    state = (lo, hi, thr0, done0, jnp.int32(0))
    thr = wide(lax.while_loop(search_more, lambda st: search(search(st)), state)[2])
    n_gt = count(lambda half, kpos: half > thr)
    n_ge = count(lambda half, kpos: half >= thr)
    need = DSA_TOPK - n_gt

    cut_bits = max(1, (s_ref.shape[0] * TILE - 1).bit_length())

    def cut_step(b, cut):
        cand = cut | (jnp.int32(1) << (cut_bits - 1 - b))
        cw = wide(cand)
        n = count(lambda half, kpos: jnp.logical_and(half == thr, kpos < cw))
        return jnp.where(n < need, cand, cut)

    cut = lax.cond(jnp.max(n_ge) > DSA_TOPK,
                   lambda: lax.fori_loop(0, cut_bits, cut_step, jnp.zeros((1, TILE), I32)),
                   lambda: jnp.full((1, TILE), 2 ** 30, I32))
    cut = wide(jnp.where(n_ge > DSA_TOPK, cut, 2 ** 30))

    def to_mask(c, carry):
        blk = s_ref[c].reshape(sub, 8, TILE)
        kpos = c * TILE + key_in
        keep = jnp.logical_or(blk > thr, jnp.logical_and(blk == thr, kpos <= cut))
        mask_t = jnp.where(jnp.logical_and(keep, kpos <= qpos), 0.0, NEG).reshape(TILE, TILE)
        s_ref[c] = mask_t.T
        return carry

    lax.fori_loop(0, nch, to_mask, 0)


def _dsa_kernel(qi_ref, ki_ref, wi_ref, q_ref, k_ref, v_ref, bias_ref, o_ref,
                s_ref, wb_ref, sc_ref, mx_ref, acc_ref, stage_ref):
    i = pl.program_id(1)
    pair = pl.program_id(2)

    @pl.when(jnp.logical_and(pair == 0, i == 0))
    def _():
        row = lax.broadcasted_iota(I32, (TILE, TILE), 0)
        col = lax.broadcasted_iota(I32, (TILE, TILE), 1)
        s_ref[0] = jnp.where(col <= row, 0.0, NEG)

    @pl.when(jnp.logical_and(pair == 0, i >= 1))
    def _():
        _dsa_select(i, qi_ref, ki_ref, wi_ref, s_ref, wb_ref)

    qs = _split_pair(q_ref[0])
    nq = pl.num_programs(1)
    n_far = jnp.maximum(i - 1, 0)
    _sm_init(mx_ref, acc_ref)

    def rows(ref, j, n):
        off = pl.multiple_of(j * TILE, TILE)
        return ref[0, pl.ds(off, n * TILE), :]

    def logits(j, slot, msk, near=None):
        k_j = rows(k_ref, j, 1)
        for hd in range(2):
            s = _nt(qs[hd], k_j) + msk
            if near is not None:
                s = s + bias_ref[hd, near]
            _sm_logits(hd, slot, s, sc_ref, mx_ref)

    def far_logits(g, carry):
        for u in range(FAR_GROUP):
            j = FAR_GROUP * g + u
            logits(j, j, jnp.where(j < i - 1, s_ref[jnp.minimum(j, i)], NEG))
        return carry

    n_grp = (n_far + FAR_GROUP - 1) // FAR_GROUP
    lax.fori_loop(0, n_grp, far_logits, 0)
    j_prev = jnp.maximum(i - 1, 0)
    logits(j_prev, nq, jnp.where(i >= 1, s_ref[j_prev], NEG), near=1)
    logits(i, nq + 1, s_ref[i], near=0)
    _sm_rowmax(mx_ref)

    _sm_far_values(n_grp, lambda j, n: rows(v_ref, j, n), sc_ref, mx_ref, acc_ref, stage_ref)
    v_near = jnp.concatenate([rows(v_ref, j_prev, 1), rows(v_ref, i, 1)], axis=0)
    for hd in range(2):
        _sm_values(hd, [nq, nq + 1], v_near, sc_ref, mx_ref, acc_ref)
    o_ref[0] = _sm_output(acc_ref).astype(BF16)


def _dsa(qi, ki4, wi, qb, kb, vb, bias_near):
    bn, t, _ = qb.shape
    nq = t // TILE
    npair = B_HEADS // 2
    qrow = lambda w: pl.BlockSpec((1, TILE, w), lambda b, i, p: (b, i, 0))
    return pl.pallas_call(
        _dsa_kernel,
        grid=(bn, nq, npair),
        in_specs=[qrow(IDX_HEADS * IDX_DIM),
                  pl.BlockSpec((1, t, LANES), lambda b, i, p: (b, 0, 0), pipeline_mode=pl.Buffered(1)),
                  qrow(LANES),
                  pl.BlockSpec((1, TILE, LANES), lambda b, i, p: (b, i, p)),
                  pl.BlockSpec((1, t, LANES), lambda b, i, p: (b, 0, p)),
                  pl.BlockSpec((1, t, LANES), lambda b, i, p: (b, 0, p)),
                  pl.BlockSpec((2, 2, TILE, TILE), lambda b, i, p: (p, 0, 0, 0))],
        out_specs=pl.BlockSpec((1, TILE, LANES), lambda b, i, p: (b, i, p)),
        out_shape=jax.ShapeDtypeStruct((bn, t, W_B), BF16),
        scratch_shapes=[pltpu.VMEM((nq, TILE, TILE), F32),
                        pltpu.VMEM((IDX_HEADS, 8, TILE), F32),
                        pltpu.VMEM((2, nq + 2, TILE, TILE), F32),
                        pltpu.VMEM((2, TILE, LANES), F32),
                        pltpu.VMEM((2, TILE, LANES), F32),
                        pltpu.VMEM((2, TILE, FAR_GROUP * TILE), BF16)],
        compiler_params=_params(("arbitrary", "arbitrary", "arbitrary")),
        name="dsa",
    )(qi, ki4, wi, qb, kb, vb, bias_near)


def _post_kernel(x_ref, aa_ref, ab_ref, gm_ref, wg_ref, woa_ref, wob_ref, wout_ref,
                 gx_ref, wq_ref, kvm_ref, wox_ref, gmoe_ref, wrt_ref, brt_ref,
                 x2_ref, hm_ref, rw_ref, ids_ref, cnt_ref, carry_ref):
    step = pl.program_id(0)
    xf = x_ref[...]
    h = _rms(xf, gm_ref[...]).astype(BF16)
    gates = jax.nn.sigmoid(_mm(h, wg_ref[...]))
    oa = _mm(aa_ref[...], woa_ref[...])
    ob = _mm(ab_ref[...], wob_ref[...])
    mrg = gates[:, :D_MODEL] * oa + gates[:, D_MODEL:] * ob
    x1 = xf + _mm(mrg.astype(BF16), wout_ref[...])

    hx = _rms(x1, gx_ref[...]).astype(BF16)
    q = (_mm(hx, wq_ref[...]) * (HEAD_DIM ** -0.5)).astype(BF16)
    kvm = kvm_ref[0]
    lane = lax.broadcasted_iota(I32, (TILE, LANES), 1)
    outs = []
    for p_idx in range(X_HEADS // 2):
        lo = p_idx * LANES
        qs = _split_pair(q[:, lo:lo + LANES])
        km = kvm[:, lo:lo + LANES]
        vm = kvm[:, W_X + lo:W_X + lo + LANES]
        o = []
        for hd in range(2):
            s = _nt(qs[hd], km)
            p = jnp.exp(s - jnp.max(s, axis=1, keepdims=True))
            o.append(_mm(p.astype(BF16), vm) / jnp.sum(p, axis=1, keepdims=True))
        outs.append(jnp.where(lane < HEAD_DIM, o[0], o[1]))
    xo = jnp.concatenate(outs, axis=1).astype(BF16)
    x2 = x1 + _mm(xo, wox_ref[...])
    x2_ref[...] = x2

    hm = _rms(x2, gmoe_ref[...])
    hm_ref[...] = hm
    logits = jnp.dot(hm, wrt_ref[...], preferred_element_type=F32,
                     precision=lax.Precision.HIGHEST) + brt_ref[...]
    big = LANES

    def argmax(v):
        top = jnp.max(v, axis=1, keepdims=True)
        return top, jnp.min(jnp.where(v == top, lane, big), axis=1, keepdims=True)

    is_grp = lane < R_EXP
    gtop, gsel = argmax(jnp.where(is_grp, logits, -jnp.inf))
    gw = 1.0 / jnp.sum(jnp.where(is_grp, jnp.exp(logits - gtop), 0.0), axis=1, keepdims=True)
    first = R_EXP + gsel * EXPERTS_PER_GROUP
    inside = jnp.logical_and(lane >= first, lane < first + EXPERTS_PER_GROUP)
    within = jnp.where(inside, logits, -jnp.inf)
    v0, i0 = argmax(within)
    v1, i1 = argmax(jnp.where(lane == i0, -jnp.inf, within))
    e1 = jnp.exp(v1 - v0)
    w0 = gw * (1.0 / (1.0 + e1))
    w1 = gw * (e1 / (1.0 + e1))
    rw_ref[...] = jnp.where(lane == 0, w0, jnp.where(lane == 1, w1, 0.0))

    @pl.when(step == 0)
    def _():
        carry_ref[...] = jnp.zeros(carry_ref.shape, F32)

    hit0 = lane == i0
    hit1 = lane == i1
    onehot = jnp.where(jnp.logical_or(hit0, hit1), 1.0, 0.0)
    tri = (lax.broadcasted_iota(I32, (TILE, TILE), 1) < lax.broadcasted_iota(I32, (TILE, TILE), 0))
    base = carry_ref[...] + _mm(tri.astype(BF16), onehot.astype(BF16))
    r0 = jnp.sum(jnp.where(hit0, base, 0.0), axis=1, keepdims=True)
    r1 = jnp.sum(jnp.where(hit1, base, 0.0), axis=1, keepdims=True)
    total = carry_ref[...] + jnp.sum(onehot, axis=0, keepdims=True)
    carry_ref[...] = total
    cnt_ref[...] = total.astype(I32)
    slab = jnp.where(lane == 0, (i0 - R_EXP).astype(F32),
                     jnp.where(lane == 1, (i1 - R_EXP).astype(F32),
                               jnp.where(lane == 2, r0, jnp.where(lane == 3, r1, 0.0))))
    ids_ref[0] = slab.T[:8].astype(I32)


def _post(xf, aa, ab, kvm, tiles_per_batch, weights):
    n, d = xf.shape
    nt = n // TILE
    row = lambda w: pl.BlockSpec((TILE, w), lambda i: (i, 0))
    full = lambda a: pl.BlockSpec(a.shape, lambda i: (0,) * a.ndim)
    gm, wg, woa, wob, wout, gx, wq, wox, gmoe, wrt, brt = weights
    return pl.pallas_call(
        _post_kernel,
        grid=(nt,),
        in_specs=[row(d), row(W_A), row(W_B), full(gm), full(wg), full(woa), full(wob), full(wout),
                  full(gx), full(wq),
                  pl.BlockSpec((1,) + kvm.shape[1:], lambda i: (i // tiles_per_batch, 0, 0)),
                  full(wox), full(gmoe), full(wrt), full(brt)],
        out_specs=[row(d), row(d), row(LANES),
                   pl.BlockSpec((1, 8, TILE), lambda i: (i, 0, 0)),
                   pl.BlockSpec((1, LANES), lambda i: (0, 0))],
        out_shape=[jax.ShapeDtypeStruct((n, d), F32), jax.ShapeDtypeStruct((n, d), F32),
                   jax.ShapeDtypeStruct((n, LANES), F32),
                   jax.ShapeDtypeStruct((nt, 8, TILE), I32),
                   jax.ShapeDtypeStruct((1, LANES), I32)],
        scratch_shapes=[pltpu.VMEM((1, LANES), F32)],
        compiler_params=_params(("arbitrary",)),
        name="post",
    )(xf, aa, ab, gm, wg, woa, wob, wout, gx, wq, kvm, wox, gmoe, wrt, brt)


def _segment_starts(cnt_ref, seg_ref):
    def body(e, acc):
        seg_ref[e] = acc
        c = cnt_ref[0, R_EXP + e]
        return acc + ((c + (MOE_BLOCK - 1)) >> 8 << 8)
    return lax.fori_loop(0, N_EXPERTS, body, jnp.int32(0))


def _start_row_copies(ids_ref, seg_ref, rows_ref, tile_ref, sem, gather):
    def issue(t, carry):
        for k in range(2):
            dest = seg_ref[ids_ref[0, k, t]] + ids_ref[0, 2 + k, t]
            if gather:
                cp = pltpu.make_async_copy(rows_ref.at[pl.ds(dest, 1)], tile_ref.at[k, pl.ds(t, 1)], sem)
            else:
                cp = pltpu.make_async_copy(tile_ref.at[pl.ds(t, 1)], rows_ref.at[pl.ds(dest, 1)], sem)
            cp.start(priority=k)
        return carry
    lax.fori_loop(0, TILE, issue, 0, unroll=4)


def _wait_row_copies(rows_ref, tile_ref, sem):
    pltpu.make_async_copy(rows_ref.at[pl.ds(0, TILE)], tile_ref, sem).wait()


def _dispatch_kernel(ids_ref, cnt_ref, hm_ref, xr_in_ref, xr_ref, blk_ref, seg_ref, sem):
    del xr_in_ref
    step = pl.program_id(0)
    nb = blk_ref.shape[1] - 1

    @pl.when(step == 0)
    def _():
        used = _segment_starts(cnt_ref, seg_ref)

        def per_expert(e, last):
            c = cnt_ref[0, R_EXP + e]
            b0 = seg_ref[e] >> 8
            n = (c + (MOE_BLOCK - 1)) >> 8

            def fill(kk, carry):
                blk_ref[0, b0 + kk] = e
                return carry
            lax.fori_loop(0, n, fill, 0)
            return jnp.where(n > 0, e, last)
        last = lax.fori_loop(0, N_EXPERTS, per_expert, jnp.int32(0))

        def tail(b, carry):
            blk_ref[0, b] = last
            return carry
        lax.fori_loop(used >> 8, nb, tail, 0)
        blk_ref[0, nb] = used >> 8

    _start_row_copies(ids_ref, seg_ref, xr_ref, hm_ref, sem, gather=False)
    for _ in range(2):
        _wait_row_copies(xr_ref, hm_ref, sem)


def _dispatch(ids, cnt, hm, n_rows):
    n, d = hm.shape
    nt = n // TILE
    nb = n_rows // MOE_BLOCK
    zeros = jnp.zeros((n_rows, d), F32)
    return pl.pallas_call(
        _dispatch_kernel,
        grid=(nt,),
        in_specs=[pl.BlockSpec((1, 8, TILE), lambda i: (i, 0, 0), memory_space=pltpu.SMEM),
                  pl.BlockSpec(memory_space=pltpu.SMEM),
                  pl.BlockSpec((TILE, d), lambda i: (i, 0)),
                  pl.BlockSpec(memory_space=pl.ANY)],
        out_specs=[pl.BlockSpec(memory_space=pl.ANY),
                   pl.BlockSpec(memory_space=pltpu.SMEM)],
        out_shape=[jax.ShapeDtypeStruct((n_rows, d), F32),
                   jax.ShapeDtypeStruct((1, nb + 1), I32)],
        scratch_shapes=[pltpu.SMEM((N_EXPERTS,), I32), pltpu.SemaphoreType.DMA(())],
        input_output_aliases={3: 0},
        compiler_params=_params(("arbitrary",)),
        name="dispatch",
    )(ids, cnt, hm, zeros)


def _experts_kernel(blk_ref, x_ref, w1_ref, w3_ref, w2_ref, y_ref, w1b, w3b, w2b):
    b = pl.program_id(0)
    nb = pl.num_programs(0)
    e = blk_ref[0, b]
    prev = blk_ref[0, jnp.maximum(b - 1, 0)]

    @pl.when(jnp.logical_or(b == 0, e != prev))
    def _():
        w1b[...] = w1_ref[0].astype(BF16)
        w3b[...] = w3_ref[0].astype(BF16)
        w2b[...] = w2_ref[0].astype(BF16)

    used = blk_ref[0, nb]

    @pl.when(b < used)
    def _():
        xb = x_ref[...].astype(BF16)
        a = _mm(xb, w1b[...])
        g = _mm(xb, w3b[...])
        y_ref[...] = _mm((a * jax.nn.sigmoid(a) * g).astype(BF16), w2b[...])

    @pl.when(b >= used)
    def _():
        y_ref[...] = jnp.zeros(y_ref.shape, F32)


def _experts(blk, xr, w1, w3, w2):
    n_rows, d = xr.shape
    nb = n_rows // MOE_BLOCK
    wspec = lambda s: pl.BlockSpec((1,) + s, lambda b, blk: (blk[0, b], 0, 0))
    return pl.pallas_call(
        _experts_kernel,
        grid_spec=pltpu.PrefetchScalarGridSpec(
            num_scalar_prefetch=1,
            grid=(nb,),
            in_specs=[pl.BlockSpec((MOE_BLOCK, d), lambda b, blk: (b, 0)),
                      wspec((d, D_EXPERT)), wspec((d, D_EXPERT)), wspec((D_EXPERT, d))],
            out_specs=pl.BlockSpec((MOE_BLOCK, d), lambda b, blk: (b, 0)),
            scratch_shapes=[pltpu.VMEM((d, D_EXPERT), BF16), pltpu.VMEM((d, D_EXPERT), BF16),
                            pltpu.VMEM((D_EXPERT, d), BF16)]),
        out_shape=jax.ShapeDtypeStruct((n_rows, d), F32),
        compiler_params=_params(("arbitrary",)),
        name="experts",
    )(blk, xr, w1, w3, w2)


def _combine_kernel(ids_ref, ids_next_ref, cnt_ref, x2_ref, rw_ref, g_ref, yr_ref, o_ref, y_ref, seg_ref, sem):
    step = pl.program_id(0)
    slot = step % 2

    @pl.when(step == 0)
    def _():
        _segment_starts(cnt_ref, seg_ref)
        _start_row_copies(ids_ref, seg_ref, yr_ref, y_ref.at[0], sem.at[0], gather=True)

    @pl.when(step + 1 < pl.num_programs(0))
    def _():
        _start_row_copies(ids_next_ref, seg_ref, yr_ref, y_ref.at[1 - slot], sem.at[1 - slot], gather=True)

    for k in range(2):
        _wait_row_copies(yr_ref, y_ref.at[slot, k], sem.at[slot])
    rw = rw_ref[...]
    x3 = x2_ref[...] + (y_ref[slot, 0] * rw[:, 0:1] + y_ref[slot, 1] * rw[:, 1:2])
    o_ref[...] = _rms(x3, g_ref[...])


def _combine(ids, cnt, x2, rw, g, yr):
    n, d = x2.shape
    nt = n // TILE
    row = lambda w: pl.BlockSpec((TILE, w), lambda i: (i, 0))
    return pl.pallas_call(
        _combine_kernel,
        grid=(nt,),
        in_specs=[pl.BlockSpec((1, 8, TILE), lambda i: (i, 0, 0), memory_space=pltpu.SMEM),
                  pl.BlockSpec((1, 8, TILE), lambda i: (jnp.minimum(i + 1, nt - 1), 0, 0),
                               memory_space=pltpu.SMEM),
                  pl.BlockSpec(memory_space=pltpu.SMEM),
                  row(d), row(LANES), pl.BlockSpec((1, d), lambda i: (0, 0)),
                  pl.BlockSpec(memory_space=pl.ANY)],
        out_specs=row(d),
        out_shape=jax.ShapeDtypeStruct((n, d), F32),
        scratch_shapes=[pltpu.VMEM((2, 2, TILE, d), F32), pltpu.SMEM((N_EXPERTS,), I32),
                        pltpu.SemaphoreType.DMA((2,))],
        compiler_params=_params(("arbitrary",)),
        name="combine",
    )(ids, ids, cnt, x2, rw, g, yr)


def _t5_bucket(dist):
    n = jnp.maximum(dist, 0)
    max_exact = REL_BUCKETS // 2
    nf = jnp.maximum(n, 1).astype(F32)
    large = max_exact + (jnp.log(nf / max_exact) / math.log(REL_MAX_DIST / max_exact)
                         * (REL_BUCKETS - max_exact)).astype(I32)
    large = jnp.minimum(large, REL_BUCKETS - 1)
    return jnp.where(n < max_exact, n, large)


def _bias_tables(tab):
    r = jnp.arange(TILE)[:, None]
    c = jnp.arange(TILE)[None, :]
    rel = tab - tab[:, REL_BUCKETS - 1:]

    def tile(dist):
        onehot = (_t5_bucket(dist)[..., None] == jnp.arange(REL_BUCKETS)).astype(F32)
        return jnp.einsum('rcb,hb->hrc', onehot, rel, precision=lax.Precision.HIGHEST)

    own = jnp.where(r >= c, tile(r - c), NEG)
    return jnp.stack([own, tile(r - c + TILE)], axis=1).astype(F32)


def _block_mean_slabs(kmean, bn, nq):
    km = kmean.reshape(bn, nq, A_HEADS, HEAD_DIM).transpose(0, 2, 1, 3)
    pad = ((0, 0), (0, 0), (HEAD_DIM, LANES - HEAD_DIM - nq), (0, LANES - HEAD_DIM))
    return jnp.pad(km, pad).astype(BF16)


def _pad_cols(w, width):
    return jnp.pad(w, ((0, 0), (0, width - w.shape[1])))


def kernel(x, mem, rel_bias, final_norm, norm_mix, w_in, ckv_norm, w_uk, w_uv, w_oa, w_ob,
           w_out, norm_x, mem_norm, wq_x, wk_x, wv_x, wo_x, norm_moe, w_group, b_group,
           w_router, b_router, w1, w3, w2):
    bn, t, d = x.shape
    n = bn * t
    nq = t // TILE
    assert t % TILE == 0 and nq <= LANES - HEAD_DIM and norm_mix.shape[0] == 1
    near_a = _bias_tables(rel_bias[:, :A_HEADS].T)
    near_b = _bias_tables(rel_bias[:, A_HEADS:].T)
    row = lambda v: v.reshape(1, -1).astype(F32)

    wi = w_in[0]
    wp = jnp.concatenate(
        [wi[:, OFF_AQ:OFF_IK], jnp.tile(wi[:, OFF_IK:OFF_IW], (1, LANES // IDX_DIM)),
         _pad_cols(wi[:, OFF_IW:OFF_GA], LANES)], axis=1).astype(BF16)
    wkv = jnp.concatenate([w_uk[0], w_uv[0]], axis=1).astype(BF16)
    xf = x.reshape(n, d)
    qa, ka, va, qb, kb, vb, qi, ki4, widx, kmean = _proj_in(
        xf, row(norm_mix[0]), wp, row(ckv_norm[0]), wkv, nq)

    b3 = lambda a: a.reshape(bn, t, a.shape[-1])
    attn_a = _moba(b3(qa), b3(ka), b3(va), _block_mean_slabs(kmean, bn, nq), near_a)
    attn_b = _dsa(b3(qi), b3(ki4), b3(widx), b3(qb), b3(kb), b3(vb), near_b)

    kvm = _mem_kv(mem, row(mem_norm[0]), jnp.concatenate([wk_x[0], wv_x[0]], axis=1).astype(BF16))
    wrt = _pad_cols(jnp.concatenate([w_group[0], w_router[0]], axis=1), LANES).astype(F32)
    brt = _pad_cols(jnp.concatenate([b_group[0], b_router[0]]).reshape(1, -1), LANES).astype(F32)
    weights = (row(norm_mix[0]), wi[:, OFF_GA:IN_COLS].astype(BF16), w_oa[0].astype(BF16),
               w_ob[0].astype(BF16), w_out[0].astype(BF16), row(norm_x[0]), wq_x[0].astype(BF16),
               wo_x[0].astype(BF16), row(norm_moe[0]), wrt, brt)
    x2, hm, rw, ids, cnt = _post(xf, attn_a.reshape(n, W_A), attn_b.reshape(n, W_B), kvm, t // TILE, weights)

    n_rows = 2 * n + N_EXPERTS * MOE_BLOCK
    xr, blk = _dispatch(ids, cnt, hm, n_rows)
    yr = _experts(blk, xr, w1[0], w3[0], w2[0])
    out = _combine(ids, cnt, x2, rw, row(final_norm), yr)
    return out.reshape(bn, t, d)
```

```python
import functools
import math

import jax
import jax.numpy as jnp
from jax import lax
from jax.experimental import pallas as pl
from jax.experimental.pallas import tpu as pltpu

D_MODEL = 1024
HEAD_DIM = 64
A_HEADS = 8
B_HEADS = 8
W_A = A_HEADS * HEAD_DIM
W_B = B_HEADS * HEAD_DIM
MOBA_BLOCK = 256
MOBA_TOPK = 3
DSA_KV_RANK = 256
IDX_HEADS = 8
IDX_DIM = 32
DSA_TOPK = 256
REL_BUCKETS = 32
REL_MAX_DIST = 128
X_HEADS = 4
W_X = X_HEADS * HEAD_DIM
N_GROUPS = 4
EXPERTS_PER_GROUP = 8
N_EXPERTS = N_GROUPS * EXPERTS_PER_GROUP
D_EXPERT = 512
MOE_BLOCK = 256
EPS = 1e-6
NEG = -1e30

OFF_AQ = 0
OFF_AK = OFF_AQ + W_A
OFF_AV = OFF_AK + W_A
OFF_BQ = OFF_AV + W_A
OFF_CKV = OFF_BQ + W_B
OFF_IQ = OFF_CKV + DSA_KV_RANK
OFF_IK = OFF_IQ + IDX_HEADS * IDX_DIM
OFF_IW = OFF_IK + IDX_DIM
OFF_GA = OFF_IW + IDX_HEADS
OFF_GB = OFF_GA + D_MODEL
IN_COLS = OFF_GB + D_MODEL

LANES = 128
TILE = 256
VMEM_LIMIT = 56 * 1024 * 1024
BF16 = jnp.bfloat16
F32 = jnp.float32
I32 = jnp.int32

P_QA, P_KA, P_VA, P_QB = 0, 512, 1024, 1536
P_CKV, P_IQ, P_IK4, P_IW = 2048, 2304, 2560, 2688
P_COLS = 2816

R_GRP = 0
R_EXP = N_GROUPS


def _nt(a, b):
    return lax.dot_general(a, b, (((1,), (1,)), ((), ())), preferred_element_type=F32)


def _mm(a, b):
    return jnp.dot(a, b, preferred_element_type=F32)


def _rms(xf, g):
    return xf * lax.rsqrt(jnp.mean(xf * xf, axis=-1, keepdims=True) + EPS) * g


def _params(sem, vmem=VMEM_LIMIT):
    return pltpu.CompilerParams(dimension_semantics=sem, vmem_limit_bytes=vmem)


def _mem_kv_kernel(mem_ref, g_ref, w_ref, o_ref):
    m = _rms(mem_ref[0], g_ref[...]).astype(BF16)
    o_ref[0] = _mm(m, w_ref[...]).astype(BF16)


def _mem_kv(mem, g, wkv):
    bn, ml, d = mem.shape
    return pl.pallas_call(
        _mem_kv_kernel,
        grid=(bn,),
        in_specs=[pl.BlockSpec((1, ml, d), lambda b: (b, 0, 0)),
                  pl.BlockSpec((1, d), lambda b: (0, 0)),
                  pl.BlockSpec((d, 2 * W_X), lambda b: (0, 0))],
        out_specs=pl.BlockSpec((1, ml, 2 * W_X), lambda b: (b, 0, 0)),
        out_shape=jax.ShapeDtypeStruct((bn, ml, 2 * W_X), BF16),
        compiler_params=_params(("arbitrary",)),
        name="mem_kv",
    )(mem, g, wkv)


def _head_slabs(zp, extra):
    lane = lax.broadcasted_iota(I32, zp.shape, 1)
    low = lane < HEAD_DIM
    return jnp.where(low, zp, extra), jnp.where(low, pltpu.roll(zp, HEAD_DIM, 1), extra)


def _proj_in_kernel(blocks_per_seq, x_ref, g_ref, w_ref, cg_ref, wkv_ref,
                    qa_ref, ka_ref, va_ref, qb_ref, kb_ref, vb_ref, qi_ref, ki_ref, wi_ref, km_ref):
    h = _rms(x_ref[...], g_ref[...]).astype(BF16)
    z = _mm(h, w_ref[...])
    scale = HEAD_DIM ** -0.5
    zk = z[:, P_KA:P_VA]
    km_ref[0] = jnp.mean(zk, axis=0, keepdims=True)
    lane = lax.broadcasted_iota(I32, (TILE, LANES), 1)
    blk = pl.program_id(0) % blocks_per_seq
    onehot = jnp.where(lane == HEAD_DIM + blk, 1.0, 0.0)
    zero = jnp.zeros((TILE, LANES), F32)
    for p_idx in range(A_HEADS // 2):
        lo = p_idx * LANES
        qe, qo = _head_slabs(z[:, P_QA + lo:P_QA + lo + LANES] * scale, zero)
        ke, ko = _head_slabs(zk[:, lo:lo + LANES], onehot)
        qa_ref[:, 2 * lo:2 * lo + 2 * LANES] = jnp.concatenate([qe, qo], axis=1).astype(BF16)
        ka_ref[:, 2 * lo:2 * lo + 2 * LANES] = jnp.concatenate([ke, ko], axis=1).astype(BF16)
    va_ref[0] = z[:, P_VA:P_QB].T.astype(BF16)
    qb_ref[...] = (z[:, P_QB:P_CKV] * scale).astype(BF16)
    ckv = _rms(z[:, P_CKV:P_IQ], cg_ref[...]).astype(BF16)
    kv = _mm(ckv, wkv_ref[...])
    kb_ref[...] = kv[:, :W_B].astype(BF16)
    vb_ref[0] = kv[:, W_B:].T.astype(BF16)
    qi_ref[...] = z[:, P_IQ:P_IK4].astype(BF16)
    ki_ref[...] = z[:, P_IK4:P_IW].astype(BF16)
    wi_ref[...] = z[:, P_IW:P_COLS] * ((IDX_HEADS ** -0.5) * (IDX_DIM ** -0.5))


def _proj_in(xf, g, wp, cg, wkv, blocks_per_seq):
    n, d = xf.shape
    nt = n // TILE
    row = lambda w: pl.BlockSpec((TILE, w), lambda i: (i, 0))
    full = lambda a: pl.BlockSpec(a.shape, lambda i: (0,) * a.ndim)
    outs = [(2 * W_A, BF16, False), (2 * W_A, BF16, False), (W_A, BF16, True), (W_B, BF16, False),
            (W_B, BF16, False), (W_B, BF16, True), (IDX_HEADS * IDX_DIM, BF16, False), (LANES, BF16, False),
            (LANES, F32, False)]
    blocked = lambda w: pl.BlockSpec((1, w, TILE), lambda i: (i, 0, 0))
    return pl.pallas_call(
        functools.partial(_proj_in_kernel, blocks_per_seq),
        grid=(nt,),
        in_specs=[row(d), full(g), full(wp), full(cg), full(wkv)],
        out_specs=[blocked(w) if tr else row(w) for w, _, tr in outs]
        + [pl.BlockSpec((1, 1, W_A), lambda i: (i, 0, 0))],
        out_shape=[jax.ShapeDtypeStruct((nt, w, TILE) if tr else (n, w), t) for w, t, tr in outs]
        + [jax.ShapeDtypeStruct((nt, 1, W_A), F32)],
        compiler_params=_params(("arbitrary",)),
        name="proj_in",
    )(xf, g, wp, cg, wkv)


SUB = TILE // 8


def _sm_init(mx_ref, acc_ref):
    mx_ref[...] = jnp.full(mx_ref.shape, NEG, F32)
    acc_ref[...] = jnp.zeros(acc_ref.shape, F32)


LOG2E = math.log2(math.e)


def _sm_logits(hd, slot, s, sc_ref, mx_ref):
    s = s * LOG2E
    sc_ref[hd, slot] = s
    mx_ref[hd] = jnp.maximum(mx_ref[hd], jnp.max(s.reshape(SUB, 8, TILE), axis=0))


def _sm_rowmax(mx_ref):
    for hd in range(2):
        mx_ref[hd] = jnp.broadcast_to(jnp.max(mx_ref[hd], axis=0, keepdims=True), (8, TILE))


def _sm_probs(hd, slots, sc_ref, mx_ref):
    m = mx_ref[hd][None]
    ps = [jnp.exp2((sc_ref[hd, sl].reshape(SUB, 8, TILE) - m).reshape(TILE, TILE).astype(BF16))
          for sl in slots]
    return ps[0] if len(ps) == 1 else jnp.concatenate(ps, axis=0)


def _sm_accumulate(hd, pb, v, acc_ref):
    row = lax.broadcasted_iota(I32, v.shape, 0)
    own = (row < HEAD_DIM) if hd == 0 else (row >= HEAD_DIM)
    acc_ref[hd] = acc_ref[hd] + _mm(jnp.where(own, v, jnp.ones_like(v)), pb)


def _sm_values(hd, slots, v, sc_ref, mx_ref, acc_ref):
    _sm_accumulate(hd, _sm_probs(hd, slots, sc_ref, mx_ref), v, acc_ref)


FAR_GROUP = 4


def _sm_far_values(n_grp, vals, sc_ref, mx_ref, acc_ref):
    def body(g, carry):
        v = vals(FAR_GROUP * g, FAR_GROUP)
        slots = [FAR_GROUP * g + u for u in range(FAR_GROUP)]
        for hd in range(2):
            _sm_values(hd, slots, v, sc_ref, mx_ref, acc_ref)
        return carry

    lax.fori_loop(0, n_grp, body, 0)


def _sm_output(acc_ref):
    num = jnp.concatenate([acc_ref[0, :HEAD_DIM], acc_ref[1, HEAD_DIM:]], axis=0)
    den = jnp.concatenate([acc_ref[0, HEAD_DIM:], acc_ref[1, :HEAD_DIM]], axis=0)
    return (num / den).T


def _split_pair(qp):
    lane = lax.broadcasted_iota(I32, qp.shape, 1)
    zero = jnp.zeros_like(qp)
    return jnp.where(lane < HEAD_DIM, qp, zero), jnp.where(lane >= HEAD_DIM, qp, zero)


def _moba_kernel(q_ref, k_ref, v_ref, km_ref, bias_ref, o_ref, sc_ref, mx_ref, acc_ref):
    i = pl.program_id(2)
    nq = pl.num_programs(2)
    slot_prev, slot_own = nq, nq + 1
    n_slot = LANES - HEAD_DIM
    blk = lax.broadcasted_iota(I32, (n_slot, TILE), 0)
    in_blk = blk < nq
    past = blk < i
    no_mask = jnp.zeros((HEAD_DIM, TILE), F32)

    qs, q_far, q_prev = [], [], []
    for hd in range(2):
        q = q_ref[0, :, hd * LANES:(hd + 1) * LANES]
        g = jnp.where(past, _nt(km_ref[0, hd], q)[HEAD_DIM:], -jnp.inf)
        sel = jnp.zeros((n_slot, TILE), jnp.bool_)
        for _ in range(MOBA_TOPK):
            top = jnp.max(g, axis=0, keepdims=True)
            hit = jnp.logical_and(g == top, in_blk)
            first = jnp.min(jnp.where(hit, blk, n_slot), axis=0, keepdims=True)
            pick = blk == first
            sel = jnp.logical_or(sel, jnp.logical_and(pick, past))
            g = jnp.where(pick, -jnp.inf, g)
        addm = jnp.where(sel, 0.0, NEG)
        far_m = jnp.where(blk < i - 1, addm, jnp.where(in_blk, NEG, 0.0))
        prev_m = jnp.where(blk == i - 1, addm, 0.0)
        qs.append(q)
        q_far.append(q + jnp.concatenate([no_mask, far_m], axis=0).T.astype(BF16))
        q_prev.append(q + jnp.concatenate([no_mask, prev_m], axis=0).T.astype(BF16))

    def keys(j, hd):
        off = pl.multiple_of(j * TILE, TILE)
        return k_ref[0, pl.ds(off, TILE), hd * LANES:(hd + 1) * LANES]

    n_far = jnp.maximum(i - 1, 0)
    _sm_init(mx_ref, acc_ref)

    def far_logits(g, carry):
        for u in range(FAR_GROUP):
            j = FAR_GROUP * g + u
            for hd in range(2):
                _sm_logits(hd, j, _nt(keys(j, hd), q_far[hd]), sc_ref, mx_ref)
        return carry

    n_grp = (n_far + FAR_GROUP - 1) // FAR_GROUP
    lax.fori_loop(0, n_grp, far_logits, 0)

    j_prev = jnp.maximum(i - 1, 0)
    no_prev = jnp.where(i >= 1, 0.0, NEG)
    for hd in range(2):
        _sm_logits(hd, slot_prev, _nt(keys(j_prev, hd), q_prev[hd]) + (bias_ref[hd, 1] + no_prev),
                   sc_ref, mx_ref)
        _sm_logits(hd, slot_own, _nt(keys(i, hd), qs[hd]) + bias_ref[hd, 0], sc_ref, mx_ref)
    _sm_rowmax(mx_ref)

    def vals(j, n):
        return jnp.concatenate([v_ref[0, j + u] for u in range(n)], axis=1)

    _sm_far_values(n_grp, vals, sc_ref, mx_ref, acc_ref)

    v_near = jnp.concatenate([vals(j_prev, 1), vals(i, 1)], axis=1)
    for hd in range(2):
        _sm_values(hd, [slot_prev, slot_own], v_near, sc_ref, mx_ref, acc_ref)
    o_ref[0] = _sm_output(acc_ref).astype(BF16)


def _sm_scratch(nq):
    return [pltpu.VMEM((2, nq + 2, TILE, TILE), F32),
            pltpu.VMEM((2, 8, TILE), F32),
            pltpu.VMEM((2, LANES, TILE), F32)]


def _moba(qa, ka, va_t, kmp, bias_near):
    bn, nq = va_t.shape[:2]
    t = nq * TILE
    npair = A_HEADS // 2
    return pl.pallas_call(
        _moba_kernel,
        grid=(bn, npair, nq),
        in_specs=[pl.BlockSpec((1, TILE, 2 * LANES), lambda b, p, i: (b, i, p)),
                  pl.BlockSpec((1, t, 2 * LANES), lambda b, p, i: (b, 0, p)),
                  pl.BlockSpec((1, nq, LANES, TILE), lambda b, p, i: (b, 0, p, 0)),
                  pl.BlockSpec((1, 2, LANES, LANES), lambda b, p, i: (b, p, 0, 0)),
                  pl.BlockSpec((2, 2, TILE, TILE), lambda b, p, i: (p, 0, 0, 0))],
        out_specs=pl.BlockSpec((1, TILE, LANES), lambda b, p, i: (b, i, p)),
        out_shape=jax.ShapeDtypeStruct((bn, t, W_A), BF16),
        scratch_shapes=_sm_scratch(nq),
        compiler_params=_params(("arbitrary", "arbitrary", "arbitrary")),
        name="moba",
    )(qa, ka, va_t, kmp, bias_near)


def _dsa_select(i, qi_ref, ki_ref, wi_ref, s_ref, wb_ref):
    nch = i + 1
    sub = TILE // 8
    lane = lax.broadcasted_iota(I32, (TILE, LANES), 1)
    key_in = (lax.broadcasted_iota(I32, (sub, 8, TILE), 0) * 8 + lax.broadcasted_iota(I32, (sub, 8, TILE), 1))
    qpos = i * TILE + lax.broadcasted_iota(I32, (sub, 8, TILE), 2)

    def rows8(x):
        return jnp.broadcast_to(x, (8, TILE))[None]

    w_t = wi_ref[0].T
    qi = qi_ref[0]
    per_group = LANES // IDX_DIM
    qh = []
    for h in range(IDX_HEADS):
        g, r = divmod(h, per_group)
        qg = qi[:, g * LANES:(g + 1) * LANES]
        keep = jnp.logical_and(lane >= r * IDX_DIM, lane < (r + 1) * IDX_DIM)
        qh.append(jnp.where(keep, qg, jnp.zeros_like(qg)))
        wb_ref[h] = jnp.broadcast_to(w_t[h:h + 1], (8, TILE))

    def score(c, carry):
        off = pl.multiple_of(c * TILE, TILE)
        kc = ki_ref[0, pl.ds(off, TILE), :]
        acc = jnp.zeros((sub, 8, TILE), F32)
        for h in range(IDX_HEADS):
            acc = acc + jnp.maximum(_nt(kc, qh[h]), 0.0).reshape(sub, 8, TILE) * wb_ref[h][None]
        s_ref[c] = jnp.where(c * TILE + key_in <= qpos, acc, -jnp.inf).reshape(TILE, TILE)
        return carry

    lax.fori_loop(0, nch, score, 0)

    def fold(init, step):
        return lax.fori_loop(0, nch, lambda c, acc: step(acc, s_ref[c].reshape(sub, 8, TILE), c), init)

    def count(pred):
        def step(acc, blk, c):
            return acc + jnp.sum(jnp.where(pred(blk, c * TILE + key_in), 1.0, 0.0), axis=0)
        return jnp.sum(fold(jnp.zeros((8, TILE), F32), step), axis=0, keepdims=True)

    wide = rows8

    def stats(acc, blk, c):
        top, low, pos, nonneg = acc
        top = jnp.maximum(top, jnp.max(blk, axis=0))
        low = jnp.minimum(low, jnp.min(jnp.where(blk == -jnp.inf, jnp.inf, blk), axis=0))
        pos = pos + jnp.sum(jnp.where(blk > 0.0, 1.0, 0.0), axis=0)
        nonneg = nonneg + jnp.sum(jnp.where(blk >= 0.0, 1.0, 0.0), axis=0)
        return top, low, pos, nonneg

    zeros8 = jnp.zeros((8, TILE), F32)
    acc = fold((zeros8 - jnp.inf, zeros8 + jnp.inf, zeros8, zeros8), stats)
    hi = jnp.max(acc[0], axis=0, keepdims=True)
    lo = jnp.min(acc[1], axis=0, keepdims=True)
    n_pos = jnp.sum(acc[2], axis=0, keepdims=True)
    n_nonneg = jnp.sum(acc[3], axis=0, keepdims=True)

    top = hi
    top_ties = count(lambda half, kpos: half >= wide(top)) >= DSA_TOPK
    at_zero = jnp.logical_and(n_pos < DSA_TOPK, n_nonneg >= DSA_TOPK)
    lo = jnp.where(n_pos >= DSA_TOPK, 0.0, lo)
    hi = jnp.where(n_nonneg < DSA_TOPK, 0.0, hi)

    def search_more(state):
        return jnp.logical_and(jnp.min(state[3]) == 0, state[4] < 400)

    def search(state):
        lo, hi, thr, done, it = state
        mid = 0.5 * lo + 0.5 * hi
        closed = jnp.logical_or(mid <= lo, mid >= hi)
        n = count(lambda half, kpos: half >= wide(mid))
        found = jnp.where(closed, lo, mid)
        stop = jnp.logical_or(closed, n == DSA_TOPK)
        live = done == 0
        thr = jnp.where(jnp.logical_and(live, stop), found, thr)
        go = jnp.logical_and(live, jnp.logical_not(stop))
        lo = jnp.where(jnp.logical_and(go, n > DSA_TOPK), mid, lo)
        hi = jnp.where(jnp.logical_and(go, n < DSA_TOPK), mid, hi)
        return lo, hi, thr, jnp.where(stop, 1, done), it + 1

    thr0 = jnp.where(top_ties, top, 0.0)
    done0 = jnp.logical_or(top_ties, at_zero).astype(I32)
    state = (lo, hi, thr0, done0, jnp.int32(0))
    thr = wide(lax.while_loop(search_more, lambda st: search(search(st)), state)[2])
    n_gt = count(lambda half, kpos: half > thr)
    n_ge = count(lambda half, kpos: half >= thr)
    need = DSA_TOPK - n_gt

    cut_bits = max(1, (s_ref.shape[0] * TILE - 1).bit_length())

    def cut_step(b, cut):
        cand = cut | (jnp.int32(1) << (cut_bits - 1 - b))
        cw = wide(cand)
        n = count(lambda half, kpos: jnp.logical_and(half == thr, kpos < cw))
        return jnp.where(n < need, cand, cut)

    cut = lax.cond(jnp.max(n_ge) > DSA_TOPK,
                   lambda: lax.fori_loop(0, cut_bits, cut_step, jnp.zeros((1, TILE), I32)),
                   lambda: jnp.full((1, TILE), 2 ** 30, I32))
    cut = wide(jnp.where(n_ge > DSA_TOPK, cut, 2 ** 30))

    def to_mask(c, carry):
        blk = s_ref[c].reshape(sub, 8, TILE)
        kpos = c * TILE + key_in
        keep = jnp.logical_or(blk > thr, jnp.logical_and(blk == thr, kpos <= cut))
        s_ref[c] = jnp.where(jnp.logical_and(keep, kpos <= qpos), 0.0, NEG).reshape(TILE, TILE)
        return carry

    lax.fori_loop(0, nch, to_mask, 0)


def _dsa_kernel(qi_ref, ki_ref, wi_ref, q_ref, k_ref, v_ref, bias_ref, o_ref,
                s_ref, wb_ref, sc_ref, mx_ref, acc_ref):
    i = pl.program_id(1)
    pair = pl.program_id(2)

    @pl.when(jnp.logical_and(pair == 0, i == 0))
    def _():
        key = lax.broadcasted_iota(I32, (TILE, TILE), 0)
        query = lax.broadcasted_iota(I32, (TILE, TILE), 1)
        s_ref[0] = jnp.where(key <= query, 0.0, NEG)

    @pl.when(jnp.logical_and(pair == 0, i >= 1))
    def _():
        _dsa_select(i, qi_ref, ki_ref, wi_ref, s_ref, wb_ref)

    qs = _split_pair(q_ref[0])
    nq = pl.num_programs(1)
    n_far = jnp.maximum(i - 1, 0)
    _sm_init(mx_ref, acc_ref)

    def vals(j, n):
        return jnp.concatenate([v_ref[0, j + u] for u in range(n)], axis=1)

    def logits(j, slot, msk, near=None):
        k_j = k_ref[0, pl.ds(pl.multiple_of(j * TILE, TILE), TILE), :]
        for hd in range(2):
            s = _nt(k_j, qs[hd]) + msk
            if near is not None:
                s = s + bias_ref[hd, near]
            _sm_logits(hd, slot, s, sc_ref, mx_ref)

    def far_logits(g, carry):
        for u in range(FAR_GROUP):
            j = FAR_GROUP * g + u
            logits(j, j, jnp.where(j < i - 1, s_ref[jnp.minimum(j, i)], NEG))
        return carry

    n_grp = (n_far + FAR_GROUP - 1) // FAR_GROUP
    lax.fori_loop(0, n_grp, far_logits, 0)
    j_prev = jnp.maximum(i - 1, 0)
    logits(j_prev, nq, jnp.where(i >= 1, s_ref[j_prev], NEG), near=1)
    logits(i, nq + 1, s_ref[i], near=0)
    _sm_rowmax(mx_ref)

    _sm_far_values(n_grp, vals, sc_ref, mx_ref, acc_ref)
    v_near = jnp.concatenate([vals(j_prev, 1), vals(i, 1)], axis=1)
    for hd in range(2):
        _sm_values(hd, [nq, nq + 1], v_near, sc_ref, mx_ref, acc_ref)
    o_ref[0] = _sm_output(acc_ref).astype(BF16)


def _dsa(qi, ki4, wi, qb, kb, vb_t, bias_near):
    bn, t, _ = qb.shape
    nq = t // TILE
    npair = B_HEADS // 2
    qrow = lambda w: pl.BlockSpec((1, TILE, w), lambda b, i, p: (b, i, 0))
    return pl.pallas_call(
        _dsa_kernel,
        grid=(bn, nq, npair),
        in_specs=[qrow(IDX_HEADS * IDX_DIM),
                  pl.BlockSpec((1, t, LANES), lambda b, i, p: (b, 0, 0), pipeline_mode=pl.Buffered(1)),
                  qrow(LANES),
                  pl.BlockSpec((1, TILE, LANES), lambda b, i, p: (b, i, p)),
                  pl.BlockSpec((1, t, LANES), lambda b, i, p: (b, 0, p)),
                  pl.BlockSpec((1, nq, LANES, TILE), lambda b, i, p: (b, 0, p, 0)),
                  pl.BlockSpec((2, 2, TILE, TILE), lambda b, i, p: (p, 0, 0, 0))],
        out_specs=pl.BlockSpec((1, TILE, LANES), lambda b, i, p: (b, i, p)),
        out_shape=jax.ShapeDtypeStruct((bn, t, W_B), BF16),
        scratch_shapes=[pltpu.VMEM((nq, TILE, TILE), F32), pltpu.VMEM((IDX_HEADS, 8, TILE), F32)]
        + _sm_scratch(nq),
        compiler_params=_params(("arbitrary", "arbitrary", "arbitrary")),
        name="dsa",
    )(qi, ki4, wi, qb, kb, vb_t, bias_near)


def _post_kernel(x_ref, aa_ref, ab_ref, gm_ref, wg_ref, woa_ref, wob_ref, wout_ref,
                 gx_ref, wq_ref, kvm_ref, wox_ref, gmoe_ref, wrt_ref, brt_ref,
                 x2_ref, hm_ref, rw_ref, ids_ref, cnt_ref, carry_ref):
    step = pl.program_id(0)
    xf = x_ref[...]
    h = _rms(xf, gm_ref[...]).astype(BF16)
    gates = jax.nn.sigmoid(_mm(h, wg_ref[...]))
    oa = _mm(aa_ref[...], woa_ref[...])
    ob = _mm(ab_ref[...], wob_ref[...])
    mrg = gates[:, :D_MODEL] * oa + gates[:, D_MODEL:] * ob
    x1 = xf + _mm(mrg.astype(BF16), wout_ref[...])

    hx = _rms(x1, gx_ref[...]).astype(BF16)
    q = (_mm(hx, wq_ref[...]) * (HEAD_DIM ** -0.5)).astype(BF16)
    kvm = kvm_ref[0]
    lane = lax.broadcasted_iota(I32, (TILE, LANES), 1)
    outs = []
    for p_idx in range(X_HEADS // 2):
        lo = p_idx * LANES
        qs = _split_pair(q[:, lo:lo + LANES])
        km = kvm[:, lo:lo + LANES]
        vm = kvm[:, W_X + lo:W_X + lo + LANES]
        o = []
        for hd in range(2):
            s = _nt(qs[hd], km)
            p = jnp.exp(s - jnp.max(s, axis=1, keepdims=True))
            o.append(_mm(p.astype(BF16), vm) / jnp.sum(p, axis=1, keepdims=True))
        outs.append(jnp.where(lane < HEAD_DIM, o[0], o[1]))
    xo = jnp.concatenate(outs, axis=1).astype(BF16)
    x2 = x1 + _mm(xo, wox_ref[...])
    x2_ref[...] = x2

    hm = _rms(x2, gmoe_ref[...])
    hm_ref[...] = hm
    logits = jnp.dot(hm, wrt_ref[...], preferred_element_type=F32,
                     precision=lax.Precision.HIGHEST) + brt_ref[...]
    big = LANES

    def argmax(v):
        top = jnp.max(v, axis=1, keepdims=True)
        return top, jnp.min(jnp.where(v == top, lane, big), axis=1, keepdims=True)

    is_grp = lane < R_EXP
    gtop, gsel = argmax(jnp.where(is_grp, logits, -jnp.inf))
    gw = 1.0 / jnp.sum(jnp.where(is_grp, jnp.exp(logits - gtop), 0.0), axis=1, keepdims=True)
    first = R_EXP + gsel * EXPERTS_PER_GROUP
    inside = jnp.logical_and(lane >= first, lane < first + EXPERTS_PER_GROUP)
    within = jnp.where(inside, logits, -jnp.inf)
    v0, i0 = argmax(within)
    v1, i1 = argmax(jnp.where(lane == i0, -jnp.inf, within))
    e1 = jnp.exp(v1 - v0)
    w0 = gw * (1.0 / (1.0 + e1))
    w1 = gw * (e1 / (1.0 + e1))
    rw_ref[...] = jnp.where(lane == 0, w0, jnp.where(lane == 1, w1, 0.0))

    @pl.when(step == 0)
    def _():
        carry_ref[...] = jnp.zeros(carry_ref.shape, F32)

    hit0 = lane == i0
    hit1 = lane == i1
    onehot = jnp.where(jnp.logical_or(hit0, hit1), 1.0, 0.0)
    tri = (lax.broadcasted_iota(I32, (TILE, TILE), 1) < lax.broadcasted_iota(I32, (TILE, TILE), 0))
    base = carry_ref[...] + _mm(tri.astype(BF16), onehot.astype(BF16))
    r0 = jnp.sum(jnp.where(hit0, base, 0.0), axis=1, keepdims=True)
    r1 = jnp.sum(jnp.where(hit1, base, 0.0), axis=1, keepdims=True)
    total = carry_ref[...] + jnp.sum(onehot, axis=0, keepdims=True)
    carry_ref[...] = total
    cnt_ref[...] = total.astype(I32)
    slab = jnp.where(lane == 0, (i0 - R_EXP).astype(F32),
                     jnp.where(lane == 1, (i1 - R_EXP).astype(F32),
                               jnp.where(lane == 2, r0, jnp.where(lane == 3, r1, 0.0))))
    ids_ref[0] = slab.T[:8].astype(I32)


def _post(xf, aa, ab, kvm, tiles_per_batch, weights):
    n, d = xf.shape
    nt = n // TILE
    row = lambda w: pl.BlockSpec((TILE, w), lambda i: (i, 0))
    full = lambda a: pl.BlockSpec(a.shape, lambda i: (0,) * a.ndim)
    gm, wg, woa, wob, wout, gx, wq, wox, gmoe, wrt, brt = weights
    return pl.pallas_call(
        _post_kernel,
        grid=(nt,),
        in_specs=[row(d), row(W_A), row(W_B), full(gm), full(wg), full(woa), full(wob), full(wout),
                  full(gx), full(wq),
                  pl.BlockSpec((1,) + kvm.shape[1:], lambda i: (i // tiles_per_batch, 0, 0)),
                  full(wox), full(gmoe), full(wrt), full(brt)],
        out_specs=[row(d), row(d), row(LANES),
                   pl.BlockSpec((1, 8, TILE), lambda i: (i, 0, 0)),
                   pl.BlockSpec((1, LANES), lambda i: (0, 0))],
        out_shape=[jax.ShapeDtypeStruct((n, d), F32), jax.ShapeDtypeStruct((n, d), F32),
                   jax.ShapeDtypeStruct((n, LANES), F32),
                   jax.ShapeDtypeStruct((nt, 8, TILE), I32),
                   jax.ShapeDtypeStruct((1, LANES), I32)],
        scratch_shapes=[pltpu.VMEM((1, LANES), F32)],
        compiler_params=_params(("arbitrary",)),
        name="post",
    )(xf, aa, ab, gm, wg, woa, wob, wout, gx, wq, kvm, wox, gmoe, wrt, brt)


def _segment_starts(cnt_ref, seg_ref):
    def body(e, acc):
        seg_ref[e] = acc
        c = cnt_ref[0, R_EXP + e]
        return acc + ((c + (MOE_BLOCK - 1)) >> 8 << 8)
    return lax.fori_loop(0, N_EXPERTS, body, jnp.int32(0))


def _start_row_copies(ids_ref, seg_ref, rows_ref, tile_ref, sem, gather):
    def issue(t, carry):
        for k in range(2):
            dest = seg_ref[ids_ref[0, k, t]] + ids_ref[0, 2 + k, t]
            if gather:
                cp = pltpu.make_async_copy(rows_ref.at[pl.ds(dest, 1)], tile_ref.at[k, pl.ds(t, 1)], sem)
            else:
                cp = pltpu.make_async_copy(tile_ref.at[pl.ds(t, 1)], rows_ref.at[pl.ds(dest, 1)], sem)
            cp.start(priority=k)
        return carry
    lax.fori_loop(0, TILE, issue, 0, unroll=4)


def _wait_row_copies(rows_ref, tile_ref, sem):
    pltpu.make_async_copy(rows_ref.at[pl.ds(0, TILE)], tile_ref, sem).wait()


def _dispatch_kernel(ids_ref, cnt_ref, hm_ref, xr_in_ref, xr_ref, blk_ref, seg_ref, sem):
    del xr_in_ref
    step = pl.program_id(0)
    nb = blk_ref.shape[1] - 1

    @pl.when(step == 0)
    def _():
        used = _segment_starts(cnt_ref, seg_ref)

        def per_expert(e, last):
            c = cnt_ref[0, R_EXP + e]
            b0 = seg_ref[e] >> 8
            n = (c + (MOE_BLOCK - 1)) >> 8

            def fill(kk, carry):
                blk_ref[0, b0 + kk] = e
                return carry
            lax.fori_loop(0, n, fill, 0)
            return jnp.where(n > 0, e, last)
        last = lax.fori_loop(0, N_EXPERTS, per_expert, jnp.int32(0))

        def tail(b, carry):
            blk_ref[0, b] = last
            return carry
        lax.fori_loop(used >> 8, nb, tail, 0)
        blk_ref[0, nb] = used >> 8

    _start_row_copies(ids_ref, seg_ref, xr_ref, hm_ref, sem, gather=False)
    for _ in range(2):
        _wait_row_copies(xr_ref, hm_ref, sem)


def _dispatch(ids, cnt, hm, n_rows):
    n, d = hm.shape
    nt = n // TILE
    nb = n_rows // MOE_BLOCK
    zeros = jnp.zeros((n_rows, d), F32)
    return pl.pallas_call(
        _dispatch_kernel,
        grid=(nt,),
        in_specs=[pl.BlockSpec((1, 8, TILE), lambda i: (i, 0, 0), memory_space=pltpu.SMEM),
                  pl.BlockSpec(memory_space=pltpu.SMEM),
                  pl.BlockSpec((TILE, d), lambda i: (i, 0)),
                  pl.BlockSpec(memory_space=pl.ANY)],
        out_specs=[pl.BlockSpec(memory_space=pl.ANY),
                   pl.BlockSpec(memory_space=pltpu.SMEM)],
        out_shape=[jax.ShapeDtypeStruct((n_rows, d), F32),
                   jax.ShapeDtypeStruct((1, nb + 1), I32)],
        scratch_shapes=[pltpu.SMEM((N_EXPERTS,), I32), pltpu.SemaphoreType.DMA(())],
        input_output_aliases={3: 0},
        compiler_params=_params(("arbitrary",)),
        name="dispatch",
    )(ids, cnt, hm, zeros)


def _experts_kernel(blk_ref, x_ref, w1_ref, w3_ref, w2_ref, y_ref, w1b, w3b, w2b):
    b = pl.program_id(0)
    nb = pl.num_programs(0)
    e = blk_ref[0, b]
    prev = blk_ref[0, jnp.maximum(b - 1, 0)]

    @pl.when(jnp.logical_or(b == 0, e != prev))
    def _():
        w1b[...] = w1_ref[0].astype(BF16)
        w3b[...] = w3_ref[0].astype(BF16)
        w2b[...] = w2_ref[0].astype(BF16)

    used = blk_ref[0, nb]

    @pl.when(b < used)
    def _():
        xb = x_ref[...].astype(BF16)
        a = _mm(xb, w1b[...])
        g = _mm(xb, w3b[...])
        y_ref[...] = _mm((a * jax.nn.sigmoid(a) * g).astype(BF16), w2b[...])

    @pl.when(b >= used)
    def _():
        y_ref[...] = jnp.zeros(y_ref.shape, F32)


def _experts(blk, xr, w1, w3, w2):
    n_rows, d = xr.shape
    nb = n_rows // MOE_BLOCK
    wspec = lambda s: pl.BlockSpec((1,) + s, lambda b, blk: (blk[0, b], 0, 0))
    return pl.pallas_call(
        _experts_kernel,
        grid_spec=pltpu.PrefetchScalarGridSpec(
            num_scalar_prefetch=1,
            grid=(nb,),
            in_specs=[pl.BlockSpec((MOE_BLOCK, d), lambda b, blk: (b, 0)),
                      wspec((d, D_EXPERT)), wspec((d, D_EXPERT)), wspec((D_EXPERT, d))],
            out_specs=pl.BlockSpec((MOE_BLOCK, d), lambda b, blk: (b, 0)),
            scratch_shapes=[pltpu.VMEM((d, D_EXPERT), BF16), pltpu.VMEM((d, D_EXPERT), BF16),
                            pltpu.VMEM((D_EXPERT, d), BF16)]),
        out_shape=jax.ShapeDtypeStruct((n_rows, d), F32),
        compiler_params=_params(("arbitrary",)),
        name="experts",
    )(blk, xr, w1, w3, w2)


def _combine_kernel(ids_ref, ids_next_ref, cnt_ref, x2_ref, rw_ref, g_ref, yr_ref, o_ref, y_ref, seg_ref, sem):
    step = pl.program_id(0)
    slot = step % 2

    @pl.when(step == 0)
    def _():
        _segment_starts(cnt_ref, seg_ref)
        _start_row_copies(ids_ref, seg_ref, yr_ref, y_ref.at[0], sem.at[0], gather=True)

    @pl.when(step + 1 < pl.num_programs(0))
    def _():
        _start_row_copies(ids_next_ref, seg_ref, yr_ref, y_ref.at[1 - slot], sem.at[1 - slot], gather=True)

    for k in range(2):
        _wait_row_copies(yr_ref, y_ref.at[slot, k], sem.at[slot])
    rw = rw_ref[...]
    x3 = x2_ref[...] + (y_ref[slot, 0] * rw[:, 0:1] + y_ref[slot, 1] * rw[:, 1:2])
    o_ref[...] = _rms(x3, g_ref[...])


def _combine(ids, cnt, x2, rw, g, yr):
    n, d = x2.shape
    nt = n // TILE
    row = lambda w: pl.BlockSpec((TILE, w), lambda i: (i, 0))
    return pl.pallas_call(
        _combine_kernel,
        grid=(nt,),
        in_specs=[pl.BlockSpec((1, 8, TILE), lambda i: (i, 0, 0), memory_space=pltpu.SMEM),
                  pl.BlockSpec((1, 8, TILE), lambda i: (jnp.minimum(i + 1, nt - 1), 0, 0),
                               memory_space=pltpu.SMEM),
                  pl.BlockSpec(memory_space=pltpu.SMEM),
                  row(d), row(LANES), pl.BlockSpec((1, d), lambda i: (0, 0)),
                  pl.BlockSpec(memory_space=pl.ANY)],
        out_specs=row(d),
        out_shape=jax.ShapeDtypeStruct((n, d), F32),
        scratch_shapes=[pltpu.VMEM((2, 2, TILE, d), F32), pltpu.SMEM((N_EXPERTS,), I32),
                        pltpu.SemaphoreType.DMA((2,))],
        compiler_params=_params(("arbitrary",)),
        name="combine",
    )(ids, ids, cnt, x2, rw, g, yr)


def _t5_bucket(dist):
    n = jnp.maximum(dist, 0)
    max_exact = REL_BUCKETS // 2
    nf = jnp.maximum(n, 1).astype(F32)
    large = max_exact + (jnp.log(nf / max_exact) / math.log(REL_MAX_DIST / max_exact)
                         * (REL_BUCKETS - max_exact)).astype(I32)
    large = jnp.minimum(large, REL_BUCKETS - 1)
    return jnp.where(n < max_exact, n, large)


def _bias_tables(tab):
    r = jnp.arange(TILE)[:, None]
    c = jnp.arange(TILE)[None, :]
    rel = tab - tab[:, REL_BUCKETS - 1:]

    def tile(dist):
        onehot = (_t5_bucket(dist)[..., None] == jnp.arange(REL_BUCKETS)).astype(F32)
        return jnp.einsum('rcb,hb->hrc', onehot, rel, precision=lax.Precision.HIGHEST)

    own = jnp.where(r >= c, tile(r - c), NEG)
    near = jnp.stack([own, tile(r - c + TILE)], axis=1).astype(F32)
    return near.swapaxes(-1, -2)


def _block_mean_slabs(kmean, bn, nq):
    km = kmean.reshape(bn, nq, A_HEADS, HEAD_DIM).transpose(0, 2, 1, 3)
    pad = ((0, 0), (0, 0), (HEAD_DIM, LANES - HEAD_DIM - nq), (0, LANES - HEAD_DIM))
    return jnp.pad(km, pad).astype(BF16)


def _pad_cols(w, width):
    return jnp.pad(w, ((0, 0), (0, width - w.shape[1])))


def kernel(x, mem, rel_bias, final_norm, norm_mix, w_in, ckv_norm, w_uk, w_uv, w_oa, w_ob,
           w_out, norm_x, mem_norm, wq_x, wk_x, wv_x, wo_x, norm_moe, w_group, b_group,
           w_router, b_router, w1, w3, w2):
    bn, t, d = x.shape
    n = bn * t
    nq = t // TILE
    assert t % TILE == 0 and nq <= LANES - HEAD_DIM and norm_mix.shape[0] == 1
    near_a = _bias_tables(rel_bias[:, :A_HEADS].T)
    near_b = _bias_tables(rel_bias[:, A_HEADS:].T)
    row = lambda v: v.reshape(1, -1).astype(F32)

    wi = w_in[0]
    wp = jnp.concatenate(
        [wi[:, OFF_AQ:OFF_IK], jnp.tile(wi[:, OFF_IK:OFF_IW], (1, LANES // IDX_DIM)),
         _pad_cols(wi[:, OFF_IW:OFF_GA], LANES)], axis=1).astype(BF16)
    wkv = jnp.concatenate([w_uk[0], w_uv[0]], axis=1).astype(BF16)
    xf = x.reshape(n, d)
    qa, ka, va, qb, kb, vb, qi, ki4, widx, kmean = _proj_in(
        xf, row(norm_mix[0]), wp, row(ckv_norm[0]), wkv, nq)

    b3 = lambda a: a.reshape(bn, t, a.shape[-1])
    attn_a = _moba(b3(qa), b3(ka), va.reshape(bn, nq, W_A, TILE), _block_mean_slabs(kmean, bn, nq), near_a)
    attn_b = _dsa(b3(qi), b3(ki4), b3(widx), b3(qb), b3(kb), vb.reshape(bn, nq, W_B, TILE), near_b)

    kvm = _mem_kv(mem, row(mem_norm[0]), jnp.concatenate([wk_x[0], wv_x[0]], axis=1).astype(BF16))
    wrt = _pad_cols(jnp.concatenate([w_group[0], w_router[0]], axis=1), LANES).astype(F32)
    brt = _pad_cols(jnp.concatenate([b_group[0], b_router[0]]).reshape(1, -1), LANES).astype(F32)
    weights = (row(norm_mix[0]), wi[:, OFF_GA:IN_COLS].astype(BF16), w_oa[0].astype(BF16),
               w_ob[0].astype(BF16), w_out[0].astype(BF16), row(norm_x[0]), wq_x[0].astype(BF16),
               wo_x[0].astype(BF16), row(norm_moe[0]), wrt, brt)
    x2, hm, rw, ids, cnt = _post(xf, attn_a.reshape(n, W_A), attn_b.reshape(n, W_B), kvm, t // TILE, weights)

    n_rows = 2 * n + N_EXPERTS * MOE_BLOCK
    xr, blk = _dispatch(ids, cnt, hm, n_rows)
    yr = _experts(blk, xr, w1[0], w3[0], w2[0])
    out = _combine(ids, cnt, x2, rw, row(final_norm), yr)
    return out.reshape(bn, t, d)
```

```python
import functools
import math

import jax
import jax.numpy as jnp
from jax import lax
from jax.experimental import pallas as pl
from jax.experimental.pallas import tpu as pltpu

D_MODEL = 1024
HEAD_DIM = 64
A_HEADS = 8
B_HEADS = 8
W_A = A_HEADS * HEAD_DIM
W_B = B_HEADS * HEAD_DIM
MOBA_BLOCK = 256
MOBA_TOPK = 3
DSA_KV_RANK = 256
IDX_HEADS = 8
IDX_DIM = 32
DSA_TOPK = 256
REL_BUCKETS = 32
REL_MAX_DIST = 128
X_HEADS = 4
W_X = X_HEADS * HEAD_DIM
N_GROUPS = 4
EXPERTS_PER_GROUP = 8
N_EXPERTS = N_GROUPS * EXPERTS_PER_GROUP
D_EXPERT = 512
MOE_BLOCK = 256
EPS = 1e-6
NEG = -1e30

OFF_AQ = 0
OFF_AK = OFF_AQ + W_A
OFF_AV = OFF_AK + W_A
OFF_BQ = OFF_AV + W_A
OFF_CKV = OFF_BQ + W_B
OFF_IQ = OFF_CKV + DSA_KV_RANK
OFF_IK = OFF_IQ + IDX_HEADS * IDX_DIM
OFF_IW = OFF_IK + IDX_DIM
OFF_GA = OFF_IW + IDX_HEADS
OFF_GB = OFF_GA + D_MODEL
IN_COLS = OFF_GB + D_MODEL

LANES = 128
TILE = 256
VMEM_LIMIT = 56 * 1024 * 1024
BF16 = jnp.bfloat16
F32 = jnp.float32
I32 = jnp.int32

P_QA, P_KA, P_VA, P_QB = 0, 512, 1024, 1536
P_CKV, P_IQ, P_IK4, P_IW = 2048, 2304, 2560, 2688
P_COLS = 2816

R_GRP = 0
R_EXP = N_GROUPS


def _nt(a, b):
    return lax.dot_general(a, b, (((1,), (1,)), ((), ())), preferred_element_type=F32)


def _mm(a, b):
    return jnp.dot(a, b, preferred_element_type=F32)


def _rms(xf, g):
    return xf * lax.rsqrt(jnp.mean(xf * xf, axis=-1, keepdims=True) + EPS) * g


def _params(sem, vmem=VMEM_LIMIT):
    return pltpu.CompilerParams(dimension_semantics=sem, vmem_limit_bytes=vmem)


def _mem_kv_kernel(mem_ref, g_ref, w_ref, o_ref):
    m = _rms(mem_ref[0], g_ref[...]).astype(BF16)
    o_ref[0] = _mm(m, w_ref[...]).astype(BF16)


def _mem_kv(mem, g, wkv):
    bn, ml, d = mem.shape
    return pl.pallas_call(
        _mem_kv_kernel,
        grid=(bn,),
        in_specs=[pl.BlockSpec((1, ml, d), lambda b: (b, 0, 0)),
                  pl.BlockSpec((1, d), lambda b: (0, 0)),
                  pl.BlockSpec((d, 2 * W_X), lambda b: (0, 0))],
        out_specs=pl.BlockSpec((1, ml, 2 * W_X), lambda b: (b, 0, 0)),
        out_shape=jax.ShapeDtypeStruct((bn, ml, 2 * W_X), BF16),
        compiler_params=_params(("arbitrary",)),
        name="mem_kv",
    )(mem, g, wkv)


def _head_slabs(zp, extra):
    lane = lax.broadcasted_iota(I32, zp.shape, 1)
    low = lane < HEAD_DIM
    return jnp.where(low, zp, extra), jnp.where(low, pltpu.roll(zp, HEAD_DIM, 1), extra)


def _proj_in_kernel(blocks_per_seq, x_ref, g_ref, w_ref, cg_ref, wkv_ref,
                    qa_ref, ka_ref, va_ref, qb_ref, kb_ref, vb_ref, qi_ref, ki_ref, wi_ref, km_ref):
    h = _rms(x_ref[...], g_ref[...]).astype(BF16)
    z = _mm(h, w_ref[...])
    scale = HEAD_DIM ** -0.5
    zk = z[:, P_KA:P_VA]
    km_ref[0] = jnp.mean(zk, axis=0, keepdims=True)
    lane = lax.broadcasted_iota(I32, (TILE, LANES), 1)
    blk = pl.program_id(0) % blocks_per_seq
    onehot = jnp.where(lane == HEAD_DIM + blk, 1.0, 0.0)
    zero = jnp.zeros((TILE, LANES), F32)
    for p_idx in range(A_HEADS // 2):
        lo = p_idx * LANES
        qe, qo = _head_slabs(z[:, P_QA + lo:P_QA + lo + LANES] * scale, zero)
        ke, ko = _head_slabs(zk[:, lo:lo + LANES], onehot)
        qa_ref[:, 2 * lo:2 * lo + 2 * LANES] = jnp.concatenate([qe, qo], axis=1).astype(BF16)
        ka_ref[:, 2 * lo:2 * lo + 2 * LANES] = jnp.concatenate([ke, ko], axis=1).astype(BF16)
    va_ref[0] = z[:, P_VA:P_QB].T.astype(BF16)
    qb_ref[...] = (z[:, P_QB:P_CKV] * scale).astype(BF16)
    ckv = _rms(z[:, P_CKV:P_IQ], cg_ref[...]).astype(BF16)
    kv = _mm(ckv, wkv_ref[...])
    kb_ref[...] = kv[:, :W_B].astype(BF16)
    vb_ref[0] = kv[:, W_B:].T.astype(BF16)
    qi_ref[...] = z[:, P_IQ:P_IK4].astype(BF16)
    ki_ref[...] = z[:, P_IK4:P_IW].astype(BF16)
    wi_ref[...] = z[:, P_IW:P_COLS] * ((IDX_HEADS ** -0.5) * (IDX_DIM ** -0.5))


def _proj_in(xf, g, wp, cg, wkv, blocks_per_seq):
    n, d = xf.shape
    nt = n // TILE
    row = lambda w: pl.BlockSpec((TILE, w), lambda i: (i, 0))
    full = lambda a: pl.BlockSpec(a.shape, lambda i: (0,) * a.ndim)
    outs = [(2 * W_A, BF16, False), (2 * W_A, BF16, False), (W_A, BF16, True), (W_B, BF16, False),
            (W_B, BF16, False), (W_B, BF16, True), (IDX_HEADS * IDX_DIM, BF16, False), (LANES, BF16, False),
            (LANES, F32, False)]
    blocked = lambda w: pl.BlockSpec((1, w, TILE), lambda i: (i, 0, 0))
    return pl.pallas_call(
        functools.partial(_proj_in_kernel, blocks_per_seq),
        grid=(nt,),
        in_specs=[row(d), full(g), full(wp), full(cg), full(wkv)],
        out_specs=[blocked(w) if tr else row(w) for w, _, tr in outs]
        + [pl.BlockSpec((1, 1, W_A), lambda i: (i, 0, 0))],
        out_shape=[jax.ShapeDtypeStruct((nt, w, TILE) if tr else (n, w), t) for w, t, tr in outs]
        + [jax.ShapeDtypeStruct((nt, 1, W_A), F32)],
        compiler_params=_params(("arbitrary",)),
        name="proj_in",
    )(xf, g, wp, cg, wkv)


SUB = TILE // 8


def _sm_init(mx_ref, acc_ref):
    mx_ref[...] = jnp.full(mx_ref.shape, NEG, F32)
    acc_ref[...] = jnp.zeros(acc_ref.shape, F32)


LOG2E = math.log2(math.e)


def _sm_logits(hd, slot, s, sc_ref, mx_ref):
    s = s * LOG2E
    sc_ref[hd, slot] = s
    mx_ref[hd] = jnp.maximum(mx_ref[hd], jnp.max(s.reshape(SUB, 8, TILE), axis=0))


def _sm_rowmax(mx_ref):
    for hd in range(2):
        mx_ref[hd] = jnp.broadcast_to(jnp.max(mx_ref[hd], axis=0, keepdims=True), (8, TILE))


def _sm_probs(hd, slots, sc_ref, mx_ref):
    m = mx_ref[hd][None]
    ps = [jnp.exp2((sc_ref[hd, sl].reshape(SUB, 8, TILE) - m).reshape(TILE, TILE).astype(BF16))
          for sl in slots]
    return ps[0] if len(ps) == 1 else jnp.concatenate(ps, axis=0)


def _sm_accumulate(hd, pb, v, acc_ref):
    row = lax.broadcasted_iota(I32, v.shape, 0)
    own = (row < HEAD_DIM) if hd == 0 else (row >= HEAD_DIM)
    acc_ref[hd] = acc_ref[hd] + _mm(jnp.where(own, v, jnp.ones_like(v)), pb)


def _sm_values(hd, slots, v, sc_ref, mx_ref, acc_ref):
    _sm_accumulate(hd, _sm_probs(hd, slots, sc_ref, mx_ref), v, acc_ref)


FAR_GROUP = 4


def _sm_far_values(n_grp, vals, sc_ref, mx_ref, acc_ref):
    def body(g, carry):
        v = vals(FAR_GROUP * g, FAR_GROUP)
        slots = [FAR_GROUP * g + u for u in range(FAR_GROUP)]
        for hd in range(2):
            _sm_values(hd, slots, v, sc_ref, mx_ref, acc_ref)
        return carry

    lax.fori_loop(0, n_grp, body, 0)


def _sm_output(acc_ref):
    num = jnp.concatenate([acc_ref[0, :HEAD_DIM], acc_ref[1, HEAD_DIM:]], axis=0)
    den = jnp.concatenate([acc_ref[0, HEAD_DIM:], acc_ref[1, :HEAD_DIM]], axis=0)
    return (num / den).T


def _split_pair(qp):
    lane = lax.broadcasted_iota(I32, qp.shape, 1)
    zero = jnp.zeros_like(qp)
    return jnp.where(lane < HEAD_DIM, qp, zero), jnp.where(lane >= HEAD_DIM, qp, zero)


def _moba_kernel(q_ref, k_ref, v_ref, km_ref, bias_ref, o_ref, sc_ref, mx_ref, acc_ref):
    i = pl.program_id(2)
    nq = pl.num_programs(2)
    slot_prev, slot_own = nq, nq + 1
    n_slot = LANES - HEAD_DIM
    blk = lax.broadcasted_iota(I32, (n_slot, TILE), 0)
    in_blk = blk < nq
    past = blk < i
    no_mask = jnp.zeros((HEAD_DIM, TILE), F32)

    qs, q_far, q_prev = [], [], []
    for hd in range(2):
        q = q_ref[0, :, hd * LANES:(hd + 1) * LANES]
        g = jnp.where(past, _nt(km_ref[0, hd], q)[HEAD_DIM:], -jnp.inf)
        sel = jnp.zeros((n_slot, TILE), jnp.bool_)
        for _ in range(MOBA_TOPK):
            top = jnp.max(g, axis=0, keepdims=True)
            hit = jnp.logical_and(g == top, in_blk)
            first = jnp.min(jnp.where(hit, blk, n_slot), axis=0, keepdims=True)
            pick = blk == first
            sel = jnp.logical_or(sel, jnp.logical_and(pick, past))
            g = jnp.where(pick, -jnp.inf, g)
        addm = jnp.where(sel, 0.0, NEG)
        far_m = jnp.where(blk < i - 1, addm, jnp.where(in_blk, NEG, 0.0))
        prev_m = jnp.where(blk == i - 1, addm, 0.0)
        qs.append(q)
        q_far.append(q + jnp.concatenate([no_mask, far_m], axis=0).T.astype(BF16))
        q_prev.append(q + jnp.concatenate([no_mask, prev_m], axis=0).T.astype(BF16))

    def keys(j, hd):
        off = pl.multiple_of(j * TILE, TILE)
        return k_ref[0, pl.ds(off, TILE), hd * LANES:(hd + 1) * LANES]

    n_far = jnp.maximum(i - 1, 0)
    _sm_init(mx_ref, acc_ref)

    def far_logits(g, carry):
        for u in range(FAR_GROUP):
            j = FAR_GROUP * g + u
            for hd in range(2):
                _sm_logits(hd, j, _nt(keys(j, hd), q_far[hd]), sc_ref, mx_ref)
        return carry

    n_grp = (n_far + FAR_GROUP - 1) // FAR_GROUP
    lax.fori_loop(0, n_grp, far_logits, 0)

    j_prev = jnp.maximum(i - 1, 0)
    no_prev = jnp.where(i >= 1, 0.0, NEG)
    for hd in range(2):
        _sm_logits(hd, slot_prev, _nt(keys(j_prev, hd), q_prev[hd]) + (bias_ref[hd, 1] + no_prev),
                   sc_ref, mx_ref)
        _sm_logits(hd, slot_own, _nt(keys(i, hd), qs[hd]) + bias_ref[hd, 0], sc_ref, mx_ref)
    _sm_rowmax(mx_ref)

    def vals(j, n):
        return jnp.concatenate([v_ref[0, j + u] for u in range(n)], axis=1)

    _sm_far_values(n_grp, vals, sc_ref, mx_ref, acc_ref)

    v_near = jnp.concatenate([vals(j_prev, 1), vals(i, 1)], axis=1)
    for hd in range(2):
        _sm_values(hd, [slot_prev, slot_own], v_near, sc_ref, mx_ref, acc_ref)
    o_ref[0] = _sm_output(acc_ref).astype(BF16)


def _sm_scratch(nq):
    return [pltpu.VMEM((2, nq + 2, TILE, TILE), F32),
            pltpu.VMEM((2, 8, TILE), F32),
            pltpu.VMEM((2, LANES, TILE), F32)]


def _moba(qa, ka, va_t, kmp, bias_near):
    bn, nq = va_t.shape[:2]
    t = nq * TILE
    npair = A_HEADS // 2
    return pl.pallas_call(
        _moba_kernel,
        grid=(bn, npair, nq),
        in_specs=[pl.BlockSpec((1, TILE, 2 * LANES), lambda b, p, i: (b, i, p)),
                  pl.BlockSpec((1, t, 2 * LANES), lambda b, p, i: (b, 0, p)),
                  pl.BlockSpec((1, nq, LANES, TILE), lambda b, p, i: (b, 0, p, 0)),
                  pl.BlockSpec((1, 2, LANES, LANES), lambda b, p, i: (b, p, 0, 0)),
                  pl.BlockSpec((2, 2, TILE, TILE), lambda b, p, i: (p, 0, 0, 0))],
        out_specs=pl.BlockSpec((1, TILE, LANES), lambda b, p, i: (b, i, p)),
        out_shape=jax.ShapeDtypeStruct((bn, t, W_A), BF16),
        scratch_shapes=_sm_scratch(nq),
        compiler_params=_params(("arbitrary", "arbitrary", "arbitrary")),
        name="moba",
    )(qa, ka, va_t, kmp, bias_near)


def _dsa_select(i, qi_ref, ki_ref, wi_ref, s_ref, wb_ref):
    nch = i + 1
    sub = TILE // 8
    lane = lax.broadcasted_iota(I32, (TILE, LANES), 1)
    key_in = (lax.broadcasted_iota(I32, (sub, 8, TILE), 0) * 8 + lax.broadcasted_iota(I32, (sub, 8, TILE), 1))
    qpos = i * TILE + lax.broadcasted_iota(I32, (sub, 8, TILE), 2)

    def rows8(x):
        return jnp.broadcast_to(x, (8, TILE))[None]

    w_t = wi_ref[0].T
    qi = qi_ref[0]
    per_group = LANES // IDX_DIM
    qh = []
    for h in range(IDX_HEADS):
        g, r = divmod(h, per_group)
        qg = qi[:, g * LANES:(g + 1) * LANES]
        keep = jnp.logical_and(lane >= r * IDX_DIM, lane < (r + 1) * IDX_DIM)
        qh.append(jnp.where(keep, qg, jnp.zeros_like(qg)))
        wb_ref[h] = jnp.broadcast_to(w_t[h:h + 1], (8, TILE))

    def score(c, carry):
        off = pl.multiple_of(c * TILE, TILE)
        kc = ki_ref[0, pl.ds(off, TILE), :]
        acc = jnp.zeros((sub, 8, TILE), F32)
        for h in range(IDX_HEADS):
            acc = acc + jnp.maximum(_nt(kc, qh[h]), 0.0).reshape(sub, 8, TILE) * wb_ref[h][None]
        s_ref[c] = jnp.where(c * TILE + key_in <= qpos, acc, -jnp.inf).reshape(TILE, TILE)
        return carry

    n_pair = (nch + 1) // 2

    def pairs(step, init):
        return lax.fori_loop(0, n_pair, lambda g, acc: step(2 * g + 1, step(2 * g, acc)), init)

    pairs(score, 0)

    def fold(init, step):
        return pairs(lambda c, acc: step(acc, s_ref[c].reshape(sub, 8, TILE), c), init)

    def count(pred):
        def step(acc, blk, c):
            return acc + jnp.sum(jnp.where(pred(blk, c * TILE + key_in), 1.0, 0.0), axis=0)
        return jnp.sum(fold(jnp.zeros((8, TILE), F32), step), axis=0, keepdims=True)

    wide = rows8

    def stats(acc, blk, c):
        top, low, pos, nonneg = acc
        top = jnp.maximum(top, jnp.max(blk, axis=0))
        low = jnp.minimum(low, jnp.min(jnp.where(blk == -jnp.inf, jnp.inf, blk), axis=0))
        pos = pos + jnp.sum(jnp.where(blk > 0.0, 1.0, 0.0), axis=0)
        nonneg = nonneg + jnp.sum(jnp.where(blk >= 0.0, 1.0, 0.0), axis=0)
        return top, low, pos, nonneg

    zeros8 = jnp.zeros((8, TILE), F32)
    acc = fold((zeros8 - jnp.inf, zeros8 + jnp.inf, zeros8, zeros8), stats)
    hi = jnp.max(acc[0], axis=0, keepdims=True)
    lo = jnp.min(acc[1], axis=0, keepdims=True)
    n_pos = jnp.sum(acc[2], axis=0, keepdims=True)
    n_nonneg = jnp.sum(acc[3], axis=0, keepdims=True)

    top = hi
    top_ties = count(lambda half, kpos: half >= wide(top)) >= DSA_TOPK
    at_zero = jnp.logical_and(n_pos < DSA_TOPK, n_nonneg >= DSA_TOPK)
    lo = jnp.where(n_pos >= DSA_TOPK, 0.0, lo)
    hi = jnp.where(n_nonneg < DSA_TOPK, 0.0, hi)

    def search_more(state):
        return jnp.logical_and(jnp.min(state[3]) == 0, state[4] < 400)

    def search(state):
        lo, hi, thr, done, it = state
        mid = 0.5 * lo + 0.5 * hi
        closed = jnp.logical_or(mid <= lo, mid >= hi)
        n = count(lambda half, kpos: half >= wide(mid))
        found = jnp.where(closed, lo, mid)
        stop = jnp.logical_or(closed, n == DSA_TOPK)
        live = done == 0
        thr = jnp.where(jnp.logical_and(live, stop), found, thr)
        go = jnp.logical_and(live, jnp.logical_not(stop))
        lo = jnp.where(jnp.logical_and(go, n > DSA_TOPK), mid, lo)
        hi = jnp.where(jnp.logical_and(go, n < DSA_TOPK), mid, hi)
        return lo, hi, thr, jnp.where(stop, 1, done), it + 1

    thr0 = jnp.where(top_ties, top, 0.0)
    done0 = jnp.logical_or(top_ties, at_zero).astype(I32)
    state = (lo, hi, thr0, done0, jnp.int32(0))
    thr = wide(lax.while_loop(search_more, lambda st: search(search(st)), state)[2])
    n_gt = count(lambda half, kpos: half > thr)
    n_ge = count(lambda half, kpos: half >= thr)
    need = DSA_TOPK - n_gt

    cut_bits = max(1, (s_ref.shape[0] * TILE - 1).bit_length())

    def cut_step(b, cut):
        cand = cut | (jnp.int32(1) << (cut_bits - 1 - b))
        cw = wide(cand)
        n = count(lambda half, kpos: jnp.logical_and(half == thr, kpos < cw))
        return jnp.where(n < need, cand, cut)

    cut = lax.cond(jnp.max(n_ge) > DSA_TOPK,
                   lambda: lax.fori_loop(0, cut_bits, cut_step, jnp.zeros((1, TILE), I32)),
                   lambda: jnp.full((1, TILE), 2 ** 30, I32))
    cut = wide(jnp.where(n_ge > DSA_TOPK, cut, 2 ** 30))

    def to_mask(c, carry):
        blk = s_ref[c].reshape(sub, 8, TILE)
        kpos = c * TILE + key_in
        keep = jnp.logical_or(blk > thr, jnp.logical_and(blk == thr, kpos <= cut))
        s_ref[c] = jnp.where(jnp.logical_and(keep, kpos <= qpos), 0.0, NEG).reshape(TILE, TILE)
        return carry

    pairs(to_mask, 0)


def _dsa_kernel(qi_ref, ki_ref, wi_ref, q_ref, k_ref, v_ref, bias_ref, o_ref,
                s_ref, wb_ref, sc_ref, mx_ref, acc_ref):
    i = pl.program_id(1)
    pair = pl.program_id(2)

    @pl.when(jnp.logical_and(pair == 0, i == 0))
    def _():
        key = lax.broadcasted_iota(I32, (TILE, TILE), 0)
        query = lax.broadcasted_iota(I32, (TILE, TILE), 1)
        s_ref[0] = jnp.where(key <= query, 0.0, NEG)

    @pl.when(jnp.logical_and(pair == 0, i >= 1))
    def _():
        _dsa_select(i, qi_ref, ki_ref, wi_ref, s_ref, wb_ref)

    qs = _split_pair(q_ref[0])
    nq = pl.num_programs(1)
    n_far = jnp.maximum(i - 1, 0)
    _sm_init(mx_ref, acc_ref)

    def vals(j, n):
        return jnp.concatenate([v_ref[0, j + u] for u in range(n)], axis=1)

    def logits(j, slot, msk, near=None):
        k_j = k_ref[0, pl.ds(pl.multiple_of(j * TILE, TILE), TILE), :]
        for hd in range(2):
            s = _nt(k_j, qs[hd]) + msk
            if near is not None:
                s = s + bias_ref[hd, near]
            _sm_logits(hd, slot, s, sc_ref, mx_ref)

    def far_logits(g, carry):
        for u in range(FAR_GROUP):
            j = FAR_GROUP * g + u
            logits(j, j, jnp.where(j < i - 1, s_ref[jnp.minimum(j, i)], NEG))
        return carry

    n_grp = (n_far + FAR_GROUP - 1) // FAR_GROUP
    lax.fori_loop(0, n_grp, far_logits, 0)
    j_prev = jnp.maximum(i - 1, 0)
    logits(j_prev, nq, jnp.where(i >= 1, s_ref[j_prev], NEG), near=1)
    logits(i, nq + 1, s_ref[i], near=0)
    _sm_rowmax(mx_ref)

    _sm_far_values(n_grp, vals, sc_ref, mx_ref, acc_ref)
    v_near = jnp.concatenate([vals(j_prev, 1), vals(i, 1)], axis=1)
    for hd in range(2):
        _sm_values(hd, [nq, nq + 1], v_near, sc_ref, mx_ref, acc_ref)
    o_ref[0] = _sm_output(acc_ref).astype(BF16)


def _dsa(qi, ki4, wi, qb, kb, vb_t, bias_near):
    bn, t, _ = qb.shape
    nq = t // TILE
    assert nq % 2 == 0
    npair = B_HEADS // 2
    qrow = lambda w: pl.BlockSpec((1, TILE, w), lambda b, i, p: (b, i, 0))
    return pl.pallas_call(
        _dsa_kernel,
        grid=(bn, nq, npair),
        in_specs=[qrow(IDX_HEADS * IDX_DIM),
                  pl.BlockSpec((1, t, LANES), lambda b, i, p: (b, 0, 0), pipeline_mode=pl.Buffered(1)),
                  qrow(LANES),
                  pl.BlockSpec((1, TILE, LANES), lambda b, i, p: (b, i, p)),
                  pl.BlockSpec((1, t, LANES), lambda b, i, p: (b, 0, p)),
                  pl.BlockSpec((1, nq, LANES, TILE), lambda b, i, p: (b, 0, p, 0)),
                  pl.BlockSpec((2, 2, TILE, TILE), lambda b, i, p: (p, 0, 0, 0))],
        out_specs=pl.BlockSpec((1, TILE, LANES), lambda b, i, p: (b, i, p)),
        out_shape=jax.ShapeDtypeStruct((bn, t, W_B), BF16),
        scratch_shapes=[pltpu.VMEM((nq, TILE, TILE), F32), pltpu.VMEM((IDX_HEADS, 8, TILE), F32)]
        + _sm_scratch(nq),
        compiler_params=_params(("arbitrary", "arbitrary", "arbitrary")),
        name="dsa",
    )(qi, ki4, wi, qb, kb, vb_t, bias_near)


def _post_kernel(x_ref, aa_ref, ab_ref, gm_ref, wg_ref, woa_ref, wob_ref, wout_ref,
                 gx_ref, wq_ref, kvm_ref, wox_ref, gmoe_ref, wrt_ref, brt_ref,
                 x2_ref, hm_ref, rw_ref, ids_ref, cnt_ref, carry_ref):
    step = pl.program_id(0)
    xf = x_ref[...]
    h = _rms(xf, gm_ref[...]).astype(BF16)
    gates = jax.nn.sigmoid(_mm(h, wg_ref[...]))
    oa = _mm(aa_ref[...], woa_ref[...])
    ob = _mm(ab_ref[...], wob_ref[...])
    mrg = gates[:, :D_MODEL] * oa + gates[:, D_MODEL:] * ob
    x1 = xf + _mm(mrg.astype(BF16), wout_ref[...])

    hx = _rms(x1, gx_ref[...]).astype(BF16)
    q = (_mm(hx, wq_ref[...]) * (HEAD_DIM ** -0.5)).astype(BF16)
    kvm = kvm_ref[0]
    lane = lax.broadcasted_iota(I32, (TILE, LANES), 1)
    outs = []
    for p_idx in range(X_HEADS // 2):
        lo = p_idx * LANES
        qs = _split_pair(q[:, lo:lo + LANES])
        km = kvm[:, lo:lo + LANES]
        vm = kvm[:, W_X + lo:W_X + lo + LANES]
        o = []
        for hd in range(2):
            s = _nt(qs[hd], km)
            p = jnp.exp(s - jnp.max(s, axis=1, keepdims=True))
            o.append(_mm(p.astype(BF16), vm) / jnp.sum(p, axis=1, keepdims=True))
        outs.append(jnp.where(lane < HEAD_DIM, o[0], o[1]))
    xo = jnp.concatenate(outs, axis=1).astype(BF16)
    x2 = x1 + _mm(xo, wox_ref[...])
    x2_ref[...] = x2

    hm = _rms(x2, gmoe_ref[...])
    hm_ref[...] = hm
    hm_hi = hm.astype(BF16)
    hm_lo = (hm - hm_hi.astype(F32)).astype(BF16)
    hi_part = _mm(hm_hi, wrt_ref[...])
    logits = (hi_part[:, :LANES] + hi_part[:, LANES:]) + _mm(hm_lo, wrt_ref[:, :LANES]) + brt_ref[...]
    big = LANES

    def argmax(v):
        top = jnp.max(v, axis=1, keepdims=True)
        return top, jnp.min(jnp.where(v == top, lane, big), axis=1, keepdims=True)

    is_grp = lane < R_EXP
    gtop, gsel = argmax(jnp.where(is_grp, logits, -jnp.inf))
    gw = 1.0 / jnp.sum(jnp.where(is_grp, jnp.exp(logits - gtop), 0.0), axis=1, keepdims=True)
    first = R_EXP + gsel * EXPERTS_PER_GROUP
    inside = jnp.logical_and(lane >= first, lane < first + EXPERTS_PER_GROUP)
    within = jnp.where(inside, logits, -jnp.inf)
    v0, i0 = argmax(within)
    v1, i1 = argmax(jnp.where(lane == i0, -jnp.inf, within))
    e1 = jnp.exp(v1 - v0)
    w0 = gw * (1.0 / (1.0 + e1))
    w1 = gw * (e1 / (1.0 + e1))
    rw_ref[...] = jnp.where(lane == 0, w0, jnp.where(lane == 1, w1, 0.0))

    @pl.when(step == 0)
    def _():
        carry_ref[...] = jnp.zeros(carry_ref.shape, F32)

    hit0 = lane == i0
    hit1 = lane == i1
    onehot = jnp.where(jnp.logical_or(hit0, hit1), 1.0, 0.0)
    tri = (lax.broadcasted_iota(I32, (TILE, TILE), 1) < lax.broadcasted_iota(I32, (TILE, TILE), 0))
    base = carry_ref[...] + _mm(tri.astype(BF16), onehot.astype(BF16))
    r0 = jnp.sum(jnp.where(hit0, base, 0.0), axis=1, keepdims=True)
    r1 = jnp.sum(jnp.where(hit1, base, 0.0), axis=1, keepdims=True)
    total = carry_ref[...] + jnp.sum(onehot, axis=0, keepdims=True)
    carry_ref[...] = total
    cnt_ref[...] = total.astype(I32)
    slab = jnp.where(lane == 0, (i0 - R_EXP).astype(F32),
                     jnp.where(lane == 1, (i1 - R_EXP).astype(F32),
                               jnp.where(lane == 2, r0, jnp.where(lane == 3, r1, 0.0))))
    ids_ref[0] = slab.T[:8].astype(I32)


def _post(xf, aa, ab, kvm, tiles_per_batch, weights):
    n, d = xf.shape
    nt = n // TILE
    row = lambda w: pl.BlockSpec((TILE, w), lambda i: (i, 0))
    full = lambda a: pl.BlockSpec(a.shape, lambda i: (0,) * a.ndim)
    gm, wg, woa, wob, wout, gx, wq, wox, gmoe, wrt, brt = weights
    return pl.pallas_call(
        _post_kernel,
        grid=(nt,),
        in_specs=[row(d), row(W_A), row(W_B), full(gm), full(wg), full(woa), full(wob), full(wout),
                  full(gx), full(wq),
                  pl.BlockSpec((1,) + kvm.shape[1:], lambda i: (i // tiles_per_batch, 0, 0)),
                  full(wox), full(gmoe), full(wrt), full(brt)],
        out_specs=[row(d), row(d), row(LANES),
                   pl.BlockSpec((1, 8, TILE), lambda i: (i, 0, 0)),
                   pl.BlockSpec((1, LANES), lambda i: (0, 0))],
        out_shape=[jax.ShapeDtypeStruct((n, d), F32), jax.ShapeDtypeStruct((n, d), F32),
                   jax.ShapeDtypeStruct((n, LANES), F32),
                   jax.ShapeDtypeStruct((nt, 8, TILE), I32),
                   jax.ShapeDtypeStruct((1, LANES), I32)],
        scratch_shapes=[pltpu.VMEM((1, LANES), F32)],
        compiler_params=_params(("arbitrary",)),
        name="post",
    )(xf, aa, ab, gm, wg, woa, wob, wout, gx, wq, kvm, wox, gmoe, wrt, brt)


def _segment_starts(cnt_ref, seg_ref):
    def body(e, acc):
        seg_ref[e] = acc
        c = cnt_ref[0, R_EXP + e]
        return acc + ((c + (MOE_BLOCK - 1)) >> 8 << 8)
    return lax.fori_loop(0, N_EXPERTS, body, jnp.int32(0))


def _start_row_copies(ids_ref, seg_ref, rows_ref, tile_ref, sem, gather):
    def issue(t, carry):
        for k in range(2):
            dest = seg_ref[ids_ref[0, k, t]] + ids_ref[0, 2 + k, t]
            if gather:
                cp = pltpu.make_async_copy(rows_ref.at[pl.ds(dest, 1)], tile_ref.at[k, pl.ds(t, 1)], sem)
            else:
                cp = pltpu.make_async_copy(tile_ref.at[pl.ds(t, 1)], rows_ref.at[pl.ds(dest, 1)], sem)
            cp.start(priority=k)
        return carry
    lax.fori_loop(0, TILE, issue, 0, unroll=4)


def _wait_row_copies(rows_ref, tile_ref, sem):
    pltpu.make_async_copy(rows_ref.at[pl.ds(0, TILE)], tile_ref, sem).wait()


def _dispatch_kernel(ids_ref, cnt_ref, hm_ref, xr_in_ref, xr_ref, blk_ref, seg_ref, sem):
    del xr_in_ref
    step = pl.program_id(0)
    nb = blk_ref.shape[1] - 1

    @pl.when(step == 0)
    def _():
        used = _segment_starts(cnt_ref, seg_ref)

        def per_expert(e, last):
            c = cnt_ref[0, R_EXP + e]
            b0 = seg_ref[e] >> 8
            n = (c + (MOE_BLOCK - 1)) >> 8

            def fill(kk, carry):
                blk_ref[0, b0 + kk] = e
                return carry
            lax.fori_loop(0, n, fill, 0)
            return jnp.where(n > 0, e, last)
        last = lax.fori_loop(0, N_EXPERTS, per_expert, jnp.int32(0))

        def tail(b, carry):
            blk_ref[0, b] = last
            return carry
        lax.fori_loop(used >> 8, nb, tail, 0)
        blk_ref[0, nb] = used >> 8

    _start_row_copies(ids_ref, seg_ref, xr_ref, hm_ref, sem, gather=False)
    for _ in range(2):
        _wait_row_copies(xr_ref, hm_ref, sem)


def _dispatch(ids, cnt, hm, n_rows):
    n, d = hm.shape
    nt = n // TILE
    nb = n_rows // MOE_BLOCK
    zeros = jnp.zeros((n_rows, d), F32)
    return pl.pallas_call(
        _dispatch_kernel,
        grid=(nt,),
        in_specs=[pl.BlockSpec((1, 8, TILE), lambda i: (i, 0, 0), memory_space=pltpu.SMEM),
                  pl.BlockSpec(memory_space=pltpu.SMEM),
                  pl.BlockSpec((TILE, d), lambda i: (i, 0)),
                  pl.BlockSpec(memory_space=pl.ANY)],
        out_specs=[pl.BlockSpec(memory_space=pl.ANY),
                   pl.BlockSpec(memory_space=pltpu.SMEM)],
        out_shape=[jax.ShapeDtypeStruct((n_rows, d), F32),
                   jax.ShapeDtypeStruct((1, nb + 1), I32)],
        scratch_shapes=[pltpu.SMEM((N_EXPERTS,), I32), pltpu.SemaphoreType.DMA(())],
        input_output_aliases={3: 0},
        compiler_params=_params(("arbitrary",)),
        name="dispatch",
    )(ids, cnt, hm, zeros)


def _experts_kernel(blk_ref, x_ref, w1_ref, w3_ref, w2_ref, y_ref, w1b, w3b, w2b):
    b = pl.program_id(0)
    nb = pl.num_programs(0)
    e = blk_ref[0, b]
    prev = blk_ref[0, jnp.maximum(b - 1, 0)]

    @pl.when(jnp.logical_or(b == 0, e != prev))
    def _():
        w1b[...] = w1_ref[0].astype(BF16)
        w3b[...] = w3_ref[0].astype(BF16)
        w2b[...] = w2_ref[0].astype(BF16)

    used = blk_ref[0, nb]

    @pl.when(b < used)
    def _():
        xb = x_ref[...].astype(BF16)
        a = _mm(xb, w1b[...])
        g = _mm(xb, w3b[...])
        y_ref[...] = _mm((a * jax.nn.sigmoid(a) * g).astype(BF16), w2b[...])

    @pl.when(b >= used)
    def _():
        y_ref[...] = jnp.zeros(y_ref.shape, F32)


def _experts(blk, xr, w1, w3, w2):
    n_rows, d = xr.shape
    nb = n_rows // MOE_BLOCK
    wspec = lambda s: pl.BlockSpec((1,) + s, lambda b, blk: (blk[0, b], 0, 0))
    return pl.pallas_call(
        _experts_kernel,
        grid_spec=pltpu.PrefetchScalarGridSpec(
            num_scalar_prefetch=1,
            grid=(nb,),
            in_specs=[pl.BlockSpec((MOE_BLOCK, d), lambda b, blk: (b, 0)),
                      wspec((d, D_EXPERT)), wspec((d, D_EXPERT)), wspec((D_EXPERT, d))],
            out_specs=pl.BlockSpec((MOE_BLOCK, d), lambda b, blk: (b, 0)),
            scratch_shapes=[pltpu.VMEM((d, D_EXPERT), BF16), pltpu.VMEM((d, D_EXPERT), BF16),
                            pltpu.VMEM((D_EXPERT, d), BF16)]),
        out_shape=jax.ShapeDtypeStruct((n_rows, d), F32),
        compiler_params=_params(("arbitrary",)),
        name="experts",
    )(blk, xr, w1, w3, w2)


def _combine_kernel(ids_ref, ids_next_ref, cnt_ref, x2_ref, rw_ref, g_ref, yr_ref, o_ref, y_ref, seg_ref, sem):
    step = pl.program_id(0)
    slot = step % 2

    @pl.when(step == 0)
    def _():
        _segment_starts(cnt_ref, seg_ref)
        _start_row_copies(ids_ref, seg_ref, yr_ref, y_ref.at[0], sem.at[0], gather=True)

    @pl.when(step + 1 < pl.num_programs(0))
    def _():
        _start_row_copies(ids_next_ref, seg_ref, yr_ref, y_ref.at[1 - slot], sem.at[1 - slot], gather=True)

    for k in range(2):
        _wait_row_copies(yr_ref, y_ref.at[slot, k], sem.at[slot])
    rw = rw_ref[...]
    x3 = x2_ref[...] + (y_ref[slot, 0] * rw[:, 0:1] + y_ref[slot, 1] * rw[:, 1:2])
    o_ref[...] = _rms(x3, g_ref[...])


def _combine(ids, cnt, x2, rw, g, yr):
    n, d = x2.shape
    nt = n // TILE
    row = lambda w: pl.BlockSpec((TILE, w), lambda i: (i, 0))
    return pl.pallas_call(
        _combine_kernel,
        grid=(nt,),
        in_specs=[pl.BlockSpec((1, 8, TILE), lambda i: (i, 0, 0), memory_space=pltpu.SMEM),
                  pl.BlockSpec((1, 8, TILE), lambda i: (jnp.minimum(i + 1, nt - 1), 0, 0),
                               memory_space=pltpu.SMEM),
                  pl.BlockSpec(memory_space=pltpu.SMEM),
                  row(d), row(LANES), pl.BlockSpec((1, d), lambda i: (0, 0)),
                  pl.BlockSpec(memory_space=pl.ANY)],
        out_specs=row(d),
        out_shape=jax.ShapeDtypeStruct((n, d), F32),
        scratch_shapes=[pltpu.VMEM((2, 2, TILE, d), F32), pltpu.SMEM((N_EXPERTS,), I32),
                        pltpu.SemaphoreType.DMA((2,))],
        compiler_params=_params(("arbitrary",)),
        name="combine",
    )(ids, ids, cnt, x2, rw, g, yr)


def _t5_bucket(dist):
    n = jnp.maximum(dist, 0)
    max_exact = REL_BUCKETS // 2
    nf = jnp.maximum(n, 1).astype(F32)
    large = max_exact + (jnp.log(nf / max_exact) / math.log(REL_MAX_DIST / max_exact)
                         * (REL_BUCKETS - max_exact)).astype(I32)
    large = jnp.minimum(large, REL_BUCKETS - 1)
    return jnp.where(n < max_exact, n, large)


def _bias_tables(tab):
    r = jnp.arange(TILE)[:, None]
    c = jnp.arange(TILE)[None, :]
    rel = tab - tab[:, REL_BUCKETS - 1:]

    def tile(dist):
        onehot = (_t5_bucket(dist)[..., None] == jnp.arange(REL_BUCKETS)).astype(F32)
        return jnp.einsum('rcb,hb->hrc', onehot, rel, precision=lax.Precision.HIGHEST)

    own = jnp.where(r >= c, tile(r - c), NEG)
    near = jnp.stack([own, tile(r - c + TILE)], axis=1).astype(F32)
    return near.swapaxes(-1, -2)


def _block_mean_slabs(kmean, bn, nq):
    km = kmean.reshape(bn, nq, A_HEADS, HEAD_DIM).transpose(0, 2, 1, 3)
    pad = ((0, 0), (0, 0), (HEAD_DIM, LANES - HEAD_DIM - nq), (0, LANES - HEAD_DIM))
    return jnp.pad(km, pad).astype(BF16)


def _pad_cols(w, width):
    return jnp.pad(w, ((0, 0), (0, width - w.shape[1])))


def kernel(x, mem, rel_bias, final_norm, norm_mix, w_in, ckv_norm, w_uk, w_uv, w_oa, w_ob,
           w_out, norm_x, mem_norm, wq_x, wk_x, wv_x, wo_x, norm_moe, w_group, b_group,
           w_router, b_router, w1, w3, w2):
    bn, t, d = x.shape
    n = bn * t
    nq = t // TILE
    assert t % TILE == 0 and nq <= LANES - HEAD_DIM and norm_mix.shape[0] == 1
    assert nq % FAR_GROUP == 0
    near_a = _bias_tables(rel_bias[:, :A_HEADS].T)
    near_b = _bias_tables(rel_bias[:, A_HEADS:].T)
    row = lambda v: v.reshape(1, -1).astype(F32)

    wi = w_in[0]
    wp = jnp.concatenate(
        [wi[:, OFF_AQ:OFF_IK], jnp.tile(wi[:, OFF_IK:OFF_IW], (1, LANES // IDX_DIM)),
         _pad_cols(wi[:, OFF_IW:OFF_GA], LANES)], axis=1).astype(BF16)
    wkv = jnp.concatenate([w_uk[0], w_uv[0]], axis=1).astype(BF16)
    xf = x.reshape(n, d)
    qa, ka, va, qb, kb, vb, qi, ki4, widx, kmean = _proj_in(
        xf, row(norm_mix[0]), wp, row(ckv_norm[0]), wkv, nq)

    b3 = lambda a: a.reshape(bn, t, a.shape[-1])
    attn_a = _moba(b3(qa), b3(ka), va.reshape(bn, nq, W_A, TILE), _block_mean_slabs(kmean, bn, nq), near_a)
    attn_b = _dsa(b3(qi), b3(ki4), b3(widx), b3(qb), b3(kb), vb.reshape(bn, nq, W_B, TILE), near_b)

    kvm = _mem_kv(mem, row(mem_norm[0]), jnp.concatenate([wk_x[0], wv_x[0]], axis=1).astype(BF16))
    wrt = _pad_cols(jnp.concatenate([w_group[0], w_router[0]], axis=1), LANES).astype(F32)
    wrt_hi = wrt.astype(BF16)
    wrt = jnp.concatenate([wrt_hi, (wrt - wrt_hi.astype(F32)).astype(BF16)], axis=1)
    brt = _pad_cols(jnp.concatenate([b_group[0], b_router[0]]).reshape(1, -1), LANES).astype(F32)
    weights = (row(norm_mix[0]), wi[:, OFF_GA:IN_COLS].astype(BF16), w_oa[0].astype(BF16),
               w_ob[0].astype(BF16), w_out[0].astype(BF16), row(norm_x[0]), wq_x[0].astype(BF16),
               wo_x[0].astype(BF16), row(norm_moe[0]), wrt, brt)
    x2, hm, rw, ids, cnt = _post(xf, attn_a.reshape(n, W_A), attn_b.reshape(n, W_B), kvm, t // TILE, weights)

    n_rows = 2 * n + N_EXPERTS * MOE_BLOCK
    xr, blk = _dispatch(ids, cnt, hm, n_rows)
    yr = _experts(blk, xr, w1[0], w3[0], w2[0])
    out = _combine(ids, cnt, x2, rw, row(final_norm), yr)
    return out.reshape(bn, t, d)
```

```python
import functools
import math

import jax
import jax.numpy as jnp
from jax import lax
from jax.experimental import pallas as pl
from jax.experimental.pallas import tpu as pltpu

D_MODEL = 1024
HEAD_DIM = 64
A_HEADS = 8
B_HEADS = 8
W_A = A_HEADS * HEAD_DIM
W_B = B_HEADS * HEAD_DIM
MOBA_BLOCK = 256
MOBA_TOPK = 3
DSA_KV_RANK = 256
IDX_HEADS = 8
IDX_DIM = 32
DSA_TOPK = 256
REL_BUCKETS = 32
REL_MAX_DIST = 128
X_HEADS = 4
W_X = X_HEADS * HEAD_DIM
N_GROUPS = 4
EXPERTS_PER_GROUP = 8
N_EXPERTS = N_GROUPS * EXPERTS_PER_GROUP
D_EXPERT = 512
MOE_BLOCK = 256
EPS = 1e-6
NEG = -1e30

OFF_AQ = 0
OFF_AK = OFF_AQ + W_A
OFF_AV = OFF_AK + W_A
OFF_BQ = OFF_AV + W_A
OFF_CKV = OFF_BQ + W_B
OFF_IQ = OFF_CKV + DSA_KV_RANK
OFF_IK = OFF_IQ + IDX_HEADS * IDX_DIM
OFF_IW = OFF_IK + IDX_DIM
OFF_GA = OFF_IW + IDX_HEADS
OFF_GB = OFF_GA + D_MODEL
IN_COLS = OFF_GB + D_MODEL

LANES = 128
TILE = 256
VMEM_LIMIT = 56 * 1024 * 1024
BF16 = jnp.bfloat16
F32 = jnp.float32
I32 = jnp.int32

P_QA, P_KA, P_VA, P_QB = 0, 512, 1024, 1536
P_CKV, P_IQ, P_IK4, P_IW = 2048, 2304, 2560, 2688
P_COLS = 2816

R_GRP = 0
R_EXP = N_GROUPS


def _nt(a, b):
    return lax.dot_general(a, b, (((1,), (1,)), ((), ())), preferred_element_type=F32)


def _mm(a, b):
    return jnp.dot(a, b, preferred_element_type=F32)


def _rms(xf, g):
    return xf * lax.rsqrt(jnp.mean(xf * xf, axis=-1, keepdims=True) + EPS) * g


def _params(sem, vmem=VMEM_LIMIT):
    return pltpu.CompilerParams(dimension_semantics=sem, vmem_limit_bytes=vmem)


def _mem_kv_kernel(mem_ref, g_ref, w_ref, o_ref):
    m = _rms(mem_ref[0], g_ref[...]).astype(BF16)
    o_ref[0] = _mm(m, w_ref[...]).astype(BF16)


def _mem_kv(mem, g, wkv):
    bn, ml, d = mem.shape
    return pl.pallas_call(
        _mem_kv_kernel,
        grid=(bn,),
        in_specs=[pl.BlockSpec((1, ml, d), lambda b: (b, 0, 0)),
                  pl.BlockSpec((1, d), lambda b: (0, 0)),
                  pl.BlockSpec((d, 2 * W_X), lambda b: (0, 0))],
        out_specs=pl.BlockSpec((1, ml, 2 * W_X), lambda b: (b, 0, 0)),
        out_shape=jax.ShapeDtypeStruct((bn, ml, 2 * W_X), BF16),
        compiler_params=_params(("arbitrary",)),
        name="mem_kv",
    )(mem, g, wkv)


def _head_slabs(zp, extra):
    lane = lax.broadcasted_iota(I32, zp.shape, 1)
    low = lane < HEAD_DIM
    return jnp.where(low, zp, extra), jnp.where(low, pltpu.roll(zp, HEAD_DIM, 1), extra)


def _proj_in_kernel(blocks_per_seq, x_ref, g_ref, w_ref, cg_ref, wkv_ref,
                    qa_ref, ka_ref, va_ref, qb_ref, kb_ref, vb_ref, qi_ref, ki_ref, wi_ref, km_ref):
    h = _rms(x_ref[...], g_ref[...]).astype(BF16)
    z = _mm(h, w_ref[...])
    scale = HEAD_DIM ** -0.5
    zk = z[:, P_KA:P_VA]
    km_ref[0] = jnp.mean(zk, axis=0, keepdims=True)
    lane = lax.broadcasted_iota(I32, (TILE, LANES), 1)
    blk = pl.program_id(0) % blocks_per_seq
    onehot = jnp.where(lane == HEAD_DIM + blk, 1.0, 0.0)
    zero = jnp.zeros((TILE, LANES), F32)
    for p_idx in range(A_HEADS // 2):
        lo = p_idx * LANES
        qe, qo = _head_slabs(z[:, P_QA + lo:P_QA + lo + LANES] * scale, zero)
        ke, ko = _head_slabs(zk[:, lo:lo + LANES], onehot)
        qa_ref[:, 2 * lo:2 * lo + 2 * LANES] = jnp.concatenate([qe, qo], axis=1).astype(BF16)
        ka_ref[:, 2 * lo:2 * lo + 2 * LANES] = jnp.concatenate([ke, ko], axis=1).astype(BF16)
    va_ref[0] = z[:, P_VA:P_QB].T.astype(BF16)
    qb_ref[...] = (z[:, P_QB:P_CKV] * scale).astype(BF16)
    ckv = _rms(z[:, P_CKV:P_IQ], cg_ref[...]).astype(BF16)
    kv = _mm(ckv, wkv_ref[...])
    kb_ref[...] = kv[:, :W_B].astype(BF16)
    vb_ref[0] = kv[:, W_B:].T.astype(BF16)
    qi_ref[...] = z[:, P_IQ:P_IK4].astype(BF16)
    ki_ref[...] = z[:, P_IK4:P_IW].astype(BF16)
    wi_ref[...] = z[:, P_IW:P_COLS] * ((IDX_HEADS ** -0.5) * (IDX_DIM ** -0.5))


def _proj_in(xf, g, wp, cg, wkv, blocks_per_seq):
    n, d = xf.shape
    nt = n // TILE
    row = lambda w: pl.BlockSpec((TILE, w), lambda i: (i, 0))
    full = lambda a: pl.BlockSpec(a.shape, lambda i: (0,) * a.ndim)
    outs = [(2 * W_A, BF16, False), (2 * W_A, BF16, False), (W_A, BF16, True), (W_B, BF16, False),
            (W_B, BF16, False), (W_B, BF16, True), (IDX_HEADS * IDX_DIM, BF16, False), (LANES, BF16, False),
            (LANES, F32, False)]
    blocked = lambda w: pl.BlockSpec((1, w, TILE), lambda i: (i, 0, 0))
    return pl.pallas_call(
        functools.partial(_proj_in_kernel, blocks_per_seq),
        grid=(nt,),
        in_specs=[row(d), full(g), full(wp), full(cg), full(wkv)],
        out_specs=[blocked(w) if tr else row(w) for w, _, tr in outs]
        + [pl.BlockSpec((1, 1, W_A), lambda i: (i, 0, 0))],
        out_shape=[jax.ShapeDtypeStruct((nt, w, TILE) if tr else (n, w), t) for w, t, tr in outs]
        + [jax.ShapeDtypeStruct((nt, 1, W_A), F32)],
        compiler_params=_params(("arbitrary",)),
        name="proj_in",
    )(xf, g, wp, cg, wkv)


SUB = TILE // 8
LOG2E = math.log2(math.e)
FAR_GROUP = 8


def _sm_init(mx_ref, acc_ref):
    mx_ref[...] = jnp.full(mx_ref.shape, NEG, F32)
    acc_ref[...] = jnp.zeros(acc_ref.shape, F32)


def _sm_logits(hd, slot, s, sc_ref, mx_ref):
    s = s * LOG2E
    sc_ref[hd, slot] = s
    mx_ref[hd] = jnp.maximum(mx_ref[hd], jnp.max(s.reshape(SUB, 8, TILE), axis=0))


def _sm_rowmax(mx_ref):
    for hd in range(2):
        mx_ref[hd] = jnp.broadcast_to(jnp.max(mx_ref[hd], axis=0, keepdims=True), (8, TILE))


def _sm_probs(hd, slots, sc_ref, mx_ref):
    m = mx_ref[hd][None]
    ps = [jnp.exp2((sc_ref[hd, sl].reshape(SUB, 8, TILE) - m).reshape(TILE, TILE).astype(BF16))
          for sl in slots]
    return ps[0] if len(ps) == 1 else jnp.concatenate(ps, axis=0)


def _sm_accumulate(hd, pb, v, acc_ref):
    row = lax.broadcasted_iota(I32, v.shape, 0)
    own = (row < HEAD_DIM) if hd == 0 else (row >= HEAD_DIM)
    acc_ref[hd] = acc_ref[hd] + _mm(jnp.where(own, v, jnp.ones_like(v)), pb)


def _sm_values(hd, slots, v, sc_ref, mx_ref, acc_ref):
    _sm_accumulate(hd, _sm_probs(hd, slots, sc_ref, mx_ref), v, acc_ref)


def _far_plan(n_far):
    full = n_far // FAR_GROUP
    rest = n_far - full * FAR_GROUP
    half = jnp.logical_and(rest > 0, rest <= FAR_GROUP // 2)
    return full + (rest > FAR_GROUP // 2).astype(I32), half, full * FAR_GROUP


def _sm_far_loop(plan, group_fn):
    trips, half, half_base = plan
    lax.fori_loop(0, trips, lambda g, c: group_fn(FAR_GROUP * g, FAR_GROUP) or c, 0)
    pl.when(half)(lambda: group_fn(half_base, FAR_GROUP // 2))


def _sm_far_values(plan, vals, sc_ref, mx_ref, acc_ref):
    def group(j0, n):
        v = vals(j0, n)
        for hd in range(2):
            _sm_values(hd, [j0 + u for u in range(n)], v, sc_ref, mx_ref, acc_ref)

    _sm_far_loop(plan, group)


def _sm_output(acc_ref):
    num = jnp.concatenate([acc_ref[0, :HEAD_DIM], acc_ref[1, HEAD_DIM:]], axis=0)
    den = jnp.concatenate([acc_ref[0, HEAD_DIM:], acc_ref[1, :HEAD_DIM]], axis=0)
    return (num / den).T


def _split_pair(qp):
    lane = lax.broadcasted_iota(I32, qp.shape, 1)
    zero = jnp.zeros_like(qp)
    return jnp.where(lane < HEAD_DIM, qp, zero), jnp.where(lane >= HEAD_DIM, qp, zero)


def _moba_kernel(q_ref, k_ref, v_ref, km_ref, bias_ref, o_ref, sc_ref, mx_ref, acc_ref):
    i = pl.program_id(2)
    nq = pl.num_programs(2)
    slot_prev, slot_own = nq, nq + 1
    n_slot = LANES - HEAD_DIM
    blk = lax.broadcasted_iota(I32, (n_slot, TILE), 0)
    in_blk = blk < nq
    past = blk < i
    no_mask = jnp.zeros((HEAD_DIM, TILE), F32)

    qs, q_far, q_prev = [], [], []
    for hd in range(2):
        q = q_ref[0, :, hd * LANES:(hd + 1) * LANES]
        g = jnp.where(past, _nt(km_ref[0, hd], q)[HEAD_DIM:], -jnp.inf)
        sel = jnp.zeros((n_slot, TILE), jnp.bool_)
        for _ in range(MOBA_TOPK):
            top = jnp.max(g, axis=0, keepdims=True)
            hit = jnp.logical_and(g == top, in_blk)
            first = jnp.min(jnp.where(hit, blk, n_slot), axis=0, keepdims=True)
            pick = blk == first
            sel = jnp.logical_or(sel, jnp.logical_and(pick, past))
            g = jnp.where(pick, -jnp.inf, g)
        addm = jnp.where(sel, 0.0, NEG)
        far_m = jnp.where(blk < i - 1, addm, jnp.where(in_blk, NEG, 0.0))
        prev_m = jnp.where(blk == i - 1, addm, 0.0)
        qs.append(q)
        q_far.append(q + jnp.concatenate([no_mask, far_m], axis=0).T.astype(BF16))
        q_prev.append(q + jnp.concatenate([no_mask, prev_m], axis=0).T.astype(BF16))

    def keys(j, hd):
        off = pl.multiple_of(j * TILE, TILE)
        return k_ref[0, pl.ds(off, TILE), hd * LANES:(hd + 1) * LANES]

    plan = _far_plan(jnp.maximum(i - 1, 0))
    _sm_init(mx_ref, acc_ref)

    def far_logits(j0, n):
        for u in range(n):
            for hd in range(2):
                _sm_logits(hd, j0 + u, _nt(keys(j0 + u, hd), q_far[hd]), sc_ref, mx_ref)

    _sm_far_loop(plan, far_logits)

    j_prev = jnp.maximum(i - 1, 0)
    no_prev = jnp.where(i >= 1, 0.0, NEG)
    for hd in range(2):
        _sm_logits(hd, slot_prev, _nt(keys(j_prev, hd), q_prev[hd]) + (bias_ref[hd, 1] + no_prev),
                   sc_ref, mx_ref)
        _sm_logits(hd, slot_own, _nt(keys(i, hd), qs[hd]) + bias_ref[hd, 0], sc_ref, mx_ref)
    _sm_rowmax(mx_ref)

    def vals(j, n):
        return jnp.concatenate([v_ref[0, j + u] for u in range(n)], axis=1)

    _sm_far_values(plan, vals, sc_ref, mx_ref, acc_ref)

    v_near = jnp.concatenate([vals(j_prev, 1), vals(i, 1)], axis=1)
    for hd in range(2):
        _sm_values(hd, [slot_prev, slot_own], v_near, sc_ref, mx_ref, acc_ref)
    o_ref[0] = _sm_output(acc_ref).astype(BF16)


def _sm_scratch(nq):
    return [pltpu.VMEM((2, nq + 2, TILE, TILE), F32),
            pltpu.VMEM((2, 8, TILE), F32),
            pltpu.VMEM((2, LANES, TILE), F32)]


def _moba(qa, ka, va_t, kmp, bias_near):
    bn, nq = va_t.shape[:2]
    t = nq * TILE
    npair = A_HEADS // 2
    return pl.pallas_call(
        _moba_kernel,
        grid=(bn, npair, nq),
        in_specs=[pl.BlockSpec((1, TILE, 2 * LANES), lambda b, p, i: (b, i, p)),
                  pl.BlockSpec((1, t, 2 * LANES), lambda b, p, i: (b, 0, p)),
                  pl.BlockSpec((1, nq, LANES, TILE), lambda b, p, i: (b, 0, p, 0)),
                  pl.BlockSpec((1, 2, LANES, LANES), lambda b, p, i: (b, p, 0, 0)),
                  pl.BlockSpec((2, 2, TILE, TILE), lambda b, p, i: (p, 0, 0, 0))],
        out_specs=pl.BlockSpec((1, TILE, LANES), lambda b, p, i: (b, i, p)),
        out_shape=jax.ShapeDtypeStruct((bn, t, W_A), BF16),
        scratch_shapes=_sm_scratch(nq),
        compiler_params=_params(("arbitrary", "arbitrary", "arbitrary")),
        name="moba",
    )(qa, ka, va_t, kmp, bias_near)


def _dsa_select(i, qi_ref, ki_ref, wi_ref, s_ref, wb_ref):
    nch = i + 1
    sub = TILE // 8
    lane = lax.broadcasted_iota(I32, (TILE, LANES), 1)
    key_in = (lax.broadcasted_iota(I32, (sub, 8, TILE), 0) * 8 + lax.broadcasted_iota(I32, (sub, 8, TILE), 1))
    qpos = i * TILE + lax.broadcasted_iota(I32, (sub, 8, TILE), 2)

    def rows8(x):
        return jnp.broadcast_to(x, (8, TILE))[None]

    w_t = wi_ref[0].T
    qi = qi_ref[0]
    per_group = LANES // IDX_DIM
    qh = []
    for h in range(IDX_HEADS):
        g, r = divmod(h, per_group)
        qg = qi[:, g * LANES:(g + 1) * LANES]
        keep = jnp.logical_and(lane >= r * IDX_DIM, lane < (r + 1) * IDX_DIM)
        qh.append(jnp.where(keep, qg, jnp.zeros_like(qg)))
        wb_ref[h] = jnp.broadcast_to(w_t[h:h + 1], (8, TILE))

    def score(c, carry):
        off = pl.multiple_of(c * TILE, TILE)
        kc = ki_ref[0, pl.ds(off, TILE), :]
        acc = jnp.zeros((sub, 8, TILE), F32)
        for h in range(IDX_HEADS):
            acc = acc + jnp.maximum(_nt(kc, qh[h]), 0.0).reshape(sub, 8, TILE) * wb_ref[h][None]
        s_ref[c] = jnp.where(c * TILE + key_in <= qpos, acc, -jnp.inf).reshape(TILE, TILE)
        return carry

    n_pair = (nch + 1) // 2

    def pairs(step, init):
        return lax.fori_loop(0, n_pair, lambda g, acc: step(2 * g + 1, step(2 * g, acc)), init)

    pairs(score, 0)

    def fold(init, step):
        return pairs(lambda c, acc: step(acc, s_ref[c].reshape(sub, 8, TILE), c), init)

    def count(pred):
        def step(acc, blk, c):
            return acc + jnp.sum(jnp.where(pred(blk, c * TILE + key_in), 1.0, 0.0), axis=0)
        return jnp.sum(fold(jnp.zeros((8, TILE), F32), step), axis=0, keepdims=True)

    wide = rows8

    def stats(acc, blk, c):
        top, low, pos, nonneg = acc
        top = jnp.maximum(top, jnp.max(blk, axis=0))
        low = jnp.minimum(low, jnp.min(jnp.where(blk == -jnp.inf, jnp.inf, blk), axis=0))
        pos = pos + jnp.sum(jnp.where(blk > 0.0, 1.0, 0.0), axis=0)
        nonneg = nonneg + jnp.sum(jnp.where(blk >= 0.0, 1.0, 0.0), axis=0)
        return top, low, pos, nonneg

    zeros8 = jnp.zeros((8, TILE), F32)
    acc = fold((zeros8 - jnp.inf, zeros8 + jnp.inf, zeros8, zeros8), stats)
    hi = jnp.max(acc[0], axis=0, keepdims=True)
    lo = jnp.min(acc[1], axis=0, keepdims=True)
    n_pos = jnp.sum(acc[2], axis=0, keepdims=True)
    n_nonneg = jnp.sum(acc[3], axis=0, keepdims=True)

    top = hi
    top_ties = count(lambda half, kpos: half >= wide(top)) >= DSA_TOPK
    at_zero = jnp.logical_and(n_pos < DSA_TOPK, n_nonneg >= DSA_TOPK)
    lo = jnp.where(n_pos >= DSA_TOPK, 0.0, lo)
    hi = jnp.where(n_nonneg < DSA_TOPK, 0.0, hi)

    def search_more(state):
        return jnp.logical_and(jnp.min(state[3]) == 0, state[4] < 400)

    def search(state):
        lo, hi, thr, done, it = state
        mid = 0.5 * lo + 0.5 * hi
        closed = jnp.logical_or(mid <= lo, mid >= hi)
        n = count(lambda half, kpos: half >= wide(mid))
        found = jnp.where(closed, lo, mid)
        stop = jnp.logical_or(closed, n == DSA_TOPK)
        live = done == 0
        thr = jnp.where(jnp.logical_and(live, stop), found, thr)
        go = jnp.logical_and(live, jnp.logical_not(stop))
        lo = jnp.where(jnp.logical_and(go, n > DSA_TOPK), mid, lo)
        hi = jnp.where(jnp.logical_and(go, n < DSA_TOPK), mid, hi)
        return lo, hi, thr, jnp.where(stop, 1, done), it + 1

    thr0 = jnp.where(top_ties, top, 0.0)
    done0 = jnp.logical_or(top_ties, at_zero).astype(I32)
    state = (lo, hi, thr0, done0, jnp.int32(0))
    thr = wide(lax.while_loop(search_more, lambda st: search(search(st)), state)[2])
    n_gt = count(lambda half, kpos: half > thr)
    n_ge = count(lambda half, kpos: half >= thr)
    need = DSA_TOPK - n_gt

    cut_bits = max(1, (s_ref.shape[0] * TILE - 1).bit_length())

    def cut_step(b, cut):
        cand = cut | (jnp.int32(1) << (cut_bits - 1 - b))
        cw = wide(cand)
        n = count(lambda half, kpos: jnp.logical_and(half == thr, kpos < cw))
        return jnp.where(n < need, cand, cut)

    cut = lax.cond(jnp.max(n_ge) > DSA_TOPK,
                   lambda: lax.fori_loop(0, cut_bits, cut_step, jnp.zeros((1, TILE), I32)),
                   lambda: jnp.full((1, TILE), 2 ** 30, I32))
    cut = wide(jnp.where(n_ge > DSA_TOPK, cut, 2 ** 30))

    def to_mask(c, carry):
        blk = s_ref[c].reshape(sub, 8, TILE)
        kpos = c * TILE + key_in
        keep = jnp.logical_or(blk > thr, jnp.logical_and(blk == thr, kpos <= cut))
        s_ref[c] = jnp.where(jnp.logical_and(keep, kpos <= qpos), 0.0, NEG).reshape(TILE, TILE)
        return carry

    pairs(to_mask, 0)


def _dsa_kernel(qi_ref, ki_ref, wi_ref, q_ref, k_ref, v_ref, bias_ref, o_ref,
                s_ref, wb_ref, sc_ref, mx_ref, acc_ref):
    i = pl.program_id(1)
    pair = pl.program_id(2)

    @pl.when(jnp.logical_and(pair == 0, i == 0))
    def _():
        key = lax.broadcasted_iota(I32, (TILE, TILE), 0)
        query = lax.broadcasted_iota(I32, (TILE, TILE), 1)
        s_ref[0] = jnp.where(key <= query, 0.0, NEG)

    @pl.when(jnp.logical_and(pair == 0, i >= 1))
    def _():
        _dsa_select(i, qi_ref, ki_ref, wi_ref, s_ref, wb_ref)

    qs = _split_pair(q_ref[0])
    nq = pl.num_programs(1)
    plan = _far_plan(jnp.maximum(i - 1, 0))
    _sm_init(mx_ref, acc_ref)

    def vals(j, n):
        return jnp.concatenate([v_ref[0, j + u] for u in range(n)], axis=1)

    def logits(j, slot, msk, near=None):
        k_j = k_ref[0, pl.ds(pl.multiple_of(j * TILE, TILE), TILE), :]
        for hd in range(2):
            s = _nt(k_j, qs[hd]) + msk
            if near is not None:
                s = s + bias_ref[hd, near]
            _sm_logits(hd, slot, s, sc_ref, mx_ref)

    def far_logits(j0, n):
        for u in range(n):
            j = j0 + u
            logits(j, j, jnp.where(j < i - 1, s_ref[jnp.minimum(j, i)], NEG))

    _sm_far_loop(plan, far_logits)
    j_prev = jnp.maximum(i - 1, 0)
    logits(j_prev, nq, jnp.where(i >= 1, s_ref[j_prev], NEG), near=1)
    logits(i, nq + 1, s_ref[i], near=0)
    _sm_rowmax(mx_ref)

    _sm_far_values(plan, vals, sc_ref, mx_ref, acc_ref)
    v_near = jnp.concatenate([vals(j_prev, 1), vals(i, 1)], axis=1)
    for hd in range(2):
        _sm_values(hd, [nq, nq + 1], v_near, sc_ref, mx_ref, acc_ref)
    o_ref[0] = _sm_output(acc_ref).astype(BF16)


def _dsa(qi, ki4, wi, qb, kb, vb_t, bias_near):
    bn, t, _ = qb.shape
    nq = t // TILE
    assert nq % 2 == 0
    npair = B_HEADS // 2
    qrow = lambda w: pl.BlockSpec((1, TILE, w), lambda b, i, p: (b, i, 0))
    return pl.pallas_call(
        _dsa_kernel,
        grid=(bn, nq, npair),
        in_specs=[qrow(IDX_HEADS * IDX_DIM),
                  pl.BlockSpec((1, t, LANES), lambda b, i, p: (b, 0, 0), pipeline_mode=pl.Buffered(1)),
                  qrow(LANES),
                  pl.BlockSpec((1, TILE, LANES), lambda b, i, p: (b, i, p)),
                  pl.BlockSpec((1, t, LANES), lambda b, i, p: (b, 0, p)),
                  pl.BlockSpec((1, nq, LANES, TILE), lambda b, i, p: (b, 0, p, 0)),
                  pl.BlockSpec((2, 2, TILE, TILE), lambda b, i, p: (p, 0, 0, 0))],
        out_specs=pl.BlockSpec((1, TILE, LANES), lambda b, i, p: (b, i, p)),
        out_shape=jax.ShapeDtypeStruct((bn, t, W_B), BF16),
        scratch_shapes=[pltpu.VMEM((nq, TILE, TILE), F32), pltpu.VMEM((IDX_HEADS, 8, TILE), F32)]
        + _sm_scratch(nq),
        compiler_params=_params(("arbitrary", "arbitrary", "arbitrary")),
        name="dsa",
    )(qi, ki4, wi, qb, kb, vb_t, bias_near)


def _post_kernel(x_ref, aa_ref, ab_ref, gm_ref, wg_ref, woa_ref, wob_ref, wout_ref,
                 gx_ref, wq_ref, kvm_ref, wox_ref, gmoe_ref, wrt_ref, brt_ref,
                 x2_ref, hm_ref, rw_ref, ids_ref, cnt_ref, carry_ref):
    step = pl.program_id(0)
    xf = x_ref[...]
    h = _rms(xf, gm_ref[...]).astype(BF16)
    gates = jax.nn.sigmoid(_mm(h, wg_ref[...]))
    oa = _mm(aa_ref[...], woa_ref[...])
    ob = _mm(ab_ref[...], wob_ref[...])
    mrg = gates[:, :D_MODEL] * oa + gates[:, D_MODEL:] * ob
    x1 = xf + _mm(mrg.astype(BF16), wout_ref[...])

    hx = _rms(x1, gx_ref[...]).astype(BF16)
    q = (_mm(hx, wq_ref[...]) * (HEAD_DIM ** -0.5)).astype(BF16)
    kvm = kvm_ref[0]
    lane = lax.broadcasted_iota(I32, (TILE, LANES), 1)
    outs = []
    for p_idx in range(X_HEADS // 2):
        lo = p_idx * LANES
        qs = _split_pair(q[:, lo:lo + LANES])
        km = kvm[:, lo:lo + LANES]
        vm = kvm[:, W_X + lo:W_X + lo + LANES]
        o = []
        for hd in range(2):
            s = _nt(qs[hd], km)
            p = jnp.exp(s - jnp.max(s, axis=1, keepdims=True))
            o.append(_mm(p.astype(BF16), vm) / jnp.sum(p, axis=1, keepdims=True))
        outs.append(jnp.where(lane < HEAD_DIM, o[0], o[1]))
    xo = jnp.concatenate(outs, axis=1).astype(BF16)
    x2 = x1 + _mm(xo, wox_ref[...])
    x2_ref[...] = x2

    hm = _rms(x2, gmoe_ref[...])
    hm_ref[...] = hm
    hm_hi = hm.astype(BF16)
    hm_lo = (hm - hm_hi.astype(F32)).astype(BF16)
    hi_part = _mm(hm_hi, wrt_ref[...])
    logits = (hi_part[:, :LANES] + hi_part[:, LANES:]) + _mm(hm_lo, wrt_ref[:, :LANES]) + brt_ref[...]
    big = LANES

    def argmax(v):
        top = jnp.max(v, axis=1, keepdims=True)
        return top, jnp.min(jnp.where(v == top, lane, big), axis=1, keepdims=True)

    is_grp = lane < R_EXP
    gtop, gsel = argmax(jnp.where(is_grp, logits, -jnp.inf))
    gw = 1.0 / jnp.sum(jnp.where(is_grp, jnp.exp(logits - gtop), 0.0), axis=1, keepdims=True)
    first = R_EXP + gsel * EXPERTS_PER_GROUP
    inside = jnp.logical_and(lane >= first, lane < first + EXPERTS_PER_GROUP)
    within = jnp.where(inside, logits, -jnp.inf)
    v0, i0 = argmax(within)
    v1, i1 = argmax(jnp.where(lane == i0, -jnp.inf, within))
    e1 = jnp.exp(v1 - v0)
    w0 = gw * (1.0 / (1.0 + e1))
    w1 = gw * (e1 / (1.0 + e1))
    rw_ref[...] = jnp.where(lane == 0, w0, jnp.where(lane == 1, w1, 0.0))

    @pl.when(step == 0)
    def _():
        carry_ref[...] = jnp.zeros(carry_ref.shape, F32)

    hit0 = lane == i0
    hit1 = lane == i1
    onehot = jnp.where(jnp.logical_or(hit0, hit1), 1.0, 0.0)
    tri = (lax.broadcasted_iota(I32, (TILE, TILE), 1) < lax.broadcasted_iota(I32, (TILE, TILE), 0))
    base = carry_ref[...] + _mm(tri.astype(BF16), onehot.astype(BF16))
    r0 = jnp.sum(jnp.where(hit0, base, 0.0), axis=1, keepdims=True)
    r1 = jnp.sum(jnp.where(hit1, base, 0.0), axis=1, keepdims=True)
    total = carry_ref[...] + jnp.sum(onehot, axis=0, keepdims=True)
    carry_ref[...] = total
    cnt_ref[...] = total.astype(I32)
    slab = jnp.where(lane == 0, (i0 - R_EXP).astype(F32),
                     jnp.where(lane == 1, (i1 - R_EXP).astype(F32),
                               jnp.where(lane == 2, r0, jnp.where(lane == 3, r1, 0.0))))
    ids_ref[0] = slab.T[:8].astype(I32)


def _post(xf, aa, ab, kvm, tiles_per_batch, weights):
    n, d = xf.shape
    nt = n // TILE
    row = lambda w: pl.BlockSpec((TILE, w), lambda i: (i, 0))
    full = lambda a: pl.BlockSpec(a.shape, lambda i: (0,) * a.ndim)
    gm, wg, woa, wob, wout, gx, wq, wox, gmoe, wrt, brt = weights
    return pl.pallas_call(
        _post_kernel,
        grid=(nt,),
        in_specs=[row(d), row(W_A), row(W_B), full(gm), full(wg), full(woa), full(wob), full(wout),
                  full(gx), full(wq),
                  pl.BlockSpec((1,) + kvm.shape[1:], lambda i: (i // tiles_per_batch, 0, 0)),
                  full(wox), full(gmoe), full(wrt), full(brt)],
        out_specs=[row(d), row(d), row(LANES),
                   pl.BlockSpec((1, 8, TILE), lambda i: (i, 0, 0)),
                   pl.BlockSpec((1, LANES), lambda i: (0, 0))],
        out_shape=[jax.ShapeDtypeStruct((n, d), F32), jax.ShapeDtypeStruct((n, d), F32),
                   jax.ShapeDtypeStruct((n, LANES), F32),
                   jax.ShapeDtypeStruct((nt, 8, TILE), I32),
                   jax.ShapeDtypeStruct((1, LANES), I32)],
        scratch_shapes=[pltpu.VMEM((1, LANES), F32)],
        compiler_params=_params(("arbitrary",)),
        name="post",
    )(xf, aa, ab, gm, wg, woa, wob, wout, gx, wq, kvm, wox, gmoe, wrt, brt)


def _segment_starts(cnt_ref, seg_ref):
    def body(e, acc):
        seg_ref[e] = acc
        c = cnt_ref[0, R_EXP + e]
        return acc + ((c + (MOE_BLOCK - 1)) >> 8 << 8)
    return lax.fori_loop(0, N_EXPERTS, body, jnp.int32(0))


def _start_row_copies(ids_ref, seg_ref, rows_ref, tile_ref, sem, gather):
    def issue(t, carry):
        for k in range(2):
            dest = seg_ref[ids_ref[0, k, t]] + ids_ref[0, 2 + k, t]
            if gather:
                cp = pltpu.make_async_copy(rows_ref.at[pl.ds(dest, 1)], tile_ref.at[k, pl.ds(t, 1)], sem)
            else:
                cp = pltpu.make_async_copy(tile_ref.at[pl.ds(t, 1)], rows_ref.at[pl.ds(dest, 1)], sem)
            cp.start(priority=k)
        return carry
    lax.fori_loop(0, TILE, issue, 0, unroll=4)


def _wait_row_copies(rows_ref, tile_ref, sem):
    pltpu.make_async_copy(rows_ref.at[pl.ds(0, TILE)], tile_ref, sem).wait()


def _dispatch_kernel(ids_ref, cnt_ref, hm_ref, xr_in_ref, xr_ref, blk_ref, seg_ref, sem):
    del xr_in_ref
    step = pl.program_id(0)
    nb = blk_ref.shape[1] - 1

    @pl.when(step == 0)
    def _():
        used = _segment_starts(cnt_ref, seg_ref)

        def per_expert(e, last):
            c = cnt_ref[0, R_EXP + e]
            b0 = seg_ref[e] >> 8
            n = (c + (MOE_BLOCK - 1)) >> 8

            def fill(kk, carry):
                blk_ref[0, b0 + kk] = e
                return carry
            lax.fori_loop(0, n, fill, 0)
            return jnp.where(n > 0, e, last)
        last = lax.fori_loop(0, N_EXPERTS, per_expert, jnp.int32(0))

        def tail(b, carry):
            blk_ref[0, b] = last
            return carry
        lax.fori_loop(used >> 8, nb, tail, 0)
        blk_ref[0, nb] = used >> 8

    _start_row_copies(ids_ref, seg_ref, xr_ref, hm_ref, sem, gather=False)
    for _ in range(2):
        _wait_row_copies(xr_ref, hm_ref, sem)


def _dispatch(ids, cnt, hm, n_rows):
    n, d = hm.shape
    nt = n // TILE
    nb = n_rows // MOE_BLOCK
    zeros = jnp.zeros((n_rows, d), F32)
    return pl.pallas_call(
        _dispatch_kernel,
        grid=(nt,),
        in_specs=[pl.BlockSpec((1, 8, TILE), lambda i: (i, 0, 0), memory_space=pltpu.SMEM),
                  pl.BlockSpec(memory_space=pltpu.SMEM),
                  pl.BlockSpec((TILE, d), lambda i: (i, 0)),
                  pl.BlockSpec(memory_space=pl.ANY)],
        out_specs=[pl.BlockSpec(memory_space=pl.ANY),
                   pl.BlockSpec(memory_space=pltpu.SMEM)],
        out_shape=[jax.ShapeDtypeStruct((n_rows, d), F32),
                   jax.ShapeDtypeStruct((1, nb + 1), I32)],
        scratch_shapes=[pltpu.SMEM((N_EXPERTS,), I32), pltpu.SemaphoreType.DMA(())],
        input_output_aliases={3: 0},
        compiler_params=_params(("arbitrary",)),
        name="dispatch",
    )(ids, cnt, hm, zeros)


def _experts_kernel(blk_ref, x_ref, w1_ref, w3_ref, w2_ref, y_ref, w1b, w3b, w2b):
    b = pl.program_id(0)
    nb = pl.num_programs(0)
    e = blk_ref[0, b]
    prev = blk_ref[0, jnp.maximum(b - 1, 0)]

    @pl.when(jnp.logical_or(b == 0, e != prev))
    def _():
        w1b[...] = w1_ref[0].astype(BF16)
        w3b[...] = w3_ref[0].astype(BF16)
        w2b[...] = w2_ref[0].astype(BF16)

    used = blk_ref[0, nb]

    @pl.when(b < used)
    def _():
        xb = x_ref[...].astype(BF16)
        a = _mm(xb, w1b[...])
        g = _mm(xb, w3b[...])
        y_ref[...] = _mm((a * jax.nn.sigmoid(a) * g).astype(BF16), w2b[...])

    @pl.when(b >= used)
    def _():
        y_ref[...] = jnp.zeros(y_ref.shape, F32)


def _experts(blk, xr, w1, w3, w2):
    n_rows, d = xr.shape
    nb = n_rows // MOE_BLOCK
    wspec = lambda s: pl.BlockSpec((1,) + s, lambda b, blk: (blk[0, b], 0, 0))
    return pl.pallas_call(
        _experts_kernel,
        grid_spec=pltpu.PrefetchScalarGridSpec(
            num_scalar_prefetch=1,
            grid=(nb,),
            in_specs=[pl.BlockSpec((MOE_BLOCK, d), lambda b, blk: (b, 0)),
                      wspec((d, D_EXPERT)), wspec((d, D_EXPERT)), wspec((D_EXPERT, d))],
            out_specs=pl.BlockSpec((MOE_BLOCK, d), lambda b, blk: (b, 0)),
            scratch_shapes=[pltpu.VMEM((d, D_EXPERT), BF16), pltpu.VMEM((d, D_EXPERT), BF16),
                            pltpu.VMEM((D_EXPERT, d), BF16)]),
        out_shape=jax.ShapeDtypeStruct((n_rows, d), F32),
        compiler_params=_params(("arbitrary",)),
        name="experts",
    )(blk, xr, w1, w3, w2)


def _combine_kernel(ids_ref, ids_next_ref, cnt_ref, x2_ref, rw_ref, g_ref, yr_ref, o_ref, y_ref, seg_ref, sem):
    step = pl.program_id(0)
    slot = step % 2

    @pl.when(step == 0)
    def _():
        _segment_starts(cnt_ref, seg_ref)
        _start_row_copies(ids_ref, seg_ref, yr_ref, y_ref.at[0], sem.at[0], gather=True)

    @pl.when(step + 1 < pl.num_programs(0))
    def _():
        _start_row_copies(ids_next_ref, seg_ref, yr_ref, y_ref.at[1 - slot], sem.at[1 - slot], gather=True)

    for k in range(2):
        _wait_row_copies(yr_ref, y_ref.at[slot, k], sem.at[slot])
    rw = rw_ref[...]
    x3 = x2_ref[...] + (y_ref[slot, 0] * rw[:, 0:1] + y_ref[slot, 1] * rw[:, 1:2])
    o_ref[...] = _rms(x3, g_ref[...])


def _combine(ids, cnt, x2, rw, g, yr):
    n, d = x2.shape
    nt = n // TILE
    row = lambda w: pl.BlockSpec((TILE, w), lambda i: (i, 0))
    return pl.pallas_call(
        _combine_kernel,
        grid=(nt,),
        in_specs=[pl.BlockSpec((1, 8, TILE), lambda i: (i, 0, 0), memory_space=pltpu.SMEM),
                  pl.BlockSpec((1, 8, TILE), lambda i: (jnp.minimum(i + 1, nt - 1), 0, 0),
                               memory_space=pltpu.SMEM),
                  pl.BlockSpec(memory_space=pltpu.SMEM),
                  row(d), row(LANES), pl.BlockSpec((1, d), lambda i: (0, 0)),
                  pl.BlockSpec(memory_space=pl.ANY)],
        out_specs=row(d),
        out_shape=jax.ShapeDtypeStruct((n, d), F32),
        scratch_shapes=[pltpu.VMEM((2, 2, TILE, d), F32), pltpu.SMEM((N_EXPERTS,), I32),
                        pltpu.SemaphoreType.DMA((2,))],
        compiler_params=_params(("arbitrary",)),
        name="combine",
    )(ids, ids, cnt, x2, rw, g, yr)


def _t5_bucket(dist):
    n = jnp.maximum(dist, 0)
    max_exact = REL_BUCKETS // 2
    nf = jnp.maximum(n, 1).astype(F32)
    large = max_exact + (jnp.log(nf / max_exact) / math.log(REL_MAX_DIST / max_exact)
                         * (REL_BUCKETS - max_exact)).astype(I32)
    large = jnp.minimum(large, REL_BUCKETS - 1)
    return jnp.where(n < max_exact, n, large)


def _bias_tables(tab):
    r = jnp.arange(TILE)[:, None]
    c = jnp.arange(TILE)[None, :]
    rel = tab - tab[:, REL_BUCKETS - 1:]

    def tile(dist):
        onehot = (_t5_bucket(dist)[..., None] == jnp.arange(REL_BUCKETS)).astype(F32)
        return jnp.einsum('rcb,hb->hrc', onehot, rel, precision=lax.Precision.HIGHEST)

    own = jnp.where(r >= c, tile(r - c), NEG)
    near = jnp.stack([own, tile(r - c + TILE)], axis=1).astype(F32)
    return near.swapaxes(-1, -2)


def _block_mean_slabs(kmean, bn, nq):
    km = kmean.reshape(bn, nq, A_HEADS, HEAD_DIM).transpose(0, 2, 1, 3)
    pad = ((0, 0), (0, 0), (HEAD_DIM, LANES - HEAD_DIM - nq), (0, LANES - HEAD_DIM))
    return jnp.pad(km, pad).astype(BF16)


def _pad_cols(w, width):
    return jnp.pad(w, ((0, 0), (0, width - w.shape[1])))


def kernel(x, mem, rel_bias, final_norm, norm_mix, w_in, ckv_norm, w_uk, w_uv, w_oa, w_ob,
           w_out, norm_x, mem_norm, wq_x, wk_x, wv_x, wo_x, norm_moe, w_group, b_group,
           w_router, b_router, w1, w3, w2):
    bn, t, d = x.shape
    n = bn * t
    nq = t // TILE
    assert t % TILE == 0 and nq <= LANES - HEAD_DIM and norm_mix.shape[0] == 1
    assert nq % FAR_GROUP == 0
    near_a = _bias_tables(rel_bias[:, :A_HEADS].T)
    near_b = _bias_tables(rel_bias[:, A_HEADS:].T)
    row = lambda v: v.reshape(1, -1).astype(F32)

    wi = w_in[0]
    wp = jnp.concatenate(
        [wi[:, OFF_AQ:OFF_IK], jnp.tile(wi[:, OFF_IK:OFF_IW], (1, LANES // IDX_DIM)),
         _pad_cols(wi[:, OFF_IW:OFF_GA], LANES)], axis=1).astype(BF16)
    wkv = jnp.concatenate([w_uk[0], w_uv[0]], axis=1).astype(BF16)
    xf = x.reshape(n, d)
    qa, ka, va, qb, kb, vb, qi, ki4, widx, kmean = _proj_in(
        xf, row(norm_mix[0]), wp, row(ckv_norm[0]), wkv, nq)

    b3 = lambda a: a.reshape(bn, t, a.shape[-1])
    attn_a = _moba(b3(qa), b3(ka), va.reshape(bn, nq, W_A, TILE), _block_mean_slabs(kmean, bn, nq), near_a)
    attn_b = _dsa(b3(qi), b3(ki4), b3(widx), b3(qb), b3(kb), vb.reshape(bn, nq, W_B, TILE), near_b)

    kvm = _mem_kv(mem, row(mem_norm[0]), jnp.concatenate([wk_x[0], wv_x[0]], axis=1).astype(BF16))
    wrt = _pad_cols(jnp.concatenate([w_group[0], w_router[0]], axis=1), LANES).astype(F32)
    wrt_hi = wrt.astype(BF16)
    wrt = jnp.concatenate([wrt_hi, (wrt - wrt_hi.astype(F32)).astype(BF16)], axis=1)
    brt = _pad_cols(jnp.concatenate([b_group[0], b_router[0]]).reshape(1, -1), LANES).astype(F32)
    weights = (row(norm_mix[0]), wi[:, OFF_GA:IN_COLS].astype(BF16), w_oa[0].astype(BF16),
               w_ob[0].astype(BF16), w_out[0].astype(BF16), row(norm_x[0]), wq_x[0].astype(BF16),
               wo_x[0].astype(BF16), row(norm_moe[0]), wrt, brt)
    x2, hm, rw, ids, cnt = _post(xf, attn_a.reshape(n, W_A), attn_b.reshape(n, W_B), kvm, t // TILE, weights)

    n_rows = 2 * n + N_EXPERTS * MOE_BLOCK
    xr, blk = _dispatch(ids, cnt, hm, n_rows)
    yr = _experts(blk, xr, w1[0], w3[0], w2[0])
    out = _combine(ids, cnt, x2, rw, row(final_norm), yr)
    return out.reshape(bn, t, d)
```

```python
import functools
import math

import jax
import jax.numpy as jnp
from jax import lax
from jax.experimental import pallas as pl
from jax.experimental.pallas import tpu as pltpu

D_MODEL = 1024
HEAD_DIM = 64
A_HEADS = 8
B_HEADS = 8
W_A = A_HEADS * HEAD_DIM
W_B = B_HEADS * HEAD_DIM
MOBA_BLOCK = 256
MOBA_TOPK = 3
DSA_KV_RANK = 256
IDX_HEADS = 8
IDX_DIM = 32
DSA_TOPK = 256
REL_BUCKETS = 32
REL_MAX_DIST = 128
X_HEADS = 4
W_X = X_HEADS * HEAD_DIM
N_GROUPS = 4
EXPERTS_PER_GROUP = 8
N_EXPERTS = N_GROUPS * EXPERTS_PER_GROUP
D_EXPERT = 512
MOE_BLOCK = 256
EPS = 1e-6
NEG = -1e30

OFF_AQ = 0
OFF_AK = OFF_AQ + W_A
OFF_AV = OFF_AK + W_A
OFF_BQ = OFF_AV + W_A
OFF_CKV = OFF_BQ + W_B
OFF_IQ = OFF_CKV + DSA_KV_RANK
OFF_IK = OFF_IQ + IDX_HEADS * IDX_DIM
OFF_IW = OFF_IK + IDX_DIM
OFF_GA = OFF_IW + IDX_HEADS
OFF_GB = OFF_GA + D_MODEL
IN_COLS = OFF_GB + D_MODEL

LANES = 128
TILE = 256
VMEM_LIMIT = 56 * 1024 * 1024
BF16 = jnp.bfloat16
F32 = jnp.float32
I32 = jnp.int32

P_QA, P_KA, P_VA, P_QB = 0, 512, 1024, 1536
P_CKV, P_IQ, P_IK4, P_IW = 2048, 2304, 2560, 2688
P_COLS = 2816

R_GRP = 0
R_EXP = N_GROUPS


def _nt(a, b):
    return lax.dot_general(a, b, (((1,), (1,)), ((), ())), preferred_element_type=F32)


def _mm(a, b):
    return jnp.dot(a, b, preferred_element_type=F32)


def _rms(xf, g):
    return xf * lax.rsqrt(jnp.mean(xf * xf, axis=-1, keepdims=True) + EPS) * g


def _params(sem, vmem=VMEM_LIMIT):
    return pltpu.CompilerParams(dimension_semantics=sem, vmem_limit_bytes=vmem)


def _mem_kv_kernel(mem_ref, g_ref, w_ref, o_ref):
    m = _rms(mem_ref[0], g_ref[...]).astype(BF16)
    o_ref[0] = _mm(m, w_ref[...]).astype(BF16)


def _mem_kv(mem, g, wkv):
    bn, ml, d = mem.shape
    return pl.pallas_call(
        _mem_kv_kernel,
        grid=(bn,),
        in_specs=[pl.BlockSpec((1, ml, d), lambda b: (b, 0, 0)),
                  pl.BlockSpec((1, d), lambda b: (0, 0)),
                  pl.BlockSpec((d, 2 * W_X), lambda b: (0, 0))],
        out_specs=pl.BlockSpec((1, ml, 2 * W_X), lambda b: (b, 0, 0)),
        out_shape=jax.ShapeDtypeStruct((bn, ml, 2 * W_X), BF16),
        compiler_params=_params(("arbitrary",)),
        name="mem_kv",
    )(mem, g, wkv)


def _head_slabs(zp, extra):
    lane = lax.broadcasted_iota(I32, zp.shape, 1)
    low = lane < HEAD_DIM
    return jnp.where(low, zp, extra), jnp.where(low, pltpu.roll(zp, HEAD_DIM, 1), extra)


def _proj_in_kernel(blocks_per_seq, x_ref, g_ref, w_ref, cg_ref, wkv_ref,
                    qa_ref, ka_ref, va_ref, qb_ref, kb_ref, vb_ref, qi_ref, ki_ref, wi_ref, km_ref):
    h = _rms(x_ref[...], g_ref[...]).astype(BF16)
    z = _mm(h, w_ref[...])
    scale = HEAD_DIM ** -0.5
    zk = z[:, P_KA:P_VA]
    km_ref[0] = jnp.mean(zk, axis=0, keepdims=True)
    lane = lax.broadcasted_iota(I32, (TILE, LANES), 1)
    blk = pl.program_id(0) % blocks_per_seq
    onehot = jnp.where(lane == HEAD_DIM + blk, 1.0, 0.0)
    zero = jnp.zeros((TILE, LANES), F32)
    for p_idx in range(A_HEADS // 2):
        lo = p_idx * LANES
        qe, qo = _head_slabs(z[:, P_QA + lo:P_QA + lo + LANES] * scale, zero)
        ke, ko = _head_slabs(zk[:, lo:lo + LANES], onehot)
        qa_ref[:, 2 * lo:2 * lo + 2 * LANES] = jnp.concatenate([qe, qo], axis=1).astype(BF16)
        ka_ref[:, 2 * lo:2 * lo + 2 * LANES] = jnp.concatenate([ke, ko], axis=1).astype(BF16)
    va_ref[0] = z[:, P_VA:P_QB].T.astype(BF16)
    qb_ref[...] = (z[:, P_QB:P_CKV] * scale).astype(BF16)
    ckv = _rms(z[:, P_CKV:P_IQ], cg_ref[...]).astype(BF16)
    kv = _mm(ckv, wkv_ref[...])
    kb_ref[...] = kv[:, :W_B].astype(BF16)
    vb_ref[0] = kv[:, W_B:].T.astype(BF16)
    qi_ref[...] = z[:, P_IQ:P_IK4].astype(BF16)
    ki_ref[...] = z[:, P_IK4:P_IW].astype(BF16)
    wi_ref[...] = z[:, P_IW:P_COLS] * ((IDX_HEADS ** -0.5) * (IDX_DIM ** -0.5))


def _proj_in(xf, g, wp, cg, wkv, blocks_per_seq):
    n, d = xf.shape
    nt = n // TILE
    row = lambda w: pl.BlockSpec((TILE, w), lambda i: (i, 0))
    full = lambda a: pl.BlockSpec(a.shape, lambda i: (0,) * a.ndim)
    outs = [(2 * W_A, BF16, False), (2 * W_A, BF16, False), (W_A, BF16, True), (W_B, BF16, False),
            (W_B, BF16, False), (W_B, BF16, True), (IDX_HEADS * IDX_DIM, BF16, False), (LANES, BF16, False),
            (LANES, F32, False)]
    blocked = lambda w: pl.BlockSpec((1, w, TILE), lambda i: (i, 0, 0))
    return pl.pallas_call(
        functools.partial(_proj_in_kernel, blocks_per_seq),
        grid=(nt,),
        in_specs=[row(d), full(g), full(wp), full(cg), full(wkv)],
        out_specs=[blocked(w) if tr else row(w) for w, _, tr in outs]
        + [pl.BlockSpec((1, 1, W_A), lambda i: (i, 0, 0))],
        out_shape=[jax.ShapeDtypeStruct((nt, w, TILE) if tr else (n, w), t) for w, t, tr in outs]
        + [jax.ShapeDtypeStruct((nt, 1, W_A), F32)],
        compiler_params=_params(("arbitrary",)),
        name="proj_in",
    )(xf, g, wp, cg, wkv)


SUB = TILE // 8
LOG2E = math.log2(math.e)
FAR_GROUP = 8


def _sm_init(mx_ref, acc_ref):
    mx_ref[...] = jnp.full(mx_ref.shape, NEG, F32)
    acc_ref[...] = jnp.zeros(acc_ref.shape, F32)


def _sm_logits(hd, slot, s, sc_ref, mx_ref):
    s = s * LOG2E
    sc_ref[hd, slot] = s
    mx_ref[hd] = jnp.maximum(mx_ref[hd], jnp.max(s.reshape(SUB, 8, TILE), axis=0))


def _sm_rowmax(mx_ref):
    for hd in range(2):
        mx_ref[hd] = jnp.broadcast_to(jnp.max(mx_ref[hd], axis=0, keepdims=True), (8, TILE))


def _sm_probs(hd, slots, sc_ref, mx_ref):
    m = mx_ref[hd][None]
    ps = [jnp.exp2((sc_ref[hd, sl].reshape(SUB, 8, TILE) - m).reshape(TILE, TILE).astype(BF16))
          for sl in slots]
    return ps[0] if len(ps) == 1 else jnp.concatenate(ps, axis=0)


def _sm_accumulate(hd, pb, v, acc_ref):
    row = lax.broadcasted_iota(I32, v.shape, 0)
    own = (row < HEAD_DIM) if hd == 0 else (row >= HEAD_DIM)
    acc_ref[hd] = acc_ref[hd] + _mm(jnp.where(own, v, jnp.ones_like(v)), pb)


def _sm_values(hd, slots, v, sc_ref, mx_ref, acc_ref):
    _sm_accumulate(hd, _sm_probs(hd, slots, sc_ref, mx_ref), v, acc_ref)


def _far_plan(n_far):
    full = n_far // FAR_GROUP
    rest = n_far - full * FAR_GROUP
    half = jnp.logical_and(rest > 0, rest <= FAR_GROUP // 2)
    return full + (rest > FAR_GROUP // 2).astype(I32), half, full * FAR_GROUP


def _sm_far_loop(plan, group_fn):
    trips, half, half_base = plan
    lax.fori_loop(0, trips, lambda g, c: group_fn(FAR_GROUP * g, FAR_GROUP) or c, 0)
    pl.when(half)(lambda: group_fn(half_base, FAR_GROUP // 2))


def _sm_far_values(plan, vals, sc_ref, mx_ref, acc_ref):
    def group(j0, n):
        v = vals(j0, n)
        for hd in range(2):
            _sm_values(hd, [j0 + u for u in range(n)], v, sc_ref, mx_ref, acc_ref)

    _sm_far_loop(plan, group)


def _sm_output(acc_ref):
    num = jnp.concatenate([acc_ref[0, :HEAD_DIM], acc_ref[1, HEAD_DIM:]], axis=0)
    den = jnp.concatenate([acc_ref[0, HEAD_DIM:], acc_ref[1, :HEAD_DIM]], axis=0)
    return (num / den).T


def _split_pair(qp):
    lane = lax.broadcasted_iota(I32, qp.shape, 1)
    zero = jnp.zeros_like(qp)
    return jnp.where(lane < HEAD_DIM, qp, zero), jnp.where(lane >= HEAD_DIM, qp, zero)


def _moba_kernel(q_ref, k_ref, v_ref, km_ref, bias_ref, o_ref, sc_ref, mx_ref, acc_ref):
    i = pl.program_id(2)
    nq = pl.num_programs(2)
    n_slot = LANES - HEAD_DIM
    blk = lax.broadcasted_iota(I32, (n_slot, TILE), 0)
    in_blk = blk < nq
    past = blk < i
    no_mask = jnp.zeros((HEAD_DIM, TILE), F32)

    qm = []
    for hd in range(2):
        q = q_ref[0, :, hd * LANES:(hd + 1) * LANES]
        g = jnp.where(past, _nt(km_ref[0, hd], q)[HEAD_DIM:], -jnp.inf)
        sel = jnp.zeros((n_slot, TILE), jnp.bool_)
        for _ in range(MOBA_TOPK):
            top = jnp.max(g, axis=0, keepdims=True)
            hit = jnp.logical_and(g == top, in_blk)
            first = jnp.min(jnp.where(hit, blk, n_slot), axis=0, keepdims=True)
            pick = blk == first
            sel = jnp.logical_or(sel, jnp.logical_and(pick, past))
            g = jnp.where(pick, -jnp.inf, g)
        addm = jnp.where(sel, 0.0, NEG)
        mask = jnp.where(past, addm, jnp.where(jnp.logical_and(in_blk, blk > i), NEG, 0.0))
        qm.append(q + jnp.concatenate([no_mask, mask], axis=0).T.astype(BF16))

    def keys(j, hd):
        off = pl.multiple_of(j * TILE, TILE)
        return k_ref[0, pl.ds(off, TILE), hd * LANES:(hd + 1) * LANES]

    plan = _far_plan(i + 1)
    _sm_init(mx_ref, acc_ref)

    def logits(j0, n):
        for u in range(n):
            j = j0 + u
            tile = _near_bias_index(j, i)
            for hd in range(2):
                _sm_logits(hd, j, _nt(keys(j, hd), qm[hd]) + bias_ref[hd, tile], sc_ref, mx_ref)

    _sm_far_loop(plan, logits)
    _sm_rowmax(mx_ref)

    def vals(j, n):
        return jnp.concatenate([v_ref[0, j + u] for u in range(n)], axis=1)

    _sm_far_values(plan, vals, sc_ref, mx_ref, acc_ref)
    o_ref[0] = _sm_output(acc_ref).astype(BF16)


def _near_bias_index(j, i):
    return jnp.clip(j - i + 2, 0, 2)


def _sm_scratch(n_slots):
    return [pltpu.VMEM((2, n_slots, TILE, TILE), F32),
            pltpu.VMEM((2, 8, TILE), F32),
            pltpu.VMEM((2, LANES, TILE), F32)]


def _moba(qa, ka, va_t, kmp, bias_near):
    bn, nq = va_t.shape[:2]
    t = nq * TILE
    npair = A_HEADS // 2
    return pl.pallas_call(
        _moba_kernel,
        grid=(bn, npair, nq),
        in_specs=[pl.BlockSpec((1, TILE, 2 * LANES), lambda b, p, i: (b, i, p)),
                  pl.BlockSpec((1, t, 2 * LANES), lambda b, p, i: (b, 0, p)),
                  pl.BlockSpec((1, nq, LANES, TILE), lambda b, p, i: (b, 0, p, 0)),
                  pl.BlockSpec((1, 2, LANES, LANES), lambda b, p, i: (b, p, 0, 0)),
                  pl.BlockSpec((2, 3, TILE, TILE), lambda b, p, i: (p, 0, 0, 0))],
        out_specs=pl.BlockSpec((1, TILE, LANES), lambda b, p, i: (b, i, p)),
        out_shape=jax.ShapeDtypeStruct((bn, t, W_A), BF16),
        scratch_shapes=_sm_scratch(nq),
        compiler_params=_params(("arbitrary", "arbitrary", "arbitrary")),
        name="moba",
    )(qa, ka, va_t, kmp, bias_near)


def _dsa_select(i, qi_ref, ki_ref, wi_ref, s_ref, wb_ref):
    nch = i + 1
    sub = TILE // 8
    lane = lax.broadcasted_iota(I32, (TILE, LANES), 1)
    key_in = (lax.broadcasted_iota(I32, (sub, 8, TILE), 0) * 8 + lax.broadcasted_iota(I32, (sub, 8, TILE), 1))
    qpos = i * TILE + lax.broadcasted_iota(I32, (sub, 8, TILE), 2)

    def rows8(x):
        return jnp.broadcast_to(x, (8, TILE))[None]

    w_t = wi_ref[0].T
    qi = qi_ref[0]
    per_group = LANES // IDX_DIM
    qh = []
    for h in range(IDX_HEADS):
        g, r = divmod(h, per_group)
        qg = qi[:, g * LANES:(g + 1) * LANES]
        keep = jnp.logical_and(lane >= r * IDX_DIM, lane < (r + 1) * IDX_DIM)
        qh.append(jnp.where(keep, qg, jnp.zeros_like(qg)))
        wb_ref[h] = jnp.broadcast_to(w_t[h:h + 1], (8, TILE))

    def score(c, carry):
        off = pl.multiple_of(c * TILE, TILE)
        kc = ki_ref[0, pl.ds(off, TILE), :]
        acc = jnp.zeros((sub, 8, TILE), F32)
        for h in range(IDX_HEADS):
            acc = acc + jnp.maximum(_nt(kc, qh[h]), 0.0).reshape(sub, 8, TILE) * wb_ref[h][None]
        s_ref[c] = jnp.where(c * TILE + key_in <= qpos, acc, -jnp.inf).reshape(TILE, TILE)
        return carry

    n_pair = (nch + 1) // 2

    def pairs(step, init):
        return lax.fori_loop(0, n_pair, lambda g, acc: step(2 * g + 1, step(2 * g, acc)), init)

    pairs(score, 0)

    def fold(init, step):
        return pairs(lambda c, acc: step(acc, s_ref[c].reshape(sub, 8, TILE), c), init)

    def count(pred):
        def step(acc, blk, c):
            return acc + jnp.sum(jnp.where(pred(blk, c * TILE + key_in), 1.0, 0.0), axis=0)
        return jnp.sum(fold(jnp.zeros((8, TILE), F32), step), axis=0, keepdims=True)

    wide = rows8

    def stats(acc, blk, c):
        top, low, pos, nonneg = acc
        top = jnp.maximum(top, jnp.max(blk, axis=0))
        low = jnp.minimum(low, jnp.min(jnp.where(blk == -jnp.inf, jnp.inf, blk), axis=0))
        pos = pos + jnp.sum(jnp.where(blk > 0.0, 1.0, 0.0), axis=0)
        nonneg = nonneg + jnp.sum(jnp.where(blk >= 0.0, 1.0, 0.0), axis=0)
        return top, low, pos, nonneg

    zeros8 = jnp.zeros((8, TILE), F32)
    acc = fold((zeros8 - jnp.inf, zeros8 + jnp.inf, zeros8, zeros8), stats)
    hi = jnp.max(acc[0], axis=0, keepdims=True)
    lo = jnp.min(acc[1], axis=0, keepdims=True)
    n_pos = jnp.sum(acc[2], axis=0, keepdims=True)
    n_nonneg = jnp.sum(acc[3], axis=0, keepdims=True)

    top = hi
    top_ties = count(lambda half, kpos: half >= wide(top)) >= DSA_TOPK
    at_zero = jnp.logical_and(n_pos < DSA_TOPK, n_nonneg >= DSA_TOPK)
    lo = jnp.where(n_pos >= DSA_TOPK, 0.0, lo)
    hi = jnp.where(n_nonneg < DSA_TOPK, 0.0, hi)

    def search_more(state):
        return jnp.logical_and(jnp.min(state[3]) == 0, state[4] < 400)

    def search(state):
        lo, hi, thr, done, it = state
        mid = 0.5 * lo + 0.5 * hi
        closed = jnp.logical_or(mid <= lo, mid >= hi)
        n = count(lambda half, kpos: half >= wide(mid))
        found = jnp.where(closed, lo, mid)
        stop = jnp.logical_or(closed, n == DSA_TOPK)
        live = done == 0
        thr = jnp.where(jnp.logical_and(live, stop), found, thr)
        go = jnp.logical_and(live, jnp.logical_not(stop))
        lo = jnp.where(jnp.logical_and(go, n > DSA_TOPK), mid, lo)
        hi = jnp.where(jnp.logical_and(go, n < DSA_TOPK), mid, hi)
        return lo, hi, thr, jnp.where(stop, 1, done), it + 1

    thr0 = jnp.where(top_ties, top, 0.0)
    done0 = jnp.logical_or(top_ties, at_zero).astype(I32)
    state = (lo, hi, thr0, done0, jnp.int32(0))
    thr = wide(lax.while_loop(search_more, lambda st: search(search(st)), state)[2])
    n_gt = count(lambda half, kpos: half > thr)
    n_ge = count(lambda half, kpos: half >= thr)
    need = DSA_TOPK - n_gt

    cut_bits = max(1, (s_ref.shape[0] * TILE - 1).bit_length())

    def cut_step(b, cut):
        cand = cut | (jnp.int32(1) << (cut_bits - 1 - b))
        cw = wide(cand)
        n = count(lambda half, kpos: jnp.logical_and(half == thr, kpos < cw))
        return jnp.where(n < need, cand, cut)

    cut = lax.cond(jnp.max(n_ge) > DSA_TOPK,
                   lambda: lax.fori_loop(0, cut_bits, cut_step, jnp.zeros((1, TILE), I32)),
                   lambda: jnp.full((1, TILE), 2 ** 30, I32))
    cut = wide(jnp.where(n_ge > DSA_TOPK, cut, 2 ** 30))

    def to_mask(c, carry):
        blk = s_ref[c].reshape(sub, 8, TILE)
        kpos = c * TILE + key_in
        keep = jnp.logical_or(blk > thr, jnp.logical_and(blk == thr, kpos <= cut))
        s_ref[c] = jnp.where(jnp.logical_and(keep, kpos <= qpos), 0.0, NEG).reshape(TILE, TILE)
        return carry

    pairs(to_mask, 0)


def _dsa_kernel(qi_ref, ki_ref, wi_ref, q_ref, k_ref, v_ref, bias_ref, o_ref,
                s_ref, wb_ref, sc_ref, mx_ref, acc_ref):
    i = pl.program_id(1)
    pair = pl.program_id(2)

    @pl.when(jnp.logical_and(pair == 0, i == 0))
    def _():
        key = lax.broadcasted_iota(I32, (TILE, TILE), 0)
        query = lax.broadcasted_iota(I32, (TILE, TILE), 1)
        s_ref[0] = jnp.where(key <= query, 0.0, NEG)

    @pl.when(jnp.logical_and(pair == 0, i >= 1))
    def _():
        _dsa_select(i, qi_ref, ki_ref, wi_ref, s_ref, wb_ref)

    qs = _split_pair(q_ref[0])
    plan = _far_plan(jnp.maximum(i - 1, 0))
    _sm_init(mx_ref, acc_ref)

    def vals(j, n):
        return jnp.concatenate([v_ref[0, j + u] for u in range(n)], axis=1)

    def logits(j, slot, msk, near=None):
        k_j = k_ref[0, pl.ds(pl.multiple_of(j * TILE, TILE), TILE), :]
        for hd in range(2):
            s = _nt(k_j, qs[hd]) + msk
            if near is not None:
                s = s + bias_ref[hd, near]
            _sm_logits(hd, slot, s, sc_ref, mx_ref)

    def far_logits(j0, n):
        for u in range(n):
            j = j0 + u
            logits(j, j, jnp.where(j < i - 1, s_ref[jnp.minimum(j, i)], NEG))

    _sm_far_loop(plan, far_logits)
    nq = pl.num_programs(1)
    j_prev = jnp.maximum(i - 1, 0)
    logits(j_prev, nq, jnp.where(i >= 1, s_ref[j_prev], NEG), near=1)
    logits(i, nq + 1, s_ref[i], near=2)
    _sm_rowmax(mx_ref)

    _sm_far_values(plan, vals, sc_ref, mx_ref, acc_ref)
    v_near = jnp.concatenate([vals(j_prev, 1), vals(i, 1)], axis=1)
    for hd in range(2):
        _sm_values(hd, [nq, nq + 1], v_near, sc_ref, mx_ref, acc_ref)
    o_ref[0] = _sm_output(acc_ref).astype(BF16)


def _dsa(qi, ki4, wi, qb, kb, vb_t, bias_near):
    bn, t, _ = qb.shape
    nq = t // TILE
    assert nq % 2 == 0
    npair = B_HEADS // 2
    qrow = lambda w: pl.BlockSpec((1, TILE, w), lambda b, i, p: (b, i, 0))
    return pl.pallas_call(
        _dsa_kernel,
        grid=(bn, nq, npair),
        in_specs=[qrow(IDX_HEADS * IDX_DIM),
                  pl.BlockSpec((1, t, LANES), lambda b, i, p: (b, 0, 0), pipeline_mode=pl.Buffered(1)),
                  qrow(LANES),
                  pl.BlockSpec((1, TILE, LANES), lambda b, i, p: (b, i, p)),
                  pl.BlockSpec((1, t, LANES), lambda b, i, p: (b, 0, p)),
                  pl.BlockSpec((1, nq, LANES, TILE), lambda b, i, p: (b, 0, p, 0)),
                  pl.BlockSpec((2, 3, TILE, TILE), lambda b, i, p: (p, 0, 0, 0))],
        out_specs=pl.BlockSpec((1, TILE, LANES), lambda b, i, p: (b, i, p)),
        out_shape=jax.ShapeDtypeStruct((bn, t, W_B), BF16),
        scratch_shapes=[pltpu.VMEM((nq, TILE, TILE), F32), pltpu.VMEM((IDX_HEADS, 8, TILE), F32)]
        + _sm_scratch(nq + 2),
        compiler_params=_params(("arbitrary", "arbitrary", "arbitrary")),
        name="dsa",
    )(qi, ki4, wi, qb, kb, vb_t, bias_near)


def _post_kernel(x_ref, aa_ref, ab_ref, gm_ref, wg_ref, woa_ref, wob_ref, wout_ref,
                 gx_ref, wq_ref, kvm_ref, wox_ref, gmoe_ref, wrt_ref, brt_ref,
                 x2_ref, hm_ref, rw_ref, ids_ref, cnt_ref, carry_ref):
    step = pl.program_id(0)
    xf = x_ref[...]
    h = _rms(xf, gm_ref[...]).astype(BF16)
    gates = jax.nn.sigmoid(_mm(h, wg_ref[...]))
    oa = _mm(aa_ref[...], woa_ref[...])
    ob = _mm(ab_ref[...], wob_ref[...])
    mrg = gates[:, :D_MODEL] * oa + gates[:, D_MODEL:] * ob
    x1 = xf + _mm(mrg.astype(BF16), wout_ref[...])

    hx = _rms(x1, gx_ref[...]).astype(BF16)
    q = (_mm(hx, wq_ref[...]) * (HEAD_DIM ** -0.5)).astype(BF16)
    kvm = kvm_ref[0]
    lane = lax.broadcasted_iota(I32, (TILE, LANES), 1)
    outs = []
    for p_idx in range(X_HEADS // 2):
        lo = p_idx * LANES
        qs = _split_pair(q[:, lo:lo + LANES])
        km = kvm[:, lo:lo + LANES]
        vm = kvm[:, W_X + lo:W_X + lo + LANES]
        o = []
        for hd in range(2):
            s = _nt(qs[hd], km)
            p = jnp.exp(s - jnp.max(s, axis=1, keepdims=True))
            o.append(_mm(p.astype(BF16), vm) / jnp.sum(p, axis=1, keepdims=True))
        outs.append(jnp.where(lane < HEAD_DIM, o[0], o[1]))
    xo = jnp.concatenate(outs, axis=1).astype(BF16)
    x2 = x1 + _mm(xo, wox_ref[...])
    x2_ref[...] = x2

    hm = _rms(x2, gmoe_ref[...])
    hm_ref[...] = hm
    hm_hi = hm.astype(BF16)
    hm_lo = (hm - hm_hi.astype(F32)).astype(BF16)
    hi_part = _mm(hm_hi, wrt_ref[...])
    logits = (hi_part[:, :LANES] + hi_part[:, LANES:]) + _mm(hm_lo, wrt_ref[:, :LANES]) + brt_ref[...]
    big = LANES

    def argmax(v):
        top = jnp.max(v, axis=1, keepdims=True)
        return top, jnp.min(jnp.where(v == top, lane, big), axis=1, keepdims=True)

    is_grp = lane < R_EXP
    gtop, gsel = argmax(jnp.where(is_grp, logits, -jnp.inf))
    gw = 1.0 / jnp.sum(jnp.where(is_grp, jnp.exp(logits - gtop), 0.0), axis=1, keepdims=True)
    first = R_EXP + gsel * EXPERTS_PER_GROUP
    inside = jnp.logical_and(lane >= first, lane < first + EXPERTS_PER_GROUP)
    within = jnp.where(inside, logits, -jnp.inf)
    v0, i0 = argmax(within)
    v1, i1 = argmax(jnp.where(lane == i0, -jnp.inf, within))
    e1 = jnp.exp(v1 - v0)
    w0 = gw * (1.0 / (1.0 + e1))
    w1 = gw * (e1 / (1.0 + e1))
    rw_ref[...] = jnp.where(lane == 0, w0, jnp.where(lane == 1, w1, 0.0))

    @pl.when(step == 0)
    def _():
        carry_ref[...] = jnp.zeros(carry_ref.shape, F32)

    hit0 = lane == i0
    hit1 = lane == i1
    onehot = jnp.where(jnp.logical_or(hit0, hit1), 1.0, 0.0)
    tri = (lax.broadcasted_iota(I32, (TILE, TILE), 1) < lax.broadcasted_iota(I32, (TILE, TILE), 0))
    base = carry_ref[...] + _mm(tri.astype(BF16), onehot.astype(BF16))
    r0 = jnp.sum(jnp.where(hit0, base, 0.0), axis=1, keepdims=True)
    r1 = jnp.sum(jnp.where(hit1, base, 0.0), axis=1, keepdims=True)
    total = carry_ref[...] + jnp.sum(onehot, axis=0, keepdims=True)
    carry_ref[...] = total
    cnt_ref[...] = total.astype(I32)
    slab = jnp.where(lane == 0, (i0 - R_EXP).astype(F32),
                     jnp.where(lane == 1, (i1 - R_EXP).astype(F32),
                               jnp.where(lane == 2, r0, jnp.where(lane == 3, r1, 0.0))))
    ids_ref[0] = slab.T[:8].astype(I32)


def _post(xf, aa, ab, kvm, tiles_per_batch, weights):
    n, d = xf.shape
    nt = n // TILE
    row = lambda w: pl.BlockSpec((TILE, w), lambda i: (i, 0))
    full = lambda a: pl.BlockSpec(a.shape, lambda i: (0,) * a.ndim)
    gm, wg, woa, wob, wout, gx, wq, wox, gmoe, wrt, brt = weights
    return pl.pallas_call(
        _post_kernel,
        grid=(nt,),
        in_specs=[row(d), row(W_A), row(W_B), full(gm), full(wg), full(woa), full(wob), full(wout),
                  full(gx), full(wq),
                  pl.BlockSpec((1,) + kvm.shape[1:], lambda i: (i // tiles_per_batch, 0, 0)),
                  full(wox), full(gmoe), full(wrt), full(brt)],
        out_specs=[row(d), row(d), row(LANES),
                   pl.BlockSpec((1, 8, TILE), lambda i: (i, 0, 0)),
                   pl.BlockSpec((1, LANES), lambda i: (0, 0))],
        out_shape=[jax.ShapeDtypeStruct((n, d), F32), jax.ShapeDtypeStruct((n, d), F32),
                   jax.ShapeDtypeStruct((n, LANES), F32),
                   jax.ShapeDtypeStruct((nt, 8, TILE), I32),
                   jax.ShapeDtypeStruct((1, LANES), I32)],
        scratch_shapes=[pltpu.VMEM((1, LANES), F32)],
        compiler_params=_params(("arbitrary",)),
        name="post",
    )(xf, aa, ab, gm, wg, woa, wob, wout, gx, wq, kvm, wox, gmoe, wrt, brt)


def _segment_starts(cnt_ref, seg_ref):
    def body(e, acc):
        seg_ref[e] = acc
        c = cnt_ref[0, R_EXP + e]
        return acc + ((c + (MOE_BLOCK - 1)) >> 8 << 8)
    return lax.fori_loop(0, N_EXPERTS, body, jnp.int32(0))


def _start_row_copies(ids_ref, seg_ref, rows_ref, tile_ref, sem, gather):
    def issue(t, carry):
        for k in range(2):
            dest = seg_ref[ids_ref[0, k, t]] + ids_ref[0, 2 + k, t]
            if gather:
                cp = pltpu.make_async_copy(rows_ref.at[pl.ds(dest, 1)], tile_ref.at[k, pl.ds(t, 1)], sem)
            else:
                cp = pltpu.make_async_copy(tile_ref.at[pl.ds(t, 1)], rows_ref.at[pl.ds(dest, 1)], sem)
            cp.start(priority=k)
        return carry
    lax.fori_loop(0, TILE, issue, 0, unroll=4)


def _wait_row_copies(rows_ref, tile_ref, sem):
    pltpu.make_async_copy(rows_ref.at[pl.ds(0, TILE)], tile_ref, sem).wait()


def _dispatch_kernel(ids_ref, cnt_ref, hm_ref, xr_in_ref, xr_ref, blk_ref, seg_ref, sem):
    del xr_in_ref
    step = pl.program_id(0)
    nb = blk_ref.shape[1] - 1

    @pl.when(step == 0)
    def _():
        used = _segment_starts(cnt_ref, seg_ref)

        def per_expert(e, last):
            c = cnt_ref[0, R_EXP + e]
            b0 = seg_ref[e] >> 8
            n = (c + (MOE_BLOCK - 1)) >> 8

            def fill(kk, carry):
                blk_ref[0, b0 + kk] = e
                return carry
            lax.fori_loop(0, n, fill, 0)
            return jnp.where(n > 0, e, last)
        last = lax.fori_loop(0, N_EXPERTS, per_expert, jnp.int32(0))

        def tail(b, carry):
            blk_ref[0, b] = last
            return carry
        lax.fori_loop(used >> 8, nb, tail, 0)
        blk_ref[0, nb] = used >> 8

    _start_row_copies(ids_ref, seg_ref, xr_ref, hm_ref, sem, gather=False)
    for _ in range(2):
        _wait_row_copies(xr_ref, hm_ref, sem)


def _dispatch(ids, cnt, hm, n_rows):
    n, d = hm.shape
    nt = n // TILE
    nb = n_rows // MOE_BLOCK
    zeros = jnp.zeros((n_rows, d), F32)
    return pl.pallas_call(
        _dispatch_kernel,
        grid=(nt,),
        in_specs=[pl.BlockSpec((1, 8, TILE), lambda i: (i, 0, 0), memory_space=pltpu.SMEM),
                  pl.BlockSpec(memory_space=pltpu.SMEM),
                  pl.BlockSpec((TILE, d), lambda i: (i, 0)),
                  pl.BlockSpec(memory_space=pl.ANY)],
        out_specs=[pl.BlockSpec(memory_space=pl.ANY),
                   pl.BlockSpec(memory_space=pltpu.SMEM)],
        out_shape=[jax.ShapeDtypeStruct((n_rows, d), F32),
                   jax.ShapeDtypeStruct((1, nb + 1), I32)],
        scratch_shapes=[pltpu.SMEM((N_EXPERTS,), I32), pltpu.SemaphoreType.DMA(())],
        input_output_aliases={3: 0},
        compiler_params=_params(("arbitrary",)),
        name="dispatch",
    )(ids, cnt, hm, zeros)


def _experts_kernel(blk_ref, x_ref, w1_ref, w3_ref, w2_ref, y_ref, w1b, w3b, w2b):
    b = pl.program_id(0)
    nb = pl.num_programs(0)
    e = blk_ref[0, b]
    prev = blk_ref[0, jnp.maximum(b - 1, 0)]

    @pl.when(jnp.logical_or(b == 0, e != prev))
    def _():
        w1b[...] = w1_ref[0].astype(BF16)
        w3b[...] = w3_ref[0].astype(BF16)
        w2b[...] = w2_ref[0].astype(BF16)

    used = blk_ref[0, nb]

    @pl.when(b < used)
    def _():
        xb = x_ref[...].astype(BF16)
        a = _mm(xb, w1b[...])
        g = _mm(xb, w3b[...])
        y_ref[...] = _mm((a * jax.nn.sigmoid(a) * g).astype(BF16), w2b[...])

    @pl.when(b >= used)
    def _():
        y_ref[...] = jnp.zeros(y_ref.shape, F32)


def _experts(blk, xr, w1, w3, w2):
    n_rows, d = xr.shape
    nb = n_rows // MOE_BLOCK
    wspec = lambda s: pl.BlockSpec((1,) + s, lambda b, blk: (blk[0, b], 0, 0))
    return pl.pallas_call(
        _experts_kernel,
        grid_spec=pltpu.PrefetchScalarGridSpec(
            num_scalar_prefetch=1,
            grid=(nb,),
            in_specs=[pl.BlockSpec((MOE_BLOCK, d), lambda b, blk: (b, 0)),
                      wspec((d, D_EXPERT)), wspec((d, D_EXPERT)), wspec((D_EXPERT, d))],
            out_specs=pl.BlockSpec((MOE_BLOCK, d), lambda b, blk: (b, 0)),
            scratch_shapes=[pltpu.VMEM((d, D_EXPERT), BF16), pltpu.VMEM((d, D_EXPERT), BF16),
                            pltpu.VMEM((D_EXPERT, d), BF16)]),
        out_shape=jax.ShapeDtypeStruct((n_rows, d), F32),
        compiler_params=_params(("arbitrary",)),
        name="experts",
    )(blk, xr, w1, w3, w2)


def _combine_kernel(ids_ref, ids_next_ref, cnt_ref, x2_ref, rw_ref, g_ref, yr_ref, o_ref, y_ref, seg_ref, sem):
    step = pl.program_id(0)
    slot = step % 2

    @pl.when(step == 0)
    def _():
        _segment_starts(cnt_ref, seg_ref)
        _start_row_copies(ids_ref, seg_ref, yr_ref, y_ref.at[0], sem.at[0], gather=True)

    @pl.when(step + 1 < pl.num_programs(0))
    def _():
        _start_row_copies(ids_next_ref, seg_ref, yr_ref, y_ref.at[1 - slot], sem.at[1 - slot], gather=True)

    for k in range(2):
        _wait_row_copies(yr_ref, y_ref.at[slot, k], sem.at[slot])
    rw = rw_ref[...]
    x3 = x2_ref[...] + (y_ref[slot, 0] * rw[:, 0:1] + y_ref[slot, 1] * rw[:, 1:2])
    o_ref[...] = _rms(x3, g_ref[...])


def _combine(ids, cnt, x2, rw, g, yr):
    n, d = x2.shape
    nt = n // TILE
    row = lambda w: pl.BlockSpec((TILE, w), lambda i: (i, 0))
    return pl.pallas_call(
        _combine_kernel,
        grid=(nt,),
        in_specs=[pl.BlockSpec((1, 8, TILE), lambda i: (i, 0, 0), memory_space=pltpu.SMEM),
                  pl.BlockSpec((1, 8, TILE), lambda i: (jnp.minimum(i + 1, nt - 1), 0, 0),
                               memory_space=pltpu.SMEM),
                  pl.BlockSpec(memory_space=pltpu.SMEM),
                  row(d), row(LANES), pl.BlockSpec((1, d), lambda i: (0, 0)),
                  pl.BlockSpec(memory_space=pl.ANY)],
        out_specs=row(d),
        out_shape=jax.ShapeDtypeStruct((n, d), F32),
        scratch_shapes=[pltpu.VMEM((2, 2, TILE, d), F32), pltpu.SMEM((N_EXPERTS,), I32),
                        pltpu.SemaphoreType.DMA((2,))],
        compiler_params=_params(("arbitrary",)),
        name="combine",
    )(ids, ids, cnt, x2, rw, g, yr)


def _t5_bucket(dist):
    n = jnp.maximum(dist, 0)
    max_exact = REL_BUCKETS // 2
    nf = jnp.maximum(n, 1).astype(F32)
    large = max_exact + (jnp.log(nf / max_exact) / math.log(REL_MAX_DIST / max_exact)
                         * (REL_BUCKETS - max_exact)).astype(I32)
    large = jnp.minimum(large, REL_BUCKETS - 1)
    return jnp.where(n < max_exact, n, large)


def _bias_tables(tab):
    r = jnp.arange(TILE)[:, None]
    c = jnp.arange(TILE)[None, :]
    rel = tab - tab[:, REL_BUCKETS - 1:]

    def tile(dist):
        onehot = (_t5_bucket(dist)[..., None] == jnp.arange(REL_BUCKETS)).astype(F32)
        return jnp.einsum('rcb,hb->hrc', onehot, rel, precision=lax.Precision.HIGHEST)

    own = jnp.where(r >= c, tile(r - c), NEG)
    prev = tile(r - c + TILE)
    near = jnp.stack([jnp.zeros_like(prev), prev, own], axis=1).astype(F32)
    return near.swapaxes(-1, -2)


def _block_mean_slabs(kmean, bn, nq):
    km = kmean.reshape(bn, nq, A_HEADS, HEAD_DIM).transpose(0, 2, 1, 3)
    pad = ((0, 0), (0, 0), (HEAD_DIM, LANES - HEAD_DIM - nq), (0, LANES - HEAD_DIM))
    return jnp.pad(km, pad).astype(BF16)


def _pad_cols(w, width):
    return jnp.pad(w, ((0, 0), (0, width - w.shape[1])))


def kernel(x, mem, rel_bias, final_norm, norm_mix, w_in, ckv_norm, w_uk, w_uv, w_oa, w_ob,
           w_out, norm_x, mem_norm, wq_x, wk_x, wv_x, wo_x, norm_moe, w_group, b_group,
           w_router, b_router, w1, w3, w2):
    bn, t, d = x.shape
    n = bn * t
    nq = t // TILE
    assert t % TILE == 0 and nq <= LANES - HEAD_DIM and norm_mix.shape[0] == 1
    assert nq % FAR_GROUP == 0
    near_a = _bias_tables(rel_bias[:, :A_HEADS].T)
    near_b = _bias_tables(rel_bias[:, A_HEADS:].T)
    row = lambda v: v.reshape(1, -1).astype(F32)

    wi = w_in[0]
    wp = jnp.concatenate(
        [wi[:, OFF_AQ:OFF_IK], jnp.tile(wi[:, OFF_IK:OFF_IW], (1, LANES // IDX_DIM)),
         _pad_cols(wi[:, OFF_IW:OFF_GA], LANES)], axis=1).astype(BF16)
    wkv = jnp.concatenate([w_uk[0], w_uv[0]], axis=1).astype(BF16)
    xf = x.reshape(n, d)
    qa, ka, va, qb, kb, vb, qi, ki4, widx, kmean = _proj_in(
        xf, row(norm_mix[0]), wp, row(ckv_norm[0]), wkv, nq)

    b3 = lambda a: a.reshape(bn, t, a.shape[-1])
    attn_a = _moba(b3(qa), b3(ka), va.reshape(bn, nq, W_A, TILE), _block_mean_slabs(kmean, bn, nq), near_a)
    attn_b = _dsa(b3(qi), b3(ki4), b3(widx), b3(qb), b3(kb), vb.reshape(bn, nq, W_B, TILE), near_b)

    kvm = _mem_kv(mem, row(mem_norm[0]), jnp.concatenate([wk_x[0], wv_x[0]], axis=1).astype(BF16))
    wrt = _pad_cols(jnp.concatenate([w_group[0], w_router[0]], axis=1), LANES).astype(F32)
    wrt_hi = wrt.astype(BF16)
    wrt = jnp.concatenate([wrt_hi, (wrt - wrt_hi.astype(F32)).astype(BF16)], axis=1)
    brt = _pad_cols(jnp.concatenate([b_group[0], b_router[0]]).reshape(1, -1), LANES).astype(F32)
    weights = (row(norm_mix[0]), wi[:, OFF_GA:IN_COLS].astype(BF16), w_oa[0].astype(BF16),
               w_ob[0].astype(BF16), w_out[0].astype(BF16), row(norm_x[0]), wq_x[0].astype(BF16),
               wo_x[0].astype(BF16), row(norm_moe[0]), wrt, brt)
    x2, hm, rw, ids, cnt = _post(xf, attn_a.reshape(n, W_A), attn_b.reshape(n, W_B), kvm, t // TILE, weights)

    n_rows = 2 * n + N_EXPERTS * MOE_BLOCK
    xr, blk = _dispatch(ids, cnt, hm, n_rows)
    yr = _experts(blk, xr, w1[0], w3[0], w2[0])
    out = _combine(ids, cnt, x2, rw, row(final_norm), yr)
    return out.reshape(bn, t, d)
```

```python
import functools
import math

import jax
import jax.numpy as jnp
from jax import lax
from jax.experimental import pallas as pl
from jax.experimental.pallas import tpu as pltpu

D_MODEL = 1024
HEAD_DIM = 64
A_HEADS = 8
B_HEADS = 8
W_A = A_HEADS * HEAD_DIM
W_B = B_HEADS * HEAD_DIM
MOBA_BLOCK = 256
MOBA_TOPK = 3
DSA_KV_RANK = 256
IDX_HEADS = 8
IDX_DIM = 32
DSA_TOPK = 256
REL_BUCKETS = 32
REL_MAX_DIST = 128
X_HEADS = 4
W_X = X_HEADS * HEAD_DIM
N_GROUPS = 4
EXPERTS_PER_GROUP = 8
N_EXPERTS = N_GROUPS * EXPERTS_PER_GROUP
D_EXPERT = 512
MOE_BLOCK = 256
EPS = 1e-6
NEG = -1e30

OFF_AQ = 0
OFF_AK = OFF_AQ + W_A
OFF_AV = OFF_AK + W_A
OFF_BQ = OFF_AV + W_A
OFF_CKV = OFF_BQ + W_B
OFF_IQ = OFF_CKV + DSA_KV_RANK
OFF_IK = OFF_IQ + IDX_HEADS * IDX_DIM
OFF_IW = OFF_IK + IDX_DIM
OFF_GA = OFF_IW + IDX_HEADS
OFF_GB = OFF_GA + D_MODEL
IN_COLS = OFF_GB + D_MODEL

LANES = 128
TILE = 256
VMEM_LIMIT = 56 * 1024 * 1024
BF16 = jnp.bfloat16
F32 = jnp.float32
I32 = jnp.int32

P_QA, P_KA, P_VA, P_QB = 0, 512, 1024, 1536
P_CKV, P_IQ, P_IK4, P_IW = 2048, 2304, 2560, 2688
P_COLS = 2816

R_GRP = 0
R_EXP = N_GROUPS


def _nt(a, b):
    return lax.dot_general(a, b, (((1,), (1,)), ((), ())), preferred_element_type=F32)


def _mm(a, b):
    return jnp.dot(a, b, preferred_element_type=F32)


def _rms(xf, g):
    return xf * lax.rsqrt(jnp.mean(xf * xf, axis=-1, keepdims=True) + EPS) * g


def _params(sem, vmem=VMEM_LIMIT):
    return pltpu.CompilerParams(dimension_semantics=sem, vmem_limit_bytes=vmem)


def _mem_kv_kernel(mem_ref, g_ref, w_ref, o_ref):
    m = _rms(mem_ref[0], g_ref[...]).astype(BF16)
    o_ref[0] = _mm(m, w_ref[...]).astype(BF16)


def _mem_kv(mem, g, wkv):
    bn, ml, d = mem.shape
    return pl.pallas_call(
        _mem_kv_kernel,
        grid=(bn,),
        in_specs=[pl.BlockSpec((1, ml, d), lambda b: (b, 0, 0)),
                  pl.BlockSpec((1, d), lambda b: (0, 0)),
                  pl.BlockSpec((d, 2 * W_X), lambda b: (0, 0))],
        out_specs=pl.BlockSpec((1, ml, 2 * W_X), lambda b: (b, 0, 0)),
        out_shape=jax.ShapeDtypeStruct((bn, ml, 2 * W_X), BF16),
        compiler_params=_params(("arbitrary",)),
        name="mem_kv",
    )(mem, g, wkv)


def _head_slabs(zp, extra):
    lane = lax.broadcasted_iota(I32, zp.shape, 1)
    low = lane < HEAD_DIM
    return jnp.where(low, zp, extra), jnp.where(low, pltpu.roll(zp, HEAD_DIM, 1), extra)


def _proj_in_kernel(blocks_per_seq, x_ref, g_ref, w_ref, cg_ref, wkv_ref,
                    qa_ref, ka_ref, va_ref, qb_ref, kb_ref, vb_ref, qi_ref, ki_ref, wi_ref, km_ref):
    h = _rms(x_ref[...], g_ref[...]).astype(BF16)
    z = _mm(h, w_ref[...])
    scale = HEAD_DIM ** -0.5
    zk = z[:, P_KA:P_VA]
    km_ref[0] = jnp.mean(zk, axis=0, keepdims=True)
    lane = lax.broadcasted_iota(I32, (TILE, LANES), 1)
    blk = pl.program_id(0) % blocks_per_seq
    onehot = jnp.where(lane == HEAD_DIM + blk, 1.0, 0.0)
    zero = jnp.zeros((TILE, LANES), F32)
    for p_idx in range(A_HEADS // 2):
        lo = p_idx * LANES
        qe, qo = _head_slabs(z[:, P_QA + lo:P_QA + lo + LANES] * scale, zero)
        ke, ko = _head_slabs(zk[:, lo:lo + LANES], onehot)
        qa_ref[:, 2 * lo:2 * lo + 2 * LANES] = jnp.concatenate([qe, qo], axis=1).astype(BF16)
        ka_ref[:, 2 * lo:2 * lo + 2 * LANES] = jnp.concatenate([ke, ko], axis=1).astype(BF16)
    va_ref[0] = z[:, P_VA:P_QB].T.astype(BF16)
    qb_ref[...] = (z[:, P_QB:P_CKV] * scale).astype(BF16)
    ckv = _rms(z[:, P_CKV:P_IQ], cg_ref[...]).astype(BF16)
    kv = _mm(ckv, wkv_ref[...])
    kb_ref[...] = kv[:, :W_B].astype(BF16)
    vb_ref[0] = kv[:, W_B:].T.astype(BF16)
    qi_ref[...] = z[:, P_IQ:P_IK4].astype(BF16)
    ki_ref[...] = z[:, P_IK4:P_IW].astype(BF16)
    wi_ref[...] = z[:, P_IW:P_COLS] * ((IDX_HEADS ** -0.5) * (IDX_DIM ** -0.5))


def _proj_in(xf, g, wp, cg, wkv, blocks_per_seq):
    n, d = xf.shape
    nt = n // TILE
    row = lambda w: pl.BlockSpec((TILE, w), lambda i: (i, 0))
    full = lambda a: pl.BlockSpec(a.shape, lambda i: (0,) * a.ndim)
    outs = [(2 * W_A, BF16, False), (2 * W_A, BF16, False), (W_A, BF16, True), (W_B, BF16, False),
            (W_B, BF16, False), (W_B, BF16, True), (IDX_HEADS * IDX_DIM, BF16, False), (LANES, BF16, False),
            (LANES, F32, False)]
    blocked = lambda w: pl.BlockSpec((1, w, TILE), lambda i: (i, 0, 0))
    return pl.pallas_call(
        functools.partial(_proj_in_kernel, blocks_per_seq),
        grid=(nt,),
        in_specs=[row(d), full(g), full(wp), full(cg), full(wkv)],
        out_specs=[blocked(w) if tr else row(w) for w, _, tr in outs]
        + [pl.BlockSpec((1, 1, W_A), lambda i: (i, 0, 0))],
        out_shape=[jax.ShapeDtypeStruct((nt, w, TILE) if tr else (n, w), t) for w, t, tr in outs]
        + [jax.ShapeDtypeStruct((nt, 1, W_A), F32)],
        compiler_params=_params(("arbitrary",)),
        name="proj_in",
    )(xf, g, wp, cg, wkv)


SUB = TILE // 8
LOG2E = math.log2(math.e)
FAR_GROUP = 8


def _sm_init(mx_ref, acc_ref):
    mx_ref[...] = jnp.full(mx_ref.shape, NEG, F32)
    acc_ref[...] = jnp.zeros(acc_ref.shape, F32)


def _sm_logits(hd, slot, s, sc_ref, mx_ref):
    s = s * LOG2E
    sc_ref[hd, slot] = s
    mx_ref[hd] = jnp.maximum(mx_ref[hd], jnp.max(s.reshape(SUB, 8, TILE), axis=0))


def _sm_rowmax(mx_ref):
    for hd in range(2):
        mx_ref[hd] = jnp.broadcast_to(jnp.max(mx_ref[hd], axis=0, keepdims=True), (8, TILE))


def _sm_probs(hd, slots, sc_ref, mx_ref):
    m = mx_ref[hd][None]
    ps = [jnp.exp2((sc_ref[hd, sl].reshape(SUB, 8, TILE) - m).reshape(TILE, TILE).astype(BF16))
          for sl in slots]
    return ps[0] if len(ps) == 1 else jnp.concatenate(ps, axis=0)


def _sm_accumulate(hd, pb, v, acc_ref):
    row = lax.broadcasted_iota(I32, v.shape, 0)
    own = (row < HEAD_DIM) if hd == 0 else (row >= HEAD_DIM)
    acc_ref[hd] = acc_ref[hd] + _mm(jnp.where(own, v, jnp.ones_like(v)), pb)


def _sm_values(hd, slots, v, sc_ref, mx_ref, acc_ref):
    _sm_accumulate(hd, _sm_probs(hd, slots, sc_ref, mx_ref), v, acc_ref)


def _far_plan(n_far):
    full = n_far // FAR_GROUP
    rest = n_far - full * FAR_GROUP
    half = jnp.logical_and(rest > 0, rest <= FAR_GROUP // 2)
    return full + (rest > FAR_GROUP // 2).astype(I32), half, full * FAR_GROUP


def _sm_far_loop(plan, group_fn):
    trips, half, half_base = plan
    lax.fori_loop(0, trips, lambda g, c: group_fn(FAR_GROUP * g, FAR_GROUP) or c, 0)
    pl.when(half)(lambda: group_fn(half_base, FAR_GROUP // 2))


def _sm_far_values(plan, vals, sc_ref, mx_ref, acc_ref):
    def group(j0, n):
        v = vals(j0, n)
        for hd in range(2):
            _sm_values(hd, [j0 + u for u in range(n)], v, sc_ref, mx_ref, acc_ref)

    _sm_far_loop(plan, group)


def _sm_output(acc_ref):
    num = jnp.concatenate([acc_ref[0, :HEAD_DIM], acc_ref[1, HEAD_DIM:]], axis=0)
    den = jnp.concatenate([acc_ref[0, HEAD_DIM:], acc_ref[1, :HEAD_DIM]], axis=0)
    return (num / den).T


def _split_pair(qp):
    lane = lax.broadcasted_iota(I32, qp.shape, 1)
    zero = jnp.zeros_like(qp)
    return jnp.where(lane < HEAD_DIM, qp, zero), jnp.where(lane >= HEAD_DIM, qp, zero)


def _moba_kernel(q_ref, k_ref, v_ref, km_ref, bias_ref, o_ref, sc_ref, mx_ref, acc_ref):
    i = pl.program_id(2)
    nq = pl.num_programs(2)
    n_slot = LANES - HEAD_DIM
    blk = lax.broadcasted_iota(I32, (n_slot, TILE), 0)
    in_blk = blk < nq
    past = blk < i
    no_mask = jnp.zeros((HEAD_DIM, TILE), F32)

    qm = []
    for hd in range(2):
        q = q_ref[0, :, hd * LANES:(hd + 1) * LANES]
        g = jnp.where(past, _nt(km_ref[0, hd], q)[HEAD_DIM:], -jnp.inf)
        sel = jnp.zeros((n_slot, TILE), jnp.bool_)
        for _ in range(MOBA_TOPK):
            top = jnp.max(g, axis=0, keepdims=True)
            hit = jnp.logical_and(g == top, in_blk)
            first = jnp.min(jnp.where(hit, blk, n_slot), axis=0, keepdims=True)
            pick = blk == first
            sel = jnp.logical_or(sel, jnp.logical_and(pick, past))
            g = jnp.where(pick, -jnp.inf, g)
        addm = jnp.where(sel, 0.0, NEG)
        mask = jnp.where(past, addm, jnp.where(jnp.logical_and(in_blk, blk > i), NEG, 0.0))
        qm.append(q + jnp.concatenate([no_mask, mask], axis=0).T.astype(BF16))

    def keys(j, hd):
        off = pl.multiple_of(j * TILE, TILE)
        return k_ref[0, pl.ds(off, TILE), hd * LANES:(hd + 1) * LANES]

    plan = _far_plan(i + 1)
    _sm_init(mx_ref, acc_ref)

    def logits(j0, n):
        for u in range(n):
            j = j0 + u
            tile = _near_bias_index(j, i)
            for hd in range(2):
                _sm_logits(hd, j, _nt(keys(j, hd), qm[hd]) + bias_ref[hd, tile], sc_ref, mx_ref)

    _sm_far_loop(plan, logits)
    _sm_rowmax(mx_ref)

    def vals(j, n):
        return jnp.concatenate([v_ref[0, j + u] for u in range(n)], axis=1)

    _sm_far_values(plan, vals, sc_ref, mx_ref, acc_ref)
    o_ref[0] = _sm_output(acc_ref).astype(BF16)


def _near_bias_index(j, i):
    return jnp.clip(j - i + 2, 0, 2)


def _sm_scratch(n_slots):
    return [pltpu.VMEM((2, n_slots, TILE, TILE), F32),
            pltpu.VMEM((2, 8, TILE), F32),
            pltpu.VMEM((2, LANES, TILE), F32)]


def _moba(qa, ka, va_t, kmp, bias_near):
    bn, nq = va_t.shape[:2]
    t = nq * TILE
    npair = A_HEADS // 2
    return pl.pallas_call(
        _moba_kernel,
        grid=(bn, npair, nq),
        in_specs=[pl.BlockSpec((1, TILE, 2 * LANES), lambda b, p, i: (b, i, p)),
                  pl.BlockSpec((1, t, 2 * LANES), lambda b, p, i: (b, 0, p)),
                  pl.BlockSpec((1, nq, LANES, TILE), lambda b, p, i: (b, 0, p, 0)),
                  pl.BlockSpec((1, 2, LANES, LANES), lambda b, p, i: (b, p, 0, 0)),
                  pl.BlockSpec((2, 3, TILE, TILE), lambda b, p, i: (p, 0, 0, 0))],
        out_specs=pl.BlockSpec((1, TILE, LANES), lambda b, p, i: (b, i, p)),
        out_shape=jax.ShapeDtypeStruct((bn, t, W_A), BF16),
        scratch_shapes=_sm_scratch(nq),
        compiler_params=_params(("arbitrary", "arbitrary", "arbitrary")),
        name="moba",
    )(qa, ka, va_t, kmp, bias_near)


def _dsa_select(i, qi_ref, ki_ref, wi_ref, s_ref, wb_ref):
    nch = i + 1
    sub = TILE // 8
    lane = lax.broadcasted_iota(I32, (TILE, LANES), 1)
    key_in = (lax.broadcasted_iota(I32, (sub, 8, TILE), 0) * 8 + lax.broadcasted_iota(I32, (sub, 8, TILE), 1))
    qpos = i * TILE + lax.broadcasted_iota(I32, (sub, 8, TILE), 2)

    def rows8(x):
        return jnp.broadcast_to(x, (8, TILE))[None]

    w_t = wi_ref[0].T
    qi = qi_ref[0]
    per_group = LANES // IDX_DIM
    qh = []
    for h in range(IDX_HEADS):
        g, r = divmod(h, per_group)
        qg = qi[:, g * LANES:(g + 1) * LANES]
        keep = jnp.logical_and(lane >= r * IDX_DIM, lane < (r + 1) * IDX_DIM)
        qh.append(jnp.where(keep, qg, jnp.zeros_like(qg)))
        wb_ref[h] = jnp.broadcast_to(w_t[h:h + 1], (8, TILE))

    def score(c, carry):
        off = pl.multiple_of(c * TILE, TILE)
        kc = ki_ref[0, pl.ds(off, TILE), :]
        acc = jnp.zeros((sub, 8, TILE), F32)
        for h in range(IDX_HEADS):
            acc = acc + jnp.maximum(_nt(kc, qh[h]), 0.0).reshape(sub, 8, TILE) * wb_ref[h][None]
        s_ref[c] = jnp.where(c * TILE + key_in <= qpos, acc, -jnp.inf).reshape(TILE, TILE)
        return carry

    n_pair = (nch + 1) // 2

    def pairs(step, init):
        return lax.fori_loop(0, n_pair, lambda g, acc: step(2 * g + 1, step(2 * g, acc)), init)

    pairs(score, 0)

    def fold(init, step):
        return pairs(lambda c, acc: step(acc, s_ref[c].reshape(sub, 8, TILE), c), init)

    def count(pred):
        def step(acc, blk, c):
            return acc + jnp.sum(jnp.where(pred(blk, c * TILE + key_in), 1.0, 0.0), axis=0)
        return jnp.sum(fold(jnp.zeros((8, TILE), F32), step), axis=0, keepdims=True)

    wide = rows8

    def stats(acc, blk, c):
        top, low, pos, nonneg = acc
        top = jnp.maximum(top, jnp.max(blk, axis=0))
        low = jnp.minimum(low, jnp.min(jnp.where(blk == -jnp.inf, jnp.inf, blk), axis=0))
        pos = pos + jnp.sum(jnp.where(blk > 0.0, 1.0, 0.0), axis=0)
        nonneg = nonneg + jnp.sum(jnp.where(blk >= 0.0, 1.0, 0.0), axis=0)
        return top, low, pos, nonneg

    zeros8 = jnp.zeros((8, TILE), F32)
    acc = fold((zeros8 - jnp.inf, zeros8 + jnp.inf, zeros8, zeros8), stats)
    hi = jnp.max(acc[0], axis=0, keepdims=True)
    lo = jnp.min(acc[1], axis=0, keepdims=True)
    n_pos = jnp.sum(acc[2], axis=0, keepdims=True)
    n_nonneg = jnp.sum(acc[3], axis=0, keepdims=True)

    n_lo = (i * TILE + 1 + lax.broadcasted_iota(I32, (1, TILE), 1)).astype(F32)
    hi = hi + jnp.maximum(jnp.abs(hi) * 2.0 ** -20, 1e-30)
    n_hi = jnp.zeros((1, TILE), F32)
    at_zero = jnp.logical_and(n_pos < DSA_TOPK, n_nonneg >= DSA_TOPK)
    from_zero = n_pos >= DSA_TOPK
    lo, n_lo = jnp.where(from_zero, 0.0, lo), jnp.where(from_zero, n_nonneg, n_lo)
    to_zero = n_nonneg < DSA_TOPK
    hi, n_hi = jnp.where(to_zero, 0.0, hi), jnp.where(to_zero, n_nonneg, n_hi)

    def search_more(state):
        return jnp.logical_and(jnp.min(state[7]) == 0, state[8] < 400)

    def search(state):
        lo, hi, n_lo, n_hi, thr, n_ge, n_gt, done, it = state
        mid = 0.5 * lo + 0.5 * hi
        closed = jnp.logical_or(mid <= lo, mid >= hi)
        n = count(lambda half, kpos: half >= wide(mid))
        stop = jnp.logical_or(closed, n == DSA_TOPK)
        settle = jnp.logical_and(done == 0, stop)
        thr = jnp.where(settle, jnp.where(closed, lo, mid), thr)
        n_ge = jnp.where(settle, jnp.where(closed, n_lo, n), n_ge)
        n_gt = jnp.where(settle, jnp.where(closed, n_hi, 0.0), n_gt)
        go = jnp.logical_and(done == 0, jnp.logical_not(stop))
        up = jnp.logical_and(go, n > DSA_TOPK)
        down = jnp.logical_and(go, n < DSA_TOPK)
        return (jnp.where(up, mid, lo), jnp.where(down, mid, hi), jnp.where(up, n, n_lo),
                jnp.where(down, n, n_hi), thr, n_ge, n_gt, jnp.where(stop, 1, done), it + 1)

    state = (lo, hi, n_lo, n_hi, jnp.zeros((1, TILE), F32), n_nonneg, n_pos, at_zero.astype(I32), jnp.int32(0))
    state = lax.while_loop(search_more, lambda st: search(search(st)), state)
    thr, n_ge, n_gt = wide(state[4]), state[5], state[6]
    need = DSA_TOPK - n_gt

    cut_bits = max(1, (s_ref.shape[0] * TILE - 1).bit_length())

    def cut_step(b, cut):
        cand = cut | (jnp.int32(1) << (cut_bits - 1 - b))
        cw = wide(cand)
        n = count(lambda half, kpos: jnp.logical_and(half == thr, kpos < cw))
        return jnp.where(n < need, cand, cut)

    cut = lax.cond(jnp.max(n_ge) > DSA_TOPK,
                   lambda: lax.fori_loop(0, cut_bits, cut_step, jnp.zeros((1, TILE), I32)),
                   lambda: jnp.full((1, TILE), 2 ** 30, I32))
    cut = wide(jnp.where(n_ge > DSA_TOPK, cut, 2 ** 30))

    def to_mask(c, carry):
        blk = s_ref[c].reshape(sub, 8, TILE)
        kpos = c * TILE + key_in
        keep = jnp.logical_or(blk > thr, jnp.logical_and(blk == thr, kpos <= cut))
        s_ref[c] = jnp.where(jnp.logical_and(keep, kpos <= qpos), 0.0, NEG).reshape(TILE, TILE)
        return carry

    pairs(to_mask, 0)


def _dsa_kernel(qi_ref, ki_ref, wi_ref, q_ref, k_ref, v_ref, bias_ref, o_ref,
                s_ref, wb_ref, sc_ref, mx_ref, acc_ref):
    i = pl.program_id(1)
    pair = pl.program_id(2)

    @pl.when(jnp.logical_and(pair == 0, i == 0))
    def _():
        key = lax.broadcasted_iota(I32, (TILE, TILE), 0)
        query = lax.broadcasted_iota(I32, (TILE, TILE), 1)
        s_ref[0] = jnp.where(key <= query, 0.0, NEG)

    @pl.when(jnp.logical_and(pair == 0, i >= 1))
    def _():
        _dsa_select(i, qi_ref, ki_ref, wi_ref, s_ref, wb_ref)

    qs = _split_pair(q_ref[0])
    plan = _far_plan(jnp.maximum(i - 1, 0))
    _sm_init(mx_ref, acc_ref)

    def vals(j, n):
        return jnp.concatenate([v_ref[0, j + u] for u in range(n)], axis=1)

    def logits(j, slot, msk, near=None):
        k_j = k_ref[0, pl.ds(pl.multiple_of(j * TILE, TILE), TILE), :]
        for hd in range(2):
            s = _nt(k_j, qs[hd]) + msk
            if near is not None:
                s = s + bias_ref[hd, near]
            _sm_logits(hd, slot, s, sc_ref, mx_ref)

    def far_logits(j0, n):
        for u in range(n):
            j = j0 + u
            logits(j, j, jnp.where(j < i - 1, s_ref[jnp.minimum(j, i)], NEG))

    _sm_far_loop(plan, far_logits)
    nq = pl.num_programs(1)
    j_prev = jnp.maximum(i - 1, 0)
    logits(j_prev, nq, jnp.where(i >= 1, s_ref[j_prev], NEG), near=1)
    logits(i, nq + 1, s_ref[i], near=2)
    _sm_rowmax(mx_ref)

    _sm_far_values(plan, vals, sc_ref, mx_ref, acc_ref)
    v_near = jnp.concatenate([vals(j_prev, 1), vals(i, 1)], axis=1)
    for hd in range(2):
        _sm_values(hd, [nq, nq + 1], v_near, sc_ref, mx_ref, acc_ref)
    o_ref[0] = _sm_output(acc_ref).astype(BF16)


def _dsa(qi, ki4, wi, qb, kb, vb_t, bias_near):
    bn, t, _ = qb.shape
    nq = t // TILE
    assert nq % 2 == 0
    npair = B_HEADS // 2
    qrow = lambda w: pl.BlockSpec((1, TILE, w), lambda b, i, p: (b, i, 0))
    return pl.pallas_call(
        _dsa_kernel,
        grid=(bn, nq, npair),
        in_specs=[qrow(IDX_HEADS * IDX_DIM),
                  pl.BlockSpec((1, t, LANES), lambda b, i, p: (b, 0, 0), pipeline_mode=pl.Buffered(1)),
                  qrow(LANES),
                  pl.BlockSpec((1, TILE, LANES), lambda b, i, p: (b, i, p)),
                  pl.BlockSpec((1, t, LANES), lambda b, i, p: (b, 0, p)),
                  pl.BlockSpec((1, nq, LANES, TILE), lambda b, i, p: (b, 0, p, 0)),
                  pl.BlockSpec((2, 3, TILE, TILE), lambda b, i, p: (p, 0, 0, 0))],
        out_specs=pl.BlockSpec((1, TILE, LANES), lambda b, i, p: (b, i, p)),
        out_shape=jax.ShapeDtypeStruct((bn, t, W_B), BF16),
        scratch_shapes=[pltpu.VMEM((nq, TILE, TILE), F32), pltpu.VMEM((IDX_HEADS, 8, TILE), F32)]
        + _sm_scratch(nq + 2),
        compiler_params=_params(("arbitrary", "arbitrary", "arbitrary")),
        name="dsa",
    )(qi, ki4, wi, qb, kb, vb_t, bias_near)


def _post_kernel(x_ref, aa_ref, ab_ref, gm_ref, wg_ref, woa_ref, wob_ref, wout_ref,
                 gx_ref, wq_ref, kvm_ref, wox_ref, gmoe_ref, wrt_ref, brt_ref,
                 x2_ref, hm_ref, rw_ref, ids_ref, cnt_ref, carry_ref):
    step = pl.program_id(0)
    xf = x_ref[...]
    h = _rms(xf, gm_ref[...]).astype(BF16)
    gates = jax.nn.sigmoid(_mm(h, wg_ref[...]))
    oa = _mm(aa_ref[...], woa_ref[...])
    ob = _mm(ab_ref[...], wob_ref[...])
    mrg = gates[:, :D_MODEL] * oa + gates[:, D_MODEL:] * ob
    x1 = xf + _mm(mrg.astype(BF16), wout_ref[...])

    hx = _rms(x1, gx_ref[...]).astype(BF16)
    q = (_mm(hx, wq_ref[...]) * (HEAD_DIM ** -0.5)).astype(BF16)
    kvm = kvm_ref[0]
    lane = lax.broadcasted_iota(I32, (TILE, LANES), 1)
    outs = []
    for p_idx in range(X_HEADS // 2):
        lo = p_idx * LANES
        qs = _split_pair(q[:, lo:lo + LANES])
        km = kvm[:, lo:lo + LANES]
        vm = kvm[:, W_X + lo:W_X + lo + LANES]
        o = []
        for hd in range(2):
            s = _nt(qs[hd], km)
            p = jnp.exp(s - jnp.max(s, axis=1, keepdims=True))
            o.append(_mm(p.astype(BF16), vm) / jnp.sum(p, axis=1, keepdims=True))
        outs.append(jnp.where(lane < HEAD_DIM, o[0], o[1]))
    xo = jnp.concatenate(outs, axis=1).astype(BF16)
    x2 = x1 + _mm(xo, wox_ref[...])
    x2_ref[...] = x2

    hm = _rms(x2, gmoe_ref[...])
    hm_ref[...] = hm
    hm_hi = hm.astype(BF16)
    hm_lo = (hm - hm_hi.astype(F32)).astype(BF16)
    hi_part = _mm(hm_hi, wrt_ref[...])
    logits = (hi_part[:, :LANES] + hi_part[:, LANES:]) + _mm(hm_lo, wrt_ref[:, :LANES]) + brt_ref[...]
    big = LANES

    def argmax(v):
        top = jnp.max(v, axis=1, keepdims=True)
        return top, jnp.min(jnp.where(v == top, lane, big), axis=1, keepdims=True)

    is_grp = lane < R_EXP
    gtop, gsel = argmax(jnp.where(is_grp, logits, -jnp.inf))
    gw = 1.0 / jnp.sum(jnp.where(is_grp, jnp.exp(logits - gtop), 0.0), axis=1, keepdims=True)
    first = R_EXP + gsel * EXPERTS_PER_GROUP
    inside = jnp.logical_and(lane >= first, lane < first + EXPERTS_PER_GROUP)
    within = jnp.where(inside, logits, -jnp.inf)
    v0, i0 = argmax(within)
    v1, i1 = argmax(jnp.where(lane == i0, -jnp.inf, within))
    e1 = jnp.exp(v1 - v0)
    w0 = gw * (1.0 / (1.0 + e1))
    w1 = gw * (e1 / (1.0 + e1))
    rw_ref[...] = jnp.where(lane == 0, w0, jnp.where(lane == 1, w1, 0.0))

    @pl.when(step == 0)
    def _():
        carry_ref[...] = jnp.zeros(carry_ref.shape, F32)

    hit0 = lane == i0
    hit1 = lane == i1
    onehot = jnp.where(jnp.logical_or(hit0, hit1), 1.0, 0.0)
    tri = (lax.broadcasted_iota(I32, (TILE, TILE), 1) < lax.broadcasted_iota(I32, (TILE, TILE), 0))
    base = carry_ref[...] + _mm(tri.astype(BF16), onehot.astype(BF16))
    r0 = jnp.sum(jnp.where(hit0, base, 0.0), axis=1, keepdims=True)
    r1 = jnp.sum(jnp.where(hit1, base, 0.0), axis=1, keepdims=True)
    total = carry_ref[...] + jnp.sum(onehot, axis=0, keepdims=True)
    carry_ref[...] = total
    cnt_ref[...] = total.astype(I32)
    slab = jnp.where(lane == 0, (i0 - R_EXP).astype(F32),
                     jnp.where(lane == 1, (i1 - R_EXP).astype(F32),
                               jnp.where(lane == 2, r0, jnp.where(lane == 3, r1, 0.0))))
    ids_ref[0] = slab.T[:8].astype(I32)


def _post(xf, aa, ab, kvm, tiles_per_batch, weights):
    n, d = xf.shape
    nt = n // TILE
    row = lambda w: pl.BlockSpec((TILE, w), lambda i: (i, 0))
    full = lambda a: pl.BlockSpec(a.shape, lambda i: (0,) * a.ndim)
    gm, wg, woa, wob, wout, gx, wq, wox, gmoe, wrt, brt = weights
    return pl.pallas_call(
        _post_kernel,
        grid=(nt,),
        in_specs=[row(d), row(W_A), row(W_B), full(gm), full(wg), full(woa), full(wob), full(wout),
                  full(gx), full(wq),
                  pl.BlockSpec((1,) + kvm.shape[1:], lambda i: (i // tiles_per_batch, 0, 0)),
                  full(wox), full(gmoe), full(wrt), full(brt)],
        out_specs=[row(d), row(d), row(LANES),
                   pl.BlockSpec((1, 8, TILE), lambda i: (i, 0, 0)),
                   pl.BlockSpec((1, LANES), lambda i: (0, 0))],
        out_shape=[jax.ShapeDtypeStruct((n, d), F32), jax.ShapeDtypeStruct((n, d), F32),
                   jax.ShapeDtypeStruct((n, LANES), F32),
                   jax.ShapeDtypeStruct((nt, 8, TILE), I32),
                   jax.ShapeDtypeStruct((1, LANES), I32)],
        scratch_shapes=[pltpu.VMEM((1, LANES), F32)],
        compiler_params=_params(("arbitrary",)),
        name="post",
    )(xf, aa, ab, gm, wg, woa, wob, wout, gx, wq, kvm, wox, gmoe, wrt, brt)


def _segment_starts(cnt_ref, seg_ref):
    def body(e, acc):
        seg_ref[e] = acc
        c = cnt_ref[0, R_EXP + e]
        return acc + ((c + (MOE_BLOCK - 1)) >> 8 << 8)
    return lax.fori_loop(0, N_EXPERTS, body, jnp.int32(0))


def _start_row_copies(ids_ref, seg_ref, rows_ref, tile_ref, sem, gather):
    def issue(t, carry):
        for k in range(2):
            dest = seg_ref[ids_ref[0, k, t]] + ids_ref[0, 2 + k, t]
            if gather:
                cp = pltpu.make_async_copy(rows_ref.at[pl.ds(dest, 1)], tile_ref.at[k, pl.ds(t, 1)], sem)
            else:
                cp = pltpu.make_async_copy(tile_ref.at[pl.ds(t, 1)], rows_ref.at[pl.ds(dest, 1)], sem)
            cp.start(priority=k)
        return carry
    lax.fori_loop(0, TILE, issue, 0, unroll=4)


def _wait_row_copies(rows_ref, tile_ref, sem):
    pltpu.make_async_copy(rows_ref.at[pl.ds(0, TILE)], tile_ref, sem).wait()


def _dispatch_kernel(ids_ref, cnt_ref, hm_ref, xr_in_ref, xr_ref, blk_ref, seg_ref, sem):
    del xr_in_ref
    step = pl.program_id(0)
    nb = blk_ref.shape[1] - 1

    @pl.when(step == 0)
    def _():
        used = _segment_starts(cnt_ref, seg_ref)

        def per_expert(e, last):
            c = cnt_ref[0, R_EXP + e]
            b0 = seg_ref[e] >> 8
            n = (c + (MOE_BLOCK - 1)) >> 8

            def fill(kk, carry):
                blk_ref[0, b0 + kk] = e
                return carry
            lax.fori_loop(0, n, fill, 0)
            return jnp.where(n > 0, e, last)
        last = lax.fori_loop(0, N_EXPERTS, per_expert, jnp.int32(0))

        def tail(b, carry):
            blk_ref[0, b] = last
            return carry
        lax.fori_loop(used >> 8, nb, tail, 0)
        blk_ref[0, nb] = used >> 8

    _start_row_copies(ids_ref, seg_ref, xr_ref, hm_ref, sem, gather=False)
    for _ in range(2):
        _wait_row_copies(xr_ref, hm_ref, sem)


def _dispatch(ids, cnt, hm, n_rows):
    n, d = hm.shape
    nt = n // TILE
    nb = n_rows // MOE_BLOCK
    zeros = jnp.zeros((n_rows, d), F32)
    return pl.pallas_call(
        _dispatch_kernel,
        grid=(nt,),
        in_specs=[pl.BlockSpec((1, 8, TILE), lambda i: (i, 0, 0), memory_space=pltpu.SMEM),
                  pl.BlockSpec(memory_space=pltpu.SMEM),
                  pl.BlockSpec((TILE, d), lambda i: (i, 0)),
                  pl.BlockSpec(memory_space=pl.ANY)],
        out_specs=[pl.BlockSpec(memory_space=pl.ANY),
                   pl.BlockSpec(memory_space=pltpu.SMEM)],
        out_shape=[jax.ShapeDtypeStruct((n_rows, d), F32),
                   jax.ShapeDtypeStruct((1, nb + 1), I32)],
        scratch_shapes=[pltpu.SMEM((N_EXPERTS,), I32), pltpu.SemaphoreType.DMA(())],
        input_output_aliases={3: 0},
        compiler_params=_params(("arbitrary",)),
        name="dispatch",
    )(ids, cnt, hm, zeros)


def _experts_kernel(blk_ref, x_ref, w1_ref, w3_ref, w2_ref, y_ref, w1b, w3b, w2b):
    b = pl.program_id(0)
    nb = pl.num_programs(0)
    e = blk_ref[0, b]
    prev = blk_ref[0, jnp.maximum(b - 1, 0)]

    @pl.when(jnp.logical_or(b == 0, e != prev))
    def _():
        w1b[...] = w1_ref[0].astype(BF16)
        w3b[...] = w3_ref[0].astype(BF16)
        w2b[...] = w2_ref[0].astype(BF16)

    used = blk_ref[0, nb]

    @pl.when(b < used)
    def _():
        xb = x_ref[...].astype(BF16)
        a = _mm(xb, w1b[...])
        g = _mm(xb, w3b[...])
        y_ref[...] = _mm((a * jax.nn.sigmoid(a) * g).astype(BF16), w2b[...])

    @pl.when(b >= used)
    def _():
        y_ref[...] = jnp.zeros(y_ref.shape, F32)


def _experts(blk, xr, w1, w3, w2):
    n_rows, d = xr.shape
    nb = n_rows // MOE_BLOCK
    wspec = lambda s: pl.BlockSpec((1,) + s, lambda b, blk: (blk[0, b], 0, 0))
    return pl.pallas_call(
        _experts_kernel,
        grid_spec=pltpu.PrefetchScalarGridSpec(
            num_scalar_prefetch=1,
            grid=(nb,),
            in_specs=[pl.BlockSpec((MOE_BLOCK, d), lambda b, blk: (b, 0)),
                      wspec((d, D_EXPERT)), wspec((d, D_EXPERT)), wspec((D_EXPERT, d))],
            out_specs=pl.BlockSpec((MOE_BLOCK, d), lambda b, blk: (b, 0)),
            scratch_shapes=[pltpu.VMEM((d, D_EXPERT), BF16), pltpu.VMEM((d, D_EXPERT), BF16),
                            pltpu.VMEM((D_EXPERT, d), BF16)]),
        out_shape=jax.ShapeDtypeStruct((n_rows, d), F32),
        compiler_params=_params(("arbitrary",)),
        name="experts",
    )(blk, xr, w1, w3, w2)


def _combine_kernel(ids_ref, ids_next_ref, cnt_ref, x2_ref, rw_ref, g_ref, yr_ref, o_ref, y_ref, seg_ref, sem):
    step = pl.program_id(0)
    slot = step % 2

    @pl.when(step == 0)
    def _():
        _segment_starts(cnt_ref, seg_ref)
        _start_row_copies(ids_ref, seg_ref, yr_ref, y_ref.at[0], sem.at[0], gather=True)

    @pl.when(step + 1 < pl.num_programs(0))
    def _():
        _start_row_copies(ids_next_ref, seg_ref, yr_ref, y_ref.at[1 - slot], sem.at[1 - slot], gather=True)

    for k in range(2):
        _wait_row_copies(yr_ref, y_ref.at[slot, k], sem.at[slot])
    rw = rw_ref[...]
    x3 = x2_ref[...] + (y_ref[slot, 0] * rw[:, 0:1] + y_ref[slot, 1] * rw[:, 1:2])
    o_ref[...] = _rms(x3, g_ref[...])


def _combine(ids, cnt, x2, rw, g, yr):
    n, d = x2.shape
    nt = n // TILE
    row = lambda w: pl.BlockSpec((TILE, w), lambda i: (i, 0))
    return pl.pallas_call(
        _combine_kernel,
        grid=(nt,),
        in_specs=[pl.BlockSpec((1, 8, TILE), lambda i: (i, 0, 0), memory_space=pltpu.SMEM),
                  pl.BlockSpec((1, 8, TILE), lambda i: (jnp.minimum(i + 1, nt - 1), 0, 0),
                               memory_space=pltpu.SMEM),
                  pl.BlockSpec(memory_space=pltpu.SMEM),
                  row(d), row(LANES), pl.BlockSpec((1, d), lambda i: (0, 0)),
                  pl.BlockSpec(memory_space=pl.ANY)],
        out_specs=row(d),
        out_shape=jax.ShapeDtypeStruct((n, d), F32),
        scratch_shapes=[pltpu.VMEM((2, 2, TILE, d), F32), pltpu.SMEM((N_EXPERTS,), I32),
                        pltpu.SemaphoreType.DMA((2,))],
        compiler_params=_params(("arbitrary",)),
        name="combine",
    )(ids, ids, cnt, x2, rw, g, yr)


def _t5_bucket(dist):
    n = jnp.maximum(dist, 0)
    max_exact = REL_BUCKETS // 2
    nf = jnp.maximum(n, 1).astype(F32)
    large = max_exact + (jnp.log(nf / max_exact) / math.log(REL_MAX_DIST / max_exact)
                         * (REL_BUCKETS - max_exact)).astype(I32)
    large = jnp.minimum(large, REL_BUCKETS - 1)
    return jnp.where(n < max_exact, n, large)


def _bias_tables(tab):
    r = jnp.arange(TILE)[:, None]
    c = jnp.arange(TILE)[None, :]
    rel = tab - tab[:, REL_BUCKETS - 1:]

    def tile(dist):
        onehot = (_t5_bucket(dist)[..., None] == jnp.arange(REL_BUCKETS)).astype(F32)
        return jnp.einsum('rcb,hb->hrc', onehot, rel, precision=lax.Precision.HIGHEST)

    own = jnp.where(r >= c, tile(r - c), NEG)
    prev = tile(r - c + TILE)
    near = jnp.stack([jnp.zeros_like(prev), prev, own], axis=1).astype(F32)
    return near.swapaxes(-1, -2)


def _block_mean_slabs(kmean, bn, nq):
    km = kmean.reshape(bn, nq, A_HEADS, HEAD_DIM).transpose(0, 2, 1, 3)
    pad = ((0, 0), (0, 0), (HEAD_DIM, LANES - HEAD_DIM - nq), (0, LANES - HEAD_DIM))
    return jnp.pad(km, pad).astype(BF16)


def _pad_cols(w, width):
    return jnp.pad(w, ((0, 0), (0, width - w.shape[1])))


def kernel(x, mem, rel_bias, final_norm, norm_mix, w_in, ckv_norm, w_uk, w_uv, w_oa, w_ob,
           w_out, norm_x, mem_norm, wq_x, wk_x, wv_x, wo_x, norm_moe, w_group, b_group,
           w_router, b_router, w1, w3, w2):
    bn, t, d = x.shape
    n = bn * t
    nq = t // TILE
    assert t % TILE == 0 and nq <= LANES - HEAD_DIM and norm_mix.shape[0] == 1
    assert nq % FAR_GROUP == 0
    near_a = _bias_tables(rel_bias[:, :A_HEADS].T)
    near_b = _bias_tables(rel_bias[:, A_HEADS:].T)
    row = lambda v: v.reshape(1, -1).astype(F32)

    wi = w_in[0]
    wp = jnp.concatenate(
        [wi[:, OFF_AQ:OFF_IK], jnp.tile(wi[:, OFF_IK:OFF_IW], (1, LANES // IDX_DIM)),
         _pad_cols(wi[:, OFF_IW:OFF_GA], LANES)], axis=1).astype(BF16)
    wkv = jnp.concatenate([w_uk[0], w_uv[0]], axis=1).astype(BF16)
    xf = x.reshape(n, d)
    qa, ka, va, qb, kb, vb, qi, ki4, widx, kmean = _proj_in(
        xf, row(norm_mix[0]), wp, row(ckv_norm[0]), wkv, nq)

    b3 = lambda a: a.reshape(bn, t, a.shape[-1])
    attn_a = _moba(b3(qa), b3(ka), va.reshape(bn, nq, W_A, TILE), _block_mean_slabs(kmean, bn, nq), near_a)
    attn_b = _dsa(b3(qi), b3(ki4), b3(widx), b3(qb), b3(kb), vb.reshape(bn, nq, W_B, TILE), near_b)

    kvm = _mem_kv(mem, row(mem_norm[0]), jnp.concatenate([wk_x[0], wv_x[0]], axis=1).astype(BF16))
    wrt = _pad_cols(jnp.concatenate([w_group[0], w_router[0]], axis=1), LANES).astype(F32)
    wrt_hi = wrt.astype(BF16)
    wrt = jnp.concatenate([wrt_hi, (wrt - wrt_hi.astype(F32)).astype(BF16)], axis=1)
    brt = _pad_cols(jnp.concatenate([b_group[0], b_router[0]]).reshape(1, -1), LANES).astype(F32)
    weights = (row(norm_mix[0]), wi[:, OFF_GA:IN_COLS].astype(BF16), w_oa[0].astype(BF16),
               w_ob[0].astype(BF16), w_out[0].astype(BF16), row(norm_x[0]), wq_x[0].astype(BF16),
               wo_x[0].astype(BF16), row(norm_moe[0]), wrt, brt)
    x2, hm, rw, ids, cnt = _post(xf, attn_a.reshape(n, W_A), attn_b.reshape(n, W_B), kvm, t // TILE, weights)

    n_rows = 2 * n + N_EXPERTS * MOE_BLOCK
    xr, blk = _dispatch(ids, cnt, hm, n_rows)
    yr = _experts(blk, xr, w1[0], w3[0], w2[0])
    out = _combine(ids, cnt, x2, rw, row(final_norm), yr)
    return out.reshape(bn, t, d)
```

```python
import functools
import math

import jax
import jax.numpy as jnp
from jax import lax
from jax.experimental import pallas as pl
from jax.experimental.pallas import tpu as pltpu

D_MODEL = 1024
HEAD_DIM = 64
A_HEADS = 8
B_HEADS = 8
W_A = A_HEADS * HEAD_DIM
W_B = B_HEADS * HEAD_DIM
MOBA_BLOCK = 256
MOBA_TOPK = 3
DSA_KV_RANK = 256
IDX_HEADS = 8
IDX_DIM = 32
DSA_TOPK = 256
REL_BUCKETS = 32
REL_MAX_DIST = 128
X_HEADS = 4
W_X = X_HEADS * HEAD_DIM
N_GROUPS = 4
EXPERTS_PER_GROUP = 8
N_EXPERTS = N_GROUPS * EXPERTS_PER_GROUP
D_EXPERT = 512
MOE_BLOCK = 256
EPS = 1e-6
NEG = -1e30

OFF_AQ = 0
OFF_AK = OFF_AQ + W_A
OFF_AV = OFF_AK + W_A
OFF_BQ = OFF_AV + W_A
OFF_CKV = OFF_BQ + W_B
OFF_IQ = OFF_CKV + DSA_KV_RANK
OFF_IK = OFF_IQ + IDX_HEADS * IDX_DIM
OFF_IW = OFF_IK + IDX_DIM
OFF_GA = OFF_IW + IDX_HEADS
OFF_GB = OFF_GA + D_MODEL
IN_COLS = OFF_GB + D_MODEL

LANES = 128
TILE = MOBA_BLOCK
assert TILE == MOE_BLOCK
SEARCH_STEPS_MAX = 400
NO_CUT = 2 ** 30
VMEM_LIMIT = 56 * 1024 * 1024
BF16 = jnp.bfloat16
F32 = jnp.float32
I32 = jnp.int32

P_QA, P_KA, P_VA, P_QB = 0, 512, 1024, 1536
P_CKV, P_IQ, P_IK4, P_IW = 2048, 2304, 2560, 2688
P_COLS = 2816

R_EXP = N_GROUPS
MOE_SHIFT = MOE_BLOCK.bit_length() - 1
assert 1 << MOE_SHIFT == MOE_BLOCK


def _nt(a, b):
    return lax.dot_general(a, b, (((1,), (1,)), ((), ())), preferred_element_type=F32)


def _mm(a, b):
    return jnp.dot(a, b, preferred_element_type=F32)


def _rms(xf, g):
    return xf * lax.rsqrt(jnp.mean(xf * xf, axis=-1, keepdims=True) + EPS) * g


def _params(sem, vmem=VMEM_LIMIT):
    return pltpu.CompilerParams(dimension_semantics=sem, vmem_limit_bytes=vmem)


def _mem_kv_kernel(mem_ref, g_ref, w_ref, o_ref):
    m = _rms(mem_ref[0], g_ref[...]).astype(BF16)
    o_ref[0] = _mm(m, w_ref[...]).astype(BF16)


def _mem_kv(mem, g, wkv):
    bn, ml, d = mem.shape
    return pl.pallas_call(
        _mem_kv_kernel,
        grid=(bn,),
        in_specs=[pl.BlockSpec((1, ml, d), lambda b: (b, 0, 0)),
                  pl.BlockSpec((1, d), lambda b: (0, 0)),
                  pl.BlockSpec((d, 2 * W_X), lambda b: (0, 0))],
        out_specs=pl.BlockSpec((1, ml, 2 * W_X), lambda b: (b, 0, 0)),
        out_shape=jax.ShapeDtypeStruct((bn, ml, 2 * W_X), BF16),
        compiler_params=_params(("arbitrary",)),
        name="mem_kv",
    )(mem, g, wkv)


def _head_slabs(zp, extra):
    lane = lax.broadcasted_iota(I32, zp.shape, 1)
    low = lane < HEAD_DIM
    return jnp.where(low, zp, extra), jnp.where(low, pltpu.roll(zp, HEAD_DIM, 1), extra)


def _proj_in_kernel(blocks_per_seq, x_ref, g_ref, w_ref, cg_ref, wkv_ref,
                    qa_ref, ka_ref, va_ref, qb_ref, kb_ref, vb_ref, qi_ref, ki_ref, wi_ref, km_ref):
    h = _rms(x_ref[...], g_ref[...]).astype(BF16)
    z = _mm(h, w_ref[...])
    scale = HEAD_DIM ** -0.5
    zk = z[:, P_KA:P_VA]
    km_ref[0] = jnp.mean(zk, axis=0, keepdims=True)
    lane = lax.broadcasted_iota(I32, (TILE, LANES), 1)
    blk = pl.program_id(0) % blocks_per_seq
    onehot = jnp.where(lane == HEAD_DIM + blk, 1.0, 0.0)
    zero = jnp.zeros((TILE, LANES), F32)
    for p_idx in range(A_HEADS // 2):
        lo = p_idx * LANES
        qe, qo = _head_slabs(z[:, P_QA + lo:P_QA + lo + LANES] * scale, zero)
        ke, ko = _head_slabs(zk[:, lo:lo + LANES], onehot)
        qa_ref[:, 2 * lo:2 * lo + 2 * LANES] = jnp.concatenate([qe, qo], axis=1).astype(BF16)
        ka_ref[:, 2 * lo:2 * lo + 2 * LANES] = jnp.concatenate([ke, ko], axis=1).astype(BF16)
    va_ref[0] = z[:, P_VA:P_QB].T.astype(BF16)
    qb_ref[...] = (z[:, P_QB:P_CKV] * scale).astype(BF16)
    ckv = _rms(z[:, P_CKV:P_IQ], cg_ref[...]).astype(BF16)
    kv = _mm(ckv, wkv_ref[...])
    kb_ref[...] = kv[:, :W_B].astype(BF16)
    vb_ref[0] = kv[:, W_B:].T.astype(BF16)
    qi_ref[...] = z[:, P_IQ:P_IK4].astype(BF16)
    ki_ref[...] = z[:, P_IK4:P_IW].astype(BF16)
    wi_ref[...] = z[:, P_IW:P_COLS] * ((IDX_HEADS ** -0.5) * (IDX_DIM ** -0.5))


def _proj_in(xf, g, wp, cg, wkv, blocks_per_seq):
    n, d = xf.shape
    nt = n // TILE
    row = lambda w: pl.BlockSpec((TILE, w), lambda i: (i, 0))
    full = lambda a: pl.BlockSpec(a.shape, lambda i: (0,) * a.ndim)
    outs = [(2 * W_A, BF16, False), (2 * W_A, BF16, False), (W_A, BF16, True), (W_B, BF16, False),
            (W_B, BF16, False), (W_B, BF16, True), (IDX_HEADS * IDX_DIM, BF16, False), (LANES, BF16, False),
            (LANES, F32, False)]
    blocked = lambda w: pl.BlockSpec((1, w, TILE), lambda i: (i, 0, 0))
    return pl.pallas_call(
        functools.partial(_proj_in_kernel, blocks_per_seq),
        grid=(nt,),
        in_specs=[row(d), full(g), full(wp), full(cg), full(wkv)],
        out_specs=[blocked(w) if tr else row(w) for w, _, tr in outs]
        + [pl.BlockSpec((1, 1, W_A), lambda i: (i, 0, 0))],
        out_shape=[jax.ShapeDtypeStruct((nt, w, TILE) if tr else (n, w), t) for w, t, tr in outs]
        + [jax.ShapeDtypeStruct((nt, 1, W_A), F32)],
        compiler_params=_params(("arbitrary",)),
        name="proj_in",
    )(xf, g, wp, cg, wkv)


SUB = TILE // 8
LOG2E = math.log2(math.e)
FAR_GROUP = 16


def _sm_init(mx_ref, acc_ref):
    mx_ref[...] = jnp.full(mx_ref.shape, NEG, F32)
    acc_ref[...] = jnp.zeros(acc_ref.shape, F32)


def _sm_logits(hd, slot, s, sc_ref, mx_ref):
    s = s * LOG2E
    sc_ref[hd, slot] = s
    mx_ref[hd] = jnp.maximum(mx_ref[hd], jnp.max(s.reshape(SUB, 8, TILE), axis=0))


def _sm_rowmax(mx_ref):
    for hd in range(2):
        mx_ref[hd] = jnp.broadcast_to(jnp.max(mx_ref[hd], axis=0, keepdims=True), (8, TILE))


def _sm_probs(hd, slots, sc_ref, mx_ref):
    m = mx_ref[hd][None]
    ps = [jnp.exp2((sc_ref[hd, sl].reshape(SUB, 8, TILE) - m).reshape(TILE, TILE).astype(BF16))
          for sl in slots]
    return ps[0] if len(ps) == 1 else jnp.concatenate(ps, axis=0)


def _sm_accumulate(hd, pb, v, acc_ref):
    row = lax.broadcasted_iota(I32, v.shape, 0)
    own = (row < HEAD_DIM) if hd == 0 else (row >= HEAD_DIM)
    acc_ref[hd] = acc_ref[hd] + _mm(jnp.where(own, v, jnp.ones_like(v)), pb)


def _sm_values(hd, slots, v, sc_ref, mx_ref, acc_ref):
    _sm_accumulate(hd, _sm_probs(hd, slots, sc_ref, mx_ref), v, acc_ref)


def _far_plan(n_blocks):
    quarter = FAR_GROUP // 4
    padded = (n_blocks + quarter - 1) // quarter * quarter
    full = padded // FAR_GROUP
    rest = padded - full * FAR_GROUP
    has_half = rest >= FAR_GROUP // 2
    half_base = full * FAR_GROUP
    quarter_base = half_base + jnp.where(has_half, FAR_GROUP // 2, 0)
    return full, has_half, half_base, rest % (FAR_GROUP // 2) != 0, quarter_base


def _sm_far_loop(plan, group_fn):
    full, has_half, half_base, has_quarter, quarter_base = plan
    lax.fori_loop(0, full, lambda g, c: group_fn(FAR_GROUP * g, FAR_GROUP) or c, 0)
    pl.when(has_half)(lambda: group_fn(half_base, FAR_GROUP // 2))
    pl.when(has_quarter)(lambda: group_fn(quarter_base, FAR_GROUP // 4))


def _sm_far_values(plan, vals, sc_ref, mx_ref, acc_ref):
    def group(j0, n):
        v = vals(j0, n)
        for hd in range(2):
            _sm_values(hd, [j0 + u for u in range(n)], v, sc_ref, mx_ref, acc_ref)

    _sm_far_loop(plan, group)


def _sm_output(acc_ref):
    num = jnp.concatenate([acc_ref[0, :HEAD_DIM], acc_ref[1, HEAD_DIM:]], axis=0)
    den = jnp.concatenate([acc_ref[0, HEAD_DIM:], acc_ref[1, :HEAD_DIM]], axis=0)
    return (num / den).T


def _split_pair(qp):
    lane = lax.broadcasted_iota(I32, qp.shape, 1)
    zero = jnp.zeros_like(qp)
    return jnp.where(lane < HEAD_DIM, qp, zero), jnp.where(lane >= HEAD_DIM, qp, zero)


def _moba_kernel(q_ref, k_ref, v_ref, km_ref, bias_ref, o_ref, sc_ref, mx_ref, acc_ref):
    i = pl.program_id(2)
    nq = pl.num_programs(2)
    n_slot = LANES - HEAD_DIM
    blk = lax.broadcasted_iota(I32, (n_slot, TILE), 0)
    in_blk = blk < nq
    past = blk < i
    no_mask = jnp.zeros((HEAD_DIM, TILE), F32)

    qm = []
    for hd in range(2):
        q = q_ref[0, :, hd * LANES:(hd + 1) * LANES]
        g = jnp.where(past, _nt(km_ref[0, hd], q)[HEAD_DIM:], -jnp.inf)
        sel = jnp.zeros((n_slot, TILE), jnp.bool_)
        for _ in range(MOBA_TOPK):
            top = jnp.max(g, axis=0, keepdims=True)
            hit = jnp.logical_and(g == top, in_blk)
            first = jnp.min(jnp.where(hit, blk, n_slot), axis=0, keepdims=True)
            pick = blk == first
            sel = jnp.logical_or(sel, jnp.logical_and(pick, past))
            g = jnp.where(pick, -jnp.inf, g)
        addm = jnp.where(sel, 0.0, NEG)
        mask = jnp.where(past, addm, jnp.where(jnp.logical_and(in_blk, blk > i), NEG, 0.0))
        qm.append(q + jnp.concatenate([no_mask, mask], axis=0).T.astype(BF16))

    def keys(j, hd):
        off = pl.multiple_of(j * TILE, TILE)
        return k_ref[0, pl.ds(off, TILE), hd * LANES:(hd + 1) * LANES]

    plan = _far_plan(i + 1)
    _sm_init(mx_ref, acc_ref)

    def logits(j0, n):
        for u in range(n):
            j = j0 + u
            tile = _near_bias_index(j, i)
            for hd in range(2):
                _sm_logits(hd, j, _nt(keys(j, hd), qm[hd]) + bias_ref[hd, tile], sc_ref, mx_ref)

    _sm_far_loop(plan, logits)
    _sm_rowmax(mx_ref)

    def vals(j, n):
        return jnp.concatenate([v_ref[0, j + u] for u in range(n)], axis=1)

    _sm_far_values(plan, vals, sc_ref, mx_ref, acc_ref)
    o_ref[0] = _sm_output(acc_ref).astype(BF16)


def _near_bias_index(j, i):
    return jnp.clip(j - i + 2, 0, 2)


def _sm_scratch(n_slots):
    return [pltpu.VMEM((2, n_slots, TILE, TILE), F32),
            pltpu.VMEM((2, 8, TILE), F32),
            pltpu.VMEM((2, LANES, TILE), F32)]


def _moba(qa, ka, va_t, kmp, bias_near):
    bn, nq = va_t.shape[:2]
    t = nq * TILE
    npair = A_HEADS // 2
    return pl.pallas_call(
        _moba_kernel,
        grid=(bn, npair, nq),
        in_specs=[pl.BlockSpec((1, TILE, 2 * LANES), lambda b, p, i: (b, i, p)),
                  pl.BlockSpec((1, t, 2 * LANES), lambda b, p, i: (b, 0, p)),
                  pl.BlockSpec((1, nq, LANES, TILE), lambda b, p, i: (b, 0, p, 0)),
                  pl.BlockSpec((1, 2, LANES, LANES), lambda b, p, i: (b, p, 0, 0)),
                  pl.BlockSpec((2, 3, TILE, TILE), lambda b, p, i: (p, 0, 0, 0))],
        out_specs=pl.BlockSpec((1, TILE, LANES), lambda b, p, i: (b, i, p)),
        out_shape=jax.ShapeDtypeStruct((bn, t, W_A), BF16),
        scratch_shapes=_sm_scratch(nq),
        compiler_params=_params(("arbitrary", "arbitrary", "arbitrary")),
        name="moba",
    )(qa, ka, va_t, kmp, bias_near)


def _dsa_select(i, qi_ref, ki_ref, wi_ref, s_ref, wb_ref):
    nch = i + 1
    sub = TILE // 8
    lane = lax.broadcasted_iota(I32, (TILE, LANES), 1)
    key_in = (lax.broadcasted_iota(I32, (sub, 8, TILE), 0) * 8 + lax.broadcasted_iota(I32, (sub, 8, TILE), 1))
    qpos = i * TILE + lax.broadcasted_iota(I32, (sub, 8, TILE), 2)

    def rows8(x):
        return jnp.broadcast_to(x, (8, TILE))[None]

    w_t = wi_ref[0].T
    qi = qi_ref[0]
    per_group = LANES // IDX_DIM
    qh = []
    for h in range(IDX_HEADS):
        g, r = divmod(h, per_group)
        qg = qi[:, g * LANES:(g + 1) * LANES]
        keep = jnp.logical_and(lane >= r * IDX_DIM, lane < (r + 1) * IDX_DIM)
        qh.append(jnp.where(keep, qg, jnp.zeros_like(qg)))
        wb_ref[h] = jnp.broadcast_to(w_t[h:h + 1], (8, TILE))

    def score(c, carry):
        off = pl.multiple_of(c * TILE, TILE)
        kc = ki_ref[0, pl.ds(off, TILE), :]
        acc = jnp.zeros((sub, 8, TILE), F32)
        for h in range(IDX_HEADS):
            acc = acc + jnp.maximum(_nt(kc, qh[h]), 0.0).reshape(sub, 8, TILE) * wb_ref[h][None]
        s_ref[c] = jnp.where(c * TILE + key_in <= qpos, acc, -jnp.inf).reshape(TILE, TILE)
        return carry

    n_pair = (nch + 1) // 2

    def pairs(step, init):
        return lax.fori_loop(0, n_pair, lambda g, acc: step(2 * g + 1, step(2 * g, acc)), init)

    pairs(score, 0)

    def fold(init, step):
        return pairs(lambda c, acc: step(acc, s_ref[c].reshape(sub, 8, TILE), c), init)

    def count(pred):
        def step(acc, blk, c):
            return acc + jnp.sum(jnp.where(pred(blk, c * TILE + key_in), 1.0, 0.0), axis=0)
        return jnp.sum(fold(jnp.zeros((8, TILE), F32), step), axis=0, keepdims=True)

    wide = rows8

    def stats(acc, blk, c):
        top, low, pos, nonneg = acc
        top = jnp.maximum(top, jnp.max(blk, axis=0))
        low = jnp.minimum(low, jnp.min(jnp.where(blk == -jnp.inf, jnp.inf, blk), axis=0))
        pos = pos + jnp.sum(jnp.where(blk > 0.0, 1.0, 0.0), axis=0)
        nonneg = nonneg + jnp.sum(jnp.where(blk >= 0.0, 1.0, 0.0), axis=0)
        return top, low, pos, nonneg

    zeros8 = jnp.zeros((8, TILE), F32)
    acc = fold((zeros8 - jnp.inf, zeros8 + jnp.inf, zeros8, zeros8), stats)
    hi = jnp.max(acc[0], axis=0, keepdims=True)
    lo = jnp.min(acc[1], axis=0, keepdims=True)
    n_pos = jnp.sum(acc[2], axis=0, keepdims=True)
    n_nonneg = jnp.sum(acc[3], axis=0, keepdims=True)

    n_lo = (i * TILE + 1 + lax.broadcasted_iota(I32, (1, TILE), 1)).astype(F32)
    hi = hi + jnp.maximum(jnp.abs(hi) * 2.0 ** -20, 1e-30)
    n_hi = jnp.zeros((1, TILE), F32)
    at_zero = jnp.logical_and(n_pos < DSA_TOPK, n_nonneg >= DSA_TOPK)
    from_zero = n_pos >= DSA_TOPK
    lo, n_lo = jnp.where(from_zero, 0.0, lo), jnp.where(from_zero, n_nonneg, n_lo)
    to_zero = n_nonneg < DSA_TOPK
    hi, n_hi = jnp.where(to_zero, 0.0, hi), jnp.where(to_zero, n_nonneg, n_hi)

    def search_more(state):
        return jnp.logical_and(jnp.min(state[7]) == 0, state[8] < SEARCH_STEPS_MAX)

    def search(state):
        lo, hi, n_lo, n_hi, thr, n_ge, n_gt, done, it = state
        mid = 0.5 * lo + 0.5 * hi
        closed = jnp.logical_or(mid <= lo, mid >= hi)
        n = count(lambda half, kpos: half >= wide(mid))
        stop = jnp.logical_or(closed, n == DSA_TOPK)
        settle = jnp.logical_and(done == 0, stop)
        thr = jnp.where(settle, jnp.where(closed, lo, mid), thr)
        n_ge = jnp.where(settle, jnp.where(closed, n_lo, n), n_ge)
        n_gt = jnp.where(settle, jnp.where(closed, n_hi, 0.0), n_gt)
        go = jnp.logical_and(done == 0, jnp.logical_not(stop))
        up = jnp.logical_and(go, n > DSA_TOPK)
        down = jnp.logical_and(go, n < DSA_TOPK)
        return (jnp.where(up, mid, lo), jnp.where(down, mid, hi), jnp.where(up, n, n_lo),
                jnp.where(down, n, n_hi), thr, n_ge, n_gt, jnp.where(stop, 1, done), it + 1)

    state = (lo, hi, n_lo, n_hi, jnp.zeros((1, TILE), F32), n_nonneg, n_pos, at_zero.astype(I32), jnp.int32(0))
    state = lax.while_loop(search_more, lambda st: search(search(st)), state)
    thr, n_ge, n_gt = wide(state[4]), state[5], state[6]
    need = DSA_TOPK - n_gt

    cut_bits = max(1, (s_ref.shape[0] * TILE - 1).bit_length())

    def cut_step(b, cut):
        cand = cut | (jnp.int32(1) << (cut_bits - 1 - b))
        cw = wide(cand)
        n = count(lambda half, kpos: jnp.logical_and(half == thr, kpos < cw))
        return jnp.where(n < need, cand, cut)

    cut = lax.cond(jnp.max(n_ge) > DSA_TOPK,
                   lambda: lax.fori_loop(0, cut_bits, cut_step, jnp.zeros((1, TILE), I32)),
                   lambda: jnp.full((1, TILE), NO_CUT, I32))
    cut = wide(jnp.where(n_ge > DSA_TOPK, cut, NO_CUT))

    def to_mask(c, carry):
        blk = s_ref[c].reshape(sub, 8, TILE)
        kpos = c * TILE + key_in
        keep = jnp.logical_or(blk > thr, jnp.logical_and(blk == thr, kpos <= cut))
        s_ref[c] = jnp.where(jnp.logical_and(keep, kpos <= qpos), 0.0, NEG).reshape(TILE, TILE)
        return carry

    pairs(to_mask, 0)


def _dsa_kernel(qi_ref, ki_ref, wi_ref, q_ref, k_ref, v_ref, bias_ref, o_ref,
                s_ref, wb_ref, sc_ref, mx_ref, acc_ref):
    i = pl.program_id(1)
    pair = pl.program_id(2)

    @pl.when(jnp.logical_and(pair == 0, i == 0))
    def _():
        key = lax.broadcasted_iota(I32, (TILE, TILE), 0)
        query = lax.broadcasted_iota(I32, (TILE, TILE), 1)
        s_ref[0] = jnp.where(key <= query, 0.0, NEG)

    @pl.when(jnp.logical_and(pair == 0, i >= 1))
    def _():
        _dsa_select(i, qi_ref, ki_ref, wi_ref, s_ref, wb_ref)

    qs = _split_pair(q_ref[0])
    plan = _far_plan(jnp.maximum(i - 1, 0))
    _sm_init(mx_ref, acc_ref)

    def vals(j, n):
        return jnp.concatenate([v_ref[0, j + u] for u in range(n)], axis=1)

    def logits(j, slot, msk, near=None):
        k_j = k_ref[0, pl.ds(pl.multiple_of(j * TILE, TILE), TILE), :]
        for hd in range(2):
            s = _nt(k_j, qs[hd]) + msk
            if near is not None:
                s = s + bias_ref[hd, near]
            _sm_logits(hd, slot, s, sc_ref, mx_ref)

    def far_logits(j0, n):
        for u in range(n):
            j = j0 + u
            logits(j, j, jnp.where(j < i - 1, s_ref[jnp.minimum(j, i)], NEG))

    _sm_far_loop(plan, far_logits)
    nq = pl.num_programs(1)
    j_prev = jnp.maximum(i - 1, 0)
    logits(j_prev, nq, jnp.where(i >= 1, s_ref[j_prev], NEG), near=1)
    logits(i, nq + 1, s_ref[i], near=2)
    _sm_rowmax(mx_ref)

    _sm_far_values(plan, vals, sc_ref, mx_ref, acc_ref)
    v_near = jnp.concatenate([vals(j_prev, 1), vals(i, 1)], axis=1)
    for hd in range(2):
        _sm_values(hd, [nq, nq + 1], v_near, sc_ref, mx_ref, acc_ref)
    o_ref[0] = _sm_output(acc_ref).astype(BF16)


def _dsa(qi, ki4, wi, qb, kb, vb_t, bias_near):
    bn, t, _ = qb.shape
    nq = t // TILE
    assert nq % 2 == 0
    npair = B_HEADS // 2
    qrow = lambda w: pl.BlockSpec((1, TILE, w), lambda b, i, p: (b, i, 0))
    return pl.pallas_call(
        _dsa_kernel,
        grid=(bn, nq, npair),
        in_specs=[qrow(IDX_HEADS * IDX_DIM),
                  pl.BlockSpec((1, t, LANES), lambda b, i, p: (b, 0, 0), pipeline_mode=pl.Buffered(1)),
                  qrow(LANES),
                  pl.BlockSpec((1, TILE, LANES), lambda b, i, p: (b, i, p)),
                  pl.BlockSpec((1, t, LANES), lambda b, i, p: (b, 0, p)),
                  pl.BlockSpec((1, nq, LANES, TILE), lambda b, i, p: (b, 0, p, 0)),
                  pl.BlockSpec((2, 3, TILE, TILE), lambda b, i, p: (p, 0, 0, 0))],
        out_specs=pl.BlockSpec((1, TILE, LANES), lambda b, i, p: (b, i, p)),
        out_shape=jax.ShapeDtypeStruct((bn, t, W_B), BF16),
        scratch_shapes=[pltpu.VMEM((nq, TILE, TILE), F32), pltpu.VMEM((IDX_HEADS, 8, TILE), F32)]
        + _sm_scratch(nq + 2),
        compiler_params=_params(("arbitrary", "arbitrary", "arbitrary")),
        name="dsa",
    )(qi, ki4, wi, qb, kb, vb_t, bias_near)


def _post_kernel(x_ref, aa_ref, ab_ref, gm_ref, wg_ref, woa_ref, wob_ref, wout_ref,
                 gx_ref, wq_ref, kvm_ref, wox_ref, gmoe_ref, wrt_ref, brt_ref,
                 x2_ref, hm_ref, rw_ref, ids_ref, cnt_ref, carry_ref):
    step = pl.program_id(0)
    xf = x_ref[...]
    h = _rms(xf, gm_ref[...]).astype(BF16)
    gates = jax.nn.sigmoid(_mm(h, wg_ref[...]))
    oa = _mm(aa_ref[...], woa_ref[...])
    ob = _mm(ab_ref[...], wob_ref[...])
    mrg = gates[:, :D_MODEL] * oa + gates[:, D_MODEL:] * ob
    x1 = xf + _mm(mrg.astype(BF16), wout_ref[...])

    hx = _rms(x1, gx_ref[...]).astype(BF16)
    q = (_mm(hx, wq_ref[...]) * (HEAD_DIM ** -0.5)).astype(BF16)
    kvm = kvm_ref[0]
    lane = lax.broadcasted_iota(I32, (TILE, LANES), 1)
    outs = []
    for p_idx in range(X_HEADS // 2):
        lo = p_idx * LANES
        qs = _split_pair(q[:, lo:lo + LANES])
        km = kvm[:, lo:lo + LANES]
        vm = kvm[:, W_X + lo:W_X + lo + LANES]
        o = []
        for hd in range(2):
            s = _nt(qs[hd], km)
            p = jnp.exp(s - jnp.max(s, axis=1, keepdims=True))
            o.append(_mm(p.astype(BF16), vm) / jnp.sum(p, axis=1, keepdims=True))
        outs.append(jnp.where(lane < HEAD_DIM, o[0], o[1]))
    xo = jnp.concatenate(outs, axis=1).astype(BF16)
    x2 = x1 + _mm(xo, wox_ref[...])
    x2_ref[...] = x2

    hm = _rms(x2, gmoe_ref[...])
    hm_ref[...] = hm
    hm_hi = hm.astype(BF16)
    hm_lo = (hm - hm_hi.astype(F32)).astype(BF16)
    hi_part = _mm(hm_hi, wrt_ref[...])
    logits = (hi_part[:, :LANES] + hi_part[:, LANES:]) + _mm(hm_lo, wrt_ref[:, :LANES]) + brt_ref[...]
    big = LANES

    def argmax(v):
        top = jnp.max(v, axis=1, keepdims=True)
        return top, jnp.min(jnp.where(v == top, lane, big), axis=1, keepdims=True)

    is_grp = lane < R_EXP
    gtop, gsel = argmax(jnp.where(is_grp, logits, -jnp.inf))
    gw = 1.0 / jnp.sum(jnp.where(is_grp, jnp.exp(logits - gtop), 0.0), axis=1, keepdims=True)
    first = R_EXP + gsel * EXPERTS_PER_GROUP
    inside = jnp.logical_and(lane >= first, lane < first + EXPERTS_PER_GROUP)
    within = jnp.where(inside, logits, -jnp.inf)
    v0, i0 = argmax(within)
    v1, i1 = argmax(jnp.where(lane == i0, -jnp.inf, within))
    e1 = jnp.exp(v1 - v0)
    w0 = gw * (1.0 / (1.0 + e1))
    w1 = gw * (e1 / (1.0 + e1))
    rw_ref[...] = jnp.where(lane == 0, w0, jnp.where(lane == 1, w1, 0.0))

    @pl.when(step == 0)
    def _():
        carry_ref[...] = jnp.zeros(carry_ref.shape, F32)

    hit0 = lane == i0
    hit1 = lane == i1
    onehot = jnp.where(jnp.logical_or(hit0, hit1), 1.0, 0.0)
    tri = (lax.broadcasted_iota(I32, (TILE, TILE), 1) < lax.broadcasted_iota(I32, (TILE, TILE), 0))
    base = carry_ref[...] + _mm(tri.astype(BF16), onehot.astype(BF16))
    r0 = jnp.sum(jnp.where(hit0, base, 0.0), axis=1, keepdims=True)
    r1 = jnp.sum(jnp.where(hit1, base, 0.0), axis=1, keepdims=True)
    total = carry_ref[...] + jnp.sum(onehot, axis=0, keepdims=True)
    carry_ref[...] = total
    cnt_ref[...] = total.astype(I32)
    slab = jnp.where(lane == 0, (i0 - R_EXP).astype(F32),
                     jnp.where(lane == 1, (i1 - R_EXP).astype(F32),
                               jnp.where(lane == 2, r0, jnp.where(lane == 3, r1, 0.0))))
    ids_ref[0] = slab.T[:8].astype(I32)


def _post(xf, aa, ab, kvm, tiles_per_batch, weights):
    n, d = xf.shape
    nt = n // TILE
    row = lambda w: pl.BlockSpec((TILE, w), lambda i: (i, 0))
    full = lambda a: pl.BlockSpec(a.shape, lambda i: (0,) * a.ndim)
    gm, wg, woa, wob, wout, gx, wq, wox, gmoe, wrt, brt = weights
    return pl.pallas_call(
        _post_kernel,
        grid=(nt,),
        in_specs=[row(d), row(W_A), row(W_B), full(gm), full(wg), full(woa), full(wob), full(wout),
                  full(gx), full(wq),
                  pl.BlockSpec((1,) + kvm.shape[1:], lambda i: (i // tiles_per_batch, 0, 0)),
                  full(wox), full(gmoe), full(wrt), full(brt)],
        out_specs=[row(d), row(d), row(LANES),
                   pl.BlockSpec((1, 8, TILE), lambda i: (i, 0, 0)),
                   pl.BlockSpec((1, LANES), lambda i: (0, 0))],
        out_shape=[jax.ShapeDtypeStruct((n, d), F32), jax.ShapeDtypeStruct((n, d), F32),
                   jax.ShapeDtypeStruct((n, LANES), F32),
                   jax.ShapeDtypeStruct((nt, 8, TILE), I32),
                   jax.ShapeDtypeStruct((1, LANES), I32)],
        scratch_shapes=[pltpu.VMEM((1, LANES), F32)],
        compiler_params=_params(("arbitrary",)),
        name="post",
    )(xf, aa, ab, gm, wg, woa, wob, wout, gx, wq, kvm, wox, gmoe, wrt, brt)


def _segment_starts(cnt_ref, seg_ref):
    def body(e, acc):
        seg_ref[e] = acc
        c = cnt_ref[0, R_EXP + e]
        return acc + ((c + (MOE_BLOCK - 1)) >> MOE_SHIFT << MOE_SHIFT)
    return lax.fori_loop(0, N_EXPERTS, body, jnp.int32(0))


def _start_row_copies(ids_ref, seg_ref, rows_ref, tile_ref, sem, gather):
    def issue(t, carry):
        for k in range(2):
            dest = seg_ref[ids_ref[0, k, t]] + ids_ref[0, 2 + k, t]
            if gather:
                cp = pltpu.make_async_copy(rows_ref.at[pl.ds(dest, 1)], tile_ref.at[k, pl.ds(t, 1)], sem)
            else:
                cp = pltpu.make_async_copy(tile_ref.at[pl.ds(t, 1)], rows_ref.at[pl.ds(dest, 1)], sem)
            cp.start(priority=k)
        return carry
    lax.fori_loop(0, TILE, issue, 0, unroll=4)


def _wait_row_copies(rows_ref, tile_ref, sem):
    pltpu.make_async_copy(rows_ref.at[pl.ds(0, TILE)], tile_ref, sem).wait()


def _dispatch_kernel(ids_ref, cnt_ref, hm_ref, xr_in_ref, xr_ref, blk_ref, seg_ref, sem):
    del xr_in_ref
    step = pl.program_id(0)
    nb = blk_ref.shape[1] - 1

    @pl.when(step == 0)
    def _():
        used = _segment_starts(cnt_ref, seg_ref)

        def per_expert(e, last):
            c = cnt_ref[0, R_EXP + e]
            b0 = seg_ref[e] >> MOE_SHIFT
            n = (c + (MOE_BLOCK - 1)) >> MOE_SHIFT

            def fill(kk, carry):
                blk_ref[0, b0 + kk] = e
                return carry
            lax.fori_loop(0, n, fill, 0)
            return jnp.where(n > 0, e, last)
        last = lax.fori_loop(0, N_EXPERTS, per_expert, jnp.int32(0))

        def tail(b, carry):
            blk_ref[0, b] = last
            return carry
        lax.fori_loop(used >> MOE_SHIFT, nb, tail, 0)
        blk_ref[0, nb] = used >> MOE_SHIFT

    _start_row_copies(ids_ref, seg_ref, xr_ref, hm_ref, sem, gather=False)
    for _ in range(2):
        _wait_row_copies(xr_ref, hm_ref, sem)


def _dispatch(ids, cnt, hm, n_rows):
    n, d = hm.shape
    nt = n // TILE
    nb = n_rows // MOE_BLOCK
    zeros = jnp.zeros((n_rows, d), F32)
    return pl.pallas_call(
        _dispatch_kernel,
        grid=(nt,),
        in_specs=[pl.BlockSpec((1, 8, TILE), lambda i: (i, 0, 0), memory_space=pltpu.SMEM),
                  pl.BlockSpec(memory_space=pltpu.SMEM),
                  pl.BlockSpec((TILE, d), lambda i: (i, 0)),
                  pl.BlockSpec(memory_space=pl.ANY)],
        out_specs=[pl.BlockSpec(memory_space=pl.ANY),
                   pl.BlockSpec(memory_space=pltpu.SMEM)],
        out_shape=[jax.ShapeDtypeStruct((n_rows, d), F32),
                   jax.ShapeDtypeStruct((1, nb + 1), I32)],
        scratch_shapes=[pltpu.SMEM((N_EXPERTS,), I32), pltpu.SemaphoreType.DMA(())],
        input_output_aliases={3: 0},
        compiler_params=_params(("arbitrary",)),
        name="dispatch",
    )(ids, cnt, hm, zeros)


def _experts_kernel(blk_ref, x_ref, w1_ref, w3_ref, w2_ref, y_ref, w1b, w3b, w2b):
    b = pl.program_id(0)
    nb = pl.num_programs(0)
    e = blk_ref[0, b]
    prev = blk_ref[0, jnp.maximum(b - 1, 0)]

    @pl.when(jnp.logical_or(b == 0, e != prev))
    def _():
        w1b[...] = w1_ref[0].astype(BF16)
        w3b[...] = w3_ref[0].astype(BF16)
        w2b[...] = w2_ref[0].astype(BF16)

    used = blk_ref[0, nb]

    @pl.when(b < used)
    def _():
        xb = x_ref[...].astype(BF16)
        a = _mm(xb, w1b[...])
        g = _mm(xb, w3b[...])
        y_ref[...] = _mm((a * jax.nn.sigmoid(a) * g).astype(BF16), w2b[...])

    @pl.when(b >= used)
    def _():
        y_ref[...] = jnp.zeros(y_ref.shape, F32)


def _experts(blk, xr, w1, w3, w2):
    n_rows, d = xr.shape
    nb = n_rows // MOE_BLOCK
    wspec = lambda s: pl.BlockSpec((1,) + s, lambda b, blk: (blk[0, b], 0, 0))
    return pl.pallas_call(
        _experts_kernel,
        grid_spec=pltpu.PrefetchScalarGridSpec(
            num_scalar_prefetch=1,
            grid=(nb,),
            in_specs=[pl.BlockSpec((MOE_BLOCK, d), lambda b, blk: (b, 0)),
                      wspec((d, D_EXPERT)), wspec((d, D_EXPERT)), wspec((D_EXPERT, d))],
            out_specs=pl.BlockSpec((MOE_BLOCK, d), lambda b, blk: (b, 0)),
            scratch_shapes=[pltpu.VMEM((d, D_EXPERT), BF16), pltpu.VMEM((d, D_EXPERT), BF16),
                            pltpu.VMEM((D_EXPERT, d), BF16)]),
        out_shape=jax.ShapeDtypeStruct((n_rows, d), F32),
        compiler_params=_params(("arbitrary",)),
        name="experts",
    )(blk, xr, w1, w3, w2)


def _combine_kernel(ids_ref, ids_next_ref, cnt_ref, x2_ref, rw_ref, g_ref, yr_ref, o_ref, y_ref, seg_ref, sem):
    step = pl.program_id(0)
    slot = step % 2

    @pl.when(step == 0)
    def _():
        _segment_starts(cnt_ref, seg_ref)
        _start_row_copies(ids_ref, seg_ref, yr_ref, y_ref.at[0], sem.at[0], gather=True)

    @pl.when(step + 1 < pl.num_programs(0))
    def _():
        _start_row_copies(ids_next_ref, seg_ref, yr_ref, y_ref.at[1 - slot], sem.at[1 - slot], gather=True)

    for k in range(2):
        _wait_row_copies(yr_ref, y_ref.at[slot, k], sem.at[slot])
    rw = rw_ref[...]
    x3 = x2_ref[...] + (y_ref[slot, 0] * rw[:, 0:1] + y_ref[slot, 1] * rw[:, 1:2])
    o_ref[...] = _rms(x3, g_ref[...])


def _combine(ids, cnt, x2, rw, g, yr):
    n, d = x2.shape
    nt = n // TILE
    row = lambda w: pl.BlockSpec((TILE, w), lambda i: (i, 0))
    return pl.pallas_call(
        _combine_kernel,
        grid=(nt,),
        in_specs=[pl.BlockSpec((1, 8, TILE), lambda i: (i, 0, 0), memory_space=pltpu.SMEM),
                  pl.BlockSpec((1, 8, TILE), lambda i: (jnp.minimum(i + 1, nt - 1), 0, 0),
                               memory_space=pltpu.SMEM),
                  pl.BlockSpec(memory_space=pltpu.SMEM),
                  row(d), row(LANES), pl.BlockSpec((1, d), lambda i: (0, 0)),
                  pl.BlockSpec(memory_space=pl.ANY)],
        out_specs=row(d),
        out_shape=jax.ShapeDtypeStruct((n, d), F32),
        scratch_shapes=[pltpu.VMEM((2, 2, TILE, d), F32), pltpu.SMEM((N_EXPERTS,), I32),
                        pltpu.SemaphoreType.DMA((2,))],
        compiler_params=_params(("arbitrary",)),
        name="combine",
    )(ids, ids, cnt, x2, rw, g, yr)


def _t5_bucket(dist):
    n = jnp.maximum(dist, 0)
    max_exact = REL_BUCKETS // 2
    nf = jnp.maximum(n, 1).astype(F32)
    large = max_exact + (jnp.log(nf / max_exact) / math.log(REL_MAX_DIST / max_exact)
                         * (REL_BUCKETS - max_exact)).astype(I32)
    large = jnp.minimum(large, REL_BUCKETS - 1)
    return jnp.where(n < max_exact, n, large)


def _bias_tables(tab):
    r = jnp.arange(TILE)[:, None]
    c = jnp.arange(TILE)[None, :]
    rel = tab - tab[:, REL_BUCKETS - 1:]

    def tile(dist):
        onehot = (_t5_bucket(dist)[..., None] == jnp.arange(REL_BUCKETS)).astype(F32)
        return jnp.einsum('rcb,hb->hrc', onehot, rel, precision=lax.Precision.HIGHEST)

    own = jnp.where(r >= c, tile(r - c), NEG)
    prev = tile(r - c + TILE)
    near = jnp.stack([jnp.zeros_like(prev), prev, own], axis=1).astype(F32)
    return near.swapaxes(-1, -2)


def _block_mean_slabs(kmean, bn, nq):
    km = kmean.reshape(bn, nq, A_HEADS, HEAD_DIM).transpose(0, 2, 1, 3)
    pad = ((0, 0), (0, 0), (HEAD_DIM, LANES - HEAD_DIM - nq), (0, LANES - HEAD_DIM))
    return jnp.pad(km, pad).astype(BF16)


def _pad_cols(w, width):
    return jnp.pad(w, ((0, 0), (0, width - w.shape[1])))


def kernel(x, mem, rel_bias, final_norm, norm_mix, w_in, ckv_norm, w_uk, w_uv, w_oa, w_ob,
           w_out, norm_x, mem_norm, wq_x, wk_x, wv_x, wo_x, norm_moe, w_group, b_group,
           w_router, b_router, w1, w3, w2):
    bn, t, d = x.shape
    n = bn * t
    nq = t // TILE
    assert t % TILE == 0 and nq <= LANES - HEAD_DIM and norm_mix.shape[0] == 1
    assert nq % (FAR_GROUP // 4) == 0
    near_a = _bias_tables(rel_bias[:, :A_HEADS].T)
    near_b = _bias_tables(rel_bias[:, A_HEADS:].T)
    row = lambda v: v.reshape(1, -1).astype(F32)

    wi = w_in[0]
    wp = jnp.concatenate(
        [wi[:, OFF_AQ:OFF_IK], jnp.tile(wi[:, OFF_IK:OFF_IW], (1, LANES // IDX_DIM)),
         _pad_cols(wi[:, OFF_IW:OFF_GA], LANES)], axis=1).astype(BF16)
    wkv = jnp.concatenate([w_uk[0], w_uv[0]], axis=1).astype(BF16)
    xf = x.reshape(n, d)
    qa, ka, va, qb, kb, vb, qi, ki4, widx, kmean = _proj_in(
        xf, row(norm_mix[0]), wp, row(ckv_norm[0]), wkv, nq)

    b3 = lambda a: a.reshape(bn, t, a.shape[-1])
    attn_a = _moba(b3(qa), b3(ka), va.reshape(bn, nq, W_A, TILE), _block_mean_slabs(kmean, bn, nq), near_a)
    attn_b = _dsa(b3(qi), b3(ki4), b3(widx), b3(qb), b3(kb), vb.reshape(bn, nq, W_B, TILE), near_b)

    kvm = _mem_kv(mem, row(mem_norm[0]), jnp.concatenate([wk_x[0], wv_x[0]], axis=1).astype(BF16))
    wrt = _pad_cols(jnp.concatenate([w_group[0], w_router[0]], axis=1), LANES).astype(F32)
    wrt_hi = wrt.astype(BF16)
    wrt = jnp.concatenate([wrt_hi, (wrt - wrt_hi.astype(F32)).astype(BF16)], axis=1)
    brt = _pad_cols(jnp.concatenate([b_group[0], b_router[0]]).reshape(1, -1), LANES).astype(F32)
    weights = (row(norm_mix[0]), wi[:, OFF_GA:IN_COLS].astype(BF16), w_oa[0].astype(BF16),
               w_ob[0].astype(BF16), w_out[0].astype(BF16), row(norm_x[0]), wq_x[0].astype(BF16),
               wo_x[0].astype(BF16), row(norm_moe[0]), wrt, brt)
    x2, hm, rw, ids, cnt = _post(xf, attn_a.reshape(n, W_A), attn_b.reshape(n, W_B), kvm, t // TILE, weights)

    n_rows = 2 * n + N_EXPERTS * MOE_BLOCK
    xr, blk = _dispatch(ids, cnt, hm, n_rows)
    yr = _experts(blk, xr, w1[0], w3[0], w2[0])
    out = _combine(ids, cnt, x2, rw, row(final_norm), yr)
    return out.reshape(bn, t, d)
```

```python
import functools
import math

import jax
import jax.numpy as jnp
from jax import lax
from jax.experimental import pallas as pl
from jax.experimental.pallas import tpu as pltpu

D_MODEL = 1024
HEAD_DIM = 64
A_HEADS = 8
B_HEADS = 8
W_A = A_HEADS * HEAD_DIM
W_B = B_HEADS * HEAD_DIM
MOBA_BLOCK = 256
MOBA_TOPK = 3
DSA_KV_RANK = 256
IDX_HEADS = 8
IDX_DIM = 32
DSA_TOPK = 256
REL_BUCKETS = 32
REL_MAX_DIST = 128
X_HEADS = 4
W_X = X_HEADS * HEAD_DIM
N_GROUPS = 4
EXPERTS_PER_GROUP = 8
N_EXPERTS = N_GROUPS * EXPERTS_PER_GROUP
D_EXPERT = 512
MOE_BLOCK = 256
EPS = 1e-6
NEG = -1e30

OFF_AQ = 0
OFF_AK = OFF_AQ + W_A
OFF_AV = OFF_AK + W_A
OFF_BQ = OFF_AV + W_A
OFF_CKV = OFF_BQ + W_B
OFF_IQ = OFF_CKV + DSA_KV_RANK
OFF_IK = OFF_IQ + IDX_HEADS * IDX_DIM
OFF_IW = OFF_IK + IDX_DIM
OFF_GA = OFF_IW + IDX_HEADS
OFF_GB = OFF_GA + D_MODEL
IN_COLS = OFF_GB + D_MODEL

LANES = 128
TILE = MOBA_BLOCK
assert TILE == MOE_BLOCK
SEARCH_STEPS_MAX = 400
NO_CUT = 2 ** 30
VMEM_LIMIT = 56 * 1024 * 1024
BF16 = jnp.bfloat16
F32 = jnp.float32
I32 = jnp.int32

P_QA, P_KA, P_VA, P_QB = 0, 512, 1024, 1536
P_CKV, P_IQ, P_IK4, P_IW = 2048, 2304, 2560, 2688
P_COLS = 2816

R_EXP = N_GROUPS
MOE_SHIFT = MOE_BLOCK.bit_length() - 1
assert 1 << MOE_SHIFT == MOE_BLOCK


def _nt(a, b):
    return lax.dot_general(a, b, (((1,), (1,)), ((), ())), preferred_element_type=F32)


def _mm(a, b):
    return jnp.dot(a, b, preferred_element_type=F32)


def _rms(xf, g):
    return xf * lax.rsqrt(jnp.mean(xf * xf, axis=-1, keepdims=True) + EPS) * g


def _params(sem, vmem=VMEM_LIMIT):
    return pltpu.CompilerParams(dimension_semantics=sem, vmem_limit_bytes=vmem)


def _mem_kv_kernel(mem_ref, g_ref, w_ref, o_ref):
    m = _rms(mem_ref[0], g_ref[...]).astype(BF16)
    o_ref[0] = _mm(m, w_ref[...]).astype(BF16)


def _mem_kv(mem, g, wkv):
    bn, ml, d = mem.shape
    return pl.pallas_call(
        _mem_kv_kernel,
        grid=(bn,),
        in_specs=[pl.BlockSpec((1, ml, d), lambda b: (b, 0, 0)),
                  pl.BlockSpec((1, d), lambda b: (0, 0)),
                  pl.BlockSpec((d, 2 * W_X), lambda b: (0, 0))],
        out_specs=pl.BlockSpec((1, ml, 2 * W_X), lambda b: (b, 0, 0)),
        out_shape=jax.ShapeDtypeStruct((bn, ml, 2 * W_X), BF16),
        compiler_params=_params(("arbitrary",)),
        name="mem_kv",
    )(mem, g, wkv)


def _head_slabs(zp, extra):
    lane = lax.broadcasted_iota(I32, zp.shape, 1)
    low = lane < HEAD_DIM
    return jnp.where(low, zp, extra), jnp.where(low, pltpu.roll(zp, HEAD_DIM, 1), extra)


def _proj_in_kernel(blocks_per_seq, x_ref, g_ref, w_ref, cg_ref, wkv_ref,
                    qa_ref, ka_ref, va_ref, qb_ref, kb_ref, vb_ref, qi_ref, ki_ref, wi_ref, km_ref):
    h = _rms(x_ref[...], g_ref[...]).astype(BF16)
    z = _mm(h, w_ref[...])
    scale = HEAD_DIM ** -0.5
    zk = z[:, P_KA:P_VA]
    km_ref[0] = jnp.mean(zk, axis=0, keepdims=True)
    lane = lax.broadcasted_iota(I32, (TILE, LANES), 1)
    blk = pl.program_id(0) % blocks_per_seq
    onehot = jnp.where(lane == HEAD_DIM + blk, 1.0, 0.0)
    zero = jnp.zeros((TILE, LANES), F32)
    for p_idx in range(A_HEADS // 2):
        lo = p_idx * LANES
        qe, qo = _head_slabs(z[:, P_QA + lo:P_QA + lo + LANES] * scale, zero)
        ke, ko = _head_slabs(zk[:, lo:lo + LANES], onehot)
        qa_ref[:, 2 * lo:2 * lo + 2 * LANES] = jnp.concatenate([qe, qo], axis=1).astype(BF16)
        ka_ref[:, 2 * lo:2 * lo + 2 * LANES] = jnp.concatenate([ke, ko], axis=1).astype(BF16)
    va_ref[0] = z[:, P_VA:P_QB].T.astype(BF16)
    qb_ref[...] = (z[:, P_QB:P_CKV] * scale).astype(BF16)
    ckv = _rms(z[:, P_CKV:P_IQ], cg_ref[...]).astype(BF16)
    kv = _mm(ckv, wkv_ref[...])
    kb_ref[...] = kv[:, :W_B].astype(BF16)
    vb_ref[0] = kv[:, W_B:].T.astype(BF16)
    qi_ref[...] = z[:, P_IQ:P_IK4].astype(BF16)
    ki_ref[...] = z[:, P_IK4:P_IW].astype(BF16)
    wi_ref[...] = z[:, P_IW:P_COLS] * ((IDX_HEADS ** -0.5) * (IDX_DIM ** -0.5))


def _proj_in(xf, g, wp, cg, wkv, blocks_per_seq):
    n, d = xf.shape
    nt = n // TILE
    row = lambda w: pl.BlockSpec((TILE, w), lambda i: (i, 0))
    full = lambda a: pl.BlockSpec(a.shape, lambda i: (0,) * a.ndim)
    outs = [(2 * W_A, BF16, False), (2 * W_A, BF16, False), (W_A, BF16, True), (W_B, BF16, False),
            (W_B, BF16, False), (W_B, BF16, True), (IDX_HEADS * IDX_DIM, BF16, False), (LANES, BF16, False),
            (LANES, F32, False)]
    blocked = lambda w: pl.BlockSpec((1, w, TILE), lambda i: (i, 0, 0))
    return pl.pallas_call(
        functools.partial(_proj_in_kernel, blocks_per_seq),
        grid=(nt,),
        in_specs=[row(d), full(g), full(wp), full(cg), full(wkv)],
        out_specs=[blocked(w) if tr else row(w) for w, _, tr in outs]
        + [pl.BlockSpec((1, 1, W_A), lambda i: (i, 0, 0))],
        out_shape=[jax.ShapeDtypeStruct((nt, w, TILE) if tr else (n, w), t) for w, t, tr in outs]
        + [jax.ShapeDtypeStruct((nt, 1, W_A), F32)],
        compiler_params=_params(("arbitrary",)),
        name="proj_in",
    )(xf, g, wp, cg, wkv)


SUB = TILE // 8
LOG2E = math.log2(math.e)
FAR_GROUP = 16


def _sm_init(mx_ref, acc_ref):
    mx_ref[...] = jnp.full(mx_ref.shape, NEG, F32)
    acc_ref[...] = jnp.zeros(acc_ref.shape, F32)


def _sm_logits(hd, slot, s, sc_ref, mx_ref):
    s = s * LOG2E
    sc_ref[hd, slot] = s
    mx_ref[hd] = jnp.maximum(mx_ref[hd], jnp.max(s.reshape(SUB, 8, TILE), axis=0))


def _sm_rowmax(mx_ref):
    for hd in range(2):
        mx_ref[hd] = jnp.broadcast_to(jnp.max(mx_ref[hd], axis=0, keepdims=True), (8, TILE))


def _sm_probs(hd, slots, sc_ref, mx_ref):
    m = mx_ref[hd][None]
    ps = [jnp.exp2((sc_ref[hd, sl].reshape(SUB, 8, TILE) - m).reshape(TILE, TILE).astype(BF16))
          for sl in slots]
    return ps[0] if len(ps) == 1 else jnp.concatenate(ps, axis=0)


def _sm_accumulate(hd, pb, v, acc_ref):
    row = lax.broadcasted_iota(I32, v.shape, 0)
    own = (row < HEAD_DIM) if hd == 0 else (row >= HEAD_DIM)
    acc_ref[hd] = acc_ref[hd] + _mm(jnp.where(own, v, jnp.ones_like(v)), pb)


def _sm_values(hd, slots, v, sc_ref, mx_ref, acc_ref):
    _sm_accumulate(hd, _sm_probs(hd, slots, sc_ref, mx_ref), v, acc_ref)


def _far_plan(n_blocks):
    quarter = FAR_GROUP // 4
    padded = (n_blocks + quarter - 1) // quarter * quarter
    full = padded // FAR_GROUP
    rest = padded - full * FAR_GROUP
    has_half = rest >= FAR_GROUP // 2
    half_base = full * FAR_GROUP
    quarter_base = half_base + jnp.where(has_half, FAR_GROUP // 2, 0)
    return full, has_half, half_base, rest % (FAR_GROUP // 2) != 0, quarter_base


def _sm_far_loop(plan, group_fn):
    full, has_half, half_base, has_quarter, quarter_base = plan
    lax.fori_loop(0, full, lambda g, c: group_fn(FAR_GROUP * g, FAR_GROUP) or c, 0)
    pl.when(has_half)(lambda: group_fn(half_base, FAR_GROUP // 2))
    pl.when(has_quarter)(lambda: group_fn(quarter_base, FAR_GROUP // 4))


def _sm_far_values(plan, vals, sc_ref, mx_ref, acc_ref):
    def group(j0, n):
        v = vals(j0, n)
        for hd in range(2):
            _sm_values(hd, [j0 + u for u in range(n)], v, sc_ref, mx_ref, acc_ref)

    _sm_far_loop(plan, group)


def _sm_output(acc_ref):
    num = jnp.concatenate([acc_ref[0, :HEAD_DIM], acc_ref[1, HEAD_DIM:]], axis=0)
    den = jnp.concatenate([acc_ref[0, HEAD_DIM:], acc_ref[1, :HEAD_DIM]], axis=0)
    return (num / den).T


def _split_pair(qp):
    lane = lax.broadcasted_iota(I32, qp.shape, 1)
    zero = jnp.zeros_like(qp)
    return jnp.where(lane < HEAD_DIM, qp, zero), jnp.where(lane >= HEAD_DIM, qp, zero)


def _moba_kernel(q_ref, k_ref, v_ref, km_ref, bias_ref, o_ref, sc_ref, mx_ref, acc_ref):
    i = pl.program_id(2)
    nq = pl.num_programs(2)
    n_slot = LANES - HEAD_DIM
    blk = lax.broadcasted_iota(I32, (n_slot, TILE), 0)
    in_blk = blk < nq
    past = blk < i
    no_mask = jnp.zeros((HEAD_DIM, TILE), F32)

    qm = []
    for hd in range(2):
        q = q_ref[0, :, hd * LANES:(hd + 1) * LANES]
        g = jnp.where(past, _nt(km_ref[0, hd], q)[HEAD_DIM:], -jnp.inf)
        sel = jnp.zeros((n_slot, TILE), jnp.bool_)
        for _ in range(MOBA_TOPK):
            top = jnp.max(g, axis=0, keepdims=True)
            hit = jnp.logical_and(g == top, in_blk)
            first = jnp.min(jnp.where(hit, blk, n_slot), axis=0, keepdims=True)
            pick = blk == first
            sel = jnp.logical_or(sel, jnp.logical_and(pick, past))
            g = jnp.where(pick, -jnp.inf, g)
        addm = jnp.where(sel, 0.0, NEG)
        mask = jnp.where(past, addm, jnp.where(jnp.logical_and(in_blk, blk > i), NEG, 0.0))
        qm.append(q + jnp.concatenate([no_mask, mask], axis=0).T.astype(BF16))

    def keys(j, hd):
        off = pl.multiple_of(j * TILE, TILE)
        return k_ref[0, pl.ds(off, TILE), hd * LANES:(hd + 1) * LANES]

    plan = _far_plan(i + 1)
    _sm_init(mx_ref, acc_ref)

    def logits(j0, n):
        for u in range(n):
            j = j0 + u
            tile = _near_bias_index(j, i)
            for hd in range(2):
                _sm_logits(hd, j, _nt(keys(j, hd), qm[hd]) + bias_ref[hd, tile], sc_ref, mx_ref)

    _sm_far_loop(plan, logits)
    _sm_rowmax(mx_ref)

    def vals(j, n):
        return jnp.concatenate([v_ref[0, j + u] for u in range(n)], axis=1)

    _sm_far_values(plan, vals, sc_ref, mx_ref, acc_ref)
    o_ref[0] = _sm_output(acc_ref).astype(BF16)


def _near_bias_index(j, i):
    return jnp.clip(j - i + 2, 0, 2)


def _sm_scratch(n_slots):
    return [pltpu.VMEM((2, n_slots, TILE, TILE), F32),
            pltpu.VMEM((2, 8, TILE), F32),
            pltpu.VMEM((2, LANES, TILE), F32)]


def _moba(qa, ka, va_t, kmp, bias_near):
    bn, nq = va_t.shape[:2]
    t = nq * TILE
    npair = A_HEADS // 2
    return pl.pallas_call(
        _moba_kernel,
        grid=(bn, npair, nq),
        in_specs=[pl.BlockSpec((1, TILE, 2 * LANES), lambda b, p, i: (b, i, p)),
                  pl.BlockSpec((1, t, 2 * LANES), lambda b, p, i: (b, 0, p)),
                  pl.BlockSpec((1, nq, LANES, TILE), lambda b, p, i: (b, 0, p, 0)),
                  pl.BlockSpec((1, 2, LANES, LANES), lambda b, p, i: (b, p, 0, 0)),
                  pl.BlockSpec((2, 3, TILE, TILE), lambda b, p, i: (p, 0, 0, 0))],
        out_specs=pl.BlockSpec((1, TILE, LANES), lambda b, p, i: (b, i, p)),
        out_shape=jax.ShapeDtypeStruct((bn, t, W_A), BF16),
        scratch_shapes=_sm_scratch(nq),
        compiler_params=_params(("arbitrary", "arbitrary", "arbitrary")),
        name="moba",
    )(qa, ka, va_t, kmp, bias_near)


def _dsa_select(i, qi_ref, ki_ref, wi_ref, s_ref, wb_ref):
    nch = i + 1
    sub = TILE // 8
    lane = lax.broadcasted_iota(I32, (TILE, LANES), 1)
    key_in = (lax.broadcasted_iota(I32, (sub, 8, TILE), 0) * 8 + lax.broadcasted_iota(I32, (sub, 8, TILE), 1))
    qpos = i * TILE + lax.broadcasted_iota(I32, (sub, 8, TILE), 2)

    def rows8(x):
        return jnp.broadcast_to(x, (8, TILE))[None]

    w_t = wi_ref[0].T
    qi = qi_ref[0]
    per_group = LANES // IDX_DIM
    qh = []
    for h in range(IDX_HEADS):
        g, r = divmod(h, per_group)
        qg = qi[:, g * LANES:(g + 1) * LANES]
        keep = jnp.logical_and(lane >= r * IDX_DIM, lane < (r + 1) * IDX_DIM)
        qh.append(jnp.where(keep, qg, jnp.zeros_like(qg)))
        wb_ref[h] = jnp.broadcast_to(w_t[h:h + 1], (8, TILE))

    def score(c, carry):
        off = pl.multiple_of(c * TILE, TILE)
        kc = ki_ref[0, pl.ds(off, TILE), :]
        acc = jnp.zeros((sub, 8, TILE), F32)
        for h in range(IDX_HEADS):
            acc = acc + jnp.maximum(_nt(kc, qh[h]), 0.0).reshape(sub, 8, TILE) * wb_ref[h][None]
        s_ref[c] = jnp.where(c * TILE + key_in <= qpos, acc, -jnp.inf).reshape(TILE, TILE)
        return carry

    n_pair = (nch + 1) // 2

    def pairs(step, init):
        return lax.fori_loop(0, n_pair, lambda g, acc: step(2 * g + 1, step(2 * g, acc)), init)

    pairs(score, 0)

    def fold(init, step):
        return pairs(lambda c, acc: step(acc, s_ref[c].reshape(sub, 8, TILE), c), init)

    def count(pred):
        def step(acc, blk, c):
            return acc + jnp.sum(jnp.where(pred(blk, c * TILE + key_in), 1.0, 0.0), axis=0)
        return jnp.sum(fold(jnp.zeros((8, TILE), F32), step), axis=0, keepdims=True)

    wide = rows8

    def stats(acc, blk, c):
        top, low, pos, nonneg = acc
        top = jnp.maximum(top, jnp.max(blk, axis=0))
        low = jnp.minimum(low, jnp.min(jnp.where(blk == -jnp.inf, jnp.inf, blk), axis=0))
        pos = pos + jnp.sum(jnp.where(blk > 0.0, 1.0, 0.0), axis=0)
        nonneg = nonneg + jnp.sum(jnp.where(blk >= 0.0, 1.0, 0.0), axis=0)
        return top, low, pos, nonneg

    zeros8 = jnp.zeros((8, TILE), F32)
    acc = fold((zeros8 - jnp.inf, zeros8 + jnp.inf, zeros8, zeros8), stats)
    hi = jnp.max(acc[0], axis=0, keepdims=True)
    lo = jnp.min(acc[1], axis=0, keepdims=True)
    n_pos = jnp.sum(acc[2], axis=0, keepdims=True)
    n_nonneg = jnp.sum(acc[3], axis=0, keepdims=True)

    n_lo = (i * TILE + 1 + lax.broadcasted_iota(I32, (1, TILE), 1)).astype(F32)
    hi = hi + jnp.maximum(jnp.abs(hi) * 2.0 ** -20, 1e-30)
    n_hi = jnp.zeros((1, TILE), F32)
    at_zero = jnp.logical_and(n_pos < DSA_TOPK, n_nonneg >= DSA_TOPK)
    from_zero = n_pos >= DSA_TOPK
    lo, n_lo = jnp.where(from_zero, 0.0, lo), jnp.where(from_zero, n_nonneg, n_lo)
    to_zero = n_nonneg < DSA_TOPK
    hi, n_hi = jnp.where(to_zero, 0.0, hi), jnp.where(to_zero, n_nonneg, n_hi)

    def search_more(state):
        return jnp.logical_and(jnp.min(state[7]) == 0, state[8] < SEARCH_STEPS_MAX)

    def search(state):
        lo, hi, n_lo, n_hi, thr, n_ge, n_gt, done, it = state
        mid = 0.5 * lo + 0.5 * hi
        closed = jnp.logical_or(mid <= lo, mid >= hi)
        n = count(lambda half, kpos: half >= wide(mid))
        stop = jnp.logical_or(closed, n == DSA_TOPK)
        settle = jnp.logical_and(done == 0, stop)
        thr = jnp.where(settle, jnp.where(closed, lo, mid), thr)
        n_ge = jnp.where(settle, jnp.where(closed, n_lo, n), n_ge)
        n_gt = jnp.where(settle, jnp.where(closed, n_hi, 0.0), n_gt)
        go = jnp.logical_and(done == 0, jnp.logical_not(stop))
        up = jnp.logical_and(go, n > DSA_TOPK)
        down = jnp.logical_and(go, n < DSA_TOPK)
        return (jnp.where(up, mid, lo), jnp.where(down, mid, hi), jnp.where(up, n, n_lo),
                jnp.where(down, n, n_hi), thr, n_ge, n_gt, jnp.where(stop, 1, done), it + 1)

    state = (lo, hi, n_lo, n_hi, jnp.zeros((1, TILE), F32), n_nonneg, n_pos, at_zero.astype(I32), jnp.int32(0))
    state = lax.while_loop(search_more, lambda st: search(search(st)), state)
    thr, n_ge, n_gt = wide(state[4]), state[5], state[6]
    need = DSA_TOPK - n_gt

    cut_bits = max(1, (s_ref.shape[0] * TILE - 1).bit_length())

    def cut_step(b, cut):
        cand = cut | (jnp.int32(1) << (cut_bits - 1 - b))
        cw = wide(cand)
        n = count(lambda half, kpos: jnp.logical_and(half == thr, kpos < cw))
        return jnp.where(n < need, cand, cut)

    cut = lax.cond(jnp.max(n_ge) > DSA_TOPK,
                   lambda: lax.fori_loop(0, cut_bits, cut_step, jnp.zeros((1, TILE), I32)),
                   lambda: jnp.full((1, TILE), NO_CUT, I32))
    cut = wide(jnp.where(n_ge > DSA_TOPK, cut, NO_CUT))

    def to_mask(c, carry):
        blk = s_ref[c].reshape(sub, 8, TILE)
        kpos = c * TILE + key_in
        keep = jnp.logical_or(blk > thr, jnp.logical_and(blk == thr, kpos <= cut))
        s_ref[c] = jnp.where(jnp.logical_and(keep, kpos <= qpos), 0.0, NEG).reshape(TILE, TILE)
        return carry

    pairs(to_mask, 0)


def _dsa_kernel(qi_ref, ki_ref, wi_ref, q_ref, k_ref, v_ref, bias_ref, o_ref,
                s_ref, wb_ref, sc_ref, mx_ref, acc_ref):
    i = pl.program_id(1)
    pair = pl.program_id(2)

    @pl.when(jnp.logical_and(pair == 0, i == 0))
    def _():
        key = lax.broadcasted_iota(I32, (TILE, TILE), 0)
        query = lax.broadcasted_iota(I32, (TILE, TILE), 1)
        s_ref[0] = jnp.where(key <= query, 0.0, NEG)

    @pl.when(jnp.logical_and(pair == 0, i >= 1))
    def _():
        _dsa_select(i, qi_ref, ki_ref, wi_ref, s_ref, wb_ref)

    qs = _split_pair(q_ref[0])
    plan = _far_plan(jnp.maximum(i - 1, 0))
    _sm_init(mx_ref, acc_ref)

    def vals(j, n):
        return jnp.concatenate([v_ref[0, j + u] for u in range(n)], axis=1)

    def logits(j, slot, msk, near=None):
        k_j = k_ref[0, pl.ds(pl.multiple_of(j * TILE, TILE), TILE), :]
        for hd in range(2):
            s = _nt(k_j, qs[hd]) + msk
            if near is not None:
                s = s + bias_ref[hd, near]
            _sm_logits(hd, slot, s, sc_ref, mx_ref)

    def far_logits(j0, n):
        for u in range(n):
            j = j0 + u
            logits(j, j, jnp.where(j < i - 1, s_ref[jnp.minimum(j, i)], NEG))

    _sm_far_loop(plan, far_logits)
    nq = pl.num_programs(1)
    j_prev = jnp.maximum(i - 1, 0)
    logits(j_prev, nq, jnp.where(i >= 1, s_ref[j_prev], NEG), near=1)
    logits(i, nq + 1, s_ref[i], near=2)
    _sm_rowmax(mx_ref)

    _sm_far_values(plan, vals, sc_ref, mx_ref, acc_ref)
    v_near = jnp.concatenate([vals(j_prev, 1), vals(i, 1)], axis=1)
    for hd in range(2):
        _sm_values(hd, [nq, nq + 1], v_near, sc_ref, mx_ref, acc_ref)
    o_ref[0] = _sm_output(acc_ref).astype(BF16)


def _dsa(qi, ki4, wi, qb, kb, vb_t, bias_near):
    bn, t, _ = qb.shape
    nq = t // TILE
    assert nq % 2 == 0
    npair = B_HEADS // 2
    qrow = lambda w: pl.BlockSpec((1, TILE, w), lambda b, i, p: (b, i, 0))
    return pl.pallas_call(
        _dsa_kernel,
        grid=(bn, nq, npair),
        in_specs=[qrow(IDX_HEADS * IDX_DIM),
                  pl.BlockSpec((1, t, LANES), lambda b, i, p: (b, 0, 0), pipeline_mode=pl.Buffered(1)),
                  qrow(LANES),
                  pl.BlockSpec((1, TILE, LANES), lambda b, i, p: (b, i, p)),
                  pl.BlockSpec((1, t, LANES), lambda b, i, p: (b, 0, p)),
                  pl.BlockSpec((1, nq, LANES, TILE), lambda b, i, p: (b, 0, p, 0)),
                  pl.BlockSpec((2, 3, TILE, TILE), lambda b, i, p: (p, 0, 0, 0))],
        out_specs=pl.BlockSpec((1, TILE, LANES), lambda b, i, p: (b, i, p)),
        out_shape=jax.ShapeDtypeStruct((bn, t, W_B), BF16),
        scratch_shapes=[pltpu.VMEM((nq, TILE, TILE), F32), pltpu.VMEM((IDX_HEADS, 8, TILE), F32)]
        + _sm_scratch(nq + 2),
        compiler_params=_params(("arbitrary", "arbitrary", "arbitrary")),
        name="dsa",
    )(qi, ki4, wi, qb, kb, vb_t, bias_near)


def _post_kernel(x_ref, aa_ref, ab_ref, gm_ref, wg_ref, woa_ref, wob_ref, wout_ref,
                 gx_ref, wq_ref, kvm_ref, wox_ref, gmoe_ref, wrt_ref, brt_ref,
                 x2_ref, hm_ref, rw_ref, ids_ref, cnt_ref, carry_ref):
    step = pl.program_id(0)
    xf = x_ref[...]
    h = _rms(xf, gm_ref[...]).astype(BF16)
    gates = jax.nn.sigmoid(_mm(h, wg_ref[...]))
    oa = _mm(aa_ref[...], woa_ref[...])
    ob = _mm(ab_ref[...], wob_ref[...])
    mrg = gates[:, :D_MODEL] * oa + gates[:, D_MODEL:] * ob
    x1 = xf + _mm(mrg.astype(BF16), wout_ref[...])

    hx = _rms(x1, gx_ref[...]).astype(BF16)
    q = (_mm(hx, wq_ref[...]) * (HEAD_DIM ** -0.5)).astype(BF16)
    kvm = kvm_ref[0]
    lane = lax.broadcasted_iota(I32, (TILE, LANES), 1)
    outs = []
    for p_idx in range(X_HEADS // 2):
        lo = p_idx * LANES
        qs = _split_pair(q[:, lo:lo + LANES])
        km = kvm[:, lo:lo + LANES]
        vm = kvm[:, W_X + lo:W_X + lo + LANES]
        o = []
        for hd in range(2):
            s = _nt(qs[hd], km)
            p = jnp.exp(s - jnp.max(s, axis=1, keepdims=True))
            o.append(_mm(p.astype(BF16), vm) / jnp.sum(p, axis=1, keepdims=True))
        outs.append(jnp.where(lane < HEAD_DIM, o[0], o[1]))
    xo = jnp.concatenate(outs, axis=1).astype(BF16)
    x2 = x1 + _mm(xo, wox_ref[...])
    x2_ref[...] = x2

    hm = _rms(x2, gmoe_ref[...])
    hm_ref[...] = hm
    hm_hi = hm.astype(BF16)
    hm_lo = (hm - hm_hi.astype(F32)).astype(BF16)
    hi_part = _mm(hm_hi, wrt_ref[...])
    logits = (hi_part[:, :LANES] + hi_part[:, LANES:]) + _mm(hm_lo, wrt_ref[:, :LANES]) + brt_ref[...]
    big = LANES

    def argmax(v):
        top = jnp.max(v, axis=1, keepdims=True)
        return top, jnp.min(jnp.where(v == top, lane, big), axis=1, keepdims=True)

    is_grp = lane < R_EXP
    gtop, gsel = argmax(jnp.where(is_grp, logits, -jnp.inf))
    gw = 1.0 / jnp.sum(jnp.where(is_grp, jnp.exp(logits - gtop), 0.0), axis=1, keepdims=True)
    first = R_EXP + gsel * EXPERTS_PER_GROUP
    inside = jnp.logical_and(lane >= first, lane < first + EXPERTS_PER_GROUP)
    within = jnp.where(inside, logits, -jnp.inf)
    v0, i0 = argmax(within)
    v1, i1 = argmax(jnp.where(lane == i0, -jnp.inf, within))
    e1 = jnp.exp(v1 - v0)
    w0 = gw * (1.0 / (1.0 + e1))
    w1 = gw * (e1 / (1.0 + e1))
    rw_ref[...] = jnp.where(lane == 0, w0, jnp.where(lane == 1, w1, 0.0))

    @pl.when(step == 0)
    def _():
        carry_ref[...] = jnp.zeros(carry_ref.shape, F32)

    hit0 = lane == i0
    hit1 = lane == i1
    onehot = jnp.where(jnp.logical_or(hit0, hit1), 1.0, 0.0)
    tri = (lax.broadcasted_iota(I32, (TILE, TILE), 1) < lax.broadcasted_iota(I32, (TILE, TILE), 0))
    base = carry_ref[...] + _mm(tri.astype(BF16), onehot.astype(BF16))
    r0 = jnp.sum(jnp.where(hit0, base, 0.0), axis=1, keepdims=True)
    r1 = jnp.sum(jnp.where(hit1, base, 0.0), axis=1, keepdims=True)
    total = carry_ref[...] + jnp.sum(onehot, axis=0, keepdims=True)
    carry_ref[...] = total
    cnt_ref[...] = total.astype(I32)
    slab = jnp.where(lane == 0, (i0 - R_EXP).astype(F32),
                     jnp.where(lane == 1, (i1 - R_EXP).astype(F32),
                               jnp.where(lane == 2, r0, jnp.where(lane == 3, r1, 0.0))))
    ids_ref[0] = slab.T[:8].astype(I32)


def _post(xf, aa, ab, kvm, tiles_per_batch, weights):
    n, d = xf.shape
    nt = n // TILE
    row = lambda w: pl.BlockSpec((TILE, w), lambda i: (i, 0))
    full = lambda a: pl.BlockSpec(a.shape, lambda i: (0,) * a.ndim)
    gm, wg, woa, wob, wout, gx, wq, wox, gmoe, wrt, brt = weights
    return pl.pallas_call(
        _post_kernel,
        grid=(nt,),
        in_specs=[row(d), row(W_A), row(W_B), full(gm), full(wg), full(woa), full(wob), full(wout),
                  full(gx), full(wq),
                  pl.BlockSpec((1,) + kvm.shape[1:], lambda i: (i // tiles_per_batch, 0, 0)),
                  full(wox), full(gmoe), full(wrt), full(brt)],
        out_specs=[row(d), row(d), row(LANES),
                   pl.BlockSpec((1, 8, TILE), lambda i: (i, 0, 0)),
                   pl.BlockSpec((1, LANES), lambda i: (0, 0))],
        out_shape=[jax.ShapeDtypeStruct((n, d), F32), jax.ShapeDtypeStruct((n, d), F32),
                   jax.ShapeDtypeStruct((n, LANES), F32),
                   jax.ShapeDtypeStruct((nt, 8, TILE), I32),
                   jax.ShapeDtypeStruct((1, LANES), I32)],
        scratch_shapes=[pltpu.VMEM((1, LANES), F32)],
        compiler_params=_params(("arbitrary",)),
        name="post",
    )(xf, aa, ab, gm, wg, woa, wob, wout, gx, wq, kvm, wox, gmoe, wrt, brt)


def _segment_starts(cnt_ref, seg_ref):
    def body(e, acc):
        seg_ref[e] = acc
        c = cnt_ref[0, R_EXP + e]
        return acc + ((c + (MOE_BLOCK - 1)) >> MOE_SHIFT << MOE_SHIFT)
    return lax.fori_loop(0, N_EXPERTS, body, jnp.int32(0))


def _start_row_copies(ids_ref, seg_ref, rows_ref, tile_ref, sem, gather):
    for t in range(TILE):
        for k in range(2):
            dest = seg_ref[ids_ref[0, k, t]] + ids_ref[0, 2 + k, t]
            if gather:
                cp = pltpu.make_async_copy(rows_ref.at[pl.ds(dest, 1)], tile_ref.at[k, pl.ds(t, 1)], sem)
            else:
                cp = pltpu.make_async_copy(tile_ref.at[pl.ds(t, 1)], rows_ref.at[pl.ds(dest, 1)], sem)
            cp.start(priority=k)


def _wait_row_copies(rows_ref, tile_ref, sem):
    pltpu.make_async_copy(rows_ref.at[pl.ds(0, TILE)], tile_ref, sem).wait()


def _dispatch_kernel(ids_ref, cnt_ref, hm_ref, xr_in_ref, xr_ref, blk_ref, seg_ref, sem):
    del xr_in_ref
    step = pl.program_id(0)
    nb = blk_ref.shape[1] - 1

    @pl.when(step == 0)
    def _():
        used = _segment_starts(cnt_ref, seg_ref)

        def per_expert(e, last):
            c = cnt_ref[0, R_EXP + e]
            b0 = seg_ref[e] >> MOE_SHIFT
            n = (c + (MOE_BLOCK - 1)) >> MOE_SHIFT

            def fill(kk, carry):
                blk_ref[0, b0 + kk] = e
                return carry
            lax.fori_loop(0, n, fill, 0)
            return jnp.where(n > 0, e, last)
        last = lax.fori_loop(0, N_EXPERTS, per_expert, jnp.int32(0))

        def tail(b, carry):
            blk_ref[0, b] = last
            return carry
        lax.fori_loop(used >> MOE_SHIFT, nb, tail, 0)
        blk_ref[0, nb] = used >> MOE_SHIFT

    _start_row_copies(ids_ref, seg_ref, xr_ref, hm_ref, sem, gather=False)
    for _ in range(2):
        _wait_row_copies(xr_ref, hm_ref, sem)


def _dispatch(ids, cnt, hm, n_rows):
    n, d = hm.shape
    nt = n // TILE
    nb = n_rows // MOE_BLOCK
    zeros = jnp.zeros((n_rows, d), F32)
    return pl.pallas_call(
        _dispatch_kernel,
        grid=(nt,),
        in_specs=[pl.BlockSpec((1, 8, TILE), lambda i: (i, 0, 0), memory_space=pltpu.SMEM),
                  pl.BlockSpec(memory_space=pltpu.SMEM),
                  pl.BlockSpec((TILE, d), lambda i: (i, 0)),
                  pl.BlockSpec(memory_space=pl.ANY)],
        out_specs=[pl.BlockSpec(memory_space=pl.ANY),
                   pl.BlockSpec(memory_space=pltpu.SMEM)],
        out_shape=[jax.ShapeDtypeStruct((n_rows, d), F32),
                   jax.ShapeDtypeStruct((1, nb + 1), I32)],
        scratch_shapes=[pltpu.SMEM((N_EXPERTS,), I32), pltpu.SemaphoreType.DMA(())],
        input_output_aliases={3: 0},
        compiler_params=_params(("arbitrary",)),
        name="dispatch",
    )(ids, cnt, hm, zeros)


def _experts_kernel(blk_ref, x_ref, w1_ref, w3_ref, w2_ref, y_ref, w1b, w3b, w2b):
    b = pl.program_id(0)
    nb = pl.num_programs(0)
    e = blk_ref[0, b]
    prev = blk_ref[0, jnp.maximum(b - 1, 0)]

    @pl.when(jnp.logical_or(b == 0, e != prev))
    def _():
        w1b[...] = w1_ref[0].astype(BF16)
        w3b[...] = w3_ref[0].astype(BF16)
        w2b[...] = w2_ref[0].astype(BF16)

    used = blk_ref[0, nb]

    @pl.when(b < used)
    def _():
        xb = x_ref[...].astype(BF16)
        a = _mm(xb, w1b[...])
        g = _mm(xb, w3b[...])
        y_ref[...] = _mm((a * jax.nn.sigmoid(a) * g).astype(BF16), w2b[...])

    @pl.when(b >= used)
    def _():
        y_ref[...] = jnp.zeros(y_ref.shape, F32)


def _experts(blk, xr, w1, w3, w2):
    n_rows, d = xr.shape
    nb = n_rows // MOE_BLOCK
    wspec = lambda s: pl.BlockSpec((1,) + s, lambda b, blk: (blk[0, b], 0, 0))
    return pl.pallas_call(
        _experts_kernel,
        grid_spec=pltpu.PrefetchScalarGridSpec(
            num_scalar_prefetch=1,
            grid=(nb,),
            in_specs=[pl.BlockSpec((MOE_BLOCK, d), lambda b, blk: (b, 0)),
                      wspec((d, D_EXPERT)), wspec((d, D_EXPERT)), wspec((D_EXPERT, d))],
            out_specs=pl.BlockSpec((MOE_BLOCK, d), lambda b, blk: (b, 0)),
            scratch_shapes=[pltpu.VMEM((d, D_EXPERT), BF16), pltpu.VMEM((d, D_EXPERT), BF16),
                            pltpu.VMEM((D_EXPERT, d), BF16)]),
        out_shape=jax.ShapeDtypeStruct((n_rows, d), F32),
        compiler_params=_params(("arbitrary",)),
        name="experts",
    )(blk, xr, w1, w3, w2)


def _combine_kernel(ids_ref, ids_next_ref, cnt_ref, x2_ref, rw_ref, g_ref, yr_ref, o_ref, y_ref, seg_ref, sem):
    step = pl.program_id(0)
    slot = step % 2

    @pl.when(step == 0)
    def _():
        _segment_starts(cnt_ref, seg_ref)
        _start_row_copies(ids_ref, seg_ref, yr_ref, y_ref.at[0], sem.at[0], gather=True)

    @pl.when(step + 1 < pl.num_programs(0))
    def _():
        _start_row_copies(ids_next_ref, seg_ref, yr_ref, y_ref.at[1 - slot], sem.at[1 - slot], gather=True)

    for k in range(2):
        _wait_row_copies(yr_ref, y_ref.at[slot, k], sem.at[slot])
    rw = rw_ref[...]
    x3 = x2_ref[...] + (y_ref[slot, 0] * rw[:, 0:1] + y_ref[slot, 1] * rw[:, 1:2])
    o_ref[...] = _rms(x3, g_ref[...])


def _combine(ids, cnt, x2, rw, g, yr):
    n, d = x2.shape
    nt = n // TILE
    row = lambda w: pl.BlockSpec((TILE, w), lambda i: (i, 0))
    return pl.pallas_call(
        _combine_kernel,
        grid=(nt,),
        in_specs=[pl.BlockSpec((1, 8, TILE), lambda i: (i, 0, 0), memory_space=pltpu.SMEM),
                  pl.BlockSpec((1, 8, TILE), lambda i: (jnp.minimum(i + 1, nt - 1), 0, 0),
                               memory_space=pltpu.SMEM),
                  pl.BlockSpec(memory_space=pltpu.SMEM),
                  row(d), row(LANES), pl.BlockSpec((1, d), lambda i: (0, 0)),
                  pl.BlockSpec(memory_space=pl.ANY)],
        out_specs=row(d),
        out_shape=jax.ShapeDtypeStruct((n, d), F32),
        scratch_shapes=[pltpu.VMEM((2, 2, TILE, d), F32), pltpu.SMEM((N_EXPERTS,), I32),
                        pltpu.SemaphoreType.DMA((2,))],
        compiler_params=_params(("arbitrary",)),
        name="combine",
    )(ids, ids, cnt, x2, rw, g, yr)


def _t5_bucket(dist):
    n = jnp.maximum(dist, 0)
    max_exact = REL_BUCKETS // 2
    nf = jnp.maximum(n, 1).astype(F32)
    large = max_exact + (jnp.log(nf / max_exact) / math.log(REL_MAX_DIST / max_exact)
                         * (REL_BUCKETS - max_exact)).astype(I32)
    large = jnp.minimum(large, REL_BUCKETS - 1)
    return jnp.where(n < max_exact, n, large)


def _bias_tables(tab):
    r = jnp.arange(TILE)[:, None]
    c = jnp.arange(TILE)[None, :]
    rel = tab - tab[:, REL_BUCKETS - 1:]

    def tile(dist):
        onehot = (_t5_bucket(dist)[..., None] == jnp.arange(REL_BUCKETS)).astype(F32)
        return jnp.einsum('rcb,hb->hrc', onehot, rel, precision=lax.Precision.HIGHEST)

    own = jnp.where(r >= c, tile(r - c), NEG)
    prev = tile(r - c + TILE)
    near = jnp.stack([jnp.zeros_like(prev), prev, own], axis=1).astype(F32)
    return near.swapaxes(-1, -2)


def _block_mean_slabs(kmean, bn, nq):
    km = kmean.reshape(bn, nq, A_HEADS, HEAD_DIM).transpose(0, 2, 1, 3)
    pad = ((0, 0), (0, 0), (HEAD_DIM, LANES - HEAD_DIM - nq), (0, LANES - HEAD_DIM))
    return jnp.pad(km, pad).astype(BF16)


def _pad_cols(w, width):
    return jnp.pad(w, ((0, 0), (0, width - w.shape[1])))


def kernel(x, mem, rel_bias, final_norm, norm_mix, w_in, ckv_norm, w_uk, w_uv, w_oa, w_ob,
           w_out, norm_x, mem_norm, wq_x, wk_x, wv_x, wo_x, norm_moe, w_group, b_group,
           w_router, b_router, w1, w3, w2):
    bn, t, d = x.shape
    n = bn * t
    nq = t // TILE
    assert t % TILE == 0 and nq <= LANES - HEAD_DIM and norm_mix.shape[0] == 1
    assert nq % (FAR_GROUP // 4) == 0
    near_a = _bias_tables(rel_bias[:, :A_HEADS].T)
    near_b = _bias_tables(rel_bias[:, A_HEADS:].T)
    row = lambda v: v.reshape(1, -1).astype(F32)

    wi = w_in[0]
    wp = jnp.concatenate(
        [wi[:, OFF_AQ:OFF_IK], jnp.tile(wi[:, OFF_IK:OFF_IW], (1, LANES // IDX_DIM)),
         _pad_cols(wi[:, OFF_IW:OFF_GA], LANES)], axis=1).astype(BF16)
    wkv = jnp.concatenate([w_uk[0], w_uv[0]], axis=1).astype(BF16)
    xf = x.reshape(n, d)
    qa, ka, va, qb, kb, vb, qi, ki4, widx, kmean = _proj_in(
        xf, row(norm_mix[0]), wp, row(ckv_norm[0]), wkv, nq)

    b3 = lambda a: a.reshape(bn, t, a.shape[-1])
    attn_a = _moba(b3(qa), b3(ka), va.reshape(bn, nq, W_A, TILE), _block_mean_slabs(kmean, bn, nq), near_a)
    attn_b = _dsa(b3(qi), b3(ki4), b3(widx), b3(qb), b3(kb), vb.reshape(bn, nq, W_B, TILE), near_b)

    kvm = _mem_kv(mem, row(mem_norm[0]), jnp.concatenate([wk_x[0], wv_x[0]], axis=1).astype(BF16))
    wrt = _pad_cols(jnp.concatenate([w_group[0], w_router[0]], axis=1), LANES).astype(F32)
    wrt_hi = wrt.astype(BF16)
    wrt = jnp.concatenate([wrt_hi, (wrt - wrt_hi.astype(F32)).astype(BF16)], axis=1)
    brt = _pad_cols(jnp.concatenate([b_group[0], b_router[0]]).reshape(1, -1), LANES).astype(F32)
    weights = (row(norm_mix[0]), wi[:, OFF_GA:IN_COLS].astype(BF16), w_oa[0].astype(BF16),
               w_ob[0].astype(BF16), w_out[0].astype(BF16), row(norm_x[0]), wq_x[0].astype(BF16),
               wo_x[0].astype(BF16), row(norm_moe[0]), wrt, brt)
    x2, hm, rw, ids, cnt = _post(xf, attn_a.reshape(n, W_A), attn_b.reshape(n, W_B), kvm, t // TILE, weights)

    n_rows = 2 * n + N_EXPERTS * MOE_BLOCK
    xr, blk = _dispatch(ids, cnt, hm, n_rows)
    yr = _experts(blk, xr, w1[0], w3[0], w2[0])
    out = _combine(ids, cnt, x2, rw, row(final_norm), yr)
    return out.reshape(bn, t, d)
```

```python
import functools
import math

import jax
import jax.numpy as jnp
from jax import lax
from jax.experimental import pallas as pl
from jax.experimental.pallas import tpu as pltpu

D_MODEL = 1024
HEAD_DIM = 64
A_HEADS = 8
B_HEADS = 8
W_A = A_HEADS * HEAD_DIM
W_B = B_HEADS * HEAD_DIM
MOBA_BLOCK = 256
MOBA_TOPK = 3
DSA_KV_RANK = 256
IDX_HEADS = 8
IDX_DIM = 32
DSA_TOPK = 256
REL_BUCKETS = 32
REL_MAX_DIST = 128
X_HEADS = 4
W_X = X_HEADS * HEAD_DIM
N_GROUPS = 4
EXPERTS_PER_GROUP = 8
N_EXPERTS = N_GROUPS * EXPERTS_PER_GROUP
D_EXPERT = 512
MOE_BLOCK = 256
EPS = 1e-6
NEG = -1e30

OFF_AQ = 0
OFF_AK = OFF_AQ + W_A
OFF_AV = OFF_AK + W_A
OFF_BQ = OFF_AV + W_A
OFF_CKV = OFF_BQ + W_B
OFF_IQ = OFF_CKV + DSA_KV_RANK
OFF_IK = OFF_IQ + IDX_HEADS * IDX_DIM
OFF_IW = OFF_IK + IDX_DIM
OFF_GA = OFF_IW + IDX_HEADS
OFF_GB = OFF_GA + D_MODEL
IN_COLS = OFF_GB + D_MODEL

LANES = 128
TILE = MOBA_BLOCK
assert TILE == MOE_BLOCK
SEARCH_STEPS_MAX = 400
NO_CUT = 2 ** 30
VMEM_LIMIT = 56 * 1024 * 1024
BF16 = jnp.bfloat16
F32 = jnp.float32
I32 = jnp.int32

P_QA = 0
P_KA = P_QA + W_A
P_VA = P_KA + W_A
P_QB = P_VA + W_A
P_CKV = P_QB + W_B
P_IQ = P_CKV + DSA_KV_RANK
P_IK4 = P_IQ + IDX_HEADS * IDX_DIM
P_IW = P_IK4 + LANES
P_COLS = P_IW + LANES

R_EXP = N_GROUPS
MOE_SHIFT = MOE_BLOCK.bit_length() - 1
assert 1 << MOE_SHIFT == MOE_BLOCK


def _nt(a, b):
    return lax.dot_general(a, b, (((1,), (1,)), ((), ())), preferred_element_type=F32)


def _mm(a, b):
    return jnp.dot(a, b, preferred_element_type=F32)


def _rms(xf, g):
    return xf * lax.rsqrt(jnp.mean(xf * xf, axis=-1, keepdims=True) + EPS) * g


def _params(sem, vmem=VMEM_LIMIT):
    return pltpu.CompilerParams(dimension_semantics=sem, vmem_limit_bytes=vmem)


def _mem_kv_kernel(mem_ref, g_ref, w_ref, o_ref):
    m = _rms(mem_ref[0], g_ref[...]).astype(BF16)
    o_ref[0] = _mm(m, w_ref[...]).astype(BF16)


def _mem_kv(mem, g, wkv):
    bn, ml, d = mem.shape
    return pl.pallas_call(
        _mem_kv_kernel,
        grid=(bn,),
        in_specs=[pl.BlockSpec((1, ml, d), lambda b: (b, 0, 0)),
                  pl.BlockSpec((1, d), lambda b: (0, 0)),
                  pl.BlockSpec((d, 2 * W_X), lambda b: (0, 0))],
        out_specs=pl.BlockSpec((1, ml, 2 * W_X), lambda b: (b, 0, 0)),
        out_shape=jax.ShapeDtypeStruct((bn, ml, 2 * W_X), BF16),
        compiler_params=_params(("arbitrary",)),
        name="mem_kv",
    )(mem, g, wkv)


def _head_slabs(zp, extra):
    lane = lax.broadcasted_iota(I32, zp.shape, 1)
    low = lane < HEAD_DIM
    return jnp.where(low, zp, extra), jnp.where(low, pltpu.roll(zp, HEAD_DIM, 1), extra)


def _proj_in_kernel(blocks_per_seq, x_ref, g_ref, w_ref, cg_ref, wkv_ref,
                    qa_ref, ka_ref, va_ref, qb_ref, kb_ref, vb_ref, qi_ref, ki_ref, wi_ref, km_ref):
    h = _rms(x_ref[...], g_ref[...]).astype(BF16)
    z = _mm(h, w_ref[...])
    scale = HEAD_DIM ** -0.5
    zk = z[:, P_KA:P_VA]
    km_ref[0] = jnp.mean(zk, axis=0, keepdims=True)
    lane = lax.broadcasted_iota(I32, (TILE, LANES), 1)
    blk = pl.program_id(0) % blocks_per_seq
    onehot = jnp.where(lane == HEAD_DIM + blk, 1.0, 0.0)
    zero = jnp.zeros((TILE, LANES), F32)
    for p_idx in range(A_HEADS // 2):
        lo = p_idx * LANES
        qe, qo = _head_slabs(z[:, P_QA + lo:P_QA + lo + LANES] * scale, zero)
        ke, ko = _head_slabs(zk[:, lo:lo + LANES], onehot)
        qa_ref[:, 2 * lo:2 * lo + 2 * LANES] = jnp.concatenate([qe, qo], axis=1).astype(BF16)
        ka_ref[:, 2 * lo:2 * lo + 2 * LANES] = jnp.concatenate([ke, ko], axis=1).astype(BF16)
    va_ref[0] = z[:, P_VA:P_QB].T.astype(BF16)
    qb_ref[...] = (z[:, P_QB:P_CKV] * scale).astype(BF16)
    ckv = _rms(z[:, P_CKV:P_IQ], cg_ref[...]).astype(BF16)
    kv = _mm(ckv, wkv_ref[...])
    kb_ref[...] = kv[:, :W_B].astype(BF16)
    vb_ref[0] = kv[:, W_B:].T.astype(BF16)
    qi_ref[...] = z[:, P_IQ:P_IK4].astype(BF16)
    ki_ref[...] = z[:, P_IK4:P_IW].astype(BF16)
    wi_ref[...] = z[:, P_IW:P_COLS] * ((IDX_HEADS ** -0.5) * (IDX_DIM ** -0.5))


def _proj_in(xf, g, wp, cg, wkv, blocks_per_seq):
    n, d = xf.shape
    nt = n // TILE
    row = lambda w: pl.BlockSpec((TILE, w), lambda i: (i, 0))
    full = lambda a: pl.BlockSpec(a.shape, lambda i: (0,) * a.ndim)
    outs = [(2 * W_A, BF16, False), (2 * W_A, BF16, False), (W_A, BF16, True), (W_B, BF16, False),
            (W_B, BF16, False), (W_B, BF16, True), (IDX_HEADS * IDX_DIM, BF16, False), (LANES, BF16, False),
            (LANES, F32, False)]
    blocked = lambda w: pl.BlockSpec((1, w, TILE), lambda i: (i, 0, 0))
    return pl.pallas_call(
        functools.partial(_proj_in_kernel, blocks_per_seq),
        grid=(nt,),
        in_specs=[row(d), full(g), full(wp), full(cg), full(wkv)],
        out_specs=[blocked(w) if tr else row(w) for w, _, tr in outs]
        + [pl.BlockSpec((1, 1, W_A), lambda i: (i, 0, 0))],
        out_shape=[jax.ShapeDtypeStruct((nt, w, TILE) if tr else (n, w), t) for w, t, tr in outs]
        + [jax.ShapeDtypeStruct((nt, 1, W_A), F32)],
        compiler_params=_params(("arbitrary",)),
        name="proj_in",
    )(xf, g, wp, cg, wkv)


SUB = TILE // 8
LOG2E = math.log2(math.e)
FAR_GROUP = 16


def _sm_init(mx_ref, acc_ref):
    mx_ref[...] = jnp.full(mx_ref.shape, NEG, F32)
    acc_ref[...] = jnp.zeros(acc_ref.shape, F32)


def _sm_logits(hd, slot, s, sc_ref, mx_ref):
    s = s * LOG2E
    sc_ref[hd, slot] = s
    mx_ref[hd] = jnp.maximum(mx_ref[hd], jnp.max(s.reshape(SUB, 8, TILE), axis=0))


def _sm_rowmax(mx_ref):
    for hd in range(2):
        mx_ref[hd] = jnp.broadcast_to(jnp.max(mx_ref[hd], axis=0, keepdims=True), (8, TILE))


def _sm_probs(hd, slots, sc_ref, mx_ref):
    m = mx_ref[hd][None]
    ps = [jnp.exp2((sc_ref[hd, sl].reshape(SUB, 8, TILE) - m).reshape(TILE, TILE).astype(BF16))
          for sl in slots]
    return ps[0] if len(ps) == 1 else jnp.concatenate(ps, axis=0)


def _sm_accumulate(hd, pb, v, acc_ref):
    row = lax.broadcasted_iota(I32, v.shape, 0)
    own = (row < HEAD_DIM) if hd == 0 else (row >= HEAD_DIM)
    acc_ref[hd] = acc_ref[hd] + _mm(jnp.where(own, v, jnp.ones_like(v)), pb)


def _sm_values(hd, slots, v, sc_ref, mx_ref, acc_ref):
    _sm_accumulate(hd, _sm_probs(hd, slots, sc_ref, mx_ref), v, acc_ref)


def _far_plan(n_blocks):
    quarter = FAR_GROUP // 4
    padded = (n_blocks + quarter - 1) // quarter * quarter
    full = padded // FAR_GROUP
    rest = padded - full * FAR_GROUP
    has_half = rest >= FAR_GROUP // 2
    half_base = full * FAR_GROUP
    quarter_base = half_base + jnp.where(has_half, FAR_GROUP // 2, 0)
    return full, has_half, half_base, rest % (FAR_GROUP // 2) != 0, quarter_base


def _sm_far_loop(plan, group_fn):
    full, has_half, half_base, has_quarter, quarter_base = plan
    lax.fori_loop(0, full, lambda g, c: group_fn(FAR_GROUP * g, FAR_GROUP) or c, 0)
    pl.when(has_half)(lambda: group_fn(half_base, FAR_GROUP // 2))
    pl.when(has_quarter)(lambda: group_fn(quarter_base, FAR_GROUP // 4))


def _sm_far_values(plan, vals, sc_ref, mx_ref, acc_ref):
    def group(j0, n):
        v = vals(j0, n)
        for hd in range(2):
            _sm_values(hd, [j0 + u for u in range(n)], v, sc_ref, mx_ref, acc_ref)

    _sm_far_loop(plan, group)


def _sm_output(acc_ref):
    num = jnp.concatenate([acc_ref[0, :HEAD_DIM], acc_ref[1, HEAD_DIM:]], axis=0)
    den = jnp.concatenate([acc_ref[0, HEAD_DIM:], acc_ref[1, :HEAD_DIM]], axis=0)
    return (num / den).T


def _split_pair(qp):
    lane = lax.broadcasted_iota(I32, qp.shape, 1)
    zero = jnp.zeros_like(qp)
    return jnp.where(lane < HEAD_DIM, qp, zero), jnp.where(lane >= HEAD_DIM, qp, zero)


def _moba_kernel(q_ref, k_ref, v_ref, km_ref, bias_ref, o_ref, sc_ref, mx_ref, acc_ref):
    i = pl.program_id(2)
    nq = pl.num_programs(2)
    n_slot = LANES - HEAD_DIM
    blk = lax.broadcasted_iota(I32, (n_slot, TILE), 0)
    in_blk = blk < nq
    past = blk < i
    no_mask = jnp.zeros((HEAD_DIM, TILE), F32)

    qm = []
    for hd in range(2):
        q = q_ref[0, :, hd * LANES:(hd + 1) * LANES]
        g = jnp.where(past, _nt(km_ref[0, hd], q)[HEAD_DIM:], -jnp.inf)
        sel = jnp.zeros((n_slot, TILE), jnp.bool_)
        for _ in range(MOBA_TOPK):
            top = jnp.max(g, axis=0, keepdims=True)
            hit = jnp.logical_and(g == top, in_blk)
            first = jnp.min(jnp.where(hit, blk, n_slot), axis=0, keepdims=True)
            pick = blk == first
            sel = jnp.logical_or(sel, jnp.logical_and(pick, past))
            g = jnp.where(pick, -jnp.inf, g)
        addm = jnp.where(sel, 0.0, NEG)
        mask = jnp.where(past, addm, jnp.where(jnp.logical_and(in_blk, blk > i), NEG, 0.0))
        qm.append(q + jnp.concatenate([no_mask, mask], axis=0).T.astype(BF16))

    def keys(j, hd):
        off = pl.multiple_of(j * TILE, TILE)
        return k_ref[0, pl.ds(off, TILE), hd * LANES:(hd + 1) * LANES]

    plan = _far_plan(i + 1)
    _sm_init(mx_ref, acc_ref)

    def logits(j0, n):
        for u in range(n):
            j = j0 + u
            tile = _near_bias_index(j, i)
            for hd in range(2):
                _sm_logits(hd, j, _nt(keys(j, hd), qm[hd]) + bias_ref[hd, tile], sc_ref, mx_ref)

    _sm_far_loop(plan, logits)
    _sm_rowmax(mx_ref)

    def vals(j, n):
        return jnp.concatenate([v_ref[0, j + u] for u in range(n)], axis=1)

    _sm_far_values(plan, vals, sc_ref, mx_ref, acc_ref)
    o_ref[0] = _sm_output(acc_ref).astype(BF16)


def _near_bias_index(j, i):
    return jnp.clip(j - i + 2, 0, 2)


def _sm_scratch(n_slots):
    return [pltpu.VMEM((2, n_slots, TILE, TILE), F32),
            pltpu.VMEM((2, 8, TILE), F32),
            pltpu.VMEM((2, LANES, TILE), F32)]


def _moba(qa, ka, va_t, kmp, bias_near):
    bn, nq = va_t.shape[:2]
    t = nq * TILE
    npair = A_HEADS // 2
    return pl.pallas_call(
        _moba_kernel,
        grid=(bn, npair, nq),
        in_specs=[pl.BlockSpec((1, TILE, 2 * LANES), lambda b, p, i: (b, i, p)),
                  pl.BlockSpec((1, t, 2 * LANES), lambda b, p, i: (b, 0, p)),
                  pl.BlockSpec((1, nq, LANES, TILE), lambda b, p, i: (b, 0, p, 0)),
                  pl.BlockSpec((1, 2, LANES, LANES), lambda b, p, i: (b, p, 0, 0)),
                  pl.BlockSpec((2, 3, TILE, TILE), lambda b, p, i: (p, 0, 0, 0))],
        out_specs=pl.BlockSpec((1, TILE, LANES), lambda b, p, i: (b, i, p)),
        out_shape=jax.ShapeDtypeStruct((bn, t, W_A), BF16),
        scratch_shapes=_sm_scratch(nq),
        compiler_params=_params(("arbitrary", "arbitrary", "arbitrary")),
        name="moba",
    )(qa, ka, va_t, kmp, bias_near)


def _dsa_select(i, qi_ref, ki_ref, wi_ref, s_ref, wb_ref):
    nch = i + 1
    sub = TILE // 8
    lane = lax.broadcasted_iota(I32, (TILE, LANES), 1)
    key_in = (lax.broadcasted_iota(I32, (sub, 8, TILE), 0) * 8 + lax.broadcasted_iota(I32, (sub, 8, TILE), 1))
    qpos = i * TILE + lax.broadcasted_iota(I32, (sub, 8, TILE), 2)

    def rows8(x):
        return jnp.broadcast_to(x, (8, TILE))[None]

    w_t = wi_ref[0].T
    qi = qi_ref[0]
    per_group = LANES // IDX_DIM
    qh = []
    for h in range(IDX_HEADS):
        g, r = divmod(h, per_group)
        qg = qi[:, g * LANES:(g + 1) * LANES]
        keep = jnp.logical_and(lane >= r * IDX_DIM, lane < (r + 1) * IDX_DIM)
        qh.append(jnp.where(keep, qg, jnp.zeros_like(qg)))
        wb_ref[h] = jnp.broadcast_to(w_t[h:h + 1], (8, TILE))

    def score(c, carry):
        off = pl.multiple_of(c * TILE, TILE)
        kc = ki_ref[0, pl.ds(off, TILE), :]
        acc = jnp.zeros((sub, 8, TILE), F32)
        for h in range(IDX_HEADS):
            acc = acc + jnp.maximum(_nt(kc, qh[h]), 0.0).reshape(sub, 8, TILE) * wb_ref[h][None]
        s_ref[c] = jnp.where(c * TILE + key_in <= qpos, acc, -jnp.inf).reshape(TILE, TILE)
        return carry

    n_pair = (nch + 1) // 2

    def pairs(step, init):
        return lax.fori_loop(0, n_pair, lambda g, acc: step(2 * g + 1, step(2 * g, acc)), init)

    pairs(score, 0)

    def fold(init, step):
        return pairs(lambda c, acc: step(acc, s_ref[c].reshape(sub, 8, TILE), c), init)

    def count(pred):
        def step(acc, blk, c):
            return acc + jnp.sum(jnp.where(pred(blk, c * TILE + key_in), 1.0, 0.0), axis=0)
        return jnp.sum(fold(jnp.zeros((8, TILE), F32), step), axis=0, keepdims=True)

    wide = rows8

    def stats(acc, blk, c):
        top, low, pos, nonneg = acc
        top = jnp.maximum(top, jnp.max(blk, axis=0))
        low = jnp.minimum(low, jnp.min(jnp.where(blk == -jnp.inf, jnp.inf, blk), axis=0))
        pos = pos + jnp.sum(jnp.where(blk > 0.0, 1.0, 0.0), axis=0)
        nonneg = nonneg + jnp.sum(jnp.where(blk >= 0.0, 1.0, 0.0), axis=0)
        return top, low, pos, nonneg

    zeros8 = jnp.zeros((8, TILE), F32)
    acc = fold((zeros8 - jnp.inf, zeros8 + jnp.inf, zeros8, zeros8), stats)
    hi = jnp.max(acc[0], axis=0, keepdims=True)
    lo = jnp.min(acc[1], axis=0, keepdims=True)
    n_pos = jnp.sum(acc[2], axis=0, keepdims=True)
    n_nonneg = jnp.sum(acc[3], axis=0, keepdims=True)

    n_lo = (i * TILE + 1 + lax.broadcasted_iota(I32, (1, TILE), 1)).astype(F32)
    hi = hi + jnp.maximum(jnp.abs(hi) * 2.0 ** -20, 1e-30)
    n_hi = jnp.zeros((1, TILE), F32)
    at_zero = jnp.logical_and(n_pos < DSA_TOPK, n_nonneg >= DSA_TOPK)
    from_zero = n_pos >= DSA_TOPK
    lo, n_lo = jnp.where(from_zero, 0.0, lo), jnp.where(from_zero, n_nonneg, n_lo)
    to_zero = n_nonneg < DSA_TOPK
    hi, n_hi = jnp.where(to_zero, 0.0, hi), jnp.where(to_zero, n_nonneg, n_hi)

    def search_more(state):
        return jnp.logical_and(jnp.min(state[7]) == 0, state[8] < SEARCH_STEPS_MAX)

    def search(state):
        lo, hi, n_lo, n_hi, thr, n_ge, n_gt, done, it = state
        mid = 0.5 * lo + 0.5 * hi
        closed = jnp.logical_or(mid <= lo, mid >= hi)
        n = count(lambda half, kpos: half >= wide(mid))
        stop = jnp.logical_or(closed, n == DSA_TOPK)
        settle = jnp.logical_and(done == 0, stop)
        thr = jnp.where(settle, jnp.where(closed, lo, mid), thr)
        n_ge = jnp.where(settle, jnp.where(closed, n_lo, n), n_ge)
        n_gt = jnp.where(settle, jnp.where(closed, n_hi, 0.0), n_gt)
        go = jnp.logical_and(done == 0, jnp.logical_not(stop))
        up = jnp.logical_and(go, n > DSA_TOPK)
        down = jnp.logical_and(go, n < DSA_TOPK)
        return (jnp.where(up, mid, lo), jnp.where(down, mid, hi), jnp.where(up, n, n_lo),
                jnp.where(down, n, n_hi), thr, n_ge, n_gt, jnp.where(stop, 1, done), it + 1)

    state = (lo, hi, n_lo, n_hi, jnp.zeros((1, TILE), F32), n_nonneg, n_pos, at_zero.astype(I32), jnp.int32(0))
    state = lax.while_loop(search_more, lambda st: search(search(st)), state)
    thr, n_ge, n_gt = wide(state[4]), state[5], state[6]
    need = DSA_TOPK - n_gt

    cut_bits = max(1, (s_ref.shape[0] * TILE - 1).bit_length())

    def cut_step(b, cut):
        cand = cut | (jnp.int32(1) << (cut_bits - 1 - b))
        cw = wide(cand)
        n = count(lambda half, kpos: jnp.logical_and(half == thr, kpos < cw))
        return jnp.where(n < need, cand, cut)

    cut = lax.cond(jnp.max(n_ge) > DSA_TOPK,
                   lambda: lax.fori_loop(0, cut_bits, cut_step, jnp.zeros((1, TILE), I32)),
                   lambda: jnp.full((1, TILE), NO_CUT, I32))
    cut = wide(jnp.where(n_ge > DSA_TOPK, cut, NO_CUT))

    def to_mask(c, carry):
        blk = s_ref[c].reshape(sub, 8, TILE)
        kpos = c * TILE + key_in
        keep = jnp.logical_or(blk > thr, jnp.logical_and(blk == thr, kpos <= cut))
        s_ref[c] = jnp.where(jnp.logical_and(keep, kpos <= qpos), 0.0, NEG).reshape(TILE, TILE)
        return carry

    pairs(to_mask, 0)


def _dsa_kernel(qi_ref, ki_ref, wi_ref, q_ref, k_ref, v_ref, bias_ref, o_ref,
                s_ref, wb_ref, sc_ref, mx_ref, acc_ref):
    i = pl.program_id(1)
    pair = pl.program_id(2)

    @pl.when(jnp.logical_and(pair == 0, i == 0))
    def _():
        key = lax.broadcasted_iota(I32, (TILE, TILE), 0)
        query = lax.broadcasted_iota(I32, (TILE, TILE), 1)
        s_ref[0] = jnp.where(key <= query, 0.0, NEG)

    @pl.when(jnp.logical_and(pair == 0, i >= 1))
    def _():
        _dsa_select(i, qi_ref, ki_ref, wi_ref, s_ref, wb_ref)

    qs = _split_pair(q_ref[0])
    plan = _far_plan(jnp.maximum(i - 1, 0))
    _sm_init(mx_ref, acc_ref)

    def vals(j, n):
        return jnp.concatenate([v_ref[0, j + u] for u in range(n)], axis=1)

    def logits(j, slot, msk, near=None):
        k_j = k_ref[0, pl.ds(pl.multiple_of(j * TILE, TILE), TILE), :]
        for hd in range(2):
            s = _nt(k_j, qs[hd]) + msk
            if near is not None:
                s = s + bias_ref[hd, near]
            _sm_logits(hd, slot, s, sc_ref, mx_ref)

    def far_logits(j0, n):
        for u in range(n):
            j = j0 + u
            logits(j, j, jnp.where(j < i - 1, s_ref[jnp.minimum(j, i)], NEG))

    _sm_far_loop(plan, far_logits)
    nq = pl.num_programs(1)
    j_prev = jnp.maximum(i - 1, 0)
    logits(j_prev, nq, jnp.where(i >= 1, s_ref[j_prev], NEG), near=1)
    logits(i, nq + 1, s_ref[i], near=2)
    _sm_rowmax(mx_ref)

    _sm_far_values(plan, vals, sc_ref, mx_ref, acc_ref)
    v_near = jnp.concatenate([vals(j_prev, 1), vals(i, 1)], axis=1)
    for hd in range(2):
        _sm_values(hd, [nq, nq + 1], v_near, sc_ref, mx_ref, acc_ref)
    o_ref[0] = _sm_output(acc_ref).astype(BF16)


def _dsa(qi, ki4, wi, qb, kb, vb_t, bias_near):
    bn, t, _ = qb.shape
    nq = t // TILE
    assert nq % 2 == 0
    npair = B_HEADS // 2
    qrow = lambda w: pl.BlockSpec((1, TILE, w), lambda b, i, p: (b, i, 0))
    return pl.pallas_call(
        _dsa_kernel,
        grid=(bn, nq, npair),
        in_specs=[qrow(IDX_HEADS * IDX_DIM),
                  pl.BlockSpec((1, t, LANES), lambda b, i, p: (b, 0, 0), pipeline_mode=pl.Buffered(1)),
                  qrow(LANES),
                  pl.BlockSpec((1, TILE, LANES), lambda b, i, p: (b, i, p)),
                  pl.BlockSpec((1, t, LANES), lambda b, i, p: (b, 0, p)),
                  pl.BlockSpec((1, nq, LANES, TILE), lambda b, i, p: (b, 0, p, 0)),
                  pl.BlockSpec((2, 3, TILE, TILE), lambda b, i, p: (p, 0, 0, 0))],
        out_specs=pl.BlockSpec((1, TILE, LANES), lambda b, i, p: (b, i, p)),
        out_shape=jax.ShapeDtypeStruct((bn, t, W_B), BF16),
        scratch_shapes=[pltpu.VMEM((nq, TILE, TILE), F32), pltpu.VMEM((IDX_HEADS, 8, TILE), F32)]
        + _sm_scratch(nq + 2),
        compiler_params=_params(("arbitrary", "arbitrary", "arbitrary")),
        name="dsa",
    )(qi, ki4, wi, qb, kb, vb_t, bias_near)


def _post_kernel(x_ref, aa_ref, ab_ref, gm_ref, wg_ref, woa_ref, wob_ref, wout_ref,
                 gx_ref, wq_ref, kvm_ref, wox_ref, gmoe_ref, wrt_ref, brt_ref,
                 x2_ref, hm_ref, rw_ref, ids_ref, cnt_ref, carry_ref):
    step = pl.program_id(0)
    xf = x_ref[...]
    h = _rms(xf, gm_ref[...]).astype(BF16)
    gates = jax.nn.sigmoid(_mm(h, wg_ref[...]))
    oa = _mm(aa_ref[...], woa_ref[...])
    ob = _mm(ab_ref[...], wob_ref[...])
    mrg = gates[:, :D_MODEL] * oa + gates[:, D_MODEL:] * ob
    x1 = xf + _mm(mrg.astype(BF16), wout_ref[...])

    hx = _rms(x1, gx_ref[...]).astype(BF16)
    q = (_mm(hx, wq_ref[...]) * (HEAD_DIM ** -0.5)).astype(BF16)
    kvm = kvm_ref[0]
    lane = lax.broadcasted_iota(I32, (TILE, LANES), 1)
    outs = []
    for p_idx in range(X_HEADS // 2):
        lo = p_idx * LANES
        qs = _split_pair(q[:, lo:lo + LANES])
        km = kvm[:, lo:lo + LANES]
        vm = kvm[:, W_X + lo:W_X + lo + LANES]
        o = []
        for hd in range(2):
            s = _nt(qs[hd], km)
            p = jnp.exp(s - jnp.max(s, axis=1, keepdims=True))
            o.append(_mm(p.astype(BF16), vm) / jnp.sum(p, axis=1, keepdims=True))
        outs.append(jnp.where(lane < HEAD_DIM, o[0], o[1]))
    xo = jnp.concatenate(outs, axis=1).astype(BF16)
    x2 = x1 + _mm(xo, wox_ref[...])
    x2_ref[...] = x2

    hm = _rms(x2, gmoe_ref[...])
    hm_ref[...] = hm
    hm_hi = hm.astype(BF16)
    hm_lo = (hm - hm_hi.astype(F32)).astype(BF16)
    hi_part = _mm(hm_hi, wrt_ref[...])
    logits = (hi_part[:, :LANES] + hi_part[:, LANES:]) + _mm(hm_lo, wrt_ref[:, :LANES]) + brt_ref[...]
    big = LANES

    def argmax(v):
        top = jnp.max(v, axis=1, keepdims=True)
        return top, jnp.min(jnp.where(v == top, lane, big), axis=1, keepdims=True)

    is_grp = lane < R_EXP
    gtop, gsel = argmax(jnp.where(is_grp, logits, -jnp.inf))
    gw = 1.0 / jnp.sum(jnp.where(is_grp, jnp.exp(logits - gtop), 0.0), axis=1, keepdims=True)
    first = R_EXP + gsel * EXPERTS_PER_GROUP
    inside = jnp.logical_and(lane >= first, lane < first + EXPERTS_PER_GROUP)
    within = jnp.where(inside, logits, -jnp.inf)
    v0, i0 = argmax(within)
    v1, i1 = argmax(jnp.where(lane == i0, -jnp.inf, within))
    e1 = jnp.exp(v1 - v0)
    w0 = gw * (1.0 / (1.0 + e1))
    w1 = gw * (e1 / (1.0 + e1))
    rw_ref[...] = jnp.where(lane == 0, w0, jnp.where(lane == 1, w1, 0.0))

    @pl.when(step == 0)
    def _():
        carry_ref[...] = jnp.zeros(carry_ref.shape, F32)

    hit0 = lane == i0
    hit1 = lane == i1
    onehot = jnp.where(jnp.logical_or(hit0, hit1), 1.0, 0.0)
    tri = (lax.broadcasted_iota(I32, (TILE, TILE), 1) < lax.broadcasted_iota(I32, (TILE, TILE), 0))
    base = carry_ref[...] + _mm(tri.astype(BF16), onehot.astype(BF16))
    r0 = jnp.sum(jnp.where(hit0, base, 0.0), axis=1, keepdims=True)
    r1 = jnp.sum(jnp.where(hit1, base, 0.0), axis=1, keepdims=True)
    total = carry_ref[...] + jnp.sum(onehot, axis=0, keepdims=True)
    carry_ref[...] = total
    cnt_ref[...] = total.astype(I32)
    slab = jnp.where(lane == 0, (i0 - R_EXP).astype(F32),
                     jnp.where(lane == 1, (i1 - R_EXP).astype(F32),
                               jnp.where(lane == 2, r0, jnp.where(lane == 3, r1, 0.0))))
    ids_ref[0] = slab.T[:8].astype(I32)


def _post(xf, aa, ab, kvm, tiles_per_batch, weights):
    n, d = xf.shape
    nt = n // TILE
    row = lambda w: pl.BlockSpec((TILE, w), lambda i: (i, 0))
    full = lambda a: pl.BlockSpec(a.shape, lambda i: (0,) * a.ndim)
    gm, wg, woa, wob, wout, gx, wq, wox, gmoe, wrt, brt = weights
    return pl.pallas_call(
        _post_kernel,
        grid=(nt,),
        in_specs=[row(d), row(W_A), row(W_B), full(gm), full(wg), full(woa), full(wob), full(wout),
                  full(gx), full(wq),
                  pl.BlockSpec((1,) + kvm.shape[1:], lambda i: (i // tiles_per_batch, 0, 0)),
                  full(wox), full(gmoe), full(wrt), full(brt)],
        out_specs=[row(d), row(d), row(LANES),
                   pl.BlockSpec((1, 8, TILE), lambda i: (i, 0, 0)),
                   pl.BlockSpec((1, LANES), lambda i: (0, 0))],
        out_shape=[jax.ShapeDtypeStruct((n, d), F32), jax.ShapeDtypeStruct((n, d), F32),
                   jax.ShapeDtypeStruct((n, LANES), F32),
                   jax.ShapeDtypeStruct((nt, 8, TILE), I32),
                   jax.ShapeDtypeStruct((1, LANES), I32)],
        scratch_shapes=[pltpu.VMEM((1, LANES), F32)],
        compiler_params=_params(("arbitrary",)),
        name="post",
    )(xf, aa, ab, gm, wg, woa, wob, wout, gx, wq, kvm, wox, gmoe, wrt, brt)


def _segment_starts(cnt_ref, seg_ref):
    def body(e, acc):
        seg_ref[e] = acc
        c = cnt_ref[0, R_EXP + e]
        return acc + ((c + (MOE_BLOCK - 1)) >> MOE_SHIFT << MOE_SHIFT)
    return lax.fori_loop(0, N_EXPERTS, body, jnp.int32(0))


def _start_row_copies(ids_ref, seg_ref, rows_ref, tile_ref, sem, gather):
    for t in range(TILE):
        for k in range(2):
            dest = seg_ref[ids_ref[0, k, t]] + ids_ref[0, 2 + k, t]
            if gather:
                cp = pltpu.make_async_copy(rows_ref.at[pl.ds(dest, 1)], tile_ref.at[k, pl.ds(t, 1)], sem)
            else:
                cp = pltpu.make_async_copy(tile_ref.at[pl.ds(t, 1)], rows_ref.at[pl.ds(dest, 1)], sem)
            cp.start(priority=k)


def _wait_row_copies(rows_ref, tile_ref, sem):
    pltpu.make_async_copy(rows_ref.at[pl.ds(0, TILE)], tile_ref, sem).wait()


def _dispatch_kernel(ids_ref, cnt_ref, hm_ref, xr_in_ref, xr_ref, blk_ref, seg_ref, sem):
    del xr_in_ref
    step = pl.program_id(0)
    nb = blk_ref.shape[1] - 1

    @pl.when(step == 0)
    def _():
        used = _segment_starts(cnt_ref, seg_ref)

        def per_expert(e, last):
            c = cnt_ref[0, R_EXP + e]
            b0 = seg_ref[e] >> MOE_SHIFT
            n = (c + (MOE_BLOCK - 1)) >> MOE_SHIFT

            def fill(kk, carry):
                blk_ref[0, b0 + kk] = e
                return carry
            lax.fori_loop(0, n, fill, 0)
            return jnp.where(n > 0, e, last)
        last = lax.fori_loop(0, N_EXPERTS, per_expert, jnp.int32(0))

        def tail(b, carry):
            blk_ref[0, b] = last
            return carry
        lax.fori_loop(used >> MOE_SHIFT, nb, tail, 0)
        blk_ref[0, nb] = used >> MOE_SHIFT

    _start_row_copies(ids_ref, seg_ref, xr_ref, hm_ref, sem, gather=False)
    for _ in range(2):
        _wait_row_copies(xr_ref, hm_ref, sem)


def _dispatch(ids, cnt, hm, n_rows):
    n, d = hm.shape
    nt = n // TILE
    nb = n_rows // MOE_BLOCK
    zeros = jnp.zeros((n_rows, d), F32)
    return pl.pallas_call(
        _dispatch_kernel,
        grid=(nt,),
        in_specs=[pl.BlockSpec((1, 8, TILE), lambda i: (i, 0, 0), memory_space=pltpu.SMEM),
                  pl.BlockSpec(memory_space=pltpu.SMEM),
                  pl.BlockSpec((TILE, d), lambda i: (i, 0)),
                  pl.BlockSpec(memory_space=pl.ANY)],
        out_specs=[pl.BlockSpec(memory_space=pl.ANY),
                   pl.BlockSpec(memory_space=pltpu.SMEM)],
        out_shape=[jax.ShapeDtypeStruct((n_rows, d), F32),
                   jax.ShapeDtypeStruct((1, nb + 1), I32)],
        scratch_shapes=[pltpu.SMEM((N_EXPERTS,), I32), pltpu.SemaphoreType.DMA(())],
        input_output_aliases={3: 0},
        compiler_params=_params(("arbitrary",)),
        name="dispatch",
    )(ids, cnt, hm, zeros)


def _experts_kernel(blk_ref, x_ref, w1_ref, w3_ref, w2_ref, y_ref, w1b, w3b, w2b):
    b = pl.program_id(0)
    nb = pl.num_programs(0)
    e = blk_ref[0, b]
    prev = blk_ref[0, jnp.maximum(b - 1, 0)]

    @pl.when(jnp.logical_or(b == 0, e != prev))
    def _():
        w1b[...] = w1_ref[0].astype(BF16)
        w3b[...] = w3_ref[0].astype(BF16)
        w2b[...] = w2_ref[0].astype(BF16)

    used = blk_ref[0, nb]

    @pl.when(b < used)
    def _():
        xb = x_ref[...].astype(BF16)
        a = _mm(xb, w1b[...])
        g = _mm(xb, w3b[...])
        y_ref[...] = _mm((a * jax.nn.sigmoid(a) * g).astype(BF16), w2b[...])

    @pl.when(b >= used)
    def _():
        y_ref[...] = jnp.zeros(y_ref.shape, F32)


def _experts(blk, xr, w1, w3, w2):
    n_rows, d = xr.shape
    nb = n_rows // MOE_BLOCK
    wspec = lambda s: pl.BlockSpec((1,) + s, lambda b, blk: (blk[0, b], 0, 0))
    return pl.pallas_call(
        _experts_kernel,
        grid_spec=pltpu.PrefetchScalarGridSpec(
            num_scalar_prefetch=1,
            grid=(nb,),
            in_specs=[pl.BlockSpec((MOE_BLOCK, d), lambda b, blk: (b, 0)),
                      wspec((d, D_EXPERT)), wspec((d, D_EXPERT)), wspec((D_EXPERT, d))],
            out_specs=pl.BlockSpec((MOE_BLOCK, d), lambda b, blk: (b, 0)),
            scratch_shapes=[pltpu.VMEM((d, D_EXPERT), BF16), pltpu.VMEM((d, D_EXPERT), BF16),
                            pltpu.VMEM((D_EXPERT, d), BF16)]),
        out_shape=jax.ShapeDtypeStruct((n_rows, d), F32),
        compiler_params=_params(("arbitrary",)),
        name="experts",
    )(blk, xr, w1, w3, w2)


def _combine_kernel(ids_ref, ids_next_ref, cnt_ref, x2_ref, rw_ref, g_ref, yr_ref, o_ref, y_ref, seg_ref, sem):
    step = pl.program_id(0)
    slot = step % 2

    @pl.when(step == 0)
    def _():
        _segment_starts(cnt_ref, seg_ref)
        _start_row_copies(ids_ref, seg_ref, yr_ref, y_ref.at[0], sem.at[0], gather=True)

    @pl.when(step + 1 < pl.num_programs(0))
    def _():
        _start_row_copies(ids_next_ref, seg_ref, yr_ref, y_ref.at[1 - slot], sem.at[1 - slot], gather=True)

    for k in range(2):
        _wait_row_copies(yr_ref, y_ref.at[slot, k], sem.at[slot])
    rw = rw_ref[...]
    x3 = x2_ref[...] + (y_ref[slot, 0] * rw[:, 0:1] + y_ref[slot, 1] * rw[:, 1:2])
    o_ref[...] = _rms(x3, g_ref[...])


def _combine(ids, cnt, x2, rw, g, yr):
    n, d = x2.shape
    nt = n // TILE
    row = lambda w: pl.BlockSpec((TILE, w), lambda i: (i, 0))
    return pl.pallas_call(
        _combine_kernel,
        grid=(nt,),
        in_specs=[pl.BlockSpec((1, 8, TILE), lambda i: (i, 0, 0), memory_space=pltpu.SMEM),
                  pl.BlockSpec((1, 8, TILE), lambda i: (jnp.minimum(i + 1, nt - 1), 0, 0),
                               memory_space=pltpu.SMEM),
                  pl.BlockSpec(memory_space=pltpu.SMEM),
                  row(d), row(LANES), pl.BlockSpec((1, d), lambda i: (0, 0)),
                  pl.BlockSpec(memory_space=pl.ANY)],
        out_specs=row(d),
        out_shape=jax.ShapeDtypeStruct((n, d), F32),
        scratch_shapes=[pltpu.VMEM((2, 2, TILE, d), F32), pltpu.SMEM((N_EXPERTS,), I32),
                        pltpu.SemaphoreType.DMA((2,))],
        compiler_params=_params(("arbitrary",)),
        name="combine",
    )(ids, ids, cnt, x2, rw, g, yr)


def _t5_bucket(dist):
    n = jnp.maximum(dist, 0)
    max_exact = REL_BUCKETS // 2
    nf = jnp.maximum(n, 1).astype(F32)
    large = max_exact + (jnp.log(nf / max_exact) / math.log(REL_MAX_DIST / max_exact)
                         * (REL_BUCKETS - max_exact)).astype(I32)
    large = jnp.minimum(large, REL_BUCKETS - 1)
    return jnp.where(n < max_exact, n, large)


def _bias_tables(tab):
    r = jnp.arange(TILE)[:, None]
    c = jnp.arange(TILE)[None, :]
    rel = tab - tab[:, REL_BUCKETS - 1:]

    def tile(dist):
        onehot = (_t5_bucket(dist)[..., None] == jnp.arange(REL_BUCKETS)).astype(F32)
        return jnp.einsum('rcb,hb->hrc', onehot, rel, precision=lax.Precision.HIGHEST)

    own = jnp.where(r >= c, tile(r - c), NEG)
    prev = tile(r - c + TILE)
    near = jnp.stack([jnp.zeros_like(prev), prev, own], axis=1).astype(F32)
    return near.swapaxes(-1, -2)


def _block_mean_slabs(kmean, bn, nq):
    km = kmean.reshape(bn, nq, A_HEADS, HEAD_DIM).transpose(0, 2, 1, 3)
    pad = ((0, 0), (0, 0), (HEAD_DIM, LANES - HEAD_DIM - nq), (0, LANES - HEAD_DIM))
    return jnp.pad(km, pad).astype(BF16)


def _pad_cols(w, width):
    return jnp.pad(w, ((0, 0), (0, width - w.shape[1])))


def kernel(x, mem, rel_bias, final_norm, norm_mix, w_in, ckv_norm, w_uk, w_uv, w_oa, w_ob,
           w_out, norm_x, mem_norm, wq_x, wk_x, wv_x, wo_x, norm_moe, w_group, b_group,
           w_router, b_router, w1, w3, w2):
    bn, t, d = x.shape
    n = bn * t
    nq = t // TILE
    assert t % TILE == 0 and nq <= LANES - HEAD_DIM and norm_mix.shape[0] == 1
    assert nq % (FAR_GROUP // 4) == 0
    near_a = _bias_tables(rel_bias[:, :A_HEADS].T)
    near_b = _bias_tables(rel_bias[:, A_HEADS:].T)
    row = lambda v: v.reshape(1, -1).astype(F32)

    wi = w_in[0]
    wp = jnp.concatenate(
        [wi[:, OFF_AQ:OFF_IK], jnp.tile(wi[:, OFF_IK:OFF_IW], (1, LANES // IDX_DIM)),
         _pad_cols(wi[:, OFF_IW:OFF_GA], LANES)], axis=1).astype(BF16)
    wkv = jnp.concatenate([w_uk[0], w_uv[0]], axis=1).astype(BF16)
    xf = x.reshape(n, d)
    qa, ka, va, qb, kb, vb, qi, ki4, widx, kmean = _proj_in(
        xf, row(norm_mix[0]), wp, row(ckv_norm[0]), wkv, nq)

    b3 = lambda a: a.reshape(bn, t, a.shape[-1])
    attn_a = _moba(b3(qa), b3(ka), va.reshape(bn, nq, W_A, TILE), _block_mean_slabs(kmean, bn, nq), near_a)
    attn_b = _dsa(b3(qi), b3(ki4), b3(widx), b3(qb), b3(kb), vb.reshape(bn, nq, W_B, TILE), near_b)

    kvm = _mem_kv(mem, row(mem_norm[0]), jnp.concatenate([wk_x[0], wv_x[0]], axis=1).astype(BF16))
    wrt = _pad_cols(jnp.concatenate([w_group[0], w_router[0]], axis=1), LANES).astype(F32)
    wrt_hi = wrt.astype(BF16)
    wrt = jnp.concatenate([wrt_hi, (wrt - wrt_hi.astype(F32)).astype(BF16)], axis=1)
    brt = _pad_cols(jnp.concatenate([b_group[0], b_router[0]]).reshape(1, -1), LANES).astype(F32)
    weights = (row(norm_mix[0]), wi[:, OFF_GA:IN_COLS].astype(BF16), w_oa[0].astype(BF16),
               w_ob[0].astype(BF16), w_out[0].astype(BF16), row(norm_x[0]), wq_x[0].astype(BF16),
               wo_x[0].astype(BF16), row(norm_moe[0]), wrt, brt)
    x2, hm, rw, ids, cnt = _post(xf, attn_a.reshape(n, W_A), attn_b.reshape(n, W_B), kvm, t // TILE, weights)

    n_rows = 2 * n + N_EXPERTS * MOE_BLOCK
    xr, blk = _dispatch(ids, cnt, hm, n_rows)
    yr = _experts(blk, xr, w1[0], w3[0], w2[0])
    out = _combine(ids, cnt, x2, rw, row(final_norm), yr)
    return out.reshape(bn, t, d)
```

```python
import functools
import math

import jax
import jax.numpy as jnp
from jax import lax
from jax.experimental import pallas as pl
from jax.experimental.pallas import tpu as pltpu

D_MODEL = 1024
HEAD_DIM = 64
A_HEADS = 8
B_HEADS = 8
W_A = A_HEADS * HEAD_DIM
W_B = B_HEADS * HEAD_DIM
MOBA_BLOCK = 256
MOBA_TOPK = 3
DSA_KV_RANK = 256
IDX_HEADS = 8
IDX_DIM = 32
DSA_TOPK = 256
REL_BUCKETS = 32
REL_MAX_DIST = 128
X_HEADS = 4
W_X = X_HEADS * HEAD_DIM
N_GROUPS = 4
EXPERTS_PER_GROUP = 8
N_EXPERTS = N_GROUPS * EXPERTS_PER_GROUP
D_EXPERT = 512
MOE_BLOCK = 256
EPS = 1e-6
NEG = -1e30

OFF_AQ = 0
OFF_AK = OFF_AQ + W_A
OFF_AV = OFF_AK + W_A
OFF_BQ = OFF_AV + W_A
OFF_CKV = OFF_BQ + W_B
OFF_IQ = OFF_CKV + DSA_KV_RANK
OFF_IK = OFF_IQ + IDX_HEADS * IDX_DIM
OFF_IW = OFF_IK + IDX_DIM
OFF_GA = OFF_IW + IDX_HEADS
OFF_GB = OFF_GA + D_MODEL
IN_COLS = OFF_GB + D_MODEL

LANES = 128
TILE = MOBA_BLOCK
assert TILE == MOE_BLOCK
SEARCH_STEPS_MAX = 400
NO_CUT = 2 ** 30
VMEM_LIMIT = 56 * 1024 * 1024
BF16 = jnp.bfloat16
F32 = jnp.float32
I32 = jnp.int32

P_QA, P_KA, P_VA, P_QB = 0, 512, 1024, 1536
P_CKV, P_IQ, P_IK4, P_IW = 2048, 2304, 2560, 2688
P_COLS = 2816

R_EXP = N_GROUPS
MOE_SHIFT = MOE_BLOCK.bit_length() - 1
assert 1 << MOE_SHIFT == MOE_BLOCK


def _nt(a, b):
    return lax.dot_general(a, b, (((1,), (1,)), ((), ())), preferred_element_type=F32)


def _mm(a, b):
    return jnp.dot(a, b, preferred_element_type=F32)


def _rms(xf, g):
    return xf * lax.rsqrt(jnp.mean(xf * xf, axis=-1, keepdims=True) + EPS) * g


def _params(sem, vmem=VMEM_LIMIT):
    return pltpu.CompilerParams(dimension_semantics=sem, vmem_limit_bytes=vmem)


def _mem_kv_kernel(mem_ref, g_ref, w_ref, o_ref):
    m = _rms(mem_ref[0], g_ref[...]).astype(BF16)
    o_ref[0] = _mm(m, w_ref[...]).astype(BF16)


def _mem_kv(mem, g, wkv):
    bn, ml, d = mem.shape
    return pl.pallas_call(
        _mem_kv_kernel,
        grid=(bn,),
        in_specs=[pl.BlockSpec((1, ml, d), lambda b: (b, 0, 0)),
                  pl.BlockSpec((1, d), lambda b: (0, 0)),
                  pl.BlockSpec((d, 2 * W_X), lambda b: (0, 0))],
        out_specs=pl.BlockSpec((1, ml, 2 * W_X), lambda b: (b, 0, 0)),
        out_shape=jax.ShapeDtypeStruct((bn, ml, 2 * W_X), BF16),
        compiler_params=_params(("arbitrary",)),
        name="mem_kv",
    )(mem, g, wkv)


def _head_slabs(zp, extra):
    lane = lax.broadcasted_iota(I32, zp.shape, 1)
    low = lane < HEAD_DIM
    return jnp.where(low, zp, extra), jnp.where(low, pltpu.roll(zp, HEAD_DIM, 1), extra)


def _proj_in_kernel(blocks_per_seq, x_ref, g_ref, w_ref, cg_ref, wkv_ref,
                    qa_ref, ka_ref, va_ref, qb_ref, kb_ref, vb_ref, qi_ref, ki_ref, wi_ref, km_ref):
    h = _rms(x_ref[...], g_ref[...]).astype(BF16)
    z = _mm(h, w_ref[...])
    scale = HEAD_DIM ** -0.5
    zk = z[:, P_KA:P_VA]
    km_ref[0] = jnp.mean(zk, axis=0, keepdims=True)
    lane = lax.broadcasted_iota(I32, (TILE, LANES), 1)
    blk = pl.program_id(0) % blocks_per_seq
    onehot = jnp.where(lane == HEAD_DIM + blk, 1.0, 0.0)
    zero = jnp.zeros((TILE, LANES), F32)
    for p_idx in range(A_HEADS // 2):
        lo = p_idx * LANES
        qe, qo = _head_slabs(z[:, P_QA + lo:P_QA + lo + LANES] * scale, zero)
        ke, ko = _head_slabs(zk[:, lo:lo + LANES], onehot)
        qa_ref[:, 2 * lo:2 * lo + 2 * LANES] = jnp.concatenate([qe, qo], axis=1).astype(BF16)
        ka_ref[:, 2 * lo:2 * lo + 2 * LANES] = jnp.concatenate([ke, ko], axis=1).astype(BF16)
    va_ref[0] = z[:, P_VA:P_QB].T.astype(BF16)
    qb_ref[...] = (z[:, P_QB:P_CKV] * scale).astype(BF16)
    ckv = _rms(z[:, P_CKV:P_IQ], cg_ref[...]).astype(BF16)
    kv = _mm(ckv, wkv_ref[...])
    kb_ref[...] = kv[:, :W_B].astype(BF16)
    vb_ref[0] = kv[:, W_B:].T.astype(BF16)
    qi_ref[...] = z[:, P_IQ:P_IK4].astype(BF16)
    ki_ref[...] = z[:, P_IK4:P_IW].astype(BF16)
    wi_ref[...] = z[:, P_IW:P_COLS] * ((IDX_HEADS ** -0.5) * (IDX_DIM ** -0.5))


def _proj_in(xf, g, wp, cg, wkv, blocks_per_seq):
    n, d = xf.shape
    nt = n // TILE
    row = lambda w: pl.BlockSpec((TILE, w), lambda i: (i, 0))
    full = lambda a: pl.BlockSpec(a.shape, lambda i: (0,) * a.ndim)
    outs = [(2 * W_A, BF16, False), (2 * W_A, BF16, False), (W_A, BF16, True), (W_B, BF16, False),
            (W_B, BF16, False), (W_B, BF16, True), (IDX_HEADS * IDX_DIM, BF16, False), (LANES, BF16, False),
            (LANES, F32, False)]
    blocked = lambda w: pl.BlockSpec((1, w, TILE), lambda i: (i, 0, 0))
    return pl.pallas_call(
        functools.partial(_proj_in_kernel, blocks_per_seq),
        grid=(nt,),
        in_specs=[row(d), full(g), full(wp), full(cg), full(wkv)],
        out_specs=[blocked(w) if tr else row(w) for w, _, tr in outs]
        + [pl.BlockSpec((1, 1, W_A), lambda i: (i, 0, 0))],
        out_shape=[jax.ShapeDtypeStruct((nt, w, TILE) if tr else (n, w), t) for w, t, tr in outs]
        + [jax.ShapeDtypeStruct((nt, 1, W_A), F32)],
        compiler_params=_params(("arbitrary",)),
        name="proj_in",
    )(xf, g, wp, cg, wkv)


SUB = TILE // 8
LOG2E = math.log2(math.e)
FAR_GROUP = 16


def _sm_init(mx_ref, acc_ref):
    mx_ref[...] = jnp.full(mx_ref.shape, NEG, F32)
    acc_ref[...] = jnp.zeros(acc_ref.shape, F32)


def _sm_logits(hd, slot, s, sc_ref, mx_ref):
    s = s * LOG2E
    sc_ref[hd, slot] = s
    mx_ref[hd] = jnp.maximum(mx_ref[hd], jnp.max(s.reshape(SUB, 8, TILE), axis=0))


def _sm_rowmax(mx_ref):
    for hd in range(2):
        mx_ref[hd] = jnp.broadcast_to(jnp.max(mx_ref[hd], axis=0, keepdims=True), (8, TILE))


def _sm_probs(hd, slots, sc_ref, mx_ref):
    m = mx_ref[hd][None]
    ps = [jnp.exp2((sc_ref[hd, sl].reshape(SUB, 8, TILE) - m).reshape(TILE, TILE).astype(BF16))
          for sl in slots]
    return ps[0] if len(ps) == 1 else jnp.concatenate(ps, axis=0)


def _sm_accumulate(hd, pb, v, acc_ref):
    row = lax.broadcasted_iota(I32, v.shape, 0)
    own = (row < HEAD_DIM) if hd == 0 else (row >= HEAD_DIM)
    acc_ref[hd] = acc_ref[hd] + _mm(jnp.where(own, v, jnp.ones_like(v)), pb)


def _sm_values(hd, slots, v, sc_ref, mx_ref, acc_ref):
    _sm_accumulate(hd, _sm_probs(hd, slots, sc_ref, mx_ref), v, acc_ref)


def _far_plan(n_blocks):
    quarter = FAR_GROUP // 4
    padded = (n_blocks + quarter - 1) // quarter * quarter
    full = padded // FAR_GROUP
    rest = padded - full * FAR_GROUP
    has_half = rest >= FAR_GROUP // 2
    half_base = full * FAR_GROUP
    quarter_base = half_base + jnp.where(has_half, FAR_GROUP // 2, 0)
    return full, has_half, half_base, rest % (FAR_GROUP // 2) != 0, quarter_base


def _sm_far_loop(plan, group_fn):
    full, has_half, half_base, has_quarter, quarter_base = plan
    lax.fori_loop(0, full, lambda g, c: group_fn(FAR_GROUP * g, FAR_GROUP) or c, 0)
    pl.when(has_half)(lambda: group_fn(half_base, FAR_GROUP // 2))
    pl.when(has_quarter)(lambda: group_fn(quarter_base, FAR_GROUP // 4))


def _sm_far_values(plan, vals, sc_ref, mx_ref, acc_ref):
    def group(j0, n):
        v = vals(j0, n)
        for hd in range(2):
            _sm_values(hd, [j0 + u for u in range(n)], v, sc_ref, mx_ref, acc_ref)

    _sm_far_loop(plan, group)


def _sm_output(acc_ref):
    num = jnp.concatenate([acc_ref[0, :HEAD_DIM], acc_ref[1, HEAD_DIM:]], axis=0)
    den = jnp.concatenate([acc_ref[0, HEAD_DIM:], acc_ref[1, :HEAD_DIM]], axis=0)
    return (num / den).T


def _split_pair(qp):
    lane = lax.broadcasted_iota(I32, qp.shape, 1)
    zero = jnp.zeros_like(qp)
    return jnp.where(lane < HEAD_DIM, qp, zero), jnp.where(lane >= HEAD_DIM, qp, zero)


def _moba_kernel(q_ref, k_ref, v_ref, km_ref, bias_ref, o_ref, sc_ref, mx_ref, acc_ref):
    i = pl.program_id(2)
    nq = pl.num_programs(2)
    n_slot = LANES - HEAD_DIM
    blk = lax.broadcasted_iota(I32, (n_slot, TILE), 0)
    in_blk = blk < nq
    past = blk < i
    no_mask = jnp.zeros((HEAD_DIM, TILE), F32)

    qm = []
    for hd in range(2):
        q = q_ref[0, :, hd * LANES:(hd + 1) * LANES]
        g = jnp.where(past, _nt(km_ref[0, hd], q)[HEAD_DIM:], -jnp.inf)
        sel = jnp.zeros((n_slot, TILE), jnp.bool_)
        for _ in range(MOBA_TOPK):
            top = jnp.max(g, axis=0, keepdims=True)
            hit = jnp.logical_and(g == top, in_blk)
            first = jnp.min(jnp.where(hit, blk, n_slot), axis=0, keepdims=True)
            pick = blk == first
            sel = jnp.logical_or(sel, jnp.logical_and(pick, past))
            g = jnp.where(pick, -jnp.inf, g)
        addm = jnp.where(sel, 0.0, NEG)
        mask = jnp.where(past, addm, jnp.where(jnp.logical_and(in_blk, blk > i), NEG, 0.0))
        qm.append(q + jnp.concatenate([no_mask, mask], axis=0).T.astype(BF16))

    def keys(j, hd):
        off = pl.multiple_of(j * TILE, TILE)
        return k_ref[0, pl.ds(off, TILE), hd * LANES:(hd + 1) * LANES]

    plan = _far_plan(i + 1)
    _sm_init(mx_ref, acc_ref)

    def logits(j0, n):
        for u in range(n):
            j = j0 + u
            tile = _near_bias_index(j, i)
            for hd in range(2):
                _sm_logits(hd, j, _nt(keys(j, hd), qm[hd]) + bias_ref[hd, tile], sc_ref, mx_ref)

    _sm_far_loop(plan, logits)
    _sm_rowmax(mx_ref)

    def vals(j, n):
        return jnp.concatenate([v_ref[0, j + u] for u in range(n)], axis=1)

    _sm_far_values(plan, vals, sc_ref, mx_ref, acc_ref)
    o_ref[0] = _sm_output(acc_ref).astype(BF16)


def _near_bias_index(j, i):
    return jnp.clip(j - i + 2, 0, 2)


def _sm_scratch(n_slots):
    return [pltpu.VMEM((2, n_slots, TILE, TILE), F32),
            pltpu.VMEM((2, 8, TILE), F32),
            pltpu.VMEM((2, LANES, TILE), F32)]


def _moba(qa, ka, va_t, kmp, bias_near):
    bn, nq = va_t.shape[:2]
    t = nq * TILE
    npair = A_HEADS // 2
    return pl.pallas_call(
        _moba_kernel,
        grid=(bn, npair, nq),
        in_specs=[pl.BlockSpec((1, TILE, 2 * LANES), lambda b, p, i: (b, i, p)),
                  pl.BlockSpec((1, t, 2 * LANES), lambda b, p, i: (b, 0, p)),
                  pl.BlockSpec((1, nq, LANES, TILE), lambda b, p, i: (b, 0, p, 0)),
                  pl.BlockSpec((1, 2, LANES, LANES), lambda b, p, i: (b, p, 0, 0)),
                  pl.BlockSpec((2, 3, TILE, TILE), lambda b, p, i: (p, 0, 0, 0))],
        out_specs=pl.BlockSpec((1, TILE, LANES), lambda b, p, i: (b, i, p)),
        out_shape=jax.ShapeDtypeStruct((bn, t, W_A), BF16),
        scratch_shapes=_sm_scratch(nq),
        compiler_params=_params(("arbitrary", "arbitrary", "arbitrary")),
        name="moba",
    )(qa, ka, va_t, kmp, bias_near)


def _dsa_select(i, qi_ref, ki_ref, wi_ref, s_ref, b_ref, wb_ref):
    nch = i + 1
    sub = TILE // 8
    lane = lax.broadcasted_iota(I32, (TILE, LANES), 1)
    key_in = (lax.broadcasted_iota(I32, (sub, 8, TILE), 0) * 8 + lax.broadcasted_iota(I32, (sub, 8, TILE), 1))
    qpos = i * TILE + lax.broadcasted_iota(I32, (sub, 8, TILE), 2)

    def rows8(x):
        return jnp.broadcast_to(x, (8, TILE))[None]

    w_t = wi_ref[0].T
    qi = qi_ref[0]
    per_group = LANES // IDX_DIM
    qh = []
    for h in range(IDX_HEADS):
        g, r = divmod(h, per_group)
        qg = qi[:, g * LANES:(g + 1) * LANES]
        keep = jnp.logical_and(lane >= r * IDX_DIM, lane < (r + 1) * IDX_DIM)
        qh.append(jnp.where(keep, qg, jnp.zeros_like(qg)))
        wb_ref[h] = jnp.broadcast_to(w_t[h:h + 1], (8, TILE))

    def score(c, carry):
        off = pl.multiple_of(c * TILE, TILE)
        kc = ki_ref[0, pl.ds(off, TILE), :]
        acc = jnp.zeros((sub, 8, TILE), F32)
        for h in range(IDX_HEADS):
            acc = acc + jnp.maximum(_nt(kc, qh[h]), 0.0).reshape(sub, 8, TILE) * wb_ref[h][None]
        s = jnp.where(c * TILE + key_in <= qpos, acc, -jnp.inf).reshape(TILE, TILE)
        s_ref[c] = s
        bits = lax.bitcast_convert_type(s, I32)
        low = bits & 0xFFFF
        down = jnp.where(jnp.logical_and(bits < 0, low != 0), bits - low + 0x10000, bits - low)
        b_ref[c] = lax.bitcast_convert_type(down, F32).astype(BF16)
        return carry

    n_pair = (nch + 1) // 2

    def pairs(step, init):
        return lax.fori_loop(0, n_pair, lambda g, acc: step(2 * g + 1, step(2 * g, acc)), init)

    pairs(score, 0)

    def fold(init, step):
        return pairs(lambda c, acc: step(acc, s_ref[c].reshape(sub, 8, TILE), c), init)

    def count(pred):
        def step(acc, blk, c):
            return acc + jnp.sum(jnp.where(pred(blk, c * TILE + key_in), 1.0, 0.0), axis=0)
        return jnp.sum(fold(jnp.zeros((8, TILE), F32), step), axis=0, keepdims=True)

    wide = rows8

    def stats(acc, blk, c):
        top, low, pos, nonneg = acc
        top = jnp.maximum(top, jnp.max(blk, axis=0))
        low = jnp.minimum(low, jnp.min(jnp.where(blk == -jnp.inf, jnp.inf, blk), axis=0))
        pos = pos + jnp.sum(jnp.where(blk > 0.0, 1.0, 0.0), axis=0)
        nonneg = nonneg + jnp.sum(jnp.where(blk >= 0.0, 1.0, 0.0), axis=0)
        return top, low, pos, nonneg

    zeros8 = jnp.zeros((8, TILE), F32)
    acc = fold((zeros8 - jnp.inf, zeros8 + jnp.inf, zeros8, zeros8), stats)
    hi = jnp.max(acc[0], axis=0, keepdims=True)
    lo = jnp.min(acc[1], axis=0, keepdims=True)
    n_pos = jnp.sum(acc[2], axis=0, keepdims=True)
    n_nonneg = jnp.sum(acc[3], axis=0, keepdims=True)

    n_lo = (i * TILE + 1 + lax.broadcasted_iota(I32, (1, TILE), 1)).astype(F32)
    hi = hi + jnp.maximum(jnp.abs(hi) * 2.0 ** -20, 1e-30)
    n_hi = jnp.zeros((1, TILE), F32)
    at_zero = jnp.logical_and(n_pos < DSA_TOPK, n_nonneg >= DSA_TOPK)
    from_zero = n_pos >= DSA_TOPK
    lo, n_lo = jnp.where(from_zero, 0.0, lo), jnp.where(from_zero, n_nonneg, n_lo)
    to_zero = n_nonneg < DSA_TOPK
    hi, n_hi = jnp.where(to_zero, 0.0, hi), jnp.where(to_zero, n_nonneg, n_hi)

    def count_coarse(mid):
        pack = 16
        m16 = jnp.broadcast_to(mid.astype(BF16), (pack, TILE))[None]
        one, zero = jnp.ones((), BF16), jnp.zeros((), BF16)

        def step(c, acc):
            hit = jnp.where(b_ref[c].reshape(TILE // pack, pack, TILE) >= m16, one, zero)
            part = hit[0]
            for r in range(1, TILE // pack):
                part = part + hit[r]
            return acc + part.astype(F32)
        return jnp.sum(pairs(step, jnp.zeros((pack, TILE), F32)), axis=0, keepdims=True)

    def search_more(state):
        return jnp.logical_and(jnp.min(state[7]) == 0, state[8] < SEARCH_STEPS_MAX)

    def search(state, coarse):
        lo, hi, n_lo, n_hi, thr, n_ge, n_gt, done, it = state
        mid = 0.5 * lo + 0.5 * hi
        if coarse:
            mid = mid.astype(BF16).astype(F32)
        closed = jnp.logical_or(mid <= lo, mid >= hi)
        n = count_coarse(mid) if coarse else count(lambda half, kpos: half >= wide(mid))
        live = done == 0
        hit = jnp.logical_and(jnp.logical_not(closed), n == DSA_TOPK)
        stop = hit if coarse else jnp.logical_or(closed, hit)
        settle = jnp.logical_and(live, stop)
        thr = jnp.where(settle, jnp.where(hit, mid, lo), thr)
        n_ge = jnp.where(settle, jnp.where(hit, n, n_lo), n_ge)
        n_gt = jnp.where(settle, jnp.where(hit, 0.0, n_hi), n_gt)
        go = jnp.logical_and(live, jnp.logical_not(jnp.logical_or(closed, hit)))
        up = jnp.logical_and(go, n > DSA_TOPK)
        down = jnp.logical_and(go, n < DSA_TOPK)
        done = jnp.where(settle, 1, done)
        if coarse:
            done = jnp.where(jnp.logical_and(live, closed), 2, done)
        return (jnp.where(up, mid, lo), jnp.where(down, mid, hi), jnp.where(up, n, n_lo),
                jnp.where(down, n, n_hi), thr, n_ge, n_gt, done, it + 1)

    state = (lo, hi, n_lo, n_hi, jnp.zeros((1, TILE), F32), n_nonneg, n_pos, at_zero.astype(I32), jnp.int32(0))
    state = lax.while_loop(search_more, lambda st: search(search(st, True), True), state)
    state = state[:7] + (jnp.where(state[7] == 2, 0, state[7]), state[8])
    state = lax.while_loop(search_more, lambda st: search(search(st, False), False), state)
    thr, n_ge, n_gt = wide(state[4]), state[5], state[6]
    need = DSA_TOPK - n_gt

    cut_bits = max(1, (s_ref.shape[0] * TILE - 1).bit_length())

    def cut_step(b, cut):
        cand = cut | (jnp.int32(1) << (cut_bits - 1 - b))
        cw = wide(cand)
        n = count(lambda half, kpos: jnp.logical_and(half == thr, kpos < cw))
        return jnp.where(n < need, cand, cut)

    cut = lax.cond(jnp.max(n_ge) > DSA_TOPK,
                   lambda: lax.fori_loop(0, cut_bits, cut_step, jnp.zeros((1, TILE), I32)),
                   lambda: jnp.full((1, TILE), NO_CUT, I32))
    cut = wide(jnp.where(n_ge > DSA_TOPK, cut, NO_CUT))

    def to_mask(c, carry):
        blk = s_ref[c].reshape(sub, 8, TILE)
        kpos = c * TILE + key_in
        keep = jnp.logical_or(blk > thr, jnp.logical_and(blk == thr, kpos <= cut))
        s_ref[c] = jnp.where(jnp.logical_and(keep, kpos <= qpos), 0.0, NEG).reshape(TILE, TILE)
        return carry

    pairs(to_mask, 0)


def _dsa_kernel(qi_ref, ki_ref, wi_ref, q_ref, k_ref, v_ref, bias_ref, o_ref,
                s_ref, b_ref, wb_ref, sc_ref, mx_ref, acc_ref):
    i = pl.program_id(1)
    pair = pl.program_id(2)

    @pl.when(jnp.logical_and(pair == 0, i == 0))
    def _():
        key = lax.broadcasted_iota(I32, (TILE, TILE), 0)
        query = lax.broadcasted_iota(I32, (TILE, TILE), 1)
        s_ref[0] = jnp.where(key <= query, 0.0, NEG)

    @pl.when(jnp.logical_and(pair == 0, i >= 1))
    def _():
        _dsa_select(i, qi_ref, ki_ref, wi_ref, s_ref, b_ref, wb_ref)

    qs = _split_pair(q_ref[0])
    plan = _far_plan(jnp.maximum(i - 1, 0))
    _sm_init(mx_ref, acc_ref)

    def vals(j, n):
        return jnp.concatenate([v_ref[0, j + u] for u in range(n)], axis=1)

    def logits(j, slot, msk, near=None):
        k_j = k_ref[0, pl.ds(pl.multiple_of(j * TILE, TILE), TILE), :]
        for hd in range(2):
            s = _nt(k_j, qs[hd]) + msk
            if near is not None:
                s = s + bias_ref[hd, near]
            _sm_logits(hd, slot, s, sc_ref, mx_ref)

    def far_logits(j0, n):
        for u in range(n):
            j = j0 + u
            logits(j, j, jnp.where(j < i - 1, s_ref[jnp.minimum(j, i)], NEG))

    _sm_far_loop(plan, far_logits)
    nq = pl.num_programs(1)
    j_prev = jnp.maximum(i - 1, 0)
    logits(j_prev, nq, jnp.where(i >= 1, s_ref[j_prev], NEG), near=1)
    logits(i, nq + 1, s_ref[i], near=2)
    _sm_rowmax(mx_ref)

    _sm_far_values(plan, vals, sc_ref, mx_ref, acc_ref)
    v_near = jnp.concatenate([vals(j_prev, 1), vals(i, 1)], axis=1)
    for hd in range(2):
        _sm_values(hd, [nq, nq + 1], v_near, sc_ref, mx_ref, acc_ref)
    o_ref[0] = _sm_output(acc_ref).astype(BF16)


def _dsa(qi, ki4, wi, qb, kb, vb_t, bias_near):
    bn, t, _ = qb.shape
    nq = t // TILE
    assert nq % 2 == 0
    npair = B_HEADS // 2
    qrow = lambda w: pl.BlockSpec((1, TILE, w), lambda b, i, p: (b, i, 0))
    return pl.pallas_call(
        _dsa_kernel,
        grid=(bn, nq, npair),
        in_specs=[qrow(IDX_HEADS * IDX_DIM),
                  pl.BlockSpec((1, t, LANES), lambda b, i, p: (b, 0, 0), pipeline_mode=pl.Buffered(1)),
                  qrow(LANES),
                  pl.BlockSpec((1, TILE, LANES), lambda b, i, p: (b, i, p)),
                  pl.BlockSpec((1, t, LANES), lambda b, i, p: (b, 0, p)),
                  pl.BlockSpec((1, nq, LANES, TILE), lambda b, i, p: (b, 0, p, 0)),
                  pl.BlockSpec((2, 3, TILE, TILE), lambda b, i, p: (p, 0, 0, 0))],
        out_specs=pl.BlockSpec((1, TILE, LANES), lambda b, i, p: (b, i, p)),
        out_shape=jax.ShapeDtypeStruct((bn, t, W_B), BF16),
        scratch_shapes=[pltpu.VMEM((nq, TILE, TILE), F32), pltpu.VMEM((nq, TILE, TILE), BF16),
                        pltpu.VMEM((IDX_HEADS, 8, TILE), F32)]
        + _sm_scratch(nq + 2),
        compiler_params=_params(("arbitrary", "arbitrary", "arbitrary")),
        name="dsa",
    )(qi, ki4, wi, qb, kb, vb_t, bias_near)


def _post_kernel(x_ref, aa_ref, ab_ref, gm_ref, wg_ref, woa_ref, wob_ref, wout_ref,
                 gx_ref, wq_ref, kvm_ref, wox_ref, gmoe_ref, wrt_ref, brt_ref,
                 x2_ref, hm_ref, rw_ref, ids_ref, cnt_ref, carry_ref):
    step = pl.program_id(0)
    xf = x_ref[...]
    h = _rms(xf, gm_ref[...]).astype(BF16)
    gates = jax.nn.sigmoid(_mm(h, wg_ref[...]))
    oa = _mm(aa_ref[...], woa_ref[...])
    ob = _mm(ab_ref[...], wob_ref[...])
    mrg = gates[:, :D_MODEL] * oa + gates[:, D_MODEL:] * ob
    x1 = xf + _mm(mrg.astype(BF16), wout_ref[...])

    hx = _rms(x1, gx_ref[...]).astype(BF16)
    q = (_mm(hx, wq_ref[...]) * (HEAD_DIM ** -0.5)).astype(BF16)
    kvm = kvm_ref[0]
    lane = lax.broadcasted_iota(I32, (TILE, LANES), 1)
    outs = []
    for p_idx in range(X_HEADS // 2):
        lo = p_idx * LANES
        qs = _split_pair(q[:, lo:lo + LANES])
        km = kvm[:, lo:lo + LANES]
        vm = kvm[:, W_X + lo:W_X + lo + LANES]
        o = []
        for hd in range(2):
            s = _nt(qs[hd], km)
            p = jnp.exp(s - jnp.max(s, axis=1, keepdims=True))
            o.append(_mm(p.astype(BF16), vm) / jnp.sum(p, axis=1, keepdims=True))
        outs.append(jnp.where(lane < HEAD_DIM, o[0], o[1]))
    xo = jnp.concatenate(outs, axis=1).astype(BF16)
    x2 = x1 + _mm(xo, wox_ref[...])
    x2_ref[...] = x2

    hm = _rms(x2, gmoe_ref[...])
    hm_ref[...] = hm
    hm_hi = hm.astype(BF16)
    hm_lo = (hm - hm_hi.astype(F32)).astype(BF16)
    hi_part = _mm(hm_hi, wrt_ref[...])
    logits = (hi_part[:, :LANES] + hi_part[:, LANES:]) + _mm(hm_lo, wrt_ref[:, :LANES]) + brt_ref[...]
    big = LANES

    def argmax(v):
        top = jnp.max(v, axis=1, keepdims=True)
        return top, jnp.min(jnp.where(v == top, lane, big), axis=1, keepdims=True)

    is_grp = lane < R_EXP
    gtop, gsel = argmax(jnp.where(is_grp, logits, -jnp.inf))
    gw = 1.0 / jnp.sum(jnp.where(is_grp, jnp.exp(logits - gtop), 0.0), axis=1, keepdims=True)
    first = R_EXP + gsel * EXPERTS_PER_GROUP
    inside = jnp.logical_and(lane >= first, lane < first + EXPERTS_PER_GROUP)
    within = jnp.where(inside, logits, -jnp.inf)
    v0, i0 = argmax(within)
    v1, i1 = argmax(jnp.where(lane == i0, -jnp.inf, within))
    e1 = jnp.exp(v1 - v0)
    w0 = gw * (1.0 / (1.0 + e1))
    w1 = gw * (e1 / (1.0 + e1))
    rw_ref[...] = jnp.where(lane == 0, w0, jnp.where(lane == 1, w1, 0.0))

    @pl.when(step == 0)
    def _():
        carry_ref[...] = jnp.zeros(carry_ref.shape, F32)

    hit0 = lane == i0
    hit1 = lane == i1
    onehot = jnp.where(jnp.logical_or(hit0, hit1), 1.0, 0.0)
    tri = (lax.broadcasted_iota(I32, (TILE, TILE), 1) < lax.broadcasted_iota(I32, (TILE, TILE), 0))
    base = carry_ref[...] + _mm(tri.astype(BF16), onehot.astype(BF16))
    r0 = jnp.sum(jnp.where(hit0, base, 0.0), axis=1, keepdims=True)
    r1 = jnp.sum(jnp.where(hit1, base, 0.0), axis=1, keepdims=True)
    total = carry_ref[...] + jnp.sum(onehot, axis=0, keepdims=True)
    carry_ref[...] = total
    cnt_ref[...] = total.astype(I32)
    slab = jnp.where(lane == 0, (i0 - R_EXP).astype(F32),
                     jnp.where(lane == 1, (i1 - R_EXP).astype(F32),
                               jnp.where(lane == 2, r0, jnp.where(lane == 3, r1, 0.0))))
    ids_ref[0] = slab.T[:8].astype(I32)


def _post(xf, aa, ab, kvm, tiles_per_batch, weights):
    n, d = xf.shape
    nt = n // TILE
    row = lambda w: pl.BlockSpec((TILE, w), lambda i: (i, 0))
    full = lambda a: pl.BlockSpec(a.shape, lambda i: (0,) * a.ndim)
    gm, wg, woa, wob, wout, gx, wq, wox, gmoe, wrt, brt = weights
    return pl.pallas_call(
        _post_kernel,
        grid=(nt,),
        in_specs=[row(d), row(W_A), row(W_B), full(gm), full(wg), full(woa), full(wob), full(wout),
                  full(gx), full(wq),
                  pl.BlockSpec((1,) + kvm.shape[1:], lambda i: (i // tiles_per_batch, 0, 0)),
                  full(wox), full(gmoe), full(wrt), full(brt)],
        out_specs=[row(d), row(d), row(LANES),
                   pl.BlockSpec((1, 8, TILE), lambda i: (i, 0, 0)),
                   pl.BlockSpec((1, LANES), lambda i: (0, 0))],
        out_shape=[jax.ShapeDtypeStruct((n, d), F32), jax.ShapeDtypeStruct((n, d), F32),
                   jax.ShapeDtypeStruct((n, LANES), F32),
                   jax.ShapeDtypeStruct((nt, 8, TILE), I32),
                   jax.ShapeDtypeStruct((1, LANES), I32)],
        scratch_shapes=[pltpu.VMEM((1, LANES), F32)],
        compiler_params=_params(("arbitrary",)),
        name="post",
    )(xf, aa, ab, gm, wg, woa, wob, wout, gx, wq, kvm, wox, gmoe, wrt, brt)


def _segment_starts(cnt_ref, seg_ref):
    def body(e, acc):
        seg_ref[e] = acc
        c = cnt_ref[0, R_EXP + e]
        return acc + ((c + (MOE_BLOCK - 1)) >> MOE_SHIFT << MOE_SHIFT)
    return lax.fori_loop(0, N_EXPERTS, body, jnp.int32(0))


def _start_row_copies(ids_ref, seg_ref, rows_ref, tile_ref, sem, gather):
    for t in range(TILE):
        for k in range(2):
            dest = seg_ref[ids_ref[0, k, t]] + ids_ref[0, 2 + k, t]
            if gather:
                cp = pltpu.make_async_copy(rows_ref.at[pl.ds(dest, 1)], tile_ref.at[k, pl.ds(t, 1)], sem)
            else:
                cp = pltpu.make_async_copy(tile_ref.at[pl.ds(t, 1)], rows_ref.at[pl.ds(dest, 1)], sem)
            cp.start(priority=k)


def _wait_row_copies(rows_ref, tile_ref, sem):
    pltpu.make_async_copy(rows_ref.at[pl.ds(0, TILE)], tile_ref, sem).wait()


def _dispatch_kernel(ids_ref, cnt_ref, hm_ref, xr_in_ref, xr_ref, blk_ref, seg_ref, sem):
    del xr_in_ref
    step = pl.program_id(0)
    nb = blk_ref.shape[1] - 1

    @pl.when(step == 0)
    def _():
        used = _segment_starts(cnt_ref, seg_ref)

        def per_expert(e, last):
            c = cnt_ref[0, R_EXP + e]
            b0 = seg_ref[e] >> MOE_SHIFT
            n = (c + (MOE_BLOCK - 1)) >> MOE_SHIFT

            def fill(kk, carry):
                blk_ref[0, b0 + kk] = e
                return carry
            lax.fori_loop(0, n, fill, 0)
            return jnp.where(n > 0, e, last)
        last = lax.fori_loop(0, N_EXPERTS, per_expert, jnp.int32(0))

        def tail(b, carry):
            blk_ref[0, b] = last
            return carry
        lax.fori_loop(used >> MOE_SHIFT, nb, tail, 0)
        blk_ref[0, nb] = used >> MOE_SHIFT

    _start_row_copies(ids_ref, seg_ref, xr_ref, hm_ref, sem, gather=False)
    for _ in range(2):
        _wait_row_copies(xr_ref, hm_ref, sem)


def _dispatch(ids, cnt, hm, n_rows):
    n, d = hm.shape
    nt = n // TILE
    nb = n_rows // MOE_BLOCK
    zeros = jnp.zeros((n_rows, d), F32)
    return pl.pallas_call(
        _dispatch_kernel,
        grid=(nt,),
        in_specs=[pl.BlockSpec((1, 8, TILE), lambda i: (i, 0, 0), memory_space=pltpu.SMEM),
                  pl.BlockSpec(memory_space=pltpu.SMEM),
                  pl.BlockSpec((TILE, d), lambda i: (i, 0)),
                  pl.BlockSpec(memory_space=pl.ANY)],
        out_specs=[pl.BlockSpec(memory_space=pl.ANY),
                   pl.BlockSpec(memory_space=pltpu.SMEM)],
        out_shape=[jax.ShapeDtypeStruct((n_rows, d), F32),
                   jax.ShapeDtypeStruct((1, nb + 1), I32)],
        scratch_shapes=[pltpu.SMEM((N_EXPERTS,), I32), pltpu.SemaphoreType.DMA(())],
        input_output_aliases={3: 0},
        compiler_params=_params(("arbitrary",)),
        name="dispatch",
    )(ids, cnt, hm, zeros)


def _experts_kernel(blk_ref, x_ref, w1_ref, w3_ref, w2_ref, y_ref, w1b, w3b, w2b):
    b = pl.program_id(0)
    nb = pl.num_programs(0)
    e = blk_ref[0, b]
    prev = blk_ref[0, jnp.maximum(b - 1, 0)]

    @pl.when(jnp.logical_or(b == 0, e != prev))
    def _():
        w1b[...] = w1_ref[0].astype(BF16)
        w3b[...] = w3_ref[0].astype(BF16)
        w2b[...] = w2_ref[0].astype(BF16)

    used = blk_ref[0, nb]

    @pl.when(b < used)
    def _():
        xb = x_ref[...].astype(BF16)
        a = _mm(xb, w1b[...])
        g = _mm(xb, w3b[...])
        y_ref[...] = _mm((a * jax.nn.sigmoid(a) * g).astype(BF16), w2b[...])

    @pl.when(b >= used)
    def _():
        y_ref[...] = jnp.zeros(y_ref.shape, F32)


def _experts(blk, xr, w1, w3, w2):
    n_rows, d = xr.shape
    nb = n_rows // MOE_BLOCK
    wspec = lambda s: pl.BlockSpec((1,) + s, lambda b, blk: (blk[0, b], 0, 0))
    return pl.pallas_call(
        _experts_kernel,
        grid_spec=pltpu.PrefetchScalarGridSpec(
            num_scalar_prefetch=1,
            grid=(nb,),
            in_specs=[pl.BlockSpec((MOE_BLOCK, d), lambda b, blk: (b, 0)),
                      wspec((d, D_EXPERT)), wspec((d, D_EXPERT)), wspec((D_EXPERT, d))],
            out_specs=pl.BlockSpec((MOE_BLOCK, d), lambda b, blk: (b, 0)),
            scratch_shapes=[pltpu.VMEM((d, D_EXPERT), BF16), pltpu.VMEM((d, D_EXPERT), BF16),
                            pltpu.VMEM((D_EXPERT, d), BF16)]),
        out_shape=jax.ShapeDtypeStruct((n_rows, d), F32),
        compiler_params=_params(("arbitrary",)),
        name="experts",
    )(blk, xr, w1, w3, w2)


def _combine_kernel(ids_ref, ids_next_ref, cnt_ref, x2_ref, rw_ref, g_ref, yr_ref, o_ref, y_ref, seg_ref, sem):
    step = pl.program_id(0)
    slot = step % 2

    @pl.when(step == 0)
    def _():
        _segment_starts(cnt_ref, seg_ref)
        _start_row_copies(ids_ref, seg_ref, yr_ref, y_ref.at[0], sem.at[0], gather=True)

    @pl.when(step + 1 < pl.num_programs(0))
    def _():
        _start_row_copies(ids_next_ref, seg_ref, yr_ref, y_ref.at[1 - slot], sem.at[1 - slot], gather=True)

    for k in range(2):
        _wait_row_copies(yr_ref, y_ref.at[slot, k], sem.at[slot])
    rw = rw_ref[...]
    x3 = x2_ref[...] + (y_ref[slot, 0] * rw[:, 0:1] + y_ref[slot, 1] * rw[:, 1:2])
    o_ref[...] = _rms(x3, g_ref[...])


def _combine(ids, cnt, x2, rw, g, yr):
    n, d = x2.shape
    nt = n // TILE
    row = lambda w: pl.BlockSpec((TILE, w), lambda i: (i, 0))
    return pl.pallas_call(
        _combine_kernel,
        grid=(nt,),
        in_specs=[pl.BlockSpec((1, 8, TILE), lambda i: (i, 0, 0), memory_space=pltpu.SMEM),
                  pl.BlockSpec((1, 8, TILE), lambda i: (jnp.minimum(i + 1, nt - 1), 0, 0),
                               memory_space=pltpu.SMEM),
                  pl.BlockSpec(memory_space=pltpu.SMEM),
                  row(d), row(LANES), pl.BlockSpec((1, d), lambda i: (0, 0)),
                  pl.BlockSpec(memory_space=pl.ANY)],
        out_specs=row(d),
        out_shape=jax.ShapeDtypeStruct((n, d), F32),
        scratch_shapes=[pltpu.VMEM((2, 2, TILE, d), F32), pltpu.SMEM((N_EXPERTS,), I32),
                        pltpu.SemaphoreType.DMA((2,))],
        compiler_params=_params(("arbitrary",)),
        name="combine",
    )(ids, ids, cnt, x2, rw, g, yr)


def _t5_bucket(dist):
    n = jnp.maximum(dist, 0)
    max_exact = REL_BUCKETS // 2
    nf = jnp.maximum(n, 1).astype(F32)
    large = max_exact + (jnp.log(nf / max_exact) / math.log(REL_MAX_DIST / max_exact)
                         * (REL_BUCKETS - max_exact)).astype(I32)
    large = jnp.minimum(large, REL_BUCKETS - 1)
    return jnp.where(n < max_exact, n, large)


def _bias_tables(tab):
    r = jnp.arange(TILE)[:, None]
    c = jnp.arange(TILE)[None, :]
    rel = tab - tab[:, REL_BUCKETS - 1:]

    def tile(dist):
        onehot = (_t5_bucket(dist)[..., None] == jnp.arange(REL_BUCKETS)).astype(F32)
        return jnp.einsum('rcb,hb->hrc', onehot, rel, precision=lax.Precision.HIGHEST)

    own = jnp.where(r >= c, tile(r - c), NEG)
    prev = tile(r - c + TILE)
    near = jnp.stack([jnp.zeros_like(prev), prev, own], axis=1).astype(F32)
    return near.swapaxes(-1, -2)


def _block_mean_slabs(kmean, bn, nq):
    km = kmean.reshape(bn, nq, A_HEADS, HEAD_DIM).transpose(0, 2, 1, 3)
    pad = ((0, 0), (0, 0), (HEAD_DIM, LANES - HEAD_DIM - nq), (0, LANES - HEAD_DIM))
    return jnp.pad(km, pad).astype(BF16)


def _pad_cols(w, width):
    return jnp.pad(w, ((0, 0), (0, width - w.shape[1])))


def kernel(x, mem, rel_bias, final_norm, norm_mix, w_in, ckv_norm, w_uk, w_uv, w_oa, w_ob,
           w_out, norm_x, mem_norm, wq_x, wk_x, wv_x, wo_x, norm_moe, w_group, b_group,
           w_router, b_router, w1, w3, w2):
    bn, t, d = x.shape
    n = bn * t
    nq = t // TILE
    assert t % TILE == 0 and nq <= LANES - HEAD_DIM and norm_mix.shape[0] == 1
    assert nq % (FAR_GROUP // 4) == 0
    near_a = _bias_tables(rel_bias[:, :A_HEADS].T)
    near_b = _bias_tables(rel_bias[:, A_HEADS:].T)
    row = lambda v: v.reshape(1, -1).astype(F32)

    wi = w_in[0]
    wp = jnp.concatenate(
        [wi[:, OFF_AQ:OFF_IK], jnp.tile(wi[:, OFF_IK:OFF_IW], (1, LANES // IDX_DIM)),
         _pad_cols(wi[:, OFF_IW:OFF_GA], LANES)], axis=1).astype(BF16)
    wkv = jnp.concatenate([w_uk[0], w_uv[0]], axis=1).astype(BF16)
    xf = x.reshape(n, d)
    qa, ka, va, qb, kb, vb, qi, ki4, widx, kmean = _proj_in(
        xf, row(norm_mix[0]), wp, row(ckv_norm[0]), wkv, nq)

    b3 = lambda a: a.reshape(bn, t, a.shape[-1])
    attn_a = _moba(b3(qa), b3(ka), va.reshape(bn, nq, W_A, TILE), _block_mean_slabs(kmean, bn, nq), near_a)
    attn_b = _dsa(b3(qi), b3(ki4), b3(widx), b3(qb), b3(kb), vb.reshape(bn, nq, W_B, TILE), near_b)

    kvm = _mem_kv(mem, row(mem_norm[0]), jnp.concatenate([wk_x[0], wv_x[0]], axis=1).astype(BF16))
    wrt = _pad_cols(jnp.concatenate([w_group[0], w_router[0]], axis=1), LANES).astype(F32)
    wrt_hi = wrt.astype(BF16)
    wrt = jnp.concatenate([wrt_hi, (wrt - wrt_hi.astype(F32)).astype(BF16)], axis=1)
    brt = _pad_cols(jnp.concatenate([b_group[0], b_router[0]]).reshape(1, -1), LANES).astype(F32)
    weights = (row(norm_mix[0]), wi[:, OFF_GA:IN_COLS].astype(BF16), w_oa[0].astype(BF16),
               w_ob[0].astype(BF16), w_out[0].astype(BF16), row(norm_x[0]), wq_x[0].astype(BF16),
               wo_x[0].astype(BF16), row(norm_moe[0]), wrt, brt)
    x2, hm, rw, ids, cnt = _post(xf, attn_a.reshape(n, W_A), attn_b.reshape(n, W_B), kvm, t // TILE, weights)

    n_rows = 2 * n + N_EXPERTS * MOE_BLOCK
    xr, blk = _dispatch(ids, cnt, hm, n_rows)
    yr = _experts(blk, xr, w1[0], w3[0], w2[0])
    out = _combine(ids, cnt, x2, rw, row(final_norm), yr)
    return out.reshape(bn, t, d)
```

```python
import functools
import math

import jax
import jax.numpy as jnp
from jax import lax
from jax.experimental import pallas as pl
from jax.experimental.pallas import tpu as pltpu

D_MODEL = 1024
HEAD_DIM = 64
A_HEADS = 8
B_HEADS = 8
W_A = A_HEADS * HEAD_DIM
W_B = B_HEADS * HEAD_DIM
MOBA_BLOCK = 256
MOBA_TOPK = 3
DSA_KV_RANK = 256
IDX_HEADS = 8
IDX_DIM = 32
DSA_TOPK = 256
REL_BUCKETS = 32
REL_MAX_DIST = 128
X_HEADS = 4
W_X = X_HEADS * HEAD_DIM
N_GROUPS = 4
EXPERTS_PER_GROUP = 8
N_EXPERTS = N_GROUPS * EXPERTS_PER_GROUP
D_EXPERT = 512
MOE_BLOCK = 256
EPS = 1e-6
NEG = -1e30

OFF_AQ = 0
OFF_AK = OFF_AQ + W_A
OFF_AV = OFF_AK + W_A
OFF_BQ = OFF_AV + W_A
OFF_CKV = OFF_BQ + W_B
OFF_IQ = OFF_CKV + DSA_KV_RANK
OFF_IK = OFF_IQ + IDX_HEADS * IDX_DIM
OFF_IW = OFF_IK + IDX_DIM
OFF_GA = OFF_IW + IDX_HEADS
OFF_GB = OFF_GA + D_MODEL
IN_COLS = OFF_GB + D_MODEL

LANES = 128
TILE = MOBA_BLOCK
assert TILE == MOE_BLOCK
SEARCH_STEPS_MAX = 400
NO_CUT = 2 ** 30
VMEM_LIMIT = 56 * 1024 * 1024
BF16 = jnp.bfloat16
F32 = jnp.float32
I32 = jnp.int32

P_QA, P_KA, P_VA, P_QB = 0, 512, 1024, 1536
P_CKV, P_IQ, P_IK4, P_IW = 2048, 2304, 2560, 2688
P_COLS = 2816

R_EXP = N_GROUPS
MOE_SHIFT = MOE_BLOCK.bit_length() - 1
assert 1 << MOE_SHIFT == MOE_BLOCK


def _nt(a, b):
    return lax.dot_general(a, b, (((1,), (1,)), ((), ())), preferred_element_type=F32)


def _mm(a, b):
    return jnp.dot(a, b, preferred_element_type=F32)


def _rms(xf, g):
    return xf * lax.rsqrt(jnp.mean(xf * xf, axis=-1, keepdims=True) + EPS) * g


def _params(sem, vmem=VMEM_LIMIT):
    return pltpu.CompilerParams(dimension_semantics=sem, vmem_limit_bytes=vmem)


def _mem_kv_kernel(mem_ref, g_ref, w_ref, o_ref):
    m = _rms(mem_ref[0], g_ref[...]).astype(BF16)
    o_ref[0] = _mm(m, w_ref[...]).astype(BF16)


def _mem_kv(mem, g, wkv):
    bn, ml, d = mem.shape
    return pl.pallas_call(
        _mem_kv_kernel,
        grid=(bn,),
        in_specs=[pl.BlockSpec((1, ml, d), lambda b: (b, 0, 0)),
                  pl.BlockSpec((1, d), lambda b: (0, 0)),
                  pl.BlockSpec((d, 2 * W_X), lambda b: (0, 0))],
        out_specs=pl.BlockSpec((1, ml, 2 * W_X), lambda b: (b, 0, 0)),
        out_shape=jax.ShapeDtypeStruct((bn, ml, 2 * W_X), BF16),
        compiler_params=_params(("arbitrary",)),
        name="mem_kv",
    )(mem, g, wkv)


def _head_slabs(zp, extra):
    lane = lax.broadcasted_iota(I32, zp.shape, 1)
    low = lane < HEAD_DIM
    return jnp.where(low, zp, extra), jnp.where(low, pltpu.roll(zp, HEAD_DIM, 1), extra)


def _proj_in_kernel(blocks_per_seq, x_ref, g_ref, w_ref, cg_ref, wkv_ref,
                    qa_ref, ka_ref, va_ref, qb_ref, kb_ref, vb_ref, qi_ref, ki_ref, wi_ref, km_ref):
    h = _rms(x_ref[...], g_ref[...]).astype(BF16)
    z = _mm(h, w_ref[...])
    scale = HEAD_DIM ** -0.5
    zk = z[:, P_KA:P_VA]
    km_ref[0] = jnp.mean(zk, axis=0, keepdims=True)
    lane = lax.broadcasted_iota(I32, (TILE, LANES), 1)
    blk = pl.program_id(0) % blocks_per_seq
    onehot = jnp.where(lane == HEAD_DIM + blk, 1.0, 0.0)
    zero = jnp.zeros((TILE, LANES), F32)
    for p_idx in range(A_HEADS // 2):
        lo = p_idx * LANES
        qe, qo = _head_slabs(z[:, P_QA + lo:P_QA + lo + LANES] * scale, zero)
        ke, ko = _head_slabs(zk[:, lo:lo + LANES], onehot)
        qa_ref[:, 2 * lo:2 * lo + 2 * LANES] = jnp.concatenate([qe, qo], axis=1).astype(BF16)
        ka_ref[:, 2 * lo:2 * lo + 2 * LANES] = jnp.concatenate([ke, ko], axis=1).astype(BF16)
    va_ref[0] = z[:, P_VA:P_QB].T.astype(BF16)
    qb_ref[...] = (z[:, P_QB:P_CKV] * scale).astype(BF16)
    ckv = _rms(z[:, P_CKV:P_IQ], cg_ref[...]).astype(BF16)
    kv = _mm(ckv, wkv_ref[...])
    kb_ref[...] = kv[:, :W_B].astype(BF16)
    vb_ref[0] = kv[:, W_B:].T.astype(BF16)
    qi_ref[...] = z[:, P_IQ:P_IK4].astype(BF16)
    ki_ref[...] = z[:, P_IK4:P_IW].astype(BF16)
    wi_ref[...] = z[:, P_IW:P_COLS] * ((IDX_HEADS ** -0.5) * (IDX_DIM ** -0.5))


def _proj_in(xf, g, wp, cg, wkv, blocks_per_seq):
    n, d = xf.shape
    nt = n // TILE
    row = lambda w: pl.BlockSpec((TILE, w), lambda i: (i, 0))
    full = lambda a: pl.BlockSpec(a.shape, lambda i: (0,) * a.ndim)
    outs = [(2 * W_A, BF16, False), (2 * W_A, BF16, False), (W_A, BF16, True), (W_B, BF16, False),
            (W_B, BF16, False), (W_B, BF16, True), (IDX_HEADS * IDX_DIM, BF16, False), (LANES, BF16, False),
            (LANES, F32, False)]
    blocked = lambda w: pl.BlockSpec((1, w, TILE), lambda i: (i, 0, 0))
    return pl.pallas_call(
        functools.partial(_proj_in_kernel, blocks_per_seq),
        grid=(nt,),
        in_specs=[row(d), full(g), full(wp), full(cg), full(wkv)],
        out_specs=[blocked(w) if tr else row(w) for w, _, tr in outs]
        + [pl.BlockSpec((1, 1, W_A), lambda i: (i, 0, 0))],
        out_shape=[jax.ShapeDtypeStruct((nt, w, TILE) if tr else (n, w), t) for w, t, tr in outs]
        + [jax.ShapeDtypeStruct((nt, 1, W_A), F32)],
        compiler_params=_params(("arbitrary",)),
        name="proj_in",
    )(xf, g, wp, cg, wkv)


SUB = TILE // 8
LOG2E = math.log2(math.e)
FAR_GROUP = 16


def _sm_init(mx_ref, acc_ref):
    mx_ref[...] = jnp.full(mx_ref.shape, NEG, F32)
    acc_ref[...] = jnp.zeros(acc_ref.shape, F32)


def _sm_logits(hd, slot, s, sc_ref, mx_ref):
    s = s * LOG2E
    sc_ref[hd, slot] = s
    mx_ref[hd] = jnp.maximum(mx_ref[hd], jnp.max(s.reshape(SUB, 8, TILE), axis=0))


def _sm_rowmax(mx_ref):
    for hd in range(2):
        mx_ref[hd] = jnp.broadcast_to(jnp.max(mx_ref[hd], axis=0, keepdims=True), (8, TILE))


def _sm_probs(hd, slots, sc_ref, mx_ref):
    m = mx_ref[hd][None]
    ps = [jnp.exp2((sc_ref[hd, sl].reshape(SUB, 8, TILE) - m).reshape(TILE, TILE).astype(BF16))
          for sl in slots]
    return ps[0] if len(ps) == 1 else jnp.concatenate(ps, axis=0)


def _sm_accumulate(hd, pb, v, acc_ref):
    row = lax.broadcasted_iota(I32, v.shape, 0)
    own = (row < HEAD_DIM) if hd == 0 else (row >= HEAD_DIM)
    acc_ref[hd] = acc_ref[hd] + _mm(jnp.where(own, v, jnp.ones_like(v)), pb)


def _sm_values(hd, slots, v, sc_ref, mx_ref, acc_ref):
    _sm_accumulate(hd, _sm_probs(hd, slots, sc_ref, mx_ref), v, acc_ref)


def _far_plan(n_blocks):
    quarter = FAR_GROUP // 4
    padded = (n_blocks + quarter - 1) // quarter * quarter
    full = padded // FAR_GROUP
    rest = padded - full * FAR_GROUP
    has_half = rest >= FAR_GROUP // 2
    half_base = full * FAR_GROUP
    quarter_base = half_base + jnp.where(has_half, FAR_GROUP // 2, 0)
    return full, has_half, half_base, rest % (FAR_GROUP // 2) != 0, quarter_base


def _sm_far_loop(plan, group_fn):
    full, has_half, half_base, has_quarter, quarter_base = plan
    lax.fori_loop(0, full, lambda g, c: group_fn(FAR_GROUP * g, FAR_GROUP) or c, 0)
    pl.when(has_half)(lambda: group_fn(half_base, FAR_GROUP // 2))
    pl.when(has_quarter)(lambda: group_fn(quarter_base, FAR_GROUP // 4))


def _sm_far_values(plan, vals, sc_ref, mx_ref, acc_ref):
    def group(j0, n):
        v = vals(j0, n)
        for hd in range(2):
            _sm_values(hd, [j0 + u for u in range(n)], v, sc_ref, mx_ref, acc_ref)

    _sm_far_loop(plan, group)


def _sm_output(acc_ref):
    num = jnp.concatenate([acc_ref[0, :HEAD_DIM], acc_ref[1, HEAD_DIM:]], axis=0)
    den = jnp.concatenate([acc_ref[0, HEAD_DIM:], acc_ref[1, :HEAD_DIM]], axis=0)
    return (num / den).T


def _split_pair(qp):
    lane = lax.broadcasted_iota(I32, qp.shape, 1)
    zero = jnp.zeros_like(qp)
    return jnp.where(lane < HEAD_DIM, qp, zero), jnp.where(lane >= HEAD_DIM, qp, zero)


def _moba_kernel(q_ref, k_ref, v_ref, km_ref, bias_ref, o_ref, sc_ref, mx_ref, acc_ref):
    i = pl.program_id(2)
    nq = pl.num_programs(2)
    n_slot = LANES - HEAD_DIM
    blk = lax.broadcasted_iota(I32, (n_slot, TILE), 0)
    in_blk = blk < nq
    past = blk < i
    no_mask = jnp.zeros((HEAD_DIM, TILE), F32)

    qm = []
    for hd in range(2):
        q = q_ref[0, :, hd * LANES:(hd + 1) * LANES]
        g = jnp.where(past, _nt(km_ref[0, hd], q)[HEAD_DIM:], -jnp.inf)
        sel = jnp.zeros((n_slot, TILE), jnp.bool_)
        for _ in range(MOBA_TOPK):
            top = jnp.max(g, axis=0, keepdims=True)
            hit = jnp.logical_and(g == top, in_blk)
            first = jnp.min(jnp.where(hit, blk, n_slot), axis=0, keepdims=True)
            pick = blk == first
            sel = jnp.logical_or(sel, jnp.logical_and(pick, past))
            g = jnp.where(pick, -jnp.inf, g)
        addm = jnp.where(sel, 0.0, NEG)
        mask = jnp.where(past, addm, jnp.where(jnp.logical_and(in_blk, blk > i), NEG, 0.0))
        qm.append(q + jnp.concatenate([no_mask, mask], axis=0).T.astype(BF16))

    def keys(j, hd):
        off = pl.multiple_of(j * TILE, TILE)
        return k_ref[0, pl.ds(off, TILE), hd * LANES:(hd + 1) * LANES]

    plan = _far_plan(i + 1)
    _sm_init(mx_ref, acc_ref)

    def logits(j0, n):
        for u in range(n):
            j = j0 + u
            tile = _near_bias_index(j, i)
            for hd in range(2):
                _sm_logits(hd, j, _nt(keys(j, hd), qm[hd]) + bias_ref[hd, tile], sc_ref, mx_ref)

    _sm_far_loop(plan, logits)
    _sm_rowmax(mx_ref)

    def vals(j, n):
        return jnp.concatenate([v_ref[0, j + u] for u in range(n)], axis=1)

    _sm_far_values(plan, vals, sc_ref, mx_ref, acc_ref)
    o_ref[0] = _sm_output(acc_ref).astype(BF16)


def _near_bias_index(j, i):
    return jnp.clip(j - i + 2, 0, 2)


def _sm_scratch(n_slots):
    return [pltpu.VMEM((2, n_slots, TILE, TILE), F32),
            pltpu.VMEM((2, 8, TILE), F32),
            pltpu.VMEM((2, LANES, TILE), F32)]


def _moba(qa, ka, va_t, kmp, bias_near):
    bn, nq = va_t.shape[:2]
    t = nq * TILE
    npair = A_HEADS // 2
    return pl.pallas_call(
        _moba_kernel,
        grid=(bn, npair, nq),
        in_specs=[pl.BlockSpec((1, TILE, 2 * LANES), lambda b, p, i: (b, i, p)),
                  pl.BlockSpec((1, t, 2 * LANES), lambda b, p, i: (b, 0, p)),
                  pl.BlockSpec((1, nq, LANES, TILE), lambda b, p, i: (b, 0, p, 0)),
                  pl.BlockSpec((1, 2, LANES, LANES), lambda b, p, i: (b, p, 0, 0)),
                  pl.BlockSpec((2, 3, TILE, TILE), lambda b, p, i: (p, 0, 0, 0))],
        out_specs=pl.BlockSpec((1, TILE, LANES), lambda b, p, i: (b, i, p)),
        out_shape=jax.ShapeDtypeStruct((bn, t, W_A), BF16),
        scratch_shapes=_sm_scratch(nq),
        compiler_params=_params(("arbitrary", "arbitrary", "arbitrary")),
        name="moba",
    )(qa, ka, va_t, kmp, bias_near)


def _dsa_select(i, qi_ref, ki_ref, wi_ref, s_ref, wb_ref):
    nch = i + 1
    sub = TILE // 8
    lane = lax.broadcasted_iota(I32, (TILE, LANES), 1)
    key_in = (lax.broadcasted_iota(I32, (sub, 8, TILE), 0) * 8 + lax.broadcasted_iota(I32, (sub, 8, TILE), 1))
    qpos = i * TILE + lax.broadcasted_iota(I32, (sub, 8, TILE), 2)

    def rows8(x):
        return jnp.broadcast_to(x, (8, TILE))[None]

    w_t = wi_ref[0].T
    qi = qi_ref[0]
    per_group = LANES // IDX_DIM
    qh = []
    for h in range(IDX_HEADS):
        g, r = divmod(h, per_group)
        qg = qi[:, g * LANES:(g + 1) * LANES]
        keep = jnp.logical_and(lane >= r * IDX_DIM, lane < (r + 1) * IDX_DIM)
        qh.append(jnp.where(keep, qg, jnp.zeros_like(qg)))
        wb_ref[h] = jnp.broadcast_to(w_t[h:h + 1], (8, TILE))

    def score(c, carry):
        off = pl.multiple_of(c * TILE, TILE)
        kc = ki_ref[0, pl.ds(off, TILE), :]
        acc = jnp.zeros((sub, 8, TILE), F32)
        for h in range(IDX_HEADS):
            acc = acc + jnp.maximum(_nt(kc, qh[h]), 0.0).reshape(sub, 8, TILE) * wb_ref[h][None]
        s_ref[c] = jnp.where(c * TILE + key_in <= qpos, acc, -jnp.inf).reshape(TILE, TILE)
        return carry

    n_pair = (nch + 1) // 2

    def pairs(step, init):
        return lax.fori_loop(0, n_pair, lambda g, acc: step(2 * g + 1, step(2 * g, acc)), init)

    pairs(score, 0)

    def fold(init, step):
        return pairs(lambda c, acc: step(acc, s_ref[c].reshape(sub, 8, TILE), c), init)

    def count(pred):
        def step(acc, blk, c):
            return acc + jnp.sum(jnp.where(pred(blk, c * TILE + key_in), 1.0, 0.0), axis=0)
        return jnp.sum(fold(jnp.zeros((8, TILE), F32), step), axis=0, keepdims=True)

    wide = rows8

    def stats(acc, blk, c):
        top, low, pos, nonneg = acc
        top = jnp.maximum(top, jnp.max(blk, axis=0))
        low = jnp.minimum(low, jnp.min(jnp.where(blk == -jnp.inf, jnp.inf, blk), axis=0))
        pos = pos + jnp.sum(jnp.where(blk > 0.0, 1.0, 0.0), axis=0)
        nonneg = nonneg + jnp.sum(jnp.where(blk >= 0.0, 1.0, 0.0), axis=0)
        return top, low, pos, nonneg

    zeros8 = jnp.zeros((8, TILE), F32)
    acc = fold((zeros8 - jnp.inf, zeros8 + jnp.inf, zeros8, zeros8), stats)
    hi = jnp.max(acc[0], axis=0, keepdims=True)
    lo = jnp.min(acc[1], axis=0, keepdims=True)
    n_pos = jnp.sum(acc[2], axis=0, keepdims=True)
    n_nonneg = jnp.sum(acc[3], axis=0, keepdims=True)

    n_lo = (i * TILE + 1 + lax.broadcasted_iota(I32, (1, TILE), 1)).astype(F32)
    hi = hi + jnp.maximum(jnp.abs(hi) * 2.0 ** -20, 1e-30)
    n_hi = jnp.zeros((1, TILE), F32)
    at_zero = jnp.logical_and(n_pos < DSA_TOPK, n_nonneg >= DSA_TOPK)
    from_zero = n_pos >= DSA_TOPK
    lo, n_lo = jnp.where(from_zero, 0.0, lo), jnp.where(from_zero, n_nonneg, n_lo)
    to_zero = n_nonneg < DSA_TOPK
    hi, n_hi = jnp.where(to_zero, 0.0, hi), jnp.where(to_zero, n_nonneg, n_hi)

    def search_more(state):
        return jnp.logical_and(jnp.min(state[7]) == 0, state[8] < SEARCH_STEPS_MAX)

    def search(state):
        lo, hi, n_lo, n_hi, thr, n_ge, n_gt, done, it = state
        mid = 0.5 * lo + 0.5 * hi
        closed = jnp.logical_or(mid <= lo, mid >= hi)
        n = count(lambda half, kpos: half >= wide(mid))
        stop = jnp.logical_or(closed, n == DSA_TOPK)
        settle = jnp.logical_and(done == 0, stop)
        thr = jnp.where(settle, jnp.where(closed, lo, mid), thr)
        n_ge = jnp.where(settle, jnp.where(closed, n_lo, n), n_ge)
        n_gt = jnp.where(settle, jnp.where(closed, n_hi, 0.0), n_gt)
        go = jnp.logical_and(done == 0, jnp.logical_not(stop))
        up = jnp.logical_and(go, n > DSA_TOPK)
        down = jnp.logical_and(go, n < DSA_TOPK)
        return (jnp.where(up, mid, lo), jnp.where(down, mid, hi), jnp.where(up, n, n_lo),
                jnp.where(down, n, n_hi), thr, n_ge, n_gt, jnp.where(stop, 1, done), it + 1)

    state = (lo, hi, n_lo, n_hi, jnp.zeros((1, TILE), F32), n_nonneg, n_pos, at_zero.astype(I32), jnp.int32(0))
    state = lax.while_loop(search_more, lambda st: search(search(st)), state)
    thr, n_ge, n_gt = wide(state[4]), state[5], state[6]
    need = DSA_TOPK - n_gt

    cut_bits = max(1, (s_ref.shape[0] * TILE - 1).bit_length())

    def cut_step(b, cut):
        cand = cut | (jnp.int32(1) << (cut_bits - 1 - b))
        cw = wide(cand)
        n = count(lambda half, kpos: jnp.logical_and(half == thr, kpos < cw))
        return jnp.where(n < need, cand, cut)

    cut = lax.cond(jnp.max(n_ge) > DSA_TOPK,
                   lambda: lax.fori_loop(0, cut_bits, cut_step, jnp.zeros((1, TILE), I32)),
                   lambda: jnp.full((1, TILE), NO_CUT, I32))
    cut = wide(jnp.where(n_ge > DSA_TOPK, cut, NO_CUT))

    def to_mask(c, carry):
        blk = s_ref[c].reshape(sub, 8, TILE)
        kpos = c * TILE + key_in
        keep = jnp.logical_or(blk > thr, jnp.logical_and(blk == thr, kpos <= cut))
        s_ref[c] = jnp.where(jnp.logical_and(keep, kpos <= qpos), 0.0, NEG).reshape(TILE, TILE)
        return carry

    pairs(to_mask, 0)


def _dsa_kernel(qi_ref, ki_ref, wi_ref, q_ref, k_ref, v_ref, bias_ref, o_ref,
                s_ref, wb_ref, sc_ref, mx_ref, acc_ref):
    i = pl.program_id(1)
    pair = pl.program_id(2)

    @pl.when(jnp.logical_and(pair == 0, i == 0))
    def _():
        key = lax.broadcasted_iota(I32, (TILE, TILE), 0)
        query = lax.broadcasted_iota(I32, (TILE, TILE), 1)
        s_ref[0] = jnp.where(key <= query, 0.0, NEG)

    @pl.when(jnp.logical_and(pair == 0, i >= 1))
    def _():
        _dsa_select(i, qi_ref, ki_ref, wi_ref, s_ref, wb_ref)

    qs = _split_pair(q_ref[0])
    plan = _far_plan(jnp.maximum(i - 1, 0))
    _sm_init(mx_ref, acc_ref)

    def vals(j, n):
        return jnp.concatenate([v_ref[0, j + u] for u in range(n)], axis=1)

    def logits(j, slot, msk, near=None):
        k_j = k_ref[0, pl.ds(pl.multiple_of(j * TILE, TILE), TILE), :]
        for hd in range(2):
            s = _nt(k_j, qs[hd]) + msk
            if near is not None:
                s = s + bias_ref[hd, near]
            _sm_logits(hd, slot, s, sc_ref, mx_ref)

    def far_logits(j0, n):
        for u in range(n):
            j = j0 + u
            logits(j, j, jnp.where(j < i - 1, s_ref[jnp.minimum(j, i)], NEG))

    _sm_far_loop(plan, far_logits)
    nq = pl.num_programs(1)
    j_prev = jnp.maximum(i - 1, 0)
    logits(j_prev, nq, jnp.where(i >= 1, s_ref[j_prev], NEG), near=1)
    logits(i, nq + 1, s_ref[i], near=2)
    _sm_rowmax(mx_ref)

    _sm_far_values(plan, vals, sc_ref, mx_ref, acc_ref)
    v_near = jnp.concatenate([vals(j_prev, 1), vals(i, 1)], axis=1)
    for hd in range(2):
        _sm_values(hd, [nq, nq + 1], v_near, sc_ref, mx_ref, acc_ref)
    o_ref[0] = _sm_output(acc_ref).astype(BF16)


def _dsa(qi, ki4, wi, qb, kb, vb_t, bias_near):
    bn, t, _ = qb.shape
    nq = t // TILE
    assert nq % 2 == 0
    npair = B_HEADS // 2
    qrow = lambda w: pl.BlockSpec((1, TILE, w), lambda b, i, p: (b, i, 0))
    return pl.pallas_call(
        _dsa_kernel,
        grid=(bn, nq, npair),
        in_specs=[qrow(IDX_HEADS * IDX_DIM),
                  pl.BlockSpec((1, t, LANES), lambda b, i, p: (b, 0, 0), pipeline_mode=pl.Buffered(1)),
                  qrow(LANES),
                  pl.BlockSpec((1, TILE, LANES), lambda b, i, p: (b, i, p)),
                  pl.BlockSpec((1, t, LANES), lambda b, i, p: (b, 0, p)),
                  pl.BlockSpec((1, nq, LANES, TILE), lambda b, i, p: (b, 0, p, 0)),
                  pl.BlockSpec((2, 3, TILE, TILE), lambda b, i, p: (p, 0, 0, 0))],
        out_specs=pl.BlockSpec((1, TILE, LANES), lambda b, i, p: (b, i, p)),
        out_shape=jax.ShapeDtypeStruct((bn, t, W_B), BF16),
        scratch_shapes=[pltpu.VMEM((nq, TILE, TILE), F32), pltpu.VMEM((IDX_HEADS, 8, TILE), F32)]
        + _sm_scratch(nq + 2),
        compiler_params=_params(("arbitrary", "arbitrary", "arbitrary")),
        name="dsa",
    )(qi, ki4, wi, qb, kb, vb_t, bias_near)


def _post_kernel(x_ref, aa_ref, ab_ref, gm_ref, wg_ref, woa_ref, wob_ref, wout_ref,
                 gx_ref, wq_ref, kvm_ref, wox_ref, gmoe_ref, wrt_ref, brt_ref,
                 x2_ref, hm_ref, rw_ref, ids_ref, cnt_ref, carry_ref):
    step = pl.program_id(0)
    xf = x_ref[...]
    h = _rms(xf, gm_ref[...]).astype(BF16)
    gates = jax.nn.sigmoid(_mm(h, wg_ref[...]))
    oa = _mm(aa_ref[...], woa_ref[...])
    ob = _mm(ab_ref[...], wob_ref[...])
    mrg = gates[:, :D_MODEL] * oa + gates[:, D_MODEL:] * ob
    x1 = xf + _mm(mrg.astype(BF16), wout_ref[...])

    hx = _rms(x1, gx_ref[...]).astype(BF16)
    q = (_mm(hx, wq_ref[...]) * (HEAD_DIM ** -0.5)).astype(BF16)
    kvm = kvm_ref[0]
    lane = lax.broadcasted_iota(I32, (TILE, LANES), 1)
    outs = []
    for p_idx in range(X_HEADS // 2):
        lo = p_idx * LANES
        qs = _split_pair(q[:, lo:lo + LANES])
        km = kvm[:, lo:lo + LANES]
        vm = kvm[:, W_X + lo:W_X + lo + LANES]
        o = []
        for hd in range(2):
            s = _nt(qs[hd], km)
            p = jnp.exp(s - jnp.max(s, axis=1, keepdims=True))
            o.append(_mm(p.astype(BF16), vm) / jnp.sum(p, axis=1, keepdims=True))
        outs.append(jnp.where(lane < HEAD_DIM, o[0], o[1]))
    xo = jnp.concatenate(outs, axis=1).astype(BF16)
    x2 = x1 + _mm(xo, wox_ref[...])
    x2_ref[...] = x2

    hm = _rms(x2, gmoe_ref[...])
    hm_ref[...] = hm
    hm_hi = hm.astype(BF16)
    hm_lo = (hm - hm_hi.astype(F32)).astype(BF16)
    hi_part = _mm(hm_hi, wrt_ref[...])
    logits = (hi_part[:, :LANES] + hi_part[:, LANES:]) + _mm(hm_lo, wrt_ref[:, :LANES]) + brt_ref[...]
    big = LANES

    def argmax(v):
        top = jnp.max(v, axis=1, keepdims=True)
        return top, jnp.min(jnp.where(v == top, lane, big), axis=1, keepdims=True)

    is_grp = lane < R_EXP
    gtop, gsel = argmax(jnp.where(is_grp, logits, -jnp.inf))
    gw = 1.0 / jnp.sum(jnp.where(is_grp, jnp.exp(logits - gtop), 0.0), axis=1, keepdims=True)
    first = R_EXP + gsel * EXPERTS_PER_GROUP
    inside = jnp.logical_and(lane >= first, lane < first + EXPERTS_PER_GROUP)
    within = jnp.where(inside, logits, -jnp.inf)
    v0, i0 = argmax(within)
    v1, i1 = argmax(jnp.where(lane == i0, -jnp.inf, within))
    e1 = jnp.exp(v1 - v0)
    w0 = gw * (1.0 / (1.0 + e1))
    w1 = gw * (e1 / (1.0 + e1))
    rw_ref[...] = jnp.where(lane == 0, w0, jnp.where(lane == 1, w1, 0.0))

    @pl.when(step == 0)
    def _():
        carry_ref[...] = jnp.zeros(carry_ref.shape, F32)

    hit0 = lane == i0
    hit1 = lane == i1
    onehot = jnp.where(jnp.logical_or(hit0, hit1), 1.0, 0.0)
    tri = (lax.broadcasted_iota(I32, (TILE, TILE), 1) < lax.broadcasted_iota(I32, (TILE, TILE), 0))
    base = carry_ref[...] + _mm(tri.astype(BF16), onehot.astype(BF16))
    r0 = jnp.sum(jnp.where(hit0, base, 0.0), axis=1, keepdims=True)
    r1 = jnp.sum(jnp.where(hit1, base, 0.0), axis=1, keepdims=True)
    total = carry_ref[...] + jnp.sum(onehot, axis=0, keepdims=True)
    carry_ref[...] = total
    cnt_ref[...] = total.astype(I32)
    slab = jnp.where(lane == 0, (i0 - R_EXP).astype(F32),
                     jnp.where(lane == 1, (i1 - R_EXP).astype(F32),
                               jnp.where(lane == 2, r0, jnp.where(lane == 3, r1, 0.0))))
    ids_ref[0] = slab.T[:8].astype(I32)


def _post(xf, aa, ab, kvm, tiles_per_batch, weights):
    n, d = xf.shape
    nt = n // TILE
    row = lambda w: pl.BlockSpec((TILE, w), lambda i: (i, 0))
    full = lambda a: pl.BlockSpec(a.shape, lambda i: (0,) * a.ndim)
    gm, wg, woa, wob, wout, gx, wq, wox, gmoe, wrt, brt = weights
    return pl.pallas_call(
        _post_kernel,
        grid=(nt,),
        in_specs=[row(d), row(W_A), row(W_B), full(gm), full(wg), full(woa), full(wob), full(wout),
                  full(gx), full(wq),
                  pl.BlockSpec((1,) + kvm.shape[1:], lambda i: (i // tiles_per_batch, 0, 0)),
                  full(wox), full(gmoe), full(wrt), full(brt)],
        out_specs=[row(d), row(d), row(LANES),
                   pl.BlockSpec((1, 8, TILE), lambda i: (i, 0, 0)),
                   pl.BlockSpec((1, LANES), lambda i: (0, 0))],
        out_shape=[jax.ShapeDtypeStruct((n, d), F32), jax.ShapeDtypeStruct((n, d), F32),
                   jax.ShapeDtypeStruct((n, LANES), F32),
                   jax.ShapeDtypeStruct((nt, 8, TILE), I32),
                   jax.ShapeDtypeStruct((1, LANES), I32)],
        scratch_shapes=[pltpu.VMEM((1, LANES), F32)],
        compiler_params=_params(("arbitrary",)),
        name="post",
    )(xf, aa, ab, gm, wg, woa, wob, wout, gx, wq, kvm, wox, gmoe, wrt, brt)


def _segment_starts(cnt_ref, seg_ref):
    def body(e, acc):
        seg_ref[e] = acc
        c = cnt_ref[0, R_EXP + e]
        return acc + ((c + (MOE_BLOCK - 1)) >> MOE_SHIFT << MOE_SHIFT)
    return lax.fori_loop(0, N_EXPERTS, body, jnp.int32(0))


def _start_row_copies(ids_ref, seg_ref, rows_ref, tile_ref, sem, gather):
    for t in range(TILE):
        for k in range(2):
            dest = seg_ref[ids_ref[0, k, t]] + ids_ref[0, 2 + k, t]
            if gather:
                cp = pltpu.make_async_copy(rows_ref.at[pl.ds(dest, 1)], tile_ref.at[k, pl.ds(t, 1)], sem)
            else:
                cp = pltpu.make_async_copy(tile_ref.at[pl.ds(t, 1)], rows_ref.at[pl.ds(dest, 1)], sem)
            cp.start(priority=k)


def _wait_row_copies(rows_ref, tile_ref, sem):
    pltpu.make_async_copy(rows_ref.at[pl.ds(0, TILE)], tile_ref, sem).wait()


def _dispatch_kernel(ids_ref, cnt_ref, hm_ref, xr_ref, blk_ref, seg_ref, zero_ref, sem, zero_sem):
    step = pl.program_id(0)
    nb = blk_ref.shape[1] - 1

    @pl.when(step == 0)
    def _():
        used = _segment_starts(cnt_ref, seg_ref) >> MOE_SHIFT
        zero_ref[...] = jnp.zeros(zero_ref.shape, F32)

        def zero_block(b):
            rows = pl.ds(pl.multiple_of(b << MOE_SHIFT, MOE_BLOCK), MOE_BLOCK)
            return pltpu.make_async_copy(zero_ref, xr_ref.at[rows], zero_sem)

        def per_expert(e, carry):
            last, started = carry
            c = cnt_ref[0, R_EXP + e]
            b0 = seg_ref[e] >> MOE_SHIFT
            n = (c + (MOE_BLOCK - 1)) >> MOE_SHIFT

            def fill(kk, carry):
                blk_ref[0, b0 + kk] = e
                return carry
            lax.fori_loop(0, n, fill, 0)
            pl.when(n > 0)(lambda: zero_block(b0 + n - 1).start())
            return jnp.where(n > 0, e, last), started + (n > 0).astype(I32)
        last, started = lax.fori_loop(0, N_EXPERTS, per_expert, (jnp.int32(0), jnp.int32(0)))

        def tail(b, carry):
            blk_ref[0, b] = last
            zero_block(b).start()
            return carry
        lax.fori_loop(used, nb, tail, 0)
        lax.fori_loop(0, started + (nb - used), lambda _, c: zero_block(0).wait() or c, 0)
        blk_ref[0, nb] = used

    _start_row_copies(ids_ref, seg_ref, xr_ref, hm_ref, sem, gather=False)
    for _ in range(2):
        _wait_row_copies(xr_ref, hm_ref, sem)


def _dispatch(ids, cnt, hm, n_rows):
    n, d = hm.shape
    nt = n // TILE
    nb = n_rows // MOE_BLOCK
    return pl.pallas_call(
        _dispatch_kernel,
        grid=(nt,),
        in_specs=[pl.BlockSpec((1, 8, TILE), lambda i: (i, 0, 0), memory_space=pltpu.SMEM),
                  pl.BlockSpec(memory_space=pltpu.SMEM),
                  pl.BlockSpec((TILE, d), lambda i: (i, 0))],
        out_specs=[pl.BlockSpec(memory_space=pl.ANY),
                   pl.BlockSpec(memory_space=pltpu.SMEM)],
        out_shape=[jax.ShapeDtypeStruct((n_rows, d), F32),
                   jax.ShapeDtypeStruct((1, nb + 1), I32)],
        scratch_shapes=[pltpu.SMEM((N_EXPERTS,), I32), pltpu.VMEM((MOE_BLOCK, d), F32),
                        pltpu.SemaphoreType.DMA(()), pltpu.SemaphoreType.DMA(())],
        compiler_params=_params(("arbitrary",)),
        name="dispatch",
    )(ids, cnt, hm)


def _experts_kernel(blk_ref, x_ref, w1_ref, w3_ref, w2_ref, y_ref, w1b, w3b, w2b):
    b = pl.program_id(0)
    nb = pl.num_programs(0)
    e = blk_ref[0, b]
    prev = blk_ref[0, jnp.maximum(b - 1, 0)]

    @pl.when(jnp.logical_or(b == 0, e != prev))
    def _():
        w1b[...] = w1_ref[0].astype(BF16)
        w3b[...] = w3_ref[0].astype(BF16)
        w2b[...] = w2_ref[0].astype(BF16)

    used = blk_ref[0, nb]

    @pl.when(b < used)
    def _():
        xb = x_ref[...].astype(BF16)
        a = _mm(xb, w1b[...])
        g = _mm(xb, w3b[...])
        y_ref[...] = _mm((a * jax.nn.sigmoid(a) * g).astype(BF16), w2b[...])

    @pl.when(b >= used)
    def _():
        y_ref[...] = jnp.zeros(y_ref.shape, F32)


def _experts(blk, xr, w1, w3, w2):
    n_rows, d = xr.shape
    nb = n_rows // MOE_BLOCK
    wspec = lambda s: pl.BlockSpec((1,) + s, lambda b, blk: (blk[0, b], 0, 0))
    return pl.pallas_call(
        _experts_kernel,
        grid_spec=pltpu.PrefetchScalarGridSpec(
            num_scalar_prefetch=1,
            grid=(nb,),
            in_specs=[pl.BlockSpec((MOE_BLOCK, d), lambda b, blk: (b, 0)),
                      wspec((d, D_EXPERT)), wspec((d, D_EXPERT)), wspec((D_EXPERT, d))],
            out_specs=pl.BlockSpec((MOE_BLOCK, d), lambda b, blk: (b, 0)),
            scratch_shapes=[pltpu.VMEM((d, D_EXPERT), BF16), pltpu.VMEM((d, D_EXPERT), BF16),
                            pltpu.VMEM((D_EXPERT, d), BF16)]),
        out_shape=jax.ShapeDtypeStruct((n_rows, d), F32),
        compiler_params=_params(("arbitrary",)),
        name="experts",
    )(blk, xr, w1, w3, w2)


def _combine_kernel(ids_ref, ids_next_ref, cnt_ref, x2_ref, rw_ref, g_ref, yr_ref, o_ref, y_ref, seg_ref, sem):
    step = pl.program_id(0)
    slot = step % 2

    @pl.when(step == 0)
    def _():
        _segment_starts(cnt_ref, seg_ref)
        _start_row_copies(ids_ref, seg_ref, yr_ref, y_ref.at[0], sem.at[0], gather=True)

    @pl.when(step + 1 < pl.num_programs(0))
    def _():
        _start_row_copies(ids_next_ref, seg_ref, yr_ref, y_ref.at[1 - slot], sem.at[1 - slot], gather=True)

    for k in range(2):
        _wait_row_copies(yr_ref, y_ref.at[slot, k], sem.at[slot])
    rw = rw_ref[...]
    x3 = x2_ref[...] + (y_ref[slot, 0] * rw[:, 0:1] + y_ref[slot, 1] * rw[:, 1:2])
    o_ref[...] = _rms(x3, g_ref[...])


def _combine(ids, cnt, x2, rw, g, yr):
    n, d = x2.shape
    nt = n // TILE
    row = lambda w: pl.BlockSpec((TILE, w), lambda i: (i, 0))
    return pl.pallas_call(
        _combine_kernel,
        grid=(nt,),
        in_specs=[pl.BlockSpec((1, 8, TILE), lambda i: (i, 0, 0), memory_space=pltpu.SMEM),
                  pl.BlockSpec((1, 8, TILE), lambda i: (jnp.minimum(i + 1, nt - 1), 0, 0),
                               memory_space=pltpu.SMEM),
                  pl.BlockSpec(memory_space=pltpu.SMEM),
                  row(d), row(LANES), pl.BlockSpec((1, d), lambda i: (0, 0)),
                  pl.BlockSpec(memory_space=pl.ANY)],
        out_specs=row(d),
        out_shape=jax.ShapeDtypeStruct((n, d), F32),
        scratch_shapes=[pltpu.VMEM((2, 2, TILE, d), F32), pltpu.SMEM((N_EXPERTS,), I32),
                        pltpu.SemaphoreType.DMA((2,))],
        compiler_params=_params(("arbitrary",)),
        name="combine",
    )(ids, ids, cnt, x2, rw, g, yr)


def _t5_bucket(dist):
    n = jnp.maximum(dist, 0)
    max_exact = REL_BUCKETS // 2
    nf = jnp.maximum(n, 1).astype(F32)
    large = max_exact + (jnp.log(nf / max_exact) / math.log(REL_MAX_DIST / max_exact)
                         * (REL_BUCKETS - max_exact)).astype(I32)
    large = jnp.minimum(large, REL_BUCKETS - 1)
    return jnp.where(n < max_exact, n, large)


def _bias_tables(tab):
    r = jnp.arange(TILE)[:, None]
    c = jnp.arange(TILE)[None, :]
    rel = tab - tab[:, REL_BUCKETS - 1:]

    def tile(dist):
        onehot = (_t5_bucket(dist)[..., None] == jnp.arange(REL_BUCKETS)).astype(F32)
        return jnp.einsum('rcb,hb->hrc', onehot, rel, precision=lax.Precision.HIGHEST)

    own = jnp.where(r >= c, tile(r - c), NEG)
    prev = tile(r - c + TILE)
    near = jnp.stack([jnp.zeros_like(prev), prev, own], axis=1).astype(F32)
    return near.swapaxes(-1, -2)


def _block_mean_slabs(kmean, bn, nq):
    km = kmean.reshape(bn, nq, A_HEADS, HEAD_DIM).transpose(0, 2, 1, 3)
    pad = ((0, 0), (0, 0), (HEAD_DIM, LANES - HEAD_DIM - nq), (0, LANES - HEAD_DIM))
    return jnp.pad(km, pad).astype(BF16)


def _pad_cols(w, width):
    return jnp.pad(w, ((0, 0), (0, width - w.shape[1])))


def kernel(x, mem, rel_bias, final_norm, norm_mix, w_in, ckv_norm, w_uk, w_uv, w_oa, w_ob,
           w_out, norm_x, mem_norm, wq_x, wk_x, wv_x, wo_x, norm_moe, w_group, b_group,
           w_router, b_router, w1, w3, w2):
    bn, t, d = x.shape
    n = bn * t
    nq = t // TILE
    assert t % TILE == 0 and nq <= LANES - HEAD_DIM and norm_mix.shape[0] == 1
    assert nq % (FAR_GROUP // 4) == 0
    near_a = _bias_tables(rel_bias[:, :A_HEADS].T)
    near_b = _bias_tables(rel_bias[:, A_HEADS:].T)
    row = lambda v: v.reshape(1, -1).astype(F32)

    wi = w_in[0]
    wp = jnp.concatenate(
        [wi[:, OFF_AQ:OFF_IK], jnp.tile(wi[:, OFF_IK:OFF_IW], (1, LANES // IDX_DIM)),
         _pad_cols(wi[:, OFF_IW:OFF_GA], LANES)], axis=1).astype(BF16)
    wkv = jnp.concatenate([w_uk[0], w_uv[0]], axis=1).astype(BF16)
    xf = x.reshape(n, d)
    qa, ka, va, qb, kb, vb, qi, ki4, widx, kmean = _proj_in(
        xf, row(norm_mix[0]), wp, row(ckv_norm[0]), wkv, nq)

    b3 = lambda a: a.reshape(bn, t, a.shape[-1])
    attn_a = _moba(b3(qa), b3(ka), va.reshape(bn, nq, W_A, TILE), _block_mean_slabs(kmean, bn, nq), near_a)
    attn_b = _dsa(b3(qi), b3(ki4), b3(widx), b3(qb), b3(kb), vb.reshape(bn, nq, W_B, TILE), near_b)

    kvm = _mem_kv(mem, row(mem_norm[0]), jnp.concatenate([wk_x[0], wv_x[0]], axis=1).astype(BF16))
    wrt = _pad_cols(jnp.concatenate([w_group[0], w_router[0]], axis=1), LANES).astype(F32)
    wrt_hi = wrt.astype(BF16)
    wrt = jnp.concatenate([wrt_hi, (wrt - wrt_hi.astype(F32)).astype(BF16)], axis=1)
    brt = _pad_cols(jnp.concatenate([b_group[0], b_router[0]]).reshape(1, -1), LANES).astype(F32)
    weights = (row(norm_mix[0]), wi[:, OFF_GA:IN_COLS].astype(BF16), w_oa[0].astype(BF16),
               w_ob[0].astype(BF16), w_out[0].astype(BF16), row(norm_x[0]), wq_x[0].astype(BF16),
               wo_x[0].astype(BF16), row(norm_moe[0]), wrt, brt)
    x2, hm, rw, ids, cnt = _post(xf, attn_a.reshape(n, W_A), attn_b.reshape(n, W_B), kvm, t // TILE, weights)

    n_rows = 2 * n + N_EXPERTS * MOE_BLOCK
    xr, blk = _dispatch(ids, cnt, hm, n_rows)
    yr = _experts(blk, xr, w1[0], w3[0], w2[0])
    out = _combine(ids, cnt, x2, rw, row(final_norm), yr)
    return out.reshape(bn, t, d)
```

```python
import functools
import math

import jax
import jax.numpy as jnp
from jax import lax
from jax.experimental import pallas as pl
from jax.experimental.pallas import tpu as pltpu

D_MODEL = 1024
HEAD_DIM = 64
A_HEADS = 8
B_HEADS = 8
W_A = A_HEADS * HEAD_DIM
W_B = B_HEADS * HEAD_DIM
MOBA_BLOCK = 256
MOBA_TOPK = 3
DSA_KV_RANK = 256
IDX_HEADS = 8
IDX_DIM = 32
DSA_TOPK = 256
REL_BUCKETS = 32
REL_MAX_DIST = 128
X_HEADS = 4
W_X = X_HEADS * HEAD_DIM
N_GROUPS = 4
EXPERTS_PER_GROUP = 8
N_EXPERTS = N_GROUPS * EXPERTS_PER_GROUP
D_EXPERT = 512
MOE_BLOCK = 256
EPS = 1e-6
NEG = -1e30

OFF_AQ = 0
OFF_AK = OFF_AQ + W_A
OFF_AV = OFF_AK + W_A
OFF_BQ = OFF_AV + W_A
OFF_CKV = OFF_BQ + W_B
OFF_IQ = OFF_CKV + DSA_KV_RANK
OFF_IK = OFF_IQ + IDX_HEADS * IDX_DIM
OFF_IW = OFF_IK + IDX_DIM
OFF_GA = OFF_IW + IDX_HEADS
OFF_GB = OFF_GA + D_MODEL
IN_COLS = OFF_GB + D_MODEL

LANES = 128
TILE = MOBA_BLOCK
assert TILE == MOE_BLOCK
SEARCH_STEPS_MAX = 400
NO_CUT = 2 ** 30
VMEM_LIMIT = 56 * 1024 * 1024
BF16 = jnp.bfloat16
F32 = jnp.float32
I32 = jnp.int32

P_QA = 0
P_KA = P_QA + W_A
P_VA = P_KA + W_A
P_QB = P_VA + W_A
P_CKV = P_QB + W_B
P_IQ = P_CKV + DSA_KV_RANK
P_IK4 = P_IQ + IDX_HEADS * IDX_DIM
P_IW = P_IK4 + LANES
P_COLS = P_IW + LANES

R_EXP = N_GROUPS
MOE_SHIFT = MOE_BLOCK.bit_length() - 1
assert 1 << MOE_SHIFT == MOE_BLOCK


def _nt(a, b):
    return lax.dot_general(a, b, (((1,), (1,)), ((), ())), preferred_element_type=F32)


def _mm(a, b):
    return jnp.dot(a, b, preferred_element_type=F32)


def _rms(xf, g):
    return xf * lax.rsqrt(jnp.mean(xf * xf, axis=-1, keepdims=True) + EPS) * g


def _params(sem, vmem=VMEM_LIMIT):
    return pltpu.CompilerParams(dimension_semantics=sem, vmem_limit_bytes=vmem)


def _mem_kv_kernel(mem_ref, g_ref, w_ref, o_ref):
    m = _rms(mem_ref[0], g_ref[...]).astype(BF16)
    o_ref[0] = _mm(m, w_ref[...]).astype(BF16)


def _mem_kv(mem, g, wkv):
    bn, ml, d = mem.shape
    return pl.pallas_call(
        _mem_kv_kernel,
        grid=(bn,),
        in_specs=[pl.BlockSpec((1, ml, d), lambda b: (b, 0, 0)),
                  pl.BlockSpec((1, d), lambda b: (0, 0)),
                  pl.BlockSpec((d, 2 * W_X), lambda b: (0, 0))],
        out_specs=pl.BlockSpec((1, ml, 2 * W_X), lambda b: (b, 0, 0)),
        out_shape=jax.ShapeDtypeStruct((bn, ml, 2 * W_X), BF16),
        compiler_params=_params(("arbitrary",)),
        name="mem_kv",
    )(mem, g, wkv)


def _head_slabs(zp, extra):
    lane = lax.broadcasted_iota(I32, zp.shape, 1)
    low = lane < HEAD_DIM
    return jnp.where(low, zp, extra), jnp.where(low, pltpu.roll(zp, HEAD_DIM, 1), extra)


def _proj_in_kernel(blocks_per_seq, x_ref, g_ref, w_ref, cg_ref, wkv_ref,
                    qa_ref, ka_ref, va_ref, qb_ref, kb_ref, vb_ref, qi_ref, ki_ref, wi_ref, km_ref):
    h = _rms(x_ref[...], g_ref[...]).astype(BF16)
    z = _mm(h, w_ref[...])
    scale = HEAD_DIM ** -0.5
    zk = z[:, P_KA:P_VA]
    km_ref[0] = jnp.mean(zk, axis=0, keepdims=True)
    lane = lax.broadcasted_iota(I32, (TILE, LANES), 1)
    blk = pl.program_id(0) % blocks_per_seq
    onehot = jnp.where(lane == HEAD_DIM + blk, 1.0, 0.0)
    zero = jnp.zeros((TILE, LANES), F32)
    for p_idx in range(A_HEADS // 2):
        lo = p_idx * LANES
        qe, qo = _head_slabs(z[:, P_QA + lo:P_QA + lo + LANES] * scale, zero)
        ke, ko = _head_slabs(zk[:, lo:lo + LANES], onehot)
        qa_ref[:, 2 * lo:2 * lo + 2 * LANES] = jnp.concatenate([qe, qo], axis=1).astype(BF16)
        ka_ref[:, 2 * lo:2 * lo + 2 * LANES] = jnp.concatenate([ke, ko], axis=1).astype(BF16)
    va_ref[0] = z[:, P_VA:P_QB].T.astype(BF16)
    qb_ref[...] = (z[:, P_QB:P_CKV] * scale).astype(BF16)
    ckv = _rms(z[:, P_CKV:P_IQ], cg_ref[...]).astype(BF16)
    kv = _mm(ckv, wkv_ref[...])
    kb_ref[...] = kv[:, :W_B].astype(BF16)
    vb_ref[0] = kv[:, W_B:].T.astype(BF16)
    qi_ref[...] = z[:, P_IQ:P_IK4].astype(BF16)
    ki_ref[...] = z[:, P_IK4:P_IW].astype(BF16)
    wi_ref[...] = z[:, P_IW:P_COLS] * ((IDX_HEADS ** -0.5) * (IDX_DIM ** -0.5))


def _proj_in(xf, g, wp, cg, wkv, blocks_per_seq):
    n, d = xf.shape
    nt = n // TILE
    row = lambda w: pl.BlockSpec((TILE, w), lambda i: (i, 0))
    full = lambda a: pl.BlockSpec(a.shape, lambda i: (0,) * a.ndim)
    outs = [(2 * W_A, BF16, False), (2 * W_A, BF16, False), (W_A, BF16, True), (W_B, BF16, False),
            (W_B, BF16, False), (W_B, BF16, True), (IDX_HEADS * IDX_DIM, BF16, False), (LANES, BF16, False),
            (LANES, F32, False)]
    blocked = lambda w: pl.BlockSpec((1, w, TILE), lambda i: (i, 0, 0))
    return pl.pallas_call(
        functools.partial(_proj_in_kernel, blocks_per_seq),
        grid=(nt,),
        in_specs=[row(d), full(g), full(wp), full(cg), full(wkv)],
        out_specs=[blocked(w) if tr else row(w) for w, _, tr in outs]
        + [pl.BlockSpec((1, 1, W_A), lambda i: (i, 0, 0))],
        out_shape=[jax.ShapeDtypeStruct((nt, w, TILE) if tr else (n, w), t) for w, t, tr in outs]
        + [jax.ShapeDtypeStruct((nt, 1, W_A), F32)],
        compiler_params=_params(("arbitrary",)),
        name="proj_in",
    )(xf, g, wp, cg, wkv)


SUB = TILE // 8
LOG2E = math.log2(math.e)
FAR_GROUP = 16


def _sm_init(mx_ref, acc_ref):
    mx_ref[...] = jnp.full(mx_ref.shape, NEG, F32)
    acc_ref[...] = jnp.zeros(acc_ref.shape, F32)


def _sm_logits(hd, slot, s, sc_ref, mx_ref):
    s = s * LOG2E
    sc_ref[hd, slot] = s
    mx_ref[hd] = jnp.maximum(mx_ref[hd], jnp.max(s.reshape(SUB, 8, TILE), axis=0))


def _sm_rowmax(mx_ref):
    for hd in range(2):
        mx_ref[hd] = jnp.broadcast_to(jnp.max(mx_ref[hd], axis=0, keepdims=True), (8, TILE))


def _sm_probs(hd, slots, sc_ref, mx_ref):
    m = mx_ref[hd][None]
    ps = [jnp.exp2((sc_ref[hd, sl].reshape(SUB, 8, TILE) - m).reshape(TILE, TILE).astype(BF16))
          for sl in slots]
    return ps[0] if len(ps) == 1 else jnp.concatenate(ps, axis=0)


def _sm_accumulate(hd, pb, v, acc_ref):
    row = lax.broadcasted_iota(I32, v.shape, 0)
    own = (row < HEAD_DIM) if hd == 0 else (row >= HEAD_DIM)
    acc_ref[hd] = acc_ref[hd] + _mm(jnp.where(own, v, jnp.ones_like(v)), pb)


def _sm_values(hd, slots, v, sc_ref, mx_ref, acc_ref):
    _sm_accumulate(hd, _sm_probs(hd, slots, sc_ref, mx_ref), v, acc_ref)


def _far_plan(n_blocks):
    quarter = FAR_GROUP // 4
    padded = (n_blocks + quarter - 1) // quarter * quarter
    full = padded // FAR_GROUP
    rest = padded - full * FAR_GROUP
    has_half = rest >= FAR_GROUP // 2
    half_base = full * FAR_GROUP
    quarter_base = half_base + jnp.where(has_half, FAR_GROUP // 2, 0)
    return full, has_half, half_base, rest % (FAR_GROUP // 2) != 0, quarter_base


def _sm_far_loop(plan, group_fn):
    full, has_half, half_base, has_quarter, quarter_base = plan
    lax.fori_loop(0, full, lambda g, c: group_fn(FAR_GROUP * g, FAR_GROUP) or c, 0)
    pl.when(has_half)(lambda: group_fn(half_base, FAR_GROUP // 2))
    pl.when(has_quarter)(lambda: group_fn(quarter_base, FAR_GROUP // 4))


def _sm_far_values(plan, vals, sc_ref, mx_ref, acc_ref):
    def group(j0, n):
        v = vals(j0, n)
        for hd in range(2):
            _sm_values(hd, [j0 + u for u in range(n)], v, sc_ref, mx_ref, acc_ref)

    _sm_far_loop(plan, group)


def _sm_output(acc_ref):
    num = jnp.concatenate([acc_ref[0, :HEAD_DIM], acc_ref[1, HEAD_DIM:]], axis=0)
    den = jnp.concatenate([acc_ref[0, HEAD_DIM:], acc_ref[1, :HEAD_DIM]], axis=0)
    return (num / den).T


def _split_pair(qp):
    lane = lax.broadcasted_iota(I32, qp.shape, 1)
    zero = jnp.zeros_like(qp)
    return jnp.where(lane < HEAD_DIM, qp, zero), jnp.where(lane >= HEAD_DIM, qp, zero)


def _moba_kernel(q_ref, k_ref, v_ref, km_ref, bias_ref, o_ref, sc_ref, mx_ref, acc_ref):
    i = pl.program_id(2)
    nq = pl.num_programs(2)
    n_slot = LANES - HEAD_DIM
    blk = lax.broadcasted_iota(I32, (n_slot, TILE), 0)
    in_blk = blk < nq
    past = blk < i
    no_mask = jnp.zeros((HEAD_DIM, TILE), F32)

    qm = []
    for hd in range(2):
        q = q_ref[0, :, hd * LANES:(hd + 1) * LANES]
        g = jnp.where(past, _nt(km_ref[0, hd], q)[HEAD_DIM:], -jnp.inf)
        sel = jnp.zeros((n_slot, TILE), jnp.bool_)
        for _ in range(MOBA_TOPK):
            top = jnp.max(g, axis=0, keepdims=True)
            hit = jnp.logical_and(g == top, in_blk)
            first = jnp.min(jnp.where(hit, blk, n_slot), axis=0, keepdims=True)
            pick = blk == first
            sel = jnp.logical_or(sel, jnp.logical_and(pick, past))
            g = jnp.where(pick, -jnp.inf, g)
        addm = jnp.where(sel, 0.0, NEG)
        mask = jnp.where(past, addm, jnp.where(jnp.logical_and(in_blk, blk > i), NEG, 0.0))
        qm.append(q + jnp.concatenate([no_mask, mask], axis=0).T.astype(BF16))

    def keys(j, hd):
        off = pl.multiple_of(j * TILE, TILE)
        return k_ref[0, pl.ds(off, TILE), hd * LANES:(hd + 1) * LANES]

    plan = _far_plan(i + 1)
    _sm_init(mx_ref, acc_ref)

    def logits(j0, n):
        for u in range(n):
            j = j0 + u
            tile = _near_bias_index(j, i)
            for hd in range(2):
                _sm_logits(hd, j, _nt(keys(j, hd), qm[hd]) + bias_ref[hd, tile], sc_ref, mx_ref)

    _sm_far_loop(plan, logits)
    _sm_rowmax(mx_ref)

    def vals(j, n):
        return jnp.concatenate([v_ref[0, j + u] for u in range(n)], axis=1)

    _sm_far_values(plan, vals, sc_ref, mx_ref, acc_ref)
    o_ref[0] = _sm_output(acc_ref).astype(BF16)


def _near_bias_index(j, i):
    return jnp.clip(j - i + 2, 0, 2)


def _sm_scratch(n_slots):
    return [pltpu.VMEM((2, n_slots, TILE, TILE), F32),
            pltpu.VMEM((2, 8, TILE), F32),
            pltpu.VMEM((2, LANES, TILE), F32)]


def _moba(qa, ka, va_t, kmp, bias_near):
    bn, nq = va_t.shape[:2]
    t = nq * TILE
    npair = A_HEADS // 2
    return pl.pallas_call(
        _moba_kernel,
        grid=(bn, npair, nq),
        in_specs=[pl.BlockSpec((1, TILE, 2 * LANES), lambda b, p, i: (b, i, p)),
                  pl.BlockSpec((1, t, 2 * LANES), lambda b, p, i: (b, 0, p)),
                  pl.BlockSpec((1, nq, LANES, TILE), lambda b, p, i: (b, 0, p, 0)),
                  pl.BlockSpec((1, 2, LANES, LANES), lambda b, p, i: (b, p, 0, 0)),
                  pl.BlockSpec((2, 3, TILE, TILE), lambda b, p, i: (p, 0, 0, 0))],
        out_specs=pl.BlockSpec((1, TILE, LANES), lambda b, p, i: (b, i, p)),
        out_shape=jax.ShapeDtypeStruct((bn, t, W_A), BF16),
        scratch_shapes=_sm_scratch(nq),
        compiler_params=_params(("arbitrary", "arbitrary", "arbitrary")),
        name="moba",
    )(qa, ka, va_t, kmp, bias_near)


def _dsa_select(i, qi_ref, ki_ref, wi_ref, s_ref, wb_ref):
    nch = i + 1
    sub = TILE // 8
    lane = lax.broadcasted_iota(I32, (TILE, LANES), 1)
    key_in = (lax.broadcasted_iota(I32, (sub, 8, TILE), 0) * 8 + lax.broadcasted_iota(I32, (sub, 8, TILE), 1))
    qpos = i * TILE + lax.broadcasted_iota(I32, (sub, 8, TILE), 2)

    def rows8(x):
        return jnp.broadcast_to(x, (8, TILE))[None]

    w_t = wi_ref[0].T
    qi = qi_ref[0]
    per_group = LANES // IDX_DIM
    qh = []
    for h in range(IDX_HEADS):
        g, r = divmod(h, per_group)
        qg = qi[:, g * LANES:(g + 1) * LANES]
        keep = jnp.logical_and(lane >= r * IDX_DIM, lane < (r + 1) * IDX_DIM)
        qh.append(jnp.where(keep, qg, jnp.zeros_like(qg)))
        wb_ref[h] = jnp.broadcast_to(w_t[h:h + 1], (8, TILE))

    def score(c, carry):
        off = pl.multiple_of(c * TILE, TILE)
        kc = ki_ref[0, pl.ds(off, TILE), :]
        acc = jnp.zeros((sub, 8, TILE), F32)
        for h in range(IDX_HEADS):
            acc = acc + jnp.maximum(_nt(kc, qh[h]), 0.0).reshape(sub, 8, TILE) * wb_ref[h][None]
        s_ref[c] = jnp.where(c * TILE + key_in <= qpos, acc, -jnp.inf).reshape(TILE, TILE)
        return carry

    n_pair = (nch + 1) // 2

    def pairs(step, init):
        return lax.fori_loop(0, n_pair, lambda g, acc: step(2 * g + 1, step(2 * g, acc)), init)

    pairs(score, 0)

    def fold(init, step):
        return pairs(lambda c, acc: step(acc, s_ref[c].reshape(sub, 8, TILE), c), init)

    def count(pred):
        def step(acc, blk, c):
            return acc + jnp.sum(jnp.where(pred(blk, c * TILE + key_in), 1.0, 0.0), axis=0)
        return jnp.sum(fold(jnp.zeros((8, TILE), F32), step), axis=0, keepdims=True)

    wide = rows8

    def stats(acc, blk, c):
        top, low, pos, nonneg = acc
        top = jnp.maximum(top, jnp.max(blk, axis=0))
        low = jnp.minimum(low, jnp.min(jnp.where(blk == -jnp.inf, jnp.inf, blk), axis=0))
        pos = pos + jnp.sum(jnp.where(blk > 0.0, 1.0, 0.0), axis=0)
        nonneg = nonneg + jnp.sum(jnp.where(blk >= 0.0, 1.0, 0.0), axis=0)
        return top, low, pos, nonneg

    zeros8 = jnp.zeros((8, TILE), F32)
    acc = fold((zeros8 - jnp.inf, zeros8 + jnp.inf, zeros8, zeros8), stats)
    hi = jnp.max(acc[0], axis=0, keepdims=True)
    lo = jnp.min(acc[1], axis=0, keepdims=True)
    n_pos = jnp.sum(acc[2], axis=0, keepdims=True)
    n_nonneg = jnp.sum(acc[3], axis=0, keepdims=True)

    n_lo = (i * TILE + 1 + lax.broadcasted_iota(I32, (1, TILE), 1)).astype(F32)
    hi = hi + jnp.maximum(jnp.abs(hi) * 2.0 ** -20, 1e-30)
    n_hi = jnp.zeros((1, TILE), F32)
    at_zero = jnp.logical_and(n_pos < DSA_TOPK, n_nonneg >= DSA_TOPK)
    from_zero = n_pos >= DSA_TOPK
    lo, n_lo = jnp.where(from_zero, 0.0, lo), jnp.where(from_zero, n_nonneg, n_lo)
    to_zero = n_nonneg < DSA_TOPK
    hi, n_hi = jnp.where(to_zero, 0.0, hi), jnp.where(to_zero, n_nonneg, n_hi)

    def search_more(state):
        return jnp.logical_and(jnp.min(state[7]) == 0, state[8] < SEARCH_STEPS_MAX)

    def search(state):
        lo, hi, n_lo, n_hi, thr, n_ge, n_gt, done, it = state
        mid = 0.5 * lo + 0.5 * hi
        closed = jnp.logical_or(mid <= lo, mid >= hi)
        n = count(lambda half, kpos: half >= wide(mid))
        stop = jnp.logical_or(closed, n == DSA_TOPK)
        settle = jnp.logical_and(done == 0, stop)
        thr = jnp.where(settle, jnp.where(closed, lo, mid), thr)
        n_ge = jnp.where(settle, jnp.where(closed, n_lo, n), n_ge)
        n_gt = jnp.where(settle, jnp.where(closed, n_hi, 0.0), n_gt)
        go = jnp.logical_and(done == 0, jnp.logical_not(stop))
        up = jnp.logical_and(go, n > DSA_TOPK)
        down = jnp.logical_and(go, n < DSA_TOPK)
        return (jnp.where(up, mid, lo), jnp.where(down, mid, hi), jnp.where(up, n, n_lo),
                jnp.where(down, n, n_hi), thr, n_ge, n_gt, jnp.where(stop, 1, done), it + 1)

    state = (lo, hi, n_lo, n_hi, jnp.zeros((1, TILE), F32), n_nonneg, n_pos, at_zero.astype(I32), jnp.int32(0))
    state = lax.while_loop(search_more, lambda st: search(search(st)), state)
    thr, n_ge, n_gt = wide(state[4]), state[5], state[6]
    need = DSA_TOPK - n_gt

    cut_bits = max(1, (s_ref.shape[0] * TILE - 1).bit_length())

    def cut_step(b, cut):
        cand = cut | (jnp.int32(1) << (cut_bits - 1 - b))
        cw = wide(cand)
        n = count(lambda half, kpos: jnp.logical_and(half == thr, kpos < cw))
        return jnp.where(n < need, cand, cut)

    cut = lax.cond(jnp.max(n_ge) > DSA_TOPK,
                   lambda: lax.fori_loop(0, cut_bits, cut_step, jnp.zeros((1, TILE), I32)),
                   lambda: jnp.full((1, TILE), NO_CUT, I32))
    cut = wide(jnp.where(n_ge > DSA_TOPK, cut, NO_CUT))

    def to_mask(c, carry):
        blk = s_ref[c].reshape(sub, 8, TILE)
        kpos = c * TILE + key_in
        keep = jnp.logical_or(blk > thr, jnp.logical_and(blk == thr, kpos <= cut))
        s_ref[c] = jnp.where(jnp.logical_and(keep, kpos <= qpos), 0.0, NEG).reshape(TILE, TILE)
        return carry

    pairs(to_mask, 0)


def _dsa_kernel(qi_ref, ki_ref, wi_ref, q_ref, k_ref, v_ref, bias_ref, o_ref,
                s_ref, wb_ref, sc_ref, mx_ref, acc_ref):
    i = pl.program_id(1)
    pair = pl.program_id(2)

    @pl.when(jnp.logical_and(pair == 0, i == 0))
    def _():
        key = lax.broadcasted_iota(I32, (TILE, TILE), 0)
        query = lax.broadcasted_iota(I32, (TILE, TILE), 1)
        s_ref[0] = jnp.where(key <= query, 0.0, NEG)

    @pl.when(jnp.logical_and(pair == 0, i >= 1))
    def _():
        _dsa_select(i, qi_ref, ki_ref, wi_ref, s_ref, wb_ref)

    qs = _split_pair(q_ref[0])
    plan = _far_plan(jnp.maximum(i - 1, 0))
    _sm_init(mx_ref, acc_ref)

    def vals(j, n):
        return jnp.concatenate([v_ref[0, j + u] for u in range(n)], axis=1)

    def logits(j, slot, msk, near=None):
        k_j = k_ref[0, pl.ds(pl.multiple_of(j * TILE, TILE), TILE), :]
        for hd in range(2):
            s = _nt(k_j, qs[hd]) + msk
            if near is not None:
                s = s + bias_ref[hd, near]
            _sm_logits(hd, slot, s, sc_ref, mx_ref)

    def far_logits(j0, n):
        for u in range(n):
            j = j0 + u
            logits(j, j, jnp.where(j < i - 1, s_ref[jnp.minimum(j, i)], NEG))

    _sm_far_loop(plan, far_logits)
    nq = pl.num_programs(1)
    j_prev = jnp.maximum(i - 1, 0)
    logits(j_prev, nq, jnp.where(i >= 1, s_ref[j_prev], NEG), near=1)
    logits(i, nq + 1, s_ref[i], near=2)
    _sm_rowmax(mx_ref)

    _sm_far_values(plan, vals, sc_ref, mx_ref, acc_ref)
    v_near = jnp.concatenate([vals(j_prev, 1), vals(i, 1)], axis=1)
    for hd in range(2):
        _sm_values(hd, [nq, nq + 1], v_near, sc_ref, mx_ref, acc_ref)
    o_ref[0] = _sm_output(acc_ref).astype(BF16)


def _dsa(qi, ki4, wi, qb, kb, vb_t, bias_near):
    bn, t, _ = qb.shape
    nq = t // TILE
    assert nq % 2 == 0
    npair = B_HEADS // 2
    qrow = lambda w: pl.BlockSpec((1, TILE, w), lambda b, i, p: (b, i, 0))
    return pl.pallas_call(
        _dsa_kernel,
        grid=(bn, nq, npair),
        in_specs=[qrow(IDX_HEADS * IDX_DIM),
                  pl.BlockSpec((1, t, LANES), lambda b, i, p: (b, 0, 0), pipeline_mode=pl.Buffered(1)),
                  qrow(LANES),
                  pl.BlockSpec((1, TILE, LANES), lambda b, i, p: (b, i, p)),
                  pl.BlockSpec((1, t, LANES), lambda b, i, p: (b, 0, p)),
                  pl.BlockSpec((1, nq, LANES, TILE), lambda b, i, p: (b, 0, p, 0)),
                  pl.BlockSpec((2, 3, TILE, TILE), lambda b, i, p: (p, 0, 0, 0))],
        out_specs=pl.BlockSpec((1, TILE, LANES), lambda b, i, p: (b, i, p)),
        out_shape=jax.ShapeDtypeStruct((bn, t, W_B), BF16),
        scratch_shapes=[pltpu.VMEM((nq, TILE, TILE), F32), pltpu.VMEM((IDX_HEADS, 8, TILE), F32)]
        + _sm_scratch(nq + 2),
        compiler_params=_params(("arbitrary", "arbitrary", "arbitrary")),
        name="dsa",
    )(qi, ki4, wi, qb, kb, vb_t, bias_near)


def _post_kernel(x_ref, aa_ref, ab_ref, gm_ref, wg_ref, woa_ref, wob_ref, wout_ref,
                 gx_ref, wq_ref, kvm_ref, wox_ref, gmoe_ref, wrt_ref, brt_ref,
                 x2_ref, hm_ref, rw_ref, ids_ref, cnt_ref, carry_ref):
    step = pl.program_id(0)
    xf = x_ref[...]
    h = _rms(xf, gm_ref[...]).astype(BF16)
    gates = jax.nn.sigmoid(_mm(h, wg_ref[...]))
    oa = _mm(aa_ref[...], woa_ref[...])
    ob = _mm(ab_ref[...], wob_ref[...])
    mrg = gates[:, :D_MODEL] * oa + gates[:, D_MODEL:] * ob
    x1 = xf + _mm(mrg.astype(BF16), wout_ref[...])

    hx = _rms(x1, gx_ref[...]).astype(BF16)
    q = (_mm(hx, wq_ref[...]) * (HEAD_DIM ** -0.5)).astype(BF16)
    kvm = kvm_ref[0]
    lane = lax.broadcasted_iota(I32, (TILE, LANES), 1)
    outs = []
    for p_idx in range(X_HEADS // 2):
        lo = p_idx * LANES
        qs = _split_pair(q[:, lo:lo + LANES])
        km = kvm[:, lo:lo + LANES]
        vm = kvm[:, W_X + lo:W_X + lo + LANES]
        o = []
        for hd in range(2):
            s = _nt(qs[hd], km)
            p = jnp.exp(s - jnp.max(s, axis=1, keepdims=True))
            o.append(_mm(p.astype(BF16), vm) / jnp.sum(p, axis=1, keepdims=True))
        outs.append(jnp.where(lane < HEAD_DIM, o[0], o[1]))
    xo = jnp.concatenate(outs, axis=1).astype(BF16)
    x2 = x1 + _mm(xo, wox_ref[...])
    x2_ref[...] = x2

    hm = _rms(x2, gmoe_ref[...])
    hm_ref[...] = hm
    hm_hi = hm.astype(BF16)
    hm_lo = (hm - hm_hi.astype(F32)).astype(BF16)
    hi_part = _mm(hm_hi, wrt_ref[...])
    logits = (hi_part[:, :LANES] + hi_part[:, LANES:]) + _mm(hm_lo, wrt_ref[:, :LANES]) + brt_ref[...]
    big = LANES

    def argmax(v):
        top = jnp.max(v, axis=1, keepdims=True)
        return top, jnp.min(jnp.where(v == top, lane, big), axis=1, keepdims=True)

    is_grp = lane < R_EXP
    gtop, gsel = argmax(jnp.where(is_grp, logits, -jnp.inf))
    gw = 1.0 / jnp.sum(jnp.where(is_grp, jnp.exp(logits - gtop), 0.0), axis=1, keepdims=True)
    first = R_EXP + gsel * EXPERTS_PER_GROUP
    inside = jnp.logical_and(lane >= first, lane < first + EXPERTS_PER_GROUP)
    within = jnp.where(inside, logits, -jnp.inf)
    v0, i0 = argmax(within)
    v1, i1 = argmax(jnp.where(lane == i0, -jnp.inf, within))
    e1 = jnp.exp(v1 - v0)
    w0 = gw * (1.0 / (1.0 + e1))
    w1 = gw * (e1 / (1.0 + e1))
    rw_ref[...] = jnp.where(lane == 0, w0, jnp.where(lane == 1, w1, 0.0))

    @pl.when(step == 0)
    def _():
        carry_ref[...] = jnp.zeros(carry_ref.shape, F32)

    hit0 = lane == i0
    hit1 = lane == i1
    onehot = jnp.where(jnp.logical_or(hit0, hit1), 1.0, 0.0)
    tri = (lax.broadcasted_iota(I32, (TILE, TILE), 1) < lax.broadcasted_iota(I32, (TILE, TILE), 0))
    base = carry_ref[...] + _mm(tri.astype(BF16), onehot.astype(BF16))
    r0 = jnp.sum(jnp.where(hit0, base, 0.0), axis=1, keepdims=True)
    r1 = jnp.sum(jnp.where(hit1, base, 0.0), axis=1, keepdims=True)
    total = carry_ref[...] + jnp.sum(onehot, axis=0, keepdims=True)
    carry_ref[...] = total
    cnt_ref[...] = total.astype(I32)
    slab = jnp.where(lane == 0, (i0 - R_EXP).astype(F32),
                     jnp.where(lane == 1, (i1 - R_EXP).astype(F32),
                               jnp.where(lane == 2, r0, jnp.where(lane == 3, r1, 0.0))))
    ids_ref[0] = slab.T[:8].astype(I32)


def _post(xf, aa, ab, kvm, tiles_per_batch, weights):
    n, d = xf.shape
    nt = n // TILE
    row = lambda w: pl.BlockSpec((TILE, w), lambda i: (i, 0))
    full = lambda a: pl.BlockSpec(a.shape, lambda i: (0,) * a.ndim)
    gm, wg, woa, wob, wout, gx, wq, wox, gmoe, wrt, brt = weights
    return pl.pallas_call(
        _post_kernel,
        grid=(nt,),
        in_specs=[row(d), row(W_A), row(W_B), full(gm), full(wg), full(woa), full(wob), full(wout),
                  full(gx), full(wq),
                  pl.BlockSpec((1,) + kvm.shape[1:], lambda i: (i // tiles_per_batch, 0, 0)),
                  full(wox), full(gmoe), full(wrt), full(brt)],
        out_specs=[row(d), row(d), row(LANES),
                   pl.BlockSpec((1, 8, TILE), lambda i: (i, 0, 0)),
                   pl.BlockSpec((1, LANES), lambda i: (0, 0))],
        out_shape=[jax.ShapeDtypeStruct((n, d), F32), jax.ShapeDtypeStruct((n, d), F32),
                   jax.ShapeDtypeStruct((n, LANES), F32),
                   jax.ShapeDtypeStruct((nt, 8, TILE), I32),
                   jax.ShapeDtypeStruct((1, LANES), I32)],
        scratch_shapes=[pltpu.VMEM((1, LANES), F32)],
        compiler_params=_params(("arbitrary",)),
        name="post",
    )(xf, aa, ab, gm, wg, woa, wob, wout, gx, wq, kvm, wox, gmoe, wrt, brt)


def _segment_starts(cnt_ref, seg_ref):
    def body(e, acc):
        seg_ref[e] = acc
        c = cnt_ref[0, R_EXP + e]
        return acc + ((c + (MOE_BLOCK - 1)) >> MOE_SHIFT << MOE_SHIFT)
    return lax.fori_loop(0, N_EXPERTS, body, jnp.int32(0))


MOE_STEP_TILES = 2
MOE_STEP = MOE_STEP_TILES * TILE


def _start_row_copies(ids_ref, seg_ref, rows_ref, tile_ref, sem, gather):
    for sub in range(MOE_STEP_TILES):
        for t in range(TILE):
            row = sub * TILE + t
            for k in range(2):
                dest = seg_ref[ids_ref[sub, k, t]] + ids_ref[sub, 2 + k, t]
                if gather:
                    cp = pltpu.make_async_copy(rows_ref.at[pl.ds(dest, 1)], tile_ref.at[k, pl.ds(row, 1)], sem)
                else:
                    cp = pltpu.make_async_copy(tile_ref.at[pl.ds(row, 1)], rows_ref.at[pl.ds(dest, 1)], sem)
                cp.start(priority=k)


def _wait_row_copies(rows_ref, tile_ref, sem):
    pltpu.make_async_copy(rows_ref.at[pl.ds(0, MOE_STEP)], tile_ref, sem).wait()


def _dispatch_kernel(ids_ref, cnt_ref, hm_ref, xr_ref, blk_ref, seg_ref, zero_ref, sem, zero_sem):
    step = pl.program_id(0)
    nb = blk_ref.shape[1] - 1

    @pl.when(step == 0)
    def _():
        used = _segment_starts(cnt_ref, seg_ref) >> MOE_SHIFT
        zero_ref[...] = jnp.zeros(zero_ref.shape, F32)

        def zero_block(b):
            rows = pl.ds(pl.multiple_of(b << MOE_SHIFT, MOE_BLOCK), MOE_BLOCK)
            return pltpu.make_async_copy(zero_ref, xr_ref.at[rows], zero_sem)

        def per_expert(e, carry):
            last, started = carry
            c = cnt_ref[0, R_EXP + e]
            b0 = seg_ref[e] >> MOE_SHIFT
            n = (c + (MOE_BLOCK - 1)) >> MOE_SHIFT

            def fill(kk, carry):
                blk_ref[0, b0 + kk] = e
                return carry
            lax.fori_loop(0, n, fill, 0)
            pl.when(n > 0)(lambda: zero_block(b0 + n - 1).start())
            return jnp.where(n > 0, e, last), started + (n > 0).astype(I32)
        last, started = lax.fori_loop(0, N_EXPERTS, per_expert, (jnp.int32(0), jnp.int32(0)))

        def tail(b, carry):
            blk_ref[0, b] = last
            zero_block(b).start()
            return carry
        lax.fori_loop(used, nb, tail, 0)
        lax.fori_loop(0, started + (nb - used), lambda _, c: zero_block(0).wait() or c, 0)
        blk_ref[0, nb] = used

    _start_row_copies(ids_ref, seg_ref, xr_ref, hm_ref, sem, gather=False)
    for _ in range(2):
        _wait_row_copies(xr_ref, hm_ref, sem)


def _dispatch(ids, cnt, hm, n_rows):
    n, d = hm.shape
    assert n % MOE_STEP == 0
    nb = n_rows // MOE_BLOCK
    return pl.pallas_call(
        _dispatch_kernel,
        grid=(n // MOE_STEP,),
        in_specs=[pl.BlockSpec((MOE_STEP_TILES, 8, TILE), lambda i: (i, 0, 0), memory_space=pltpu.SMEM),
                  pl.BlockSpec(memory_space=pltpu.SMEM),
                  pl.BlockSpec((MOE_STEP, d), lambda i: (i, 0))],
        out_specs=[pl.BlockSpec(memory_space=pl.ANY),
                   pl.BlockSpec(memory_space=pltpu.SMEM)],
        out_shape=[jax.ShapeDtypeStruct((n_rows, d), F32),
                   jax.ShapeDtypeStruct((1, nb + 1), I32)],
        scratch_shapes=[pltpu.SMEM((N_EXPERTS,), I32), pltpu.VMEM((MOE_BLOCK, d), F32),
                        pltpu.SemaphoreType.DMA(()), pltpu.SemaphoreType.DMA(())],
        compiler_params=_params(("arbitrary",)),
        name="dispatch",
    )(ids, cnt, hm)


def _experts_kernel(blk_ref, x_ref, w1_ref, w3_ref, w2_ref, y_ref,
                    w1b, w3b, w2b, w1f, w3f, w2f, slot_ref, sem):
    b = pl.program_id(0)
    nb = pl.num_programs(0)
    e = blk_ref[0, b]
    prev = blk_ref[0, jnp.maximum(b - 1, 0)]
    used = blk_ref[0, nb]

    def fetch(expert, slot):
        return [pltpu.make_async_copy(w_hbm.at[expert], w_f32.at[slot], sem.at[slot])
                for w_hbm, w_f32 in ((w1_ref, w1f), (w3_ref, w3f), (w2_ref, w2f))]

    @pl.when(b == 0)
    def _():
        slot_ref[0] = 0
        for cp in fetch(e, 0):
            cp.start()

    @pl.when(jnp.logical_or(b == 0, e != prev))
    def _():
        slot = slot_ref[0]
        for cp in fetch(e, slot):
            cp.wait()
        w1b[...] = w1f[slot].astype(BF16)
        w3b[...] = w3f[slot].astype(BF16)
        w2b[...] = w2f[slot].astype(BF16)
        nxt = lax.while_loop(lambda j: jnp.logical_and(j < used, blk_ref[0, jnp.minimum(j, nb - 1)] == e),
                             lambda j: j + 1, b + 1)

        @pl.when(nxt < used)
        def _():
            for cp in fetch(blk_ref[0, nxt], 1 - slot):
                cp.start()
            slot_ref[0] = 1 - slot

    @pl.when(b < used)
    def _():
        xb = x_ref[...].astype(BF16)
        a = _mm(xb, w1b[...])
        g = _mm(xb, w3b[...])
        y_ref[...] = _mm((a * jax.nn.sigmoid(a) * g).astype(BF16), w2b[...])

    @pl.when(b >= used)
    def _():
        y_ref[...] = jnp.zeros(y_ref.shape, F32)


def _experts(blk, xr, w1, w3, w2):
    n_rows, d = xr.shape
    nb = n_rows // MOE_BLOCK
    hbm = pl.BlockSpec(memory_space=pl.ANY)
    return pl.pallas_call(
        _experts_kernel,
        grid_spec=pltpu.PrefetchScalarGridSpec(
            num_scalar_prefetch=1,
            grid=(nb,),
            in_specs=[pl.BlockSpec((MOE_BLOCK, d), lambda b, blk: (b, 0)), hbm, hbm, hbm],
            out_specs=pl.BlockSpec((MOE_BLOCK, d), lambda b, blk: (b, 0)),
            scratch_shapes=[pltpu.VMEM((d, D_EXPERT), BF16), pltpu.VMEM((d, D_EXPERT), BF16),
                            pltpu.VMEM((D_EXPERT, d), BF16),
                            pltpu.VMEM((2, d, D_EXPERT), F32), pltpu.VMEM((2, d, D_EXPERT), F32),
                            pltpu.VMEM((2, D_EXPERT, d), F32),
                            pltpu.SMEM((1,), I32), pltpu.SemaphoreType.DMA((2,))]),
        out_shape=jax.ShapeDtypeStruct((n_rows, d), F32),
        compiler_params=_params(("arbitrary",)),
        name="experts",
    )(blk, xr, w1, w3, w2)


def _combine_kernel(ids_ref, ids_next_ref, cnt_ref, x2_ref, rw_ref, g_ref, yr_ref, o_ref, y_ref, seg_ref, sem):
    step = pl.program_id(0)
    slot = step % 2

    @pl.when(step == 0)
    def _():
        _segment_starts(cnt_ref, seg_ref)
        _start_row_copies(ids_ref, seg_ref, yr_ref, y_ref.at[0], sem.at[0], gather=True)

    @pl.when(step + 1 < pl.num_programs(0))
    def _():
        _start_row_copies(ids_next_ref, seg_ref, yr_ref, y_ref.at[1 - slot], sem.at[1 - slot], gather=True)

    for k in range(2):
        _wait_row_copies(yr_ref, y_ref.at[slot, k], sem.at[slot])
    rw = rw_ref[...]
    x3 = x2_ref[...] + (y_ref[slot, 0] * rw[:, 0:1] + y_ref[slot, 1] * rw[:, 1:2])
    o_ref[...] = _rms(x3, g_ref[...])


def _combine(ids, cnt, x2, rw, g, yr):
    n, d = x2.shape
    steps = n // MOE_STEP
    row = lambda w: pl.BlockSpec((MOE_STEP, w), lambda i: (i, 0))
    ids_spec = lambda step_of: pl.BlockSpec((MOE_STEP_TILES, 8, TILE), lambda i: (step_of(i), 0, 0),
                                            memory_space=pltpu.SMEM)
    return pl.pallas_call(
        _combine_kernel,
        grid=(steps,),
        in_specs=[ids_spec(lambda i: i), ids_spec(lambda i: jnp.minimum(i + 1, steps - 1)),
                  pl.BlockSpec(memory_space=pltpu.SMEM),
                  row(d), row(LANES), pl.BlockSpec((1, d), lambda i: (0, 0)),
                  pl.BlockSpec(memory_space=pl.ANY)],
        out_specs=row(d),
        out_shape=jax.ShapeDtypeStruct((n, d), F32),
        scratch_shapes=[pltpu.VMEM((2, 2, MOE_STEP, d), F32), pltpu.SMEM((N_EXPERTS,), I32),
                        pltpu.SemaphoreType.DMA((2,))],
        compiler_params=_params(("arbitrary",)),
        name="combine",
    )(ids, ids, cnt, x2, rw, g, yr)


def _t5_bucket(dist):
    n = jnp.maximum(dist, 0)
    max_exact = REL_BUCKETS // 2
    nf = jnp.maximum(n, 1).astype(F32)
    large = max_exact + (jnp.log(nf / max_exact) / math.log(REL_MAX_DIST / max_exact)
                         * (REL_BUCKETS - max_exact)).astype(I32)
    large = jnp.minimum(large, REL_BUCKETS - 1)
    return jnp.where(n < max_exact, n, large)


def _bias_tables(tab):
    r = jnp.arange(TILE)[:, None]
    c = jnp.arange(TILE)[None, :]
    rel = tab - tab[:, REL_BUCKETS - 1:]

    def tile(dist):
        onehot = (_t5_bucket(dist)[..., None] == jnp.arange(REL_BUCKETS)).astype(F32)
        return jnp.einsum('rcb,hb->hrc', onehot, rel, precision=lax.Precision.HIGHEST)

    own = jnp.where(r >= c, tile(r - c), NEG)
    prev = tile(r - c + TILE)
    near = jnp.stack([jnp.zeros_like(prev), prev, own], axis=1).astype(F32)
    return near.swapaxes(-1, -2)


def _block_mean_slabs(kmean, bn, nq):
    km = kmean.reshape(bn, nq, A_HEADS, HEAD_DIM).transpose(0, 2, 1, 3)
    pad = ((0, 0), (0, 0), (HEAD_DIM, LANES - HEAD_DIM - nq), (0, LANES - HEAD_DIM))
    return jnp.pad(km, pad).astype(BF16)


def _pad_cols(w, width):
    return jnp.pad(w, ((0, 0), (0, width - w.shape[1])))


def kernel(x, mem, rel_bias, final_norm, norm_mix, w_in, ckv_norm, w_uk, w_uv, w_oa, w_ob,
           w_out, norm_x, mem_norm, wq_x, wk_x, wv_x, wo_x, norm_moe, w_group, b_group,
           w_router, b_router, w1, w3, w2):
    bn, t, d = x.shape
    n = bn * t
    nq = t // TILE
    assert t % TILE == 0 and nq <= LANES - HEAD_DIM and norm_mix.shape[0] == 1
    assert nq % (FAR_GROUP // 4) == 0
    near_a = _bias_tables(rel_bias[:, :A_HEADS].T)
    near_b = _bias_tables(rel_bias[:, A_HEADS:].T)
    row = lambda v: v.reshape(1, -1).astype(F32)

    wi = w_in[0]
    wp = jnp.concatenate(
        [wi[:, OFF_AQ:OFF_IK], jnp.tile(wi[:, OFF_IK:OFF_IW], (1, LANES // IDX_DIM)),
         _pad_cols(wi[:, OFF_IW:OFF_GA], LANES)], axis=1).astype(BF16)
    wkv = jnp.concatenate([w_uk[0], w_uv[0]], axis=1).astype(BF16)
    xf = x.reshape(n, d)
    qa, ka, va, qb, kb, vb, qi, ki4, widx, kmean = _proj_in(
        xf, row(norm_mix[0]), wp, row(ckv_norm[0]), wkv, nq)

    b3 = lambda a: a.reshape(bn, t, a.shape[-1])
    attn_a = _moba(b3(qa), b3(ka), va.reshape(bn, nq, W_A, TILE), _block_mean_slabs(kmean, bn, nq), near_a)
    attn_b = _dsa(b3(qi), b3(ki4), b3(widx), b3(qb), b3(kb), vb.reshape(bn, nq, W_B, TILE), near_b)

    kvm = _mem_kv(mem, row(mem_norm[0]), jnp.concatenate([wk_x[0], wv_x[0]], axis=1).astype(BF16))
    wrt = _pad_cols(jnp.concatenate([w_group[0], w_router[0]], axis=1), LANES).astype(F32)
    wrt_hi = wrt.astype(BF16)
    wrt = jnp.concatenate([wrt_hi, (wrt - wrt_hi.astype(F32)).astype(BF16)], axis=1)
    brt = _pad_cols(jnp.concatenate([b_group[0], b_router[0]]).reshape(1, -1), LANES).astype(F32)
    weights = (row(norm_mix[0]), wi[:, OFF_GA:IN_COLS].astype(BF16), w_oa[0].astype(BF16),
               w_ob[0].astype(BF16), w_out[0].astype(BF16), row(norm_x[0]), wq_x[0].astype(BF16),
               wo_x[0].astype(BF16), row(norm_moe[0]), wrt, brt)
    x2, hm, rw, ids, cnt = _post(xf, attn_a.reshape(n, W_A), attn_b.reshape(n, W_B), kvm, t // TILE, weights)

    n_rows = 2 * n + N_EXPERTS * MOE_BLOCK
    xr, blk = _dispatch(ids, cnt, hm, n_rows)
    yr = _experts(blk, xr, w1[0], w3[0], w2[0])
    out = _combine(ids, cnt, x2, rw, row(final_norm), yr)
    return out.reshape(bn, t, d)
```
